```python
import jax, jax.numpy as jnp
from jax import lax
import numpy as np

D_MODEL = 1024
BATCH = 8
SEQ = 2048
DEPTH = 2
DEC_BATCH = 32
DEC_SEQ = 4
PAST_LEN = 8192
PAGE_SIZE = 128

N_EVEN = (DEPTH + 1) // 2
N_ODD = DEPTH // 2
D_FF = 2816
EPS = 1e-6
H_A = 8
DH_A = 64
H_IDX = 16
D_IDX = 64
TOPK_MAX = 256
IDX_Q_BLOCK = 64
H_B = 8
DH_B = 64
R_DECAY = 64
R_AAA = 64
R_GATE = 128
D_B_IN = 3 * H_B * DH_B + R_DECAY + R_AAA + R_GATE
LNX_EPS = 64e-5
H_C = 4
DK_C = 64
DV_C = 128
ROPE_BASE = 10000.0
H_D = 4
DK_D = 64
DV_D = 128
CONV_W = 4
CHUNK = 64
D_AB_IN = 3 * H_A * DH_A + H_IDX * D_IDX + D_IDX + H_IDX + D_B_IN
D_AB_OUT = H_A * DH_A + H_B * DH_B
D_CD_IN = 2 * H_C * DK_C + 2 * H_C * DV_C + 2 * H_D * DK_D + 2 * H_D * DV_D + 2 * H_D
D_CD_OUT = H_C * DV_C + H_D * DV_D

kernel_name = 'hybrid_dsa_rwkv7_retnet_mlstm_step'

F32 = jnp.float32


def rmsnorm(x, g):
    xf = x.astype(F32)
    y = xf * lax.rsqrt(jnp.mean(xf * xf, axis=-1, keepdims=True) + EPS)
    return (y * g.astype(F32)).astype(x.dtype)


def head_norm(x, g, b=None, eps=1e-5):
    xf = x.astype(F32)
    mu = jnp.mean(xf, axis=-1, keepdims=True)
    var = jnp.mean(jnp.square(xf - mu), axis=-1, keepdims=True)
    y = (xf - mu) * lax.rsqrt(var + eps) * g.astype(F32)
    if b is not None:
        y = y + b.astype(F32)
    return y


def swiglu(x, wg, wu, wd):
    return (jax.nn.silu(x @ wg) * (x @ wu)) @ wd


def split_cols(z, sizes):
    return jnp.split(z, [int(c) for c in np.cumsum(sizes)[:-1]], axis=-1)


def rope(x, pos):
    half = x.shape[-1] // 2
    inv = ROPE_BASE ** (-jnp.arange(half, dtype=F32) / half)
    ang = pos.astype(F32)[:, None] * inv[None, :]
    cos = jnp.cos(ang)[None, :, None, :]
    sin = jnp.sin(ang)[None, :, None, :]
    x1, x2 = x[..., :half], x[..., half:]
    return jnp.concatenate([x1 * cos - x2 * sin, x1 * sin + x2 * cos], axis=-1)


def gather_rows(a, idx):
    return jax.vmap(lambda ab, ib: ab[ib])(a, idx)


def indexer_scores(qi, ki, wi):
    s = jnp.einsum('bqhd,bsd->bqhs', qi, ki)
    return jnp.einsum('bqhs,bqh->bqs', jax.nn.relu(s), wi).astype(F32)


def sparse_attend(q, kg, vg, valid):
    logits = jnp.einsum('bqhd,bqkhd->bqhk', q.astype(F32), kg.astype(F32)) * DH_A ** -0.5
    logits = jnp.where(valid[:, :, None, :], logits, -jnp.inf)
    p = jax.nn.softmax(logits, axis=-1)
    return jnp.einsum('bqhk,bqkhd->bqhd', p, vg.astype(F32))


def dsa_prompt(q, k, v, qi, ki, wi):
    B, T = q.shape[:2]
    topk = min(TOPK_MAX, T // 4)
    nb = T // IDX_Q_BLOCK

    def blocks(a):
        return jnp.swapaxes(a.reshape((B, nb, IDX_Q_BLOCK) + a.shape[2:]), 0, 1)

    t_blk = jnp.arange(T).reshape(nb, IDX_Q_BLOCK)
    s_pos = jnp.arange(T)

    def one_block(args):
        qb, qib, wib, tb = args
        isc = indexer_scores(qib, ki, wib)
        isc = jnp.where((s_pos[None, :] <= tb[:, None])[None], isc, -jnp.inf)
        vals, sel = lax.top_k(isc, topk)
        return sparse_attend(qb, gather_rows(k, sel), gather_rows(v, sel), jnp.isfinite(vals))

    out = lax.map(one_block, (blocks(q), blocks(qi), blocks(wi), t_blk))
    return jnp.swapaxes(out, 0, 1).reshape(B, T, H_A, DH_A)


def dsa_sample(q, k, v, qi, ki, wi, ck, cv, cki, page_table):
    DB, S = q.shape[:2]
    past = page_table.shape[1] * PAGE_SIZE
    L = past + S
    topk = min(TOPK_MAX, L // 4)
    ki_past = cki[page_table].reshape(DB, past, D_IDX).astype(ki.dtype)
    ki_all = jnp.concatenate([ki_past, ki], axis=1)
    isc = indexer_scores(qi, ki_all, wi)
    adm = jnp.arange(L)[None, :] <= (past + jnp.arange(S))[:, None]
    isc = jnp.where(adm[None], isc, -jnp.inf)
    vals, sel = lax.top_k(isc, topk)
    is_past = (sel < past)[..., None, None]
    phys = jax.vmap(lambda pt, s: pt[s])(page_table, jnp.minimum(sel, past - 1) // PAGE_SIZE)
    off = sel % PAGE_SIZE
    new_i = jnp.clip(sel - past, 0, S - 1)
    kg = jnp.where(is_past, ck[phys, off].astype(k.dtype), gather_rows(k, new_i))
    vg = jnp.where(is_past, cv[phys, off].astype(v.dtype), gather_rows(v, new_i))
    return sparse_attend(q, kg, vg, jnp.isfinite(vals))


def rwkv_step(S, inp):
    r_t, w_t, k_t, v_t, a_t, b_t = inp
    sa = jnp.einsum('bhij,bhj->bhi', S, a_t)
    S = S * w_t[:, :, None, :] + sa[..., None] * b_t[:, :, None, :] + v_t[..., None] * k_t[:, :, None, :]
    return S, jnp.einsum('bhij,bhj->bhi', S, r_t)


def rwkv7(zb, shift_prev, S0, e, P):
    B, T, _ = zb.shape
    zb = zb.astype(F32)
    prev = jnp.concatenate([shift_prev.astype(F32)[:, None], zb[:, :-1]], axis=1)
    z = zb + (prev - zb) * P['rwkv_mu'][e].astype(F32)
    r, k, v, xw, xa, xg = split_cols(z, [H_B * DH_B] * 3 + [R_DECAY, R_AAA, R_GATE])
    w_log = -jax.nn.softplus(-(P['rwkv_w0'][e] + jnp.tanh(xw) @ P['rwkv_w2'][e])) - 0.5
    decay = jnp.exp(-jnp.exp(w_log))
    a = jax.nn.sigmoid(P['rwkv_a0'][e] + xa @ P['rwkv_a2'][e])
    g = jax.nn.sigmoid(xg) @ P['rwkv_g2'][e]

    def hs(t):
        return t.reshape(B, T, H_B, DH_B)

    r, k, v, decay, a = hs(r), hs(k), hs(v), hs(decay), hs(a)
    kk = k * P['rwkv_kk'][e].reshape(H_B, DH_B)
    kk = kk * lax.rsqrt(jnp.maximum(jnp.sum(kk * kk, axis=-1, keepdims=True), 1e-24))
    k = k * (1.0 + (a - 1.0) * P['rwkv_ka'][e].reshape(H_B, DH_B))

    def tm(t):
        return jnp.swapaxes(t, 0, 1)

    S_fin, y = lax.scan(rwkv_step, S0.astype(F32), (tm(r), tm(decay), tm(k), tm(v), tm(-kk), tm(kk * a)))
    y = tm(y)
    y = head_norm(y, P['rwkv_lnx_g'][e].reshape(H_B, DH_B), P['rwkv_lnx_b'][e].reshape(H_B, DH_B), eps=LNX_EPS)
    y = y + jnp.sum(r * k * P['rwkv_rk'][e], axis=-1, keepdims=True) * v
    y = y.reshape(B, T, H_B * DH_B) * g
    return y, zb[:, -1], S_fin


def chunked_scan(step, state, xs):
    B, T = xs[0].shape[:2]
    lc = CHUNK if T % CHUNK == 0 else T
    n = T // lc
    xs_c = tuple(jnp.swapaxes(a.reshape((B, n, lc) + a.shape[2:]), 0, 1) for a in xs)
    state, ys = lax.scan(step, state, xs_c)
    return state, jnp.swapaxes(ys, 0, 1).reshape((B, T) + ys.shape[3:])


def retention_chunk(S, inp):
    q, k, v = inp
    lc = q.shape[1]
    lg = jnp.log1p(-jnp.exp2(-5.0 - jnp.arange(H_C, dtype=F32)))
    j = jnp.arange(lc, dtype=F32)
    diff = j[:, None] - j[None, :]
    dmask = jnp.where(diff >= 0, jnp.exp(jnp.maximum(diff, 0.0)[None] * lg[:, None, None]), 0.0)
    scores = jnp.einsum('bjhd,bshd->bhjs', q, k) * dmask[None]
    intra = jnp.einsum('bhjs,bshe->bjhe', scores, v)
    inter = jnp.einsum('bjhd,bhde->bjhe', q, S) * jnp.exp((j[:, None] + 1.0) * lg[None, :])[None, :, :, None]
    w_s = jnp.exp((lc - 1.0 - j)[:, None] * lg[None, :])
    S_new = jnp.exp(lc * lg)[None, :, None, None] * S + jnp.einsum('bshd,bshe,sh->bhde', k, v, w_s)
    return S_new, intra + inter


def mlstm_chunk(state, inp):
    C, n, m = state
    q, k, v, logi, logf = inp
    lc = q.shape[1]
    b = jnp.cumsum(logf, axis=1)
    causal = jnp.tril(jnp.ones((lc, lc), dtype=bool))
    D = b[:, :, None, :] - b[:, None, :, :] + logi[:, None, :, :]
    D = jnp.where(causal[None, :, :, None], D, -jnp.inf)
    inter = b + m[:, None, :]
    m_j = jnp.maximum(inter, jnp.max(D, axis=2))
    A = jnp.exp(D - m_j[:, :, None, :]) * jnp.einsum('bjhd,bshd->bjsh', q, k)
    sc = jnp.exp(inter - m_j)
    num = jnp.einsum('bjsh,bshe->bjhe', A, v) + sc[..., None] * jnp.einsum('bjhd,bhed->bjhe', q, C)
    den = jnp.sum(A, axis=2) + sc * jnp.einsum('bjhd,bhd->bjh', q, n)
    h = num / jnp.maximum(jnp.abs(den), jnp.exp(-m_j))[..., None]
    bL = b[:, -1]
    gs = bL[:, None, :] - b + logi
    m_new = jnp.maximum(bL + m, jnp.max(gs, axis=1))
    ws = jnp.exp(gs - m_new[:, None, :])
    carry = jnp.exp(bL + m - m_new)
    C_new = carry[..., None, None] * C + jnp.einsum('bsh,bshe,bshd->bhed', ws, v, k)
    n_new = carry[..., None] * n + jnp.einsum('bsh,bshd->bhd', ws, k)
    return (C_new, n_new, m_new), h


def causal_conv_silu(u, buf, w, b):
    T = u.shape[1]
    full = jnp.concatenate([buf.astype(u.dtype), u], axis=1)
    y = b
    for i in range(CONV_W):
        y = y + full[:, i:i + T] * w[i]
    return jax.nn.silu(y), full[:, T:]


def ab_mixer(h, e, dsa_fn, shift_prev, S0, P):
    B, T, _ = h.shape
    z = h @ P['ab_w_in'][e]
    q, k, v, qi, ki, wi, zb = split_cols(z, [H_A * DH_A] * 3 + [H_IDX * D_IDX, D_IDX, H_IDX, D_B_IN])
    q = q.reshape(B, T, H_A, DH_A)
    k = k.reshape(B, T, H_A, DH_A)
    v = v.reshape(B, T, H_A, DH_A)
    qi = qi.reshape(B, T, H_IDX, D_IDX)
    ki = rmsnorm(ki, P['kidx_g'][e])
    wi = wi * (H_IDX * D_IDX) ** -0.5
    ya = dsa_fn(e, q, k, v, qi, ki, wi).reshape(B, T, H_A * DH_A)
    yb, shift_new, S_new = rwkv7(zb, shift_prev, S0, e, P)
    y = jnp.concatenate([ya, yb], axis=-1).astype(h.dtype) @ P['ab_w_out'][e]
    return y, (k, v, ki, S_new, shift_new)


def cd_mixer(h, pos, o, ret_S, mC, mn, mm, conv_buf, P):
    B, T, _ = h.shape
    z = h @ P['cd_w_in'][o]
    qc, kc, vc, gc, qkd, vd, ig, fg, og = split_cols(
        z, [H_C * DK_C, H_C * DK_C, H_C * DV_C, H_C * DV_C, 2 * H_D * DK_D, H_D * DV_D, H_D, H_D, H_D * DV_D])
    qc = rope(qc.astype(F32).reshape(B, T, H_C, DK_C), pos)
    kc = rope(kc.astype(F32).reshape(B, T, H_C, DK_C), pos) * DK_C ** -0.5
    vc = vc.astype(F32).reshape(B, T, H_C, DV_C)
    ret_S, yc = chunked_scan(retention_chunk, ret_S.astype(F32), (qc, kc, vc))
    yc = head_norm(yc, P['ret_gn'][o]).reshape(B, T, H_C * DV_C) * jax.nn.silu(gc.astype(F32))
    qkd, conv_new = causal_conv_silu(qkd, conv_buf, P['conv_w'][o], P['conv_b'][o])
    qd, kd = split_cols(qkd.astype(F32), [H_D * DK_D, H_D * DK_D])
    qd = qd.reshape(B, T, H_D, DK_D)
    kd = kd.reshape(B, T, H_D, DK_D) * DK_D ** -0.5
    vd = vd.astype(F32).reshape(B, T, H_D, DV_D)
    b_if = P['mlstm_if_b'][o].astype(F32)
    logi = ig.astype(F32) + b_if[0]
    logf = jax.nn.log_sigmoid(fg.astype(F32) + b_if[1])
    (mC, mn, mm), hd = chunked_scan(mlstm_chunk, (mC.astype(F32), mn.astype(F32), mm.astype(F32)),
                                    (qd, kd, vd, logi, logf))
    yd = head_norm(hd, P['mlstm_gn'][o]).reshape(B, T, H_D * DV_D) * jax.nn.sigmoid(og.astype(F32))
    y = jnp.concatenate([yc, yd], axis=-1).astype(h.dtype) @ P['cd_w_out'][o]
    return y, (ret_S, mC, mn, mm, conv_new)


def trunk(x, pos, dsa_fn, ab_init, cd_init, P):
    ab_new, cd_new = [], []
    for l in range(DEPTH):
        g = P['norm_g'][l]
        hf = swiglu(rmsnorm(x, g[0]), P['ffn_wg'][l, 0], P['ffn_wu'][l, 0], P['ffn_wd'][l, 0])
        x = x + 0.5 * rmsnorm(hf, g[1])
        h = rmsnorm(x, g[2])
        if l % 2 == 0:
            y, st = ab_mixer(h, l // 2, dsa_fn, *ab_init(l // 2), P)
            ab_new.append(st)
        else:
            y, st = cd_mixer(h, pos, l // 2, *cd_init(l // 2), P)
            cd_new.append(st)
        x = x + rmsnorm(y, g[3])
        hf = swiglu(rmsnorm(x, g[4]), P['ffn_wg'][l, 1], P['ffn_wu'][l, 1], P['ffn_wd'][l, 1])
        x = x + 0.5 * rmsnorm(hf, g[5])
    ab = tuple(jnp.stack(s) for s in zip(*ab_new))
    cd = tuple(jnp.stack(s) for s in zip(*cd_new))
    return x, ab, cd


def setup_inputs(seed: int = 0) -> dict:
    key = jax.random.key(seed)
    ks = iter(jax.random.split(key, 64))

    def nrm(shape, s=1.0):
        return jax.random.normal(next(ks), shape, F32) * s

    n_pages = PAST_LEN // PAGE_SIZE
    n_pool = (5 * DEC_BATCH * n_pages + 3) // 4
    kv_shape = (N_EVEN, n_pool, PAGE_SIZE, H_A, DH_A)
    d = {}
    d['x_prompt'] = nrm((BATCH, SEQ, D_MODEL))
    d['x_sample'] = nrm((DEC_BATCH, DEC_SEQ, D_MODEL))
    d['cache_k'] = nrm(kv_shape)
    d['cache_v'] = nrm(kv_shape)
    d['cache_kidx'] = nrm((N_EVEN, n_pool, PAGE_SIZE, D_IDX))
    d['state_rwkv'] = nrm((N_EVEN, DEC_BATCH, H_B, DH_B, DH_B), 0.3)
    d['state_shift'] = nrm((N_EVEN, DEC_BATCH, D_B_IN))
    d['state_ret'] = nrm((N_ODD, DEC_BATCH, H_C, DK_C, DV_C), 0.1)
    d['state_mlstm_C'] = nrm((N_ODD, DEC_BATCH, H_D, DV_D, DK_D), 0.1)
    d['state_mlstm_n'] = nrm((N_ODD, DEC_BATCH, H_D, DK_D), 0.1)
    d['state_mlstm_m'] = nrm((N_ODD, DEC_BATCH, H_D))
    d['state_conv'] = nrm((N_ODD, DEC_BATCH, CONV_W - 1, 2 * H_D * DK_D))
    perm = jax.random.permutation(next(ks), n_pool)
    d['page_table'] = perm[:DEC_BATCH * n_pages].reshape(DEC_BATCH, n_pages).astype(jnp.int32)
    d['norm_g'] = 1.0 + nrm((DEPTH, 6, D_MODEL), 0.02)
    d['ffn_wg'] = nrm((DEPTH, 2, D_MODEL, D_FF), D_MODEL ** -0.5)
    d['ffn_wu'] = nrm((DEPTH, 2, D_MODEL, D_FF), D_MODEL ** -0.5)
    d['ffn_wd'] = nrm((DEPTH, 2, D_FF, D_MODEL), D_FF ** -0.5)
    d['ab_w_in'] = nrm((N_EVEN, D_MODEL, D_AB_IN), D_MODEL ** -0.5)
    d['ab_w_out'] = nrm((N_EVEN, D_AB_OUT, D_MODEL), D_AB_OUT ** -0.5)
    d['kidx_g'] = 1.0 + nrm((N_EVEN, D_IDX), 0.02)
    d['rwkv_mu'] = jax.random.uniform(next(ks), (N_EVEN, D_B_IN), F32)
    d['rwkv_w0'] = nrm((N_EVEN, H_B * DH_B), 0.5)
    d['rwkv_w2'] = nrm((N_EVEN, R_DECAY, H_B * DH_B), R_DECAY ** -0.5)
    d['rwkv_a0'] = nrm((N_EVEN, H_B * DH_B), 0.1)
    d['rwkv_a2'] = nrm((N_EVEN, R_AAA, H_B * DH_B), R_AAA ** -0.5)
    d['rwkv_g2'] = nrm((N_EVEN, R_GATE, H_B * DH_B), R_GATE ** -0.5)
    d['rwkv_kk'] = 1.0 + nrm((N_EVEN, H_B * DH_B), 0.1)
    d['rwkv_ka'] = 1.0 + nrm((N_EVEN, H_B * DH_B), 0.1)
    d['rwkv_rk'] = nrm((N_EVEN, H_B, DH_B), 0.1)
    d['rwkv_lnx_g'] = 1.0 + nrm((N_EVEN, H_B * DH_B), 0.02)
    d['rwkv_lnx_b'] = nrm((N_EVEN, H_B * DH_B), 0.02)
    d['cd_w_in'] = nrm((N_ODD, D_MODEL, D_CD_IN), D_MODEL ** -0.5)
    d['cd_w_out'] = nrm((N_ODD, D_CD_OUT, D_MODEL), D_CD_OUT ** -0.5)
    d['ret_gn'] = 1.0 + nrm((N_ODD, H_C, DV_C), 0.02)
    d['conv_w'] = nrm((N_ODD, CONV_W, 2 * H_D * DK_D), CONV_W ** -0.5)
    d['conv_b'] = nrm((N_ODD, 2 * H_D * DK_D), 0.01)
    f_b = jnp.broadcast_to(jnp.linspace(3.0, 6.0, H_D, dtype=F32), (N_ODD, H_D))
    d['mlstm_if_b'] = jnp.stack([nrm((N_ODD, H_D), 0.1), f_b + nrm((N_ODD, H_D), 0.1)], axis=1)
    d['mlstm_gn'] = 1.0 + nrm((N_ODD, H_D, DV_D), 0.02)
    return d


def reference(x_prompt, x_sample, cache_k, cache_v, cache_kidx, state_rwkv, state_shift, state_ret,
              state_mlstm_C, state_mlstm_n, state_mlstm_m, state_conv, page_table, norm_g, ffn_wg, ffn_wu,
              ffn_wd, ab_w_in, ab_w_out, kidx_g, rwkv_mu, rwkv_w0, rwkv_w2, rwkv_a0, rwkv_a2, rwkv_g2, rwkv_kk,
              rwkv_ka, rwkv_rk, rwkv_lnx_g, rwkv_lnx_b, cd_w_in, cd_w_out, ret_gn, conv_w, conv_b, mlstm_if_b,
              mlstm_gn):
    P = dict(norm_g=norm_g, ffn_wg=ffn_wg, ffn_wu=ffn_wu, ffn_wd=ffn_wd, ab_w_in=ab_w_in, ab_w_out=ab_w_out,
             kidx_g=kidx_g, rwkv_mu=rwkv_mu, rwkv_w0=rwkv_w0, rwkv_w2=rwkv_w2, rwkv_a0=rwkv_a0, rwkv_a2=rwkv_a2,
             rwkv_g2=rwkv_g2, rwkv_kk=rwkv_kk, rwkv_ka=rwkv_ka, rwkv_rk=rwkv_rk, rwkv_lnx_g=rwkv_lnx_g,
             rwkv_lnx_b=rwkv_lnx_b, cd_w_in=cd_w_in, cd_w_out=cd_w_out, ret_gn=ret_gn, conv_w=conv_w,
             conv_b=conv_b, mlstm_if_b=mlstm_if_b, mlstm_gn=mlstm_gn)
    B, T, _ = x_prompt.shape
    DB, DS, _ = x_sample.shape
    past = page_table.shape[1] * PAGE_SIZE

    def ab_zero(e):
        return (jnp.zeros((B, D_B_IN), F32), jnp.zeros((B, H_B, DH_B, DH_B), F32))

    def cd_zero(o):
        return (jnp.zeros((B, H_C, DK_C, DV_C), F32), jnp.zeros((B, H_D, DV_D, DK_D), F32),
                jnp.zeros((B, H_D, DK_D), F32), jnp.zeros((B, H_D), F32),
                jnp.zeros((B, CONV_W - 1, 2 * H_D * DK_D), x_prompt.dtype))

    def ab_cached(e):
        return (state_shift[e], state_rwkv[e])

    def cd_cached(o):
        return (state_ret[o], state_mlstm_C[o], state_mlstm_n[o], state_mlstm_m[o], state_conv[o])

    def dsa_p(e, q, k, v, qi, ki, wi):
        return dsa_prompt(q, k, v, qi, ki, wi)

    def dsa_s(e, q, k, v, qi, ki, wi):
        return dsa_sample(q, k, v, qi, ki, wi, cache_k[e], cache_v[e], cache_kidx[e], page_table)

    y_p, (kp, vp, kip, rwp, shp), (rtp, cp, nvp, mp, cvp) = trunk(
        x_prompt, jnp.arange(T), dsa_p, ab_zero, cd_zero, P)
    y_s, (ks_, vs_, kis, rws, shs), (rts, cs, nvs, ms, cvs) = trunk(
        x_sample, past + jnp.arange(DS), dsa_s, ab_cached, cd_cached, P)
    return (y_p, y_s, kp, vp, kip, rwp, shp, rtp, cp, nvp, mp, cvp,
            ks_, vs_, kis, rws, shs, rts, cs, nvs, ms, cvs)
```

```python
import functools
import math

import numpy as np
import jax
import jax.numpy as jnp
from jax import lax
from jax.experimental import pallas as pl
from jax.experimental.pallas import tpu as pltpu

F32 = jnp.float32
BF16 = jnp.bfloat16
I32 = jnp.int32
HI = lax.Precision.HIGHEST

LANES = 128
SUBLANES = 8
VMEM_LIMIT = 56 * 1024 * 1024

EPS = 1e-6
PAGE_SIZE = 128
H_A, DH_A, H_IDX, D_IDX, TOPK_MAX = 8, 64, 16, 64, 256
H_B, DH_B, R_DECAY, R_AAA, R_GATE = 8, 64, 64, 64, 128
D_B_IN = 3 * H_B * DH_B + R_DECAY + R_AAA + R_GATE
LNX_EPS = 64e-5
H_C, DK_C, DV_C, ROPE_BASE = 4, 64, 128, 10000.0
H_D, DK_D, DV_D, CONV_W = 4, 64, 128, 4
HN_EPS = 1e-5
NEG_BIG = -1e30
INT_MIN = -(2 ** 31)

NT = (((1,), (1,)), ((), ()))
TN = (((0,), (0,)), ((), ()))


def _cparams(sem):
    return pltpu.CompilerParams(dimension_semantics=sem, vmem_limit_bytes=VMEM_LIMIT)


def _rms(x, g):
    return x * lax.rsqrt(jnp.mean(x * x, axis=-1, keepdims=True) + EPS) * g


def _softplus(x):
    return jnp.maximum(x, 0.0) + jnp.log(1.0 + jnp.exp(-jnp.abs(x)))


def _row_tile(m, want):
    t = min(want, m)
    while m % t:
        t //= 2
    return t


def _col_tile(n, want):
    best = LANES
    for t in range(LANES, min(n, want) + 1, LANES):
        if n % t == 0:
            best = t
    return best


def _ffn_body(x_ref, g0_ref, g1_ref, wg_ref, wu_ref, wd_ref, o_ref, h_scr, acc_scr):
    j = pl.program_id(1)

    @pl.when(j == 0)
    def _():
        h_scr[...] = _rms(x_ref[...], g0_ref[...]).astype(BF16)
        acc_scr[...] = jnp.zeros_like(acc_scr)

    h = h_scr[...]
    g = jnp.dot(h, wg_ref[...], preferred_element_type=F32)
    u = jnp.dot(h, wu_ref[...], preferred_element_type=F32)
    a = g * jax.nn.sigmoid(g) * u
    acc_scr[...] += jnp.dot(a.astype(BF16), wd_ref[...], preferred_element_type=F32)

    @pl.when(j == pl.num_programs(1) - 1)
    def _():
        o_ref[...] = x_ref[...] + 0.5 * _rms(acc_scr[...], g1_ref[...])


def ffn_call(x, g0, g1, wg, wu, wd):
    m, d = x.shape
    ff = wg.shape[1]
    tm = _row_tile(m, 512)
    tf = _col_tile(ff, 1408)
    return pl.pallas_call(
        _ffn_body,
        grid=(m // tm, ff // tf),
        in_specs=[
            pl.BlockSpec((tm, d), lambda i, j: (i, 0)),
            pl.BlockSpec((1, d), lambda i, j: (0, 0)),
            pl.BlockSpec((1, d), lambda i, j: (0, 0)),
            pl.BlockSpec((d, tf), lambda i, j: (0, j)),
            pl.BlockSpec((d, tf), lambda i, j: (0, j)),
            pl.BlockSpec((tf, d), lambda i, j: (j, 0)),
        ],
        out_specs=pl.BlockSpec((tm, d), lambda i, j: (i, 0)),
        out_shape=jax.ShapeDtypeStruct((m, d), F32),
        scratch_shapes=[pltpu.VMEM((tm, d), BF16), pltpu.VMEM((tm, d), F32)],
        compiler_params=_cparams(("parallel", "arbitrary")),
        name="ffn",
    )(x, g0.reshape(1, d), g1.reshape(1, d), wg, wu, wd)


def _proj_in_body(x_ref, g_ref, w_ref, o_ref, h_scr):
    @pl.when(pl.program_id(1) == 0)
    def _():
        h_scr[...] = _rms(x_ref[...], g_ref[...]).astype(BF16)

    o_ref[...] = jnp.dot(h_scr[...], w_ref[...], preferred_element_type=F32).astype(o_ref.dtype)


def proj_in_call(x, g, w, out_dtype=F32):
    m, d = x.shape
    n = w.shape[1]
    tm = _row_tile(m, 512)
    tn = _col_tile(n, 1024)
    return pl.pallas_call(
        _proj_in_body,
        grid=(m // tm, n // tn),
        in_specs=[
            pl.BlockSpec((tm, d), lambda i, j: (i, 0)),
            pl.BlockSpec((1, d), lambda i, j: (0, 0)),
            pl.BlockSpec((d, tn), lambda i, j: (0, j)),
        ],
        out_specs=pl.BlockSpec((tm, tn), lambda i, j: (i, j)),
        out_shape=jax.ShapeDtypeStruct((m, n), out_dtype),
        scratch_shapes=[pltpu.VMEM((tm, d), BF16)],
        compiler_params=_cparams(("parallel", "arbitrary")),
        name="proj_in",
    )(x, g.reshape(1, d), w)


def _kiwi_body(x_ref, g_ref, w_ref, kg_ref, o_ref):
    h = _rms(x_ref[...], g_ref[...]).astype(BF16)
    z = jnp.dot(h, w_ref[...], preferred_element_type=F32)
    lane = lax.broadcasted_iota(I32, z.shape, 1)
    is_k = lane < D_IDX
    ms = jnp.sum(jnp.where(is_k, z * z, 0.0), axis=-1, keepdims=True) * (1.0 / D_IDX)
    kin = z * lax.rsqrt(ms + EPS) * kg_ref[...]
    o_ref[...] = jnp.where(is_k, kin, z * (H_IDX * D_IDX) ** -0.5)


def kiwi_call(x, g, w, kg):
    m, d = x.shape
    tm = _row_tile(m, 512)
    return pl.pallas_call(
        _kiwi_body,
        grid=(m // tm,),
        in_specs=[
            pl.BlockSpec((tm, d), lambda i: (i, 0)),
            pl.BlockSpec((1, d), lambda i: (0, 0)),
            pl.BlockSpec((d, LANES), lambda i: (0, 0)),
            pl.BlockSpec((1, LANES), lambda i: (0, 0)),
        ],
        out_specs=pl.BlockSpec((tm, LANES), lambda i: (i, 0)),
        out_shape=jax.ShapeDtypeStruct((m, LANES), F32),
        compiler_params=_cparams(("parallel",)),
        name="kiwi",
    )(x, g.reshape(1, d), w, kg)


def _proj_out_body(a1_ref, a2_ref, x_ref, g_ref, w1_ref, w2_ref, o_ref):
    y = jnp.dot(a1_ref[...].astype(BF16), w1_ref[...], preferred_element_type=F32)
    y = y + jnp.dot(a2_ref[...].astype(BF16), w2_ref[...], preferred_element_type=F32)
    o_ref[...] = x_ref[...] + _rms(y, g_ref[...])


def proj_out_call(a1, a2, x, g, w1, w2):
    m, d = x.shape
    k1, k2 = a1.shape[1], a2.shape[1]
    tm = _row_tile(m, 512)
    return pl.pallas_call(
        _proj_out_body,
        grid=(m // tm,),
        in_specs=[
            pl.BlockSpec((tm, k1), lambda i: (i, 0)),
            pl.BlockSpec((tm, k2), lambda i: (i, 0)),
            pl.BlockSpec((tm, d), lambda i: (i, 0)),
            pl.BlockSpec((1, d), lambda i: (0, 0)),
            pl.BlockSpec((k1, d), lambda i: (0, 0)),
            pl.BlockSpec((k2, d), lambda i: (0, 0)),
        ],
        out_specs=pl.BlockSpec((tm, d), lambda i: (i, 0)),
        out_shape=jax.ShapeDtypeStruct((m, d), F32),
        compiler_params=_cparams(("parallel",)),
        name="proj_out",
    )(a1, a2, x, g.reshape(1, d), w1, w2)


def _topk_bias(isc, col, topk, nbits_col):
    isc = jnp.where(isc == 0.0, 0.0, isc)
    bits = pltpu.bitcast(isc, I32)
    key = jnp.where(bits < 0, bits ^ 0x7FFFFFFF, bits)
    kf = float(topk)

    def count(mask):
        return jnp.sum(jnp.where(mask, 1.0, 0.0), axis=-1, keepdims=True)

    prefix = jnp.where(count(key >= 0) >= kf, 0, INT_MIN).astype(I32)

    def bit_step(i, prefix):
        cand = prefix + jnp.left_shift(jnp.int32(1), 30 - i)
        return jnp.where(count(key >= cand) >= kf, cand, prefix)

    thr = lax.fori_loop(0, 31, bit_step, prefix)
    gt = key > thr
    eq = key == thr
    need = kf - count(gt)

    def col_step(i, y):
        cand = y + jnp.left_shift(jnp.int32(1), nbits_col - 1 - i)
        c = jnp.sum(jnp.where(eq, jnp.where(col < cand, 1.0, 0.0), 0.0), axis=-1, keepdims=True)
        return jnp.where(c < need, cand, y)

    y = lax.fori_loop(0, nbits_col, col_step, jnp.zeros_like(thr))
    ninf = -jnp.inf
    bias = jnp.where(gt, 0.0, jnp.where(eq, jnp.where(col <= y, 0.0, ninf), ninf))
    return jnp.where(jnp.abs(isc) < jnp.inf, bias, ninf)


def _dsa_prompt_body(q_ref, k_ref, v_ref, qi_ref, ki_ref, wi_ref, o_ref, *, topk):
    qb = q_ref.shape[2]
    t = k_ref.shape[2]
    t0 = pl.program_id(1) * qb
    ki = ki_ref[0]
    wi = wi_ref[0]
    isc = jnp.zeros((qb, t), F32)
    for h in range(H_IDX):
        s = lax.dot_general(qi_ref[0, h], ki, NT, preferred_element_type=F32)
        isc = isc + wi[:, h:h + 1] * jnp.maximum(s, 0.0)
    row = lax.broadcasted_iota(I32, (qb, t), 0) + t0
    col = lax.broadcasted_iota(I32, (qb, t), 1)
    isc = jnp.where(col <= row, isc, -jnp.inf)
    bias = _topk_bias(isc, col, topk, max(1, (t - 1).bit_length()))
    scale = DH_A ** -0.5
    for h in range(H_A):
        lg = lax.dot_general(q_ref[0, h], k_ref[0, h], NT, preferred_element_type=F32) * scale + bias
        mx = jnp.max(lg, axis=-1, keepdims=True)
        p = jnp.exp(lg - mx)
        l = jnp.sum(p, axis=-1, keepdims=True)
        o = jnp.dot(p.astype(BF16), v_ref[0, h], preferred_element_type=F32)
        o_ref[0, h] = o / l


def dsa_prompt_call(q, k, v, qi, ki, wi):
    b, _, t, _ = q.shape
    qb = _row_tile(t, 256)
    topk = min(TOPK_MAX, t // 4)
    return pl.pallas_call(
        functools.partial(_dsa_prompt_body, topk=topk),
        grid=(b, t // qb),
        in_specs=[
            pl.BlockSpec((1, H_A, qb, DH_A), lambda i, j: (i, 0, j, 0)),
            pl.BlockSpec((1, H_A, t, DH_A), lambda i, j: (i, 0, 0, 0)),
            pl.BlockSpec((1, H_A, t, DH_A), lambda i, j: (i, 0, 0, 0)),
            pl.BlockSpec((1, H_IDX, qb, D_IDX), lambda i, j: (i, 0, j, 0)),
            pl.BlockSpec((1, t, D_IDX), lambda i, j: (i, 0, 0)),
            pl.BlockSpec((1, qb, H_IDX), lambda i, j: (i, j, 0)),
        ],
        out_specs=pl.BlockSpec((1, H_A, qb, DH_A), lambda i, j: (i, 0, j, 0)),
        out_shape=jax.ShapeDtypeStruct((b, H_A, t, DH_A), F32),
        compiler_params=_cparams(("parallel", "arbitrary")),
        name="dsa_prompt",
    )(q, k, v, qi, ki, wi)


def _dsa_sidx_body(pt_ref, qi_ref, wi_ref, kin_ref, kc_ref, o_ref, *, n_pages):
    p = pl.program_id(1)
    ki = jnp.where(p == n_pages, kin_ref[0], kc_ref[0]).astype(BF16)
    s = lax.dot_general(qi_ref[0], ki, NT, preferred_element_type=F32)
    nq = o_ref.shape[1]
    r = lax.broadcasted_iota(I32, (nq, nq * H_IDX), 0)
    c = lax.broadcasted_iota(I32, (nq, nq * H_IDX), 1)
    lo = r * H_IDX
    wsel = jnp.where(c >= lo, jnp.where(c < lo + H_IDX, wi_ref[0], 0.0), 0.0)
    o_ref[0] = jnp.dot(wsel, jnp.maximum(s, 0.0), precision=HI, preferred_element_type=F32)


def dsa_sidx_call(page_table, qi, wi, ki_new, cache_kidx):
    db, n_pages = page_table.shape
    nq = qi.shape[1] // H_IDX
    grid_spec = pltpu.PrefetchScalarGridSpec(
        num_scalar_prefetch=1,
        grid=(db, n_pages + 1),
        in_specs=[
            pl.BlockSpec((1, nq * H_IDX, D_IDX), lambda b, p, pt: (b, 0, 0)),
            pl.BlockSpec((1, 1, nq * H_IDX), lambda b, p, pt: (b, 0, 0)),
            pl.BlockSpec((1, PAGE_SIZE, D_IDX), lambda b, p, pt: (b, 0, 0)),
            pl.BlockSpec((1, PAGE_SIZE, D_IDX), lambda b, p, pt: (pt[b, jnp.minimum(p, n_pages - 1)], 0, 0)),
        ],
        out_specs=pl.BlockSpec((1, nq, PAGE_SIZE), lambda b, p, pt: (b, 0, p)),
    )
    return pl.pallas_call(
        functools.partial(_dsa_sidx_body, n_pages=n_pages),
        grid_spec=grid_spec,
        out_shape=jax.ShapeDtypeStruct((db, nq, (n_pages + 1) * PAGE_SIZE), F32),
        compiler_params=_cparams(("parallel", "arbitrary")),
        name="dsa_sample_idx",
    )(page_table, qi, wi, ki_new, cache_kidx)


def _dsa_satt_body(pt_ref, isc_ref, q_ref, kn_ref, vn_ref, kc_ref, vc_ref, o_ref,
                   bias_scr, qe_scr, m_scr, l_scr, acc_scr, *, n_pages, topk, n_new):
    p = pl.program_id(1)
    nq = q_ref.shape[1]
    past = n_pages * PAGE_SIZE
    ltot = past + PAGE_SIZE
    hd = H_A * DH_A

    @pl.when(p == 0)
    def _():
        isc = isc_ref[0]
        row = lax.broadcasted_iota(I32, (nq, ltot), 0)
        col = lax.broadcasted_iota(I32, (nq, ltot), 1)
        rel = col - past
        ninf = -jnp.inf
        isc = jnp.where(rel < 0, isc, jnp.where(rel < n_new, jnp.where(rel <= row, isc, ninf), ninf))
        bias = _topk_bias(isc, col, topk, max(1, (ltot - 1).bit_length()))
        for i in range(n_pages + 1):
            bias_scr[i] = bias[:, i * PAGE_SIZE:(i + 1) * PAGE_SIZE]
        q = q_ref[0]
        lane = lax.broadcasted_iota(I32, (nq, hd), 1)
        qe_scr[...] = jnp.concatenate(
            [jnp.where(lane >= h * DH_A, jnp.where(lane < (h + 1) * DH_A, q, 0.0), 0.0) for h in range(H_A)],
            axis=0).astype(BF16)
        m_scr[...] = jnp.full_like(m_scr, NEG_BIG)
        l_scr[...] = jnp.zeros_like(l_scr)
        acc_scr[...] = jnp.zeros_like(acc_scr)

    is_new = p == n_pages
    kpage = jnp.where(is_new, kn_ref[0], kc_ref[0]).astype(BF16)
    vpage = jnp.where(is_new, vn_ref[0], vc_ref[0]).astype(BF16)
    b8 = bias_scr[p]
    lg = lax.dot_general(qe_scr[...], kpage, NT, preferred_element_type=F32) * (DH_A ** -0.5)
    lg = lg + jnp.concatenate([b8] * H_A, axis=0)
    m_old = m_scr[...]
    m_new = jnp.maximum(m_old, jnp.max(lg, axis=-1, keepdims=True))
    alpha = jnp.exp(m_old - m_new)
    pr = jnp.exp(lg - m_new)
    l_scr[...] = alpha * l_scr[...] + jnp.sum(pr, axis=-1, keepdims=True)
    acc_scr[...] = alpha * acc_scr[...] + jnp.dot(pr.astype(BF16), vpage, preferred_element_type=F32)
    m_scr[...] = m_new

    @pl.when(is_new)
    def _():
        o = acc_scr[...] / l_scr[...]
        lane = lax.broadcasted_iota(I32, (nq, hd), 1)
        out = jnp.zeros((nq, hd), F32)
        for h in range(H_A):
            oh = o[h * nq:(h + 1) * nq]
            out = out + jnp.where(lane >= h * DH_A, jnp.where(lane < (h + 1) * DH_A, oh, 0.0), 0.0)
        o_ref[0] = out


def dsa_satt_call(page_table, isc, q, k_new, v_new, cache_k, cache_v, n_new):
    db, n_pages = page_table.shape
    nq = q.shape[1]
    hd = H_A * DH_A
    ltot = (n_pages + 1) * PAGE_SIZE
    topk = min(TOPK_MAX, (n_pages * PAGE_SIZE + n_new) // 4)
    cache_map = lambda b, p, pt: (pt[b, jnp.minimum(p, n_pages - 1)], 0, 0)
    grid_spec = pltpu.PrefetchScalarGridSpec(
        num_scalar_prefetch=1,
        grid=(db, n_pages + 1),
        in_specs=[
            pl.BlockSpec((1, nq, ltot), lambda b, p, pt: (b, 0, 0)),
            pl.BlockSpec((1, nq, hd), lambda b, p, pt: (b, 0, 0)),
            pl.BlockSpec((1, PAGE_SIZE, hd), lambda b, p, pt: (b, 0, 0)),
            pl.BlockSpec((1, PAGE_SIZE, hd), lambda b, p, pt: (b, 0, 0)),
            pl.BlockSpec((1, PAGE_SIZE, hd), cache_map),
            pl.BlockSpec((1, PAGE_SIZE, hd), cache_map),
        ],
        out_specs=pl.BlockSpec((1, nq, hd), lambda b, p, pt: (b, 0, 0)),
        scratch_shapes=[
            pltpu.VMEM((n_pages + 1, nq, PAGE_SIZE), F32),
            pltpu.VMEM((H_A * nq, hd), BF16),
            pltpu.VMEM((H_A * nq, 1), F32),
            pltpu.VMEM((H_A * nq, 1), F32),
            pltpu.VMEM((H_A * nq, hd), F32),
        ],
    )
    return pl.pallas_call(
        functools.partial(_dsa_satt_body, n_pages=n_pages, topk=topk, n_new=n_new),
        grid_spec=grid_spec,
        out_shape=jax.ShapeDtypeStruct((db, nq, hd), F32),
        compiler_params=_cparams(("parallel", "arbitrary")),
        name="dsa_sample_att",
    )(page_table, isc, q, k_new, v_new, cache_k, cache_v)


def _head_sum_matrix(n, group):
    r = lax.broadcasted_iota(I32, (n, n), 0) // group
    c = lax.broadcasted_iota(I32, (n, n), 1) // group
    return jnp.where(r == c, 1.0, 0.0).astype(F32)


def _rwkv_prep_body(zb_ref, sp_ref, mu_ref, w0_ref, a0_ref, w2_ref, a2_ref, g2_ref, kkp_ref, ka_ref,
                    r_ref, w_ref, k_ref, v_ref, an_ref, b_ref, g_ref, carry_scr):
    tc = pl.program_id(1)
    zb = zb_ref[0]
    tt = zb.shape[0]
    hd = H_B * DH_B

    @pl.when(tc == 0)
    def _():
        carry_scr[...] = sp_ref[0]

    rows = lax.broadcasted_iota(I32, zb.shape, 0)
    prev = jnp.where(rows == 0, carry_scr[...], pltpu.roll(zb, 1, 0))
    carry_scr[...] = zb[tt - 1:tt]
    z = zb + (prev - zb) * mu_ref[...]
    r = z[:, 0:hd]
    k = z[:, hd:2 * hd]
    v = z[:, 2 * hd:3 * hd]
    xwa = z[:, 3 * hd:3 * hd + R_DECAY + R_AAA]
    xg = z[:, 3 * hd + R_DECAY + R_AAA:]
    wl = w0_ref[...] + jnp.dot(jnp.tanh(xwa).astype(BF16), w2_ref[...], preferred_element_type=F32)
    w_log = -_softplus(-wl) - 0.5
    decay = jnp.exp(-jnp.exp(w_log))
    a = jax.nn.sigmoid(a0_ref[...] + jnp.dot(xwa.astype(BF16), a2_ref[...], preferred_element_type=F32))
    g = jnp.dot(jax.nn.sigmoid(xg).astype(BF16), g2_ref[...], preferred_element_type=F32)
    kk = k * kkp_ref[...]
    ss = jnp.dot(kk * kk, _head_sum_matrix(hd, DH_B), precision=HI, preferred_element_type=F32)
    kk = kk * lax.rsqrt(jnp.maximum(ss, 1e-24))
    r_ref[0] = r
    w_ref[0] = decay
    k_ref[0] = k * (1.0 + (a - 1.0) * ka_ref[...])
    v_ref[0] = v
    an_ref[0] = -kk
    b_ref[0] = kk * a
    g_ref[0] = g


def rwkv_prep_call(zb, shift_prev, mu, w0, a0, w2p, a2p, g2, kkp, ka):
    b, t, dz = zb.shape
    hd = H_B * DH_B
    tt = _row_tile(t, 256)
    row = lambda n: pl.BlockSpec((1, n), lambda i, j: (0, 0))
    full = lambda s: pl.BlockSpec(s, lambda i, j: (0, 0))
    out = pl.BlockSpec((1, tt, hd), lambda i, j: (i, j, 0))
    return pl.pallas_call(
        _rwkv_prep_body,
        grid=(b, t // tt),
        in_specs=[
            pl.BlockSpec((1, tt, dz), lambda i, j: (i, j, 0)),
            pl.BlockSpec((1, 1, dz), lambda i, j: (i, 0, 0)),
            row(dz), row(hd), row(hd),
            full((R_DECAY + R_AAA, hd)), full((R_DECAY + R_AAA, hd)), full((R_GATE, hd)),
            row(hd), row(hd),
        ],
        out_specs=[out] * 7,
        out_shape=[jax.ShapeDtypeStruct((b, t, hd), F32)] * 7,
        scratch_shapes=[pltpu.VMEM((1, dz), F32)],
        compiler_params=_cparams(("parallel", "arbitrary")),
        name="rwkv_prep",
    )(zb, shift_prev, mu, w0, a0, w2p, a2p, g2, kkp, ka)


RWKV_PAIRS = H_B // 2
RWKV_STEPS = SUBLANES // RWKV_PAIRS
RWKV_NB = 8


def _rwkv_scan_body(r_ref, w_ref, k_ref, v_ref, a_ref, b_ref, s0_ref, y_ref, sout_ref, s_scr, *, t_valid):
    c = pl.program_id(1)
    nb, tc = r_ref.shape[0], r_ref.shape[1]

    @pl.when(c == 0)
    def _():
        s_scr[...] = s0_ref[...]

    if t_valid < tc * RWKV_STEPS:
        y_ref[...] = jnp.zeros_like(y_ref)

    sub = lax.broadcasted_iota(I32, (DH_B, LANES), 0)
    lane = lax.broadcasted_iota(I32, (DH_B, LANES), 1)
    isel = jnp.where((lane & (DH_B - 1)) == sub, 1.0, 0.0).astype(F32)
    qblk = _head_sum_matrix(LANES, DH_B)

    def hsum(x):
        return jnp.dot(x, qblk, precision=HI, preferred_element_type=F32)

    def step(u, carry):
        for bi in range(nb):
            at, wt, kt, vt, bt, rt = (ref[bi, u] for ref in (a_ref, w_ref, k_ref, v_ref, b_ref, r_ref))
            for p in range(RWKV_PAIRS):
                s = s_scr[bi, p]
                for i in range(RWKV_STEPS):
                    row = slice(i * RWKV_PAIRS + p, i * RWKV_PAIRS + p + 1)
                    sa = hsum(s * at[row])
                    vc = hsum(isel * vt[row])
                    s = s * wt[row] + sa * bt[row] + vc * kt[row]
                    yb = hsum(s * rt[row])
                    y_ref[bi, u, row, :] = jnp.sum(yb * isel, axis=0, keepdims=True)
                s_scr[bi, p] = s
        return carry

    lax.fori_loop(0, min(t_valid, tc * RWKV_STEPS) // RWKV_STEPS, step, 0)

    @pl.when(c == pl.num_programs(1) - 1)
    def _():
        sout_ref[...] = s_scr[...]


def rwkv_scan_call(r, w, k, v, a, b, s0, t_valid):
    bsz, t, hd = r.shape
    nb = math.gcd(bsz, RWKV_NB)
    nt = t // RWKV_STEPS
    tc = _row_tile(nt, 32)
    assert t % RWKV_STEPS == 0 and t_valid % RWKV_STEPS == 0 and (t_valid == t or nt == tc)
    tiled = lambda x: x.reshape(bsz, nt, SUBLANES, LANES)
    seq = pl.BlockSpec((nb, tc, SUBLANES, LANES), lambda i, j: (i, j, 0, 0))
    st = pl.BlockSpec((nb, RWKV_PAIRS, DH_B, LANES), lambda i, j: (i, 0, 0, 0))
    y, s_fin = pl.pallas_call(
        functools.partial(_rwkv_scan_body, t_valid=t_valid),
        grid=(bsz // nb, nt // tc),
        in_specs=[seq] * 6 + [st],
        out_specs=[seq, st],
        out_shape=[jax.ShapeDtypeStruct((bsz, nt, SUBLANES, LANES), F32),
                   jax.ShapeDtypeStruct((bsz, RWKV_PAIRS, DH_B, LANES), F32)],
        scratch_shapes=[pltpu.VMEM((nb, RWKV_PAIRS, DH_B, LANES), F32)],
        compiler_params=_cparams(("parallel", "arbitrary")),
        name="rwkv_scan",
    )(tiled(r), tiled(w), tiled(k), tiled(v), tiled(a), tiled(b), s0)
    return y.reshape(bsz, t, hd), s_fin


def _rwkv_post_body(y_ref, r_ref, k_ref, v_ref, g_ref, lg_ref, lb_ref, rk_ref, o_ref):
    hd = H_B * DH_B
    avg = _head_sum_matrix(hd, DH_B) * (1.0 / DH_B)
    y = y_ref[...]
    mu = jnp.dot(y, avg, precision=HI, preferred_element_type=F32)
    d = y - mu
    var = jnp.dot(d * d, avg, precision=HI, preferred_element_type=F32)
    yn = d * lax.rsqrt(var + LNX_EPS) * lg_ref[...] + lb_ref[...]
    bonus = jnp.dot(r_ref[...] * k_ref[...] * rk_ref[...], avg * float(DH_B), precision=HI,
                    preferred_element_type=F32)
    o_ref[...] = (yn + bonus * v_ref[...]) * g_ref[...]


def rwkv_post_call(y, r, k, v, g, lnx_g, lnx_b, rk):
    m, hd = y.shape
    tm = _row_tile(m, 512)
    tok = pl.BlockSpec((tm, hd), lambda i: (i, 0))
    row = pl.BlockSpec((1, hd), lambda i: (0, 0))
    return pl.pallas_call(
        _rwkv_post_body,
        grid=(m // tm,),
        in_specs=[tok] * 5 + [row] * 3,
        out_specs=tok,
        out_shape=jax.ShapeDtypeStruct((m, hd), F32),
        compiler_params=_cparams(("parallel",)),
        name="rwkv_post",
    )(y, r, k, v, g, lnx_g, lnx_b, rk)


def _head_mask(n, h, width):
    lane = lax.broadcasted_iota(I32, (1, n), 1)
    return jnp.where(lane >= h * width, jnp.where(lane < (h + 1) * width, 1.0, 0.0), 0.0).astype(F32)


def _head_norm128(y, g):
    mu = jnp.mean(y, axis=-1, keepdims=True)
    d = y - mu
    var = jnp.mean(d * d, axis=-1, keepdims=True)
    return d * lax.rsqrt(var + HN_EPS) * g


def _ret_body(qk_ref, v_ref, gc_ref, cos_ref, sin_ref, gn_ref, s0_ref, y_ref, sout_ref, s_scr, *, l_valid):
    c = pl.program_id(1)
    lc = qk_ref.shape[1]
    hk = H_C * DK_C

    @pl.when(c == 0)
    def _():
        s_scr[...] = s0_ref[0]

    rr = lax.broadcasted_iota(I32, (hk, hk), 0)
    cc = lax.broadcasted_iota(I32, (hk, hk), 1)
    half = DK_C // 2
    same = (rr // DK_C) == (cc // DK_C)
    dr = rr & (DK_C - 1)
    dc = cc & (DK_C - 1)
    rot = jnp.where(same, jnp.where(dr == dc + half, -1.0, jnp.where(dr + half == dc, 1.0, 0.0)), 0.0).astype(F32)

    qk = qk_ref[0]
    cos = cos_ref[...]
    sin = sin_ref[...]

    def rope(x):
        return x * cos + jnp.dot(x, rot, precision=HI, preferred_element_type=F32) * sin

    qr = rope(qk[:, :hk])
    kr = rope(qk[:, hk:]) * DK_C ** -0.5
    krb = kr.astype(BF16)
    s_prev = s_scr[...]
    s_prev_b = s_prev.astype(BF16)

    jj = lax.broadcasted_iota(I32, (lc, lc), 0)
    ss = lax.broadcasted_iota(I32, (lc, lc), 1)
    diff = (jj - ss).astype(F32)
    jcol = lax.broadcasted_iota(I32, (lc, 1), 0).astype(F32)
    srow_state = lax.broadcasted_iota(I32, (hk, 1), 0) // DK_C
    s_new = jnp.zeros_like(s_prev)
    decay_rows = jnp.zeros((hk, 1), F32)
    for h in range(H_C):
        lg = math.log1p(-2.0 ** (-5.0 - h))
        mh = _head_mask(hk, h, DK_C)
        qm = (qr * mh).astype(BF16)
        dmask = jnp.where(diff >= 0, jnp.exp(jnp.maximum(diff, 0.0) * lg), 0.0)
        scores = lax.dot_general(qm, krb, NT, preferred_element_type=F32) * dmask
        vh = v_ref[0, :, h * DV_C:(h + 1) * DV_C]
        vhb = vh.astype(BF16)
        intra = jnp.dot(scores.astype(BF16), vhb, preferred_element_type=F32)
        inter = jnp.dot(qm, s_prev_b, preferred_element_type=F32) * jnp.exp((jcol + 1.0) * lg)
        yh = _head_norm128(intra + inter, gn_ref[h:h + 1, :])
        gch = gc_ref[0, :, h * DV_C:(h + 1) * DV_C]
        y_ref[0, :, h * DV_C:(h + 1) * DV_C] = yh * (gch * jax.nn.sigmoid(gch))
        w_s = jnp.where(jcol < l_valid, jnp.exp((l_valid - 1.0 - jcol) * lg), 0.0)
        kw = (kr * mh * w_s).astype(BF16)
        s_new = s_new + lax.dot_general(kw, vhb, TN, preferred_element_type=F32)
        decay_rows = jnp.where(srow_state == h, math.exp(l_valid * lg), decay_rows)
    s_fin = decay_rows * s_prev + s_new
    s_scr[...] = s_fin

    @pl.when(c == pl.num_programs(1) - 1)
    def _():
        sout_ref[0] = s_fin


def retention_call(za, cos, sin, gn, s0, l_valid):
    b, t, _ = za.shape
    lc = _row_tile(t, 128)
    assert l_valid == lc or t == lc
    hk, hv = H_C * DK_C, H_C * DV_C
    blk = lambda j: pl.BlockSpec((1, lc, 2 * hk), lambda i, c, j=j: (i, c, j))
    st = pl.BlockSpec((1, hk, DV_C), lambda i, c: (i, 0, 0))
    return pl.pallas_call(
        functools.partial(_ret_body, l_valid=l_valid),
        grid=(b, t // lc),
        in_specs=[blk(0), blk(1), blk(2),
                  pl.BlockSpec((lc, hk), lambda i, c: (c, 0)),
                  pl.BlockSpec((lc, hk), lambda i, c: (c, 0)),
                  pl.BlockSpec((H_C, DV_C), lambda i, c: (0, 0)),
                  st],
        out_specs=[pl.BlockSpec((1, lc, hv), lambda i, c: (i, c, 0)), st],
        out_shape=[jax.ShapeDtypeStruct((b, t, hv), F32), jax.ShapeDtypeStruct((b, hk, DV_C), F32)],
        scratch_shapes=[pltpu.VMEM((hk, DV_C), F32)],
        compiler_params=_cparams(("parallel", "arbitrary")),
        name="retention",
    )(za, za, za, cos, sin, gn, s0)


def _mlstm_body(qk_ref, v_ref, og_ref, gt_ref, cb_ref, cw_ref, cbias_ref, gbias_ref, gn_ref,
                c0_ref, n0_ref, m0_ref, y_ref, cout_ref, nout_ref, mout_ref,
                ext_scr, c_scr, n_scr, m_scr, *, l_valid):
    c = pl.program_id(1)
    lc = qk_ref.shape[1]
    hk = H_D * DK_D
    pad = SUBLANES

    @pl.when(c == 0)
    def _():
        ext_scr[0:pad, :] = cb_ref[0]
        c_scr[...] = c0_ref[0]
        n_scr[...] = n0_ref[0]
        m_scr[...] = m0_ref[0]

    u = qk_ref[0]
    ext_scr[pad:pad + lc, :] = u
    acc = cbias_ref[...] + u * cw_ref[CONV_W - 1:CONV_W, :]
    for i in range(CONV_W - 1):
        sh = CONV_W - 1 - i
        acc = acc + ext_scr[pad - sh:pad - sh + lc, :] * cw_ref[i:i + 1, :]
    ext_scr[0:pad, :] = u[lc - pad:lc, :]
    qkc = acc * jax.nn.sigmoid(acc)
    q = qkc[:, :hk]
    k = qkc[:, hk:] * DK_D ** -0.5
    kb = k.astype(BF16)

    gates = gt_ref[0] + gbias_ref[...]
    logsig = -_softplus(-gates)
    rowi = lax.broadcasted_iota(I32, (lc, LANES), 0)
    logsig = jnp.where(rowi < l_valid, logsig, 0.0)
    jj = lax.broadcasted_iota(I32, (lc, lc), 0)
    ss = lax.broadcasted_iota(I32, (lc, lc), 1)
    tril = jnp.where(jj >= ss, 1.0, 0.0).astype(F32)
    bcum = jnp.dot(tril, logsig, precision=HI, preferred_element_type=F32)
    lane_g = lax.broadcasted_iota(I32, (lc, LANES), 1)
    jcol = lax.broadcasted_iota(I32, (lc, 1), 0)
    causal = jj >= ss
    ninf = -jnp.inf

    c_prev = c_scr[...]
    c_prev_b = c_prev.astype(BF16)
    n_prev = n_scr[...]
    m_prev = m_scr[...]
    c_new = jnp.zeros_like(c_prev)
    carry_rows = jnp.zeros((hk, 1), F32)
    carry_lanes = jnp.zeros((1, hk), F32)
    ws_full = jnp.zeros((lc, hk), F32)
    m_out = m_prev
    srow_state = lax.broadcasted_iota(I32, (hk, 1), 0) // DK_D
    lane_state = lax.broadcasted_iota(I32, (1, hk), 1) // DK_D
    lane_m = lax.broadcasted_iota(I32, (1, LANES), 1)
    for h in range(H_D):
        mh = _head_mask(hk, h, DK_D)
        e_i = jnp.where(lane_g == h, 1.0, 0.0).astype(F32)
        e_f = jnp.where(lane_g == H_D + h, 1.0, 0.0).astype(F32)
        logi_col = jnp.where(jcol < l_valid, gates[:, h:h + 1], ninf)
        b_col = bcum[:, H_D + h:H_D + h + 1]
        i_row = lax.dot_general(e_i, gates, NT, precision=HI, preferred_element_type=F32)
        i_row = jnp.where(ss < l_valid, i_row, ninf)
        b_row = lax.dot_general(e_f, bcum, NT, precision=HI, preferred_element_type=F32)
        m_h = m_prev[:, h:h + 1]
        inter = b_col + m_h
        dmat = jnp.where(causal, b_col - b_row + i_row, ninf)
        m_j = jnp.maximum(inter, jnp.max(dmat, axis=-1, keepdims=True))
        qm = q * mh
        qmb = qm.astype(BF16)
        amat = jnp.exp(dmat - m_j) * lax.dot_general(qmb, kb, NT, preferred_element_type=F32)
        sc = jnp.exp(inter - m_j)
        vh = v_ref[0, :, h * DV_D:(h + 1) * DV_D]
        vhb = vh.astype(BF16)
        num = jnp.dot(amat.astype(BF16), vhb, preferred_element_type=F32) \
            + sc * jnp.dot(qmb, c_prev_b, preferred_element_type=F32)
        den = jnp.sum(amat, axis=-1, keepdims=True) + sc * jnp.sum(qm * n_prev, axis=-1, keepdims=True)
        hh = num / jnp.maximum(jnp.abs(den), jnp.exp(-m_j))
        ogh = og_ref[0, :, h * DV_D:(h + 1) * DV_D]
        y_ref[0, :, h * DV_D:(h + 1) * DV_D] = _head_norm128(hh, gn_ref[h:h + 1, :]) * jax.nn.sigmoid(ogh)
        b_last = b_col[l_valid - 1:l_valid, :]
        gs = b_last - b_col + logi_col
        m_new = jnp.maximum(b_last + m_h, jnp.max(gs, axis=0, keepdims=True))
        ws = jnp.exp(gs - m_new)
        carry = jnp.exp(b_last + m_h - m_new)
        c_new = c_new + lax.dot_general((k * mh).astype(BF16), (vh * ws).astype(BF16), TN,
                                        preferred_element_type=F32)
        carry_rows = jnp.where(srow_state == h, carry, carry_rows)
        carry_lanes = jnp.where(lane_state == h, carry, carry_lanes)
        ws_full = ws_full + ws * mh
        m_out = jnp.where(lane_m == h, m_new, m_out)
    c_fin = carry_rows * c_prev + c_new
    n_fin = carry_lanes * n_prev + jnp.sum(ws_full * k, axis=0, keepdims=True)
    c_scr[...] = c_fin
    n_scr[...] = n_fin
    m_scr[...] = m_out

    @pl.when(c == pl.num_programs(1) - 1)
    def _():
        cout_ref[0] = c_fin
        nout_ref[0] = n_fin
        mout_ref[0] = m_out


def mlstm_call(zb, zc, conv_buf, conv_w, conv_b, gate_bias, gn, c0, n0, m0, l_valid):
    b, t, _ = zb.shape
    lc = _row_tile(t, 128)
    assert l_valid == lc or t == lc
    hk, hv = H_D * DK_D, H_D * DV_D
    blk = lambda j: pl.BlockSpec((1, lc, 2 * hk), lambda i, c, j=j: (i, c, j))
    cst = lambda s: pl.BlockSpec(s, lambda i, c: (0,) * len(s))
    per_b = lambda s: pl.BlockSpec((1,) + s, lambda i, c: (i,) + (0,) * len(s))
    return pl.pallas_call(
        functools.partial(_mlstm_body, l_valid=l_valid),
        grid=(b, t // lc),
        in_specs=[blk(0), blk(1), blk(2),
                  pl.BlockSpec((1, lc, LANES), lambda i, c: (i, c, 0)),
                  per_b((SUBLANES, 2 * hk)),
                  cst((CONV_W, 2 * hk)), cst((1, 2 * hk)), cst((1, LANES)), cst((H_D, DV_D)),
                  per_b((hk, DV_D)), per_b((1, hk)), per_b((1, LANES))],
        out_specs=[pl.BlockSpec((1, lc, hv), lambda i, c: (i, c, 0)),
                   per_b((hk, DV_D)), per_b((1, hk)), per_b((1, LANES))],
        out_shape=[jax.ShapeDtypeStruct((b, t, hv), F32),
                   jax.ShapeDtypeStruct((b, hk, DV_D), F32),
                   jax.ShapeDtypeStruct((b, 1, hk), F32),
                   jax.ShapeDtypeStruct((b, 1, LANES), F32)],
        scratch_shapes=[pltpu.VMEM((SUBLANES + lc, 2 * hk), F32),
                        pltpu.VMEM((hk, DV_D), F32),
                        pltpu.VMEM((1, hk), F32),
                        pltpu.VMEM((1, LANES), F32)],
        compiler_params=_cparams(("parallel", "arbitrary")),
        name="mlstm",
    )(zb, zb, zb, zc, conv_buf, conv_w, conv_b, gate_bias, gn, c0, n0, m0)


def _pad_cols(w, n):
    return jnp.pad(w, ((0, 0), (0, n - w.shape[1])))


def _pad_time(a, tp):
    t = a.shape[1]
    if t == tp:
        return a
    return jnp.pad(a, ((0, 0), (0, tp - t)) + ((0, 0),) * (a.ndim - 2))


def _heads_major(a, h):
    b, t, _ = a.shape
    return a.reshape(b, t, h, -1).transpose(0, 2, 1, 3)


def _ab_mixer(xf, b, t, e, g_pre, P, prompt, shift_prev, s0, cache):
    m = b * t
    hd = H_A * DH_A
    w = P['ab_w_in'][e]
    o_qi = 3 * hd
    o_ki = o_qi + H_IDX * D_IDX
    o_wi = o_ki + D_IDX
    o_zb = o_wi + H_IDX
    qkv = proj_in_call(xf, g_pre, w[:, :o_qi].astype(BF16))
    qi = proj_in_call(xf, g_pre, w[:, o_qi:o_ki].astype(BF16), out_dtype=BF16)
    kg = jnp.pad(P['kidx_g'][e], (0, LANES - D_IDX)).reshape(1, LANES)
    kw = kiwi_call(xf, g_pre, _pad_cols(w[:, o_ki:o_zb], LANES).astype(BF16), kg)
    zb = proj_in_call(xf, g_pre, w[:, o_zb:].astype(BF16))
    q3 = qkv[:, :hd].reshape(b, t, hd)
    k3 = qkv[:, hd:2 * hd].reshape(b, t, hd)
    v3 = qkv[:, 2 * hd:].reshape(b, t, hd)
    ki3 = kw[:, :D_IDX].reshape(b, t, D_IDX)
    wi3 = kw[:, D_IDX:D_IDX + H_IDX].reshape(b, t, H_IDX)

    if prompt:
        ya = dsa_prompt_call(_heads_major(q3.astype(BF16), H_A), _heads_major(k3.astype(BF16), H_A),
                             _heads_major(v3.astype(BF16), H_A),
                             _heads_major(qi.reshape(b, t, H_IDX * D_IDX), H_IDX),
                             ki3.astype(BF16), wi3)
        ya = ya.transpose(0, 2, 1, 3).reshape(m, hd)
    else:
        cache_k, cache_v, cache_kidx, page_table = cache
        n_pool = cache_k.shape[1]
        nq = SUBLANES
        qi_s = _pad_time(qi.reshape(b, t, H_IDX * D_IDX), nq).reshape(b, nq * H_IDX, D_IDX)
        wi_s = _pad_time(wi3, nq).reshape(b, 1, nq * H_IDX)
        isc = dsa_sidx_call(page_table, qi_s, wi_s, _pad_time(ki3, PAGE_SIZE), cache_kidx[e])
        ya = dsa_satt_call(page_table, isc, _pad_time(q3, nq), _pad_time(k3, PAGE_SIZE), _pad_time(v3, PAGE_SIZE),
                           cache_k[e].reshape(n_pool, PAGE_SIZE, hd), cache_v[e].reshape(n_pool, PAGE_SIZE, hd), t)
        ya = ya[:, :t].reshape(m, hd)

    tp = -(-t // SUBLANES) * SUBLANES
    hb = H_B * DH_B
    zb3 = zb.reshape(b, t, D_B_IN)
    zpad = jnp.zeros((R_DECAY, hb), F32)
    w2p = jnp.concatenate([P['rwkv_w2'][e], zpad], axis=0).astype(BF16)
    a2p = jnp.concatenate([zpad, P['rwkv_a2'][e]], axis=0).astype(BF16)
    row = lambda a: a.reshape(1, -1)
    r, dec, k2, v, an, bb, g = rwkv_prep_call(
        _pad_time(zb3, tp), shift_prev.reshape(b, 1, D_B_IN), row(P['rwkv_mu'][e]), row(P['rwkv_w0'][e]),
        row(P['rwkv_a0'][e]), w2p, a2p, P['rwkv_g2'][e].astype(BF16), row(P['rwkv_kk'][e]), row(P['rwkv_ka'][e]))
    s0p = s0.reshape(b, RWKV_PAIRS, 2, DH_B, DH_B).transpose(0, 1, 3, 2, 4).reshape(b, RWKV_PAIRS, DH_B, LANES)
    y, s_fin = rwkv_scan_call(r, dec, k2, v, an, bb, s0p, t)
    fl = lambda a: a.reshape(b * tp, hb)
    rk = jnp.broadcast_to(P['rwkv_rk'][e], (H_B, DH_B)).reshape(1, hb)
    yb = rwkv_post_call(fl(y), fl(r), fl(k2), fl(v), fl(g), row(P['rwkv_lnx_g'][e]), row(P['rwkv_lnx_b'][e]), rk)
    yb = yb.reshape(b, tp, hb)[:, :t].reshape(m, hb)
    s_new = s_fin.reshape(b, RWKV_PAIRS, DH_B, 2, DH_B).transpose(0, 1, 3, 2, 4).reshape(b, H_B, DH_B, DH_B)
    st = (k3.reshape(b, t, H_A, DH_A), v3.reshape(b, t, H_A, DH_A), ki3, s_new, zb3[:, t - 1])
    return ya, yb, st


def _cd_mixer(xf, b, t, o, g_pre, pos, P, ret_s, m_c, m_n, m_m, conv_buf):
    w = P['cd_w_in'][o]
    hk, hv = H_C * DK_C, H_C * DV_C
    o_g = 2 * hk + 2 * hv
    o_vd = o_g + 2 * H_D * DK_D
    o_ig = o_vd + H_D * DV_D
    o_og = o_ig + 2 * H_D
    za = proj_in_call(xf, g_pre, w[:, :o_g].astype(BF16))
    zb = proj_in_call(xf, g_pre, jnp.concatenate([w[:, o_g:o_ig], w[:, o_og:]], axis=1).astype(BF16))
    zc = proj_in_call(xf, g_pre, _pad_cols(w[:, o_ig:o_og], LANES).astype(BF16))
    tp = -(-t // SUBLANES) * SUBLANES
    l_valid = t if tp != t else _row_tile(t, 128)
    za3 = _pad_time(za.reshape(b, t, -1), tp)
    zb3 = zb.reshape(b, t, -1)
    zc3 = _pad_time(zc.reshape(b, t, -1), tp)

    half = DK_C // 2
    inv = ROPE_BASE ** (-jnp.arange(half, dtype=F32) / half)
    ang = _pad_time(pos.astype(F32)[None], tp)[0][:, None] * inv[None, :]
    cos = jnp.tile(jnp.cos(ang), (1, 2 * H_C))
    sin = jnp.tile(jnp.sin(ang), (1, 2 * H_C))
    yc, ret_new = retention_call(za3, cos, sin, P['ret_gn'][o], ret_s.astype(F32).reshape(b, hk, DV_C), l_valid)

    hkd = H_D * DK_D
    cb = jnp.pad(conv_buf.astype(F32), ((0, 0), (SUBLANES - (CONV_W - 1), 0), (0, 0)))
    gate_bias = jnp.pad(P['mlstm_if_b'][o].astype(F32).reshape(1, 2 * H_D), ((0, 0), (0, LANES - 2 * H_D)))
    c0 = m_c.astype(F32).transpose(0, 1, 3, 2).reshape(b, hkd, DV_D)
    n0 = m_n.astype(F32).reshape(b, 1, hkd)
    m0 = jnp.pad(m_m.astype(F32), ((0, 0), (0, LANES - H_D))).reshape(b, 1, LANES)
    yd, c_new, n_new, m_new = mlstm_call(_pad_time(zb3, tp), zc3, cb, P['conv_w'][o], P['conv_b'][o].reshape(1, -1),
                                         gate_bias, P['mlstm_gn'][o], c0, n0, m0, l_valid)
    conv_new = jnp.concatenate([conv_buf.astype(F32), zb3[:, :, :2 * hkd]], axis=1)[:, -(CONV_W - 1):]
    st = (ret_new.reshape(b, H_C, DK_C, DV_C),
          c_new.reshape(b, H_D, DK_D, DV_D).transpose(0, 1, 3, 2),
          n_new.reshape(b, H_D, DK_D),
          m_new.reshape(b, LANES)[:, :H_D],
          conv_new)
    m = b * t
    return yc[:, :t].reshape(m, hv), yd[:, :t].reshape(m, H_D * DV_D), st


def _trunk(x, pos, prompt, ab_init, cd_init, cache, P):
    b, t, d = x.shape
    xf = x.reshape(b * t, d)
    depth = P['norm_g'].shape[0]
    ab_new, cd_new = [], []
    for l in range(depth):
        g = P['norm_g'][l]
        bf = lambda a: a.astype(BF16)
        xf = ffn_call(xf, g[0], g[1], bf(P['ffn_wg'][l, 0]), bf(P['ffn_wu'][l, 0]), bf(P['ffn_wd'][l, 0]))
        if l % 2 == 0:
            e = l // 2
            shift_prev, s0 = ab_init(e)
            ya, yb, st = _ab_mixer(xf, b, t, e, g[2], P, prompt, shift_prev, s0, cache)
            ab_new.append(st)
            wo = bf(P['ab_w_out'][e])
            xf = proj_out_call(ya, yb, xf, g[3], wo[:H_A * DH_A], wo[H_A * DH_A:])
        else:
            o = l // 2
            yc, yd, st = _cd_mixer(xf, b, t, o, g[2], pos, P, *cd_init(o))
            cd_new.append(st)
            wo = bf(P['cd_w_out'][o])
            xf = proj_out_call(yc, yd, xf, g[3], wo[:H_C * DV_C], wo[H_C * DV_C:])
        xf = ffn_call(xf, g[4], g[5], bf(P['ffn_wg'][l, 1]), bf(P['ffn_wu'][l, 1]), bf(P['ffn_wd'][l, 1]))
    ab = tuple(jnp.stack(s) for s in zip(*ab_new))
    cd = tuple(jnp.stack(s) for s in zip(*cd_new))
    return xf.reshape(b, t, d), ab, cd


def kernel(x_prompt, x_sample, cache_k, cache_v, cache_kidx, state_rwkv, state_shift, state_ret, state_mlstm_C, state_mlstm_n, state_mlstm_m, state_conv, page_table, norm_g, ffn_wg, ffn_wu, ffn_wd, ab_w_in, ab_w_out, kidx_g, rwkv_mu, rwkv_w0, rwkv_w2, rwkv_a0, rwkv_a2, rwkv_g2, rwkv_kk, rwkv_ka, rwkv_rk, rwkv_lnx_g, rwkv_lnx_b, cd_w_in, cd_w_out, ret_gn, conv_w, conv_b, mlstm_if_b, mlstm_gn):
    P = dict(norm_g=norm_g, ffn_wg=ffn_wg, ffn_wu=ffn_wu, ffn_wd=ffn_wd, ab_w_in=ab_w_in, ab_w_out=ab_w_out,
             kidx_g=kidx_g, rwkv_mu=rwkv_mu, rwkv_w0=rwkv_w0, rwkv_w2=rwkv_w2, rwkv_a0=rwkv_a0, rwkv_a2=rwkv_a2,
             rwkv_g2=rwkv_g2, rwkv_kk=rwkv_kk, rwkv_ka=rwkv_ka, rwkv_rk=rwkv_rk, rwkv_lnx_g=rwkv_lnx_g,
             rwkv_lnx_b=rwkv_lnx_b, cd_w_in=cd_w_in, cd_w_out=cd_w_out, ret_gn=ret_gn, conv_w=conv_w,
             conv_b=conv_b, mlstm_if_b=mlstm_if_b, mlstm_gn=mlstm_gn)
    B, T, _ = x_prompt.shape
    DB, DS, _ = x_sample.shape
    past = page_table.shape[1] * PAGE_SIZE

    def ab_zero(e):
        return (jnp.zeros((B, D_B_IN), F32), jnp.zeros((B, H_B, DH_B, DH_B), F32))

    def cd_zero(o):
        return (jnp.zeros((B, H_C, DK_C, DV_C), F32), jnp.zeros((B, H_D, DV_D, DK_D), F32),
                jnp.zeros((B, H_D, DK_D), F32), jnp.zeros((B, H_D), F32),
                jnp.zeros((B, CONV_W - 1, 2 * H_D * DK_D), F32))

    def ab_cached(e):
        return (state_shift[e], state_rwkv[e])

    def cd_cached(o):
        return (state_ret[o], state_mlstm_C[o], state_mlstm_n[o], state_mlstm_m[o], state_conv[o])

    y_p, (kp, vp, kip, rwp, shp), (rtp, cp, nvp, mp, cvp) = _trunk(
        x_prompt, jnp.arange(T), True, ab_zero, cd_zero, None, P)
    y_s, (ks_, vs_, kis, rws, shs), (rts, cs, nvs, ms, cvs) = _trunk(
        x_sample, past + jnp.arange(DS), False, ab_cached, cd_cached,
        (cache_k, cache_v, cache_kidx, page_table), P)
    return (y_p, y_s, kp, vp, kip, rwp, shp, rtp, cp, nvp, mp, cvp,
            ks_, vs_, kis, rws, shs, rts, cs, nvs, ms, cvs)
```

```python
import functools
import math

import numpy as np
import jax
import jax.numpy as jnp
from jax import lax
from jax.experimental import pallas as pl
from jax.experimental.pallas import tpu as pltpu

F32 = jnp.float32
BF16 = jnp.bfloat16
I32 = jnp.int32
HI = lax.Precision.HIGHEST

LANES = 128
SUBLANES = 8
VMEM_LIMIT = 56 * 1024 * 1024

EPS = 1e-6
PAGE_SIZE = 128
H_A, DH_A, H_IDX, D_IDX, TOPK_MAX = 8, 64, 16, 64, 256
H_B, DH_B, R_DECAY, R_AAA, R_GATE = 8, 64, 64, 64, 128
D_B_IN = 3 * H_B * DH_B + R_DECAY + R_AAA + R_GATE
LNX_EPS = 64e-5
H_C, DK_C, DV_C, ROPE_BASE = 4, 64, 128, 10000.0
H_D, DK_D, DV_D, CONV_W = 4, 64, 128, 4
HN_EPS = 1e-5
NEG_BIG = -1e30
INT_MIN = -(2 ** 31)
INT_MAX = 2 ** 31 - 1
KEY_NEG_INF = (0xFF800000 ^ 0x7FFFFFFF) - 2 ** 32

NT = (((1,), (1,)), ((), ()))
TN = (((0,), (0,)), ((), ()))


def _cparams(sem):
    return pltpu.CompilerParams(dimension_semantics=sem, vmem_limit_bytes=VMEM_LIMIT)


def _rms(x, g):
    return x * lax.rsqrt(jnp.mean(x * x, axis=-1, keepdims=True) + EPS) * g


def _softplus(x):
    return jnp.maximum(x, 0.0) + jnp.log(1.0 + jnp.exp(-jnp.abs(x)))


def _row_tile(m, want):
    t = min(want, m)
    while m % t:
        t //= 2
    return t


def _col_tile(n, want):
    best = LANES
    for t in range(LANES, min(n, want) + 1, LANES):
        if n % t == 0:
            best = t
    return best


def _ffn_body(x_ref, g0_ref, g1_ref, wg_ref, wu_ref, wd_ref, o_ref, h_scr, acc_scr):
    j = pl.program_id(1)

    @pl.when(j == 0)
    def _():
        h_scr[...] = _rms(x_ref[...], g0_ref[...]).astype(BF16)
        acc_scr[...] = jnp.zeros_like(acc_scr)

    h = h_scr[...]
    g = jnp.dot(h, wg_ref[...], preferred_element_type=F32)
    u = jnp.dot(h, wu_ref[...], preferred_element_type=F32)
    a = g * jax.nn.sigmoid(g) * u
    acc_scr[...] += jnp.dot(a.astype(BF16), wd_ref[...], preferred_element_type=F32)

    @pl.when(j == pl.num_programs(1) - 1)
    def _():
        o_ref[...] = x_ref[...] + 0.5 * _rms(acc_scr[...], g1_ref[...])


def ffn_call(x, g0, g1, wg, wu, wd):
    m, d = x.shape
    ff = wg.shape[1]
    tm = _row_tile(m, 512)
    tf = _col_tile(ff, 1408)
    return pl.pallas_call(
        _ffn_body,
        grid=(m // tm, ff // tf),
        in_specs=[
            pl.BlockSpec((tm, d), lambda i, j: (i, 0)),
            pl.BlockSpec((1, d), lambda i, j: (0, 0)),
            pl.BlockSpec((1, d), lambda i, j: (0, 0)),
            pl.BlockSpec((d, tf), lambda i, j: (0, j)),
            pl.BlockSpec((d, tf), lambda i, j: (0, j)),
            pl.BlockSpec((tf, d), lambda i, j: (j, 0)),
        ],
        out_specs=pl.BlockSpec((tm, d), lambda i, j: (i, 0)),
        out_shape=jax.ShapeDtypeStruct((m, d), F32),
        scratch_shapes=[pltpu.VMEM((tm, d), BF16), pltpu.VMEM((tm, d), F32)],
        compiler_params=_cparams(("parallel", "arbitrary")),
        name="ffn",
    )(x, g0.reshape(1, d), g1.reshape(1, d), wg, wu, wd)


def _proj_in_body(x_ref, g_ref, w_ref, o_ref, h_scr):
    @pl.when(pl.program_id(1) == 0)
    def _():
        h_scr[...] = _rms(x_ref[...], g_ref[...]).astype(BF16)

    o_ref[...] = jnp.dot(h_scr[...], w_ref[...], preferred_element_type=F32).astype(o_ref.dtype)


def proj_in_call(x, g, w, out_dtype=F32):
    m, d = x.shape
    n = w.shape[1]
    tm = _row_tile(m, 512)
    tn = _col_tile(n, 1024)
    return pl.pallas_call(
        _proj_in_body,
        grid=(m // tm, n // tn),
        in_specs=[
            pl.BlockSpec((tm, d), lambda i, j: (i, 0)),
            pl.BlockSpec((1, d), lambda i, j: (0, 0)),
            pl.BlockSpec((d, tn), lambda i, j: (0, j)),
        ],
        out_specs=pl.BlockSpec((tm, tn), lambda i, j: (i, j)),
        out_shape=jax.ShapeDtypeStruct((m, n), out_dtype),
        scratch_shapes=[pltpu.VMEM((tm, d), BF16)],
        compiler_params=_cparams(("parallel", "arbitrary")),
        name="proj_in",
    )(x, g.reshape(1, d), w)


def _kiwi_body(x_ref, g_ref, w_ref, kg_ref, o_ref, k2_ref):
    h = _rms(x_ref[...], g_ref[...]).astype(BF16)
    z = jnp.dot(h, w_ref[...], preferred_element_type=F32)
    z1 = z[:, :LANES]
    lane = lax.broadcasted_iota(I32, z1.shape, 1)
    is_k = lane < D_IDX
    ms = jnp.sum(jnp.where(is_k, z1 * z1, 0.0), axis=-1, keepdims=True) * (1.0 / D_IDX)
    inv = lax.rsqrt(ms + EPS)
    kg = kg_ref[...]
    o_ref[...] = jnp.where(is_k, z1 * inv * kg[:, :LANES], z1 * (H_IDX * D_IDX) ** -0.5)
    k2_ref[...] = (z[:, LANES:] * inv * kg[:, LANES:]).astype(BF16)


def kiwi_call(x, g, w_ki, w_wi, kidx_g):
    m, d = x.shape
    tm = _row_tile(m, 512)
    zpad = jnp.zeros((d, LANES - D_IDX - H_IDX), F32)
    w = jnp.concatenate([w_ki, w_wi, zpad, w_ki, w_ki], axis=1).astype(BF16)
    gpad = jnp.zeros((LANES - D_IDX,), F32)
    kg = jnp.concatenate([kidx_g, gpad, kidx_g, kidx_g]).reshape(1, 2 * LANES)
    return pl.pallas_call(
        _kiwi_body,
        grid=(m // tm,),
        in_specs=[
            pl.BlockSpec((tm, d), lambda i: (i, 0)),
            pl.BlockSpec((1, d), lambda i: (0, 0)),
            pl.BlockSpec((d, 2 * LANES), lambda i: (0, 0)),
            pl.BlockSpec((1, 2 * LANES), lambda i: (0, 0)),
        ],
        out_specs=[pl.BlockSpec((tm, LANES), lambda i: (i, 0))] * 2,
        out_shape=[jax.ShapeDtypeStruct((m, LANES), F32), jax.ShapeDtypeStruct((m, LANES), BF16)],
        compiler_params=_cparams(("parallel",)),
        name="kiwi",
    )(x, g.reshape(1, d), w, kg)


def _proj_out_body(a1_ref, a2_ref, x_ref, g_ref, w1_ref, w2_ref, o_ref):
    y = jnp.dot(a1_ref[...].astype(BF16), w1_ref[...], preferred_element_type=F32)
    y = y + jnp.dot(a2_ref[...].astype(BF16), w2_ref[...], preferred_element_type=F32)
    o_ref[...] = x_ref[...] + _rms(y, g_ref[...])


def proj_out_call(a1, a2, x, g, w1, w2):
    m, d = x.shape
    k1, k2 = a1.shape[1], a2.shape[1]
    tm = _row_tile(m, 512)
    return pl.pallas_call(
        _proj_out_body,
        grid=(m // tm,),
        in_specs=[
            pl.BlockSpec((tm, k1), lambda i: (i, 0)),
            pl.BlockSpec((tm, k2), lambda i: (i, 0)),
            pl.BlockSpec((tm, d), lambda i: (i, 0)),
            pl.BlockSpec((1, d), lambda i: (0, 0)),
            pl.BlockSpec((k1, d), lambda i: (0, 0)),
            pl.BlockSpec((k2, d), lambda i: (0, 0)),
        ],
        out_specs=pl.BlockSpec((tm, d), lambda i: (i, 0)),
        out_shape=jax.ShapeDtypeStruct((m, d), F32),
        compiler_params=_cparams(("parallel",)),
        name="proj_out",
    )(a1, a2, x, g.reshape(1, d), w1, w2)


def _topk_bias(isc, col, topk, nbits_col):
    bits = pltpu.bitcast(isc, I32)
    key = jnp.where(bits < 0, bits ^ 0x7FFFFFFF, bits)
    kf = float(topk)

    def count(mask):
        return jnp.sum(jnp.where(mask, 1.0, 0.0), axis=-1, keepdims=True)

    prefix = jnp.where(count(key >= 0) >= kf, 0, INT_MIN).astype(I32)

    def bit_step(i, prefix):
        cand = prefix + jnp.left_shift(jnp.int32(1), 30 - i)
        return jnp.where(count(key >= cand) >= kf, cand, prefix)

    thr = lax.fori_loop(0, 31, bit_step, prefix)
    gt = key > thr
    eq = key == thr
    need = kf - count(gt)

    def col_step(i, y):
        cand = y + jnp.left_shift(jnp.int32(1), nbits_col - 1 - i)
        c = jnp.sum(jnp.where(eq, jnp.where(col < cand, 1.0, 0.0), 0.0), axis=-1, keepdims=True)
        return jnp.where(c < need, cand, y)

    tied = jnp.where(count(key >= thr) > kf, jnp.where(thr > KEY_NEG_INF, 1.0, 0.0), 0.0)
    y = lax.cond(jnp.max(tied) > 0.0,
                 lambda: lax.fori_loop(0, nbits_col, col_step, jnp.zeros_like(thr)),
                 lambda: jnp.full_like(thr, INT_MAX))
    ninf = -jnp.inf
    bias = jnp.where(gt, 0.0, jnp.where(eq, jnp.where(col <= y, 0.0, ninf), ninf))
    return jnp.where(jnp.abs(isc) < jnp.inf, bias, ninf)


def _dsa_prompt_body(qkv_q_ref, qkv_k_ref, qkv_v_ref, qi_ref, ki2_ref, kw_ref, o_ref, kb_scr, vb_scr, *, topk):
    qb = qkv_q_ref.shape[1]
    t = qkv_k_ref.shape[1]
    j = pl.program_id(1)
    t0 = j * qb

    @pl.when(j == 0)
    def _():
        kb_scr[...] = qkv_k_ref[0].astype(BF16)
        vb_scr[...] = qkv_v_ref[0].astype(BF16)

    lane = lax.broadcasted_iota(I32, (1, LANES), 1)
    half = (jnp.where(lane < DH_A, 1.0, 0.0), jnp.where(lane < DH_A, 0.0, 1.0))
    half_b = tuple(m.astype(BF16) for m in half)
    ki2 = ki2_ref[0]
    kw = kw_ref[0]
    isc = jnp.zeros((qb, t), F32)
    for hp in range(H_IDX // 2):
        qp = qi_ref[0, :, hp * LANES:(hp + 1) * LANES]
        for h2 in range(2):
            h = 2 * hp + h2
            s = lax.dot_general(qp * half_b[h2], ki2, NT, preferred_element_type=F32)
            isc = isc + kw[:, D_IDX + h:D_IDX + h + 1] * jnp.maximum(s, 0.0)
    row = lax.broadcasted_iota(I32, (qb, t), 0) + t0
    col = lax.broadcasted_iota(I32, (qb, t), 1)
    isc = jnp.where(col <= row, isc, -jnp.inf)
    bias = _topk_bias(isc, col, topk, max(1, (t - 1).bit_length()))
    scale = DH_A ** -0.5
    assert math.frexp(scale)[0] == 0.5
    for p in range(H_A // 2):
        sl = slice(p * LANES, (p + 1) * LANES)
        qp = qkv_q_ref[0, :, sl]
        kp = kb_scr[:, sl]
        vp = vb_scr[:, sl]
        outs = []
        for h2 in range(2):
            qm = (qp * (half[h2] * scale)).astype(BF16)
            lg = lax.dot_general(qm, kp, NT, preferred_element_type=F32) + bias
            mx = jnp.max(lg, axis=-1, keepdims=True)
            pr = jnp.exp(lg - mx)
            l = jnp.sum(pr, axis=-1, keepdims=True)
            outs.append(jnp.dot(pr.astype(BF16), vp, preferred_element_type=F32) / l)
        o_ref[0, :, sl] = jnp.where(lane < DH_A, outs[0], outs[1])


def dsa_prompt_call(qkv, qi, ki2, kw):
    b, t, _ = qkv.shape
    hd = H_A * DH_A
    qb = _row_tile(t, 256)
    topk = min(TOPK_MAX, t // 4)
    return pl.pallas_call(
        functools.partial(_dsa_prompt_body, topk=topk),
        grid=(b, t // qb),
        in_specs=[
            pl.BlockSpec((1, qb, hd), lambda i, j: (i, j, 0)),
            pl.BlockSpec((1, t, hd), lambda i, j: (i, 0, 1)),
            pl.BlockSpec((1, t, hd), lambda i, j: (i, 0, 2)),
            pl.BlockSpec((1, qb, H_IDX * D_IDX), lambda i, j: (i, j, 0)),
            pl.BlockSpec((1, t, LANES), lambda i, j: (i, 0, 0)),
            pl.BlockSpec((1, qb, LANES), lambda i, j: (i, j, 0)),
        ],
        out_specs=pl.BlockSpec((1, qb, hd), lambda i, j: (i, j, 0)),
        out_shape=jax.ShapeDtypeStruct((b, t, hd), F32),
        scratch_shapes=[pltpu.VMEM((t, hd), BF16), pltpu.VMEM((t, hd), BF16)],
        compiler_params=_cparams(("parallel", "arbitrary")),
        name="dsa_prompt",
    )(qkv, qkv, qkv, qi, ki2, kw)


DSA_PAGE_GROUP = 8


def _page_group(n_pages):
    return math.gcd(n_pages, DSA_PAGE_GROUP)


def _page_specs(shape, n_pages, group):
    def spec(g):
        return pl.BlockSpec(shape, lambda b, s, pt: (pt[b, jnp.minimum(s * group + g, n_pages - 1)], 0, 0))
    return [spec(g) for g in range(group)]


def _page_cat(refs, new_ref, is_new):
    pages = [r[0] for r in refs]
    pages[0] = jnp.where(is_new, new_ref[0], pages[0])
    return jnp.concatenate([p.astype(BF16) for p in pages], axis=0)


def _dsa_sidx_body(pt_ref, qi_ref, wi_ref, kin_ref, *rest, n_steps):
    kc_refs, o_ref = rest[:-1], rest[-1]
    ki = _page_cat(kc_refs, kin_ref, pl.program_id(1) == n_steps)
    s = lax.dot_general(qi_ref[0], ki, NT, preferred_element_type=F32)
    nq = o_ref.shape[1]
    r = lax.broadcasted_iota(I32, (nq, nq * H_IDX), 0)
    c = lax.broadcasted_iota(I32, (nq, nq * H_IDX), 1)
    lo = r * H_IDX
    wsel = jnp.where(c >= lo, jnp.where(c < lo + H_IDX, wi_ref[0], 0.0), 0.0)
    o_ref[0] = jnp.dot(wsel, jnp.maximum(s, 0.0), precision=HI, preferred_element_type=F32)


def dsa_sidx_call(page_table, qi, wi, ki_new, cache_kidx):
    db, n_pages = page_table.shape
    nq = qi.shape[1] // H_IDX
    group = _page_group(n_pages)
    n_steps = n_pages // group
    grid_spec = pltpu.PrefetchScalarGridSpec(
        num_scalar_prefetch=1,
        grid=(db, n_steps + 1),
        in_specs=[
            pl.BlockSpec((1, nq * H_IDX, D_IDX), lambda b, s, pt: (b, 0, 0)),
            pl.BlockSpec((1, 1, nq * H_IDX), lambda b, s, pt: (b, 0, 0)),
            pl.BlockSpec((1, PAGE_SIZE, D_IDX), lambda b, s, pt: (b, 0, 0)),
        ] + _page_specs((1, PAGE_SIZE, D_IDX), n_pages, group),
        out_specs=pl.BlockSpec((1, nq, group * PAGE_SIZE), lambda b, s, pt: (b, 0, s)),
    )
    return pl.pallas_call(
        functools.partial(_dsa_sidx_body, n_steps=n_steps),
        grid_spec=grid_spec,
        out_shape=jax.ShapeDtypeStruct((db, nq, (n_steps + 1) * group * PAGE_SIZE), F32),
        compiler_params=_cparams(("parallel", "arbitrary")),
        name="dsa_sample_idx",
    )(page_table, qi, wi, ki_new, *([cache_kidx] * group))


def _dsa_satt_body(pt_ref, isc_ref, q_ref, kn_ref, vn_ref, *rest, n_pages, n_steps, topk, n_new):
    group = n_pages // n_steps
    kc_refs, vc_refs = rest[:group], rest[group:2 * group]
    o_ref, bias_scr, qe_scr, m_scr, l_scr, acc_scr = rest[2 * group:]
    p = pl.program_id(1)
    nq = q_ref.shape[1]
    past = n_pages * PAGE_SIZE
    wcols = group * PAGE_SIZE
    ltot = (n_steps + 1) * wcols
    hd = H_A * DH_A

    @pl.when(p == 0)
    def _():
        isc = isc_ref[0]
        row = lax.broadcasted_iota(I32, (nq, ltot), 0)
        col = lax.broadcasted_iota(I32, (nq, ltot), 1)
        rel = col - past
        ninf = -jnp.inf
        isc = jnp.where(rel < 0, isc, jnp.where(rel < n_new, jnp.where(rel <= row, isc, ninf), ninf))
        bias = _topk_bias(isc, col, topk, max(1, (ltot - 1).bit_length()))
        for i in range(n_steps + 1):
            bias_scr[i] = bias[:, i * wcols:(i + 1) * wcols]
        q = q_ref[0]
        lane = lax.broadcasted_iota(I32, (nq, hd), 1)
        qe_scr[...] = jnp.concatenate(
            [jnp.where(lane >= h * DH_A, jnp.where(lane < (h + 1) * DH_A, q, 0.0), 0.0) for h in range(H_A)],
            axis=0).astype(BF16)
        m_scr[...] = jnp.full_like(m_scr, NEG_BIG)
        l_scr[...] = jnp.zeros_like(l_scr)
        acc_scr[...] = jnp.zeros_like(acc_scr)

    is_new = p == n_steps
    kpage = _page_cat(kc_refs, kn_ref, is_new)
    vpage = _page_cat(vc_refs, vn_ref, is_new)
    b8 = bias_scr[p]
    lg = lax.dot_general(qe_scr[...], kpage, NT, preferred_element_type=F32) * (DH_A ** -0.5)
    lg = lg + jnp.concatenate([b8] * H_A, axis=0)
    m_old = m_scr[...]
    m_new = jnp.maximum(m_old, jnp.max(lg, axis=-1, keepdims=True))
    alpha = jnp.exp(m_old - m_new)
    pr = jnp.exp(lg - m_new)
    l_scr[...] = alpha * l_scr[...] + jnp.sum(pr, axis=-1, keepdims=True)
    acc_scr[...] = alpha * acc_scr[...] + jnp.dot(pr.astype(BF16), vpage, preferred_element_type=F32)
    m_scr[...] = m_new

    @pl.when(is_new)
    def _():
        o = acc_scr[...] / l_scr[...]
        lane = lax.broadcasted_iota(I32, (nq, hd), 1)
        out = jnp.zeros((nq, hd), F32)
        for h in range(H_A):
            oh = o[h * nq:(h + 1) * nq]
            out = out + jnp.where(lane >= h * DH_A, jnp.where(lane < (h + 1) * DH_A, oh, 0.0), 0.0)
        o_ref[0] = out


def dsa_satt_call(page_table, isc, q, k_new, v_new, cache_k, cache_v, n_new):
    db, n_pages = page_table.shape
    nq = q.shape[1]
    hd = H_A * DH_A
    group = _page_group(n_pages)
    n_steps = n_pages // group
    wcols = group * PAGE_SIZE
    ltot = (n_steps + 1) * wcols
    assert isc.shape[2] == ltot
    topk = min(TOPK_MAX, (n_pages * PAGE_SIZE + n_new) // 4)
    grid_spec = pltpu.PrefetchScalarGridSpec(
        num_scalar_prefetch=1,
        grid=(db, n_steps + 1),
        in_specs=[
            pl.BlockSpec((1, nq, ltot), lambda b, p, pt: (b, 0, 0)),
            pl.BlockSpec((1, nq, hd), lambda b, p, pt: (b, 0, 0)),
            pl.BlockSpec((1, PAGE_SIZE, hd), lambda b, p, pt: (b, 0, 0)),
            pl.BlockSpec((1, PAGE_SIZE, hd), lambda b, p, pt: (b, 0, 0)),
        ] + _page_specs((1, PAGE_SIZE, hd), n_pages, group) * 2,
        out_specs=pl.BlockSpec((1, nq, hd), lambda b, p, pt: (b, 0, 0)),
        scratch_shapes=[
            pltpu.VMEM((n_steps + 1, nq, wcols), F32),
            pltpu.VMEM((H_A * nq, hd), BF16),
            pltpu.VMEM((H_A * nq, 1), F32),
            pltpu.VMEM((H_A * nq, 1), F32),
            pltpu.VMEM((H_A * nq, hd), F32),
        ],
    )
    return pl.pallas_call(
        functools.partial(_dsa_satt_body, n_pages=n_pages, n_steps=n_steps, topk=topk, n_new=n_new),
        grid_spec=grid_spec,
        out_shape=jax.ShapeDtypeStruct((db, nq, hd), F32),
        compiler_params=_cparams(("parallel", "arbitrary")),
        name="dsa_sample_att",
    )(page_table, isc, q, k_new, v_new, *([cache_k] * group), *([cache_v] * group))


def _head_sum_matrix(n, group):
    r = lax.broadcasted_iota(I32, (n, n), 0) // group
    c = lax.broadcasted_iota(I32, (n, n), 1) // group
    return jnp.where(r == c, 1.0, 0.0).astype(F32)


def _rwkv_prep_body(zb_ref, sp_ref, mu_ref, w0_ref, a0_ref, w2_ref, a2_ref, g2_ref, kkp_ref, ka_ref,
                    r_ref, w_ref, k_ref, v_ref, an_ref, b_ref, g_ref, carry_scr):
    tc = pl.program_id(1)
    zb = zb_ref[0]
    tt = zb.shape[0]
    hd = H_B * DH_B

    @pl.when(tc == 0)
    def _():
        carry_scr[...] = sp_ref[0]

    rows = lax.broadcasted_iota(I32, zb.shape, 0)
    prev = jnp.where(rows == 0, carry_scr[...], pltpu.roll(zb, 1, 0))
    carry_scr[...] = zb[tt - 1:tt]
    z = zb + (prev - zb) * mu_ref[...]
    r = z[:, 0:hd]
    k = z[:, hd:2 * hd]
    v = z[:, 2 * hd:3 * hd]
    xwa = z[:, 3 * hd:3 * hd + R_DECAY + R_AAA]
    xg = z[:, 3 * hd + R_DECAY + R_AAA:]
    wl = w0_ref[...] + jnp.dot(jnp.tanh(xwa).astype(BF16), w2_ref[...], preferred_element_type=F32)
    w_log = -_softplus(-wl) - 0.5
    decay = jnp.exp(-jnp.exp(w_log))
    a = jax.nn.sigmoid(a0_ref[...] + jnp.dot(xwa.astype(BF16), a2_ref[...], preferred_element_type=F32))
    g = jnp.dot(jax.nn.sigmoid(xg).astype(BF16), g2_ref[...], preferred_element_type=F32)
    kk = k * kkp_ref[...]
    ss = jnp.dot(kk * kk, _head_sum_matrix(hd, DH_B), precision=HI, preferred_element_type=F32)
    kk = kk * lax.rsqrt(jnp.maximum(ss, 1e-24))
    r_ref[0] = r
    w_ref[0] = decay
    k_ref[0] = k * (1.0 + (a - 1.0) * ka_ref[...])
    v_ref[0] = v
    an_ref[0] = -kk
    b_ref[0] = kk * a
    g_ref[0] = g


def rwkv_prep_call(zb, shift_prev, mu, w0, a0, w2p, a2p, g2, kkp, ka):
    b, t, dz = zb.shape
    hd = H_B * DH_B
    tt = _row_tile(t, 256)
    row = lambda n: pl.BlockSpec((1, n), lambda i, j: (0, 0))
    full = lambda s: pl.BlockSpec(s, lambda i, j: (0, 0))
    out = pl.BlockSpec((1, tt, hd), lambda i, j: (i, j, 0))
    return pl.pallas_call(
        _rwkv_prep_body,
        grid=(b, t // tt),
        in_specs=[
            pl.BlockSpec((1, tt, dz), lambda i, j: (i, j, 0)),
            pl.BlockSpec((1, 1, dz), lambda i, j: (i, 0, 0)),
            row(dz), row(hd), row(hd),
            full((R_DECAY + R_AAA, hd)), full((R_DECAY + R_AAA, hd)), full((R_GATE, hd)),
            row(hd), row(hd),
        ],
        out_specs=[out] * 7,
        out_shape=[jax.ShapeDtypeStruct((b, t, hd), F32)] * 7,
        scratch_shapes=[pltpu.VMEM((1, dz), F32)],
        compiler_params=_cparams(("parallel", "arbitrary")),
        name="rwkv_prep",
    )(zb, shift_prev, mu, w0, a0, w2p, a2p, g2, kkp, ka)


RWKV_PAIRS = H_B // 2
RWKV_STEPS = SUBLANES // RWKV_PAIRS
RWKV_NB = 8
RWKV_SUM_PIECES = 2


def _rwkv_scan_body(r_ref, w_ref, k_ref, v_ref, a_ref, b_ref, s0_ref, y_ref, sout_ref,
                    s_scr, pa_scr, pv_scr, *, t_valid):
    c = pl.program_id(1)
    nb, tc = r_ref.shape[0], r_ref.shape[1]
    chains = [(bi, p) for bi in range(nb) for p in range(RWKV_PAIRS)]
    rows = lambda n: slice(n * DH_B, (n + 1) * DH_B)

    @pl.when(c == 0)
    def _():
        for n, (bi, p) in enumerate(chains):
            s_scr[rows(n), :] = s0_ref[bi, p]

    if t_valid < tc * RWKV_STEPS:
        y_ref[...] = jnp.zeros_like(y_ref)

    sub = lax.broadcasted_iota(I32, (DH_B, LANES), 0)
    lane = lax.broadcasted_iota(I32, (DH_B, LANES), 1)
    isel = jnp.where((lane & (DH_B - 1)) == sub, 1.0, 0.0).astype(F32)
    qblk = _head_sum_matrix(LANES, DH_B).astype(BF16)

    def hsum(ref, pieces=3):
        x = ref[...]
        parts = []
        for _ in range(pieces - 1):
            h = x.astype(BF16)
            parts.append(h)
            x = x - h.astype(F32)
        parts.append(x.astype(BF16))
        return jnp.dot(jnp.concatenate(parts, axis=1), jnp.concatenate([qblk] * pieces, axis=0),
                       preferred_element_type=F32)

    def step(u, carry):
        tiles = [tuple(ref[bi, u] for ref in (a_ref, w_ref, k_ref, v_ref, b_ref, r_ref)) for bi in range(nb)]
        for i in range(RWKV_STEPS):
            for n, (bi, p) in enumerate(chains):
                at, _, _, vt, _, _ = tiles[bi]
                row = slice(i * RWKV_PAIRS + p, i * RWKV_PAIRS + p + 1)
                pa_scr[rows(n), :] = s_scr[rows(n), :] * at[row]
                pv_scr[rows(n), :] = isel * vt[row]
            sa = hsum(pa_scr, RWKV_SUM_PIECES)
            vc = hsum(pv_scr)
            for n, (bi, p) in enumerate(chains):
                _, wt, kt, _, bt, rt = tiles[bi]
                row = slice(i * RWKV_PAIRS + p, i * RWKV_PAIRS + p + 1)
                s = s_scr[rows(n), :] * wt[row] + sa[rows(n)] * bt[row] + vc[rows(n)] * kt[row]
                s_scr[rows(n), :] = s
                pa_scr[rows(n), :] = s * rt[row]
            yb = hsum(pa_scr, RWKV_SUM_PIECES)
            for n, (bi, p) in enumerate(chains):
                row = slice(i * RWKV_PAIRS + p, i * RWKV_PAIRS + p + 1)
                y_ref[bi, u, row, :] = jnp.sum(yb[rows(n)] * isel, axis=0, keepdims=True)
        return carry

    lax.fori_loop(0, min(t_valid, tc * RWKV_STEPS) // RWKV_STEPS, step, 0)

    @pl.when(c == pl.num_programs(1) - 1)
    def _():
        for n, (bi, p) in enumerate(chains):
            sout_ref[bi, p] = s_scr[rows(n), :]


def rwkv_scan_call(r, w, k, v, a, b, s0, t_valid):
    bsz, t, hd = r.shape
    nb = math.gcd(bsz, RWKV_NB)
    nt = t // RWKV_STEPS
    tc = _row_tile(nt, 32)
    assert t % RWKV_STEPS == 0 and t_valid % RWKV_STEPS == 0 and (t_valid == t or nt == tc)
    tiled = lambda x: x.reshape(bsz, nt, SUBLANES, LANES)
    seq = pl.BlockSpec((nb, tc, SUBLANES, LANES), lambda i, j: (i, j, 0, 0))
    st = pl.BlockSpec((nb, RWKV_PAIRS, DH_B, LANES), lambda i, j: (i, 0, 0, 0))
    y, s_fin = pl.pallas_call(
        functools.partial(_rwkv_scan_body, t_valid=t_valid),
        grid=(bsz // nb, nt // tc),
        in_specs=[seq] * 6 + [st],
        out_specs=[seq, st],
        out_shape=[jax.ShapeDtypeStruct((bsz, nt, SUBLANES, LANES), F32),
                   jax.ShapeDtypeStruct((bsz, RWKV_PAIRS, DH_B, LANES), F32)],
        scratch_shapes=[pltpu.VMEM((nb * RWKV_PAIRS * DH_B, LANES), F32)] * 3,
        compiler_params=_cparams(("parallel", "arbitrary")),
        name="rwkv_scan",
    )(tiled(r), tiled(w), tiled(k), tiled(v), tiled(a), tiled(b), s0)
    return y.reshape(bsz, t, hd), s_fin


def _rwkv_post_body(y_ref, r_ref, k_ref, v_ref, g_ref, lg_ref, lb_ref, rk_ref, o_ref):
    hd = H_B * DH_B
    avg = _head_sum_matrix(hd, DH_B) * (1.0 / DH_B)
    y = y_ref[...]
    mu = jnp.dot(y, avg, precision=HI, preferred_element_type=F32)
    d = y - mu
    var = jnp.dot(d * d, avg, precision=HI, preferred_element_type=F32)
    yn = d * lax.rsqrt(var + LNX_EPS) * lg_ref[...] + lb_ref[...]
    bonus = jnp.dot(r_ref[...] * k_ref[...] * rk_ref[...], avg * float(DH_B), precision=HI,
                    preferred_element_type=F32)
    o_ref[...] = (yn + bonus * v_ref[...]) * g_ref[...]


def rwkv_post_call(y, r, k, v, g, lnx_g, lnx_b, rk):
    m, hd = y.shape
    tm = _row_tile(m, 512)
    tok = pl.BlockSpec((tm, hd), lambda i: (i, 0))
    row = pl.BlockSpec((1, hd), lambda i: (0, 0))
    return pl.pallas_call(
        _rwkv_post_body,
        grid=(m // tm,),
        in_specs=[tok] * 5 + [row] * 3,
        out_specs=tok,
        out_shape=jax.ShapeDtypeStruct((m, hd), F32),
        compiler_params=_cparams(("parallel",)),
        name="rwkv_post",
    )(y, r, k, v, g, lnx_g, lnx_b, rk)


def _head_mask(n, h, width):
    lane = lax.broadcasted_iota(I32, (1, n), 1)
    return jnp.where(lane >= h * width, jnp.where(lane < (h + 1) * width, 1.0, 0.0), 0.0).astype(F32)


def _head_norm128(y, g):
    mu = jnp.mean(y, axis=-1, keepdims=True)
    d = y - mu
    var = jnp.mean(d * d, axis=-1, keepdims=True)
    return d * lax.rsqrt(var + HN_EPS) * g


def _ret_body(qk_ref, v_ref, gc_ref, cos_ref, sin_ref, gn_ref, s0_ref, y_ref, sout_ref, s_scr, *, l_valid):
    c = pl.program_id(1)
    lc = qk_ref.shape[1]
    hk = H_C * DK_C

    @pl.when(c == 0)
    def _():
        s_scr[...] = s0_ref[0]

    rr = lax.broadcasted_iota(I32, (hk, hk), 0)
    cc = lax.broadcasted_iota(I32, (hk, hk), 1)
    half = DK_C // 2
    same = (rr // DK_C) == (cc // DK_C)
    dr = rr & (DK_C - 1)
    dc = cc & (DK_C - 1)
    rot = jnp.where(same, jnp.where(dr == dc + half, -1.0, jnp.where(dr + half == dc, 1.0, 0.0)), 0.0).astype(F32)

    qk = qk_ref[0]
    cos = cos_ref[...]
    sin = sin_ref[...]

    def rope(x):
        return x * cos + jnp.dot(x, rot, precision=HI, preferred_element_type=F32) * sin

    qr = rope(qk[:, :hk])
    kr = rope(qk[:, hk:]) * DK_C ** -0.5
    krb = kr.astype(BF16)
    s_prev = s_scr[...]
    s_prev_b = s_prev.astype(BF16)

    jj = lax.broadcasted_iota(I32, (lc, lc), 0)
    ss = lax.broadcasted_iota(I32, (lc, lc), 1)
    diff = (jj - ss).astype(F32)
    jcol = lax.broadcasted_iota(I32, (lc, 1), 0).astype(F32)
    srow_state = lax.broadcasted_iota(I32, (hk, 1), 0) // DK_C
    s_new = jnp.zeros_like(s_prev)
    decay_rows = jnp.zeros((hk, 1), F32)
    for h in range(H_C):
        lg = math.log1p(-2.0 ** (-5.0 - h))
        mh = _head_mask(hk, h, DK_C)
        qm = (qr * mh).astype(BF16)
        dmask = jnp.where(diff >= 0, jnp.exp(jnp.maximum(diff, 0.0) * lg), 0.0)
        scores = lax.dot_general(qm, krb, NT, preferred_element_type=F32) * dmask
        vh = v_ref[0, :, h * DV_C:(h + 1) * DV_C]
        vhb = vh.astype(BF16)
        intra = jnp.dot(scores.astype(BF16), vhb, preferred_element_type=F32)
        inter = jnp.dot(qm, s_prev_b, preferred_element_type=F32) * jnp.exp((jcol + 1.0) * lg)
        yh = _head_norm128(intra + inter, gn_ref[h:h + 1, :])
        gch = gc_ref[0, :, h * DV_C:(h + 1) * DV_C]
        y_ref[0, :, h * DV_C:(h + 1) * DV_C] = yh * (gch * jax.nn.sigmoid(gch))
        w_s = jnp.where(jcol < l_valid, jnp.exp((l_valid - 1.0 - jcol) * lg), 0.0)
        kw = (kr * mh * w_s).astype(BF16)
        s_new = s_new + lax.dot_general(kw, vhb, TN, preferred_element_type=F32)
        decay_rows = jnp.where(srow_state == h, math.exp(l_valid * lg), decay_rows)
    s_fin = decay_rows * s_prev + s_new
    s_scr[...] = s_fin

    @pl.when(c == pl.num_programs(1) - 1)
    def _():
        sout_ref[0] = s_fin


def retention_call(za, cos, sin, gn, s0, l_valid):
    b, t, _ = za.shape
    lc = _row_tile(t, 128)
    assert l_valid == lc or t == lc
    hk, hv = H_C * DK_C, H_C * DV_C
    blk = lambda j: pl.BlockSpec((1, lc, 2 * hk), lambda i, c, j=j: (i, c, j))
    st = pl.BlockSpec((1, hk, DV_C), lambda i, c: (i, 0, 0))
    return pl.pallas_call(
        functools.partial(_ret_body, l_valid=l_valid),
        grid=(b, t // lc),
        in_specs=[blk(0), blk(1), blk(2),
                  pl.BlockSpec((lc, hk), lambda i, c: (c, 0)),
                  pl.BlockSpec((lc, hk), lambda i, c: (c, 0)),
                  pl.BlockSpec((H_C, DV_C), lambda i, c: (0, 0)),
                  st],
        out_specs=[pl.BlockSpec((1, lc, hv), lambda i, c: (i, c, 0)), st],
        out_shape=[jax.ShapeDtypeStruct((b, t, hv), F32), jax.ShapeDtypeStruct((b, hk, DV_C), F32)],
        scratch_shapes=[pltpu.VMEM((hk, DV_C), F32)],
        compiler_params=_cparams(("parallel", "arbitrary")),
        name="retention",
    )(za, za, za, cos, sin, gn, s0)


def _mlstm_body(qk_ref, v_ref, og_ref, gt_ref, cb_ref, cw_ref, cbias_ref, gbias_ref, gn_ref,
                c0_ref, n0_ref, m0_ref, y_ref, cout_ref, nout_ref, mout_ref,
                ext_scr, c_scr, n_scr, m_scr, *, l_valid):
    c = pl.program_id(1)
    lc = qk_ref.shape[1]
    hk = H_D * DK_D
    pad = SUBLANES

    @pl.when(c == 0)
    def _():
        ext_scr[0:pad, :] = cb_ref[0]
        c_scr[...] = c0_ref[0]
        n_scr[...] = n0_ref[0]
        m_scr[...] = m0_ref[0]

    u = qk_ref[0]
    ext_scr[pad:pad + lc, :] = u
    acc = cbias_ref[...] + u * cw_ref[CONV_W - 1:CONV_W, :]
    for i in range(CONV_W - 1):
        sh = CONV_W - 1 - i
        acc = acc + ext_scr[pad - sh:pad - sh + lc, :] * cw_ref[i:i + 1, :]
    ext_scr[0:pad, :] = u[lc - pad:lc, :]
    qkc = acc * jax.nn.sigmoid(acc)
    q = qkc[:, :hk]
    k = qkc[:, hk:] * DK_D ** -0.5
    kb = k.astype(BF16)

    gates = gt_ref[0] + gbias_ref[...]
    logsig = -_softplus(-gates)
    rowi = lax.broadcasted_iota(I32, (lc, LANES), 0)
    logsig = jnp.where(rowi < l_valid, logsig, 0.0)
    jj = lax.broadcasted_iota(I32, (lc, lc), 0)
    ss = lax.broadcasted_iota(I32, (lc, lc), 1)
    tril = jnp.where(jj >= ss, 1.0, 0.0).astype(F32)
    bcum = jnp.dot(tril, logsig, precision=HI, preferred_element_type=F32)
    lane_g = lax.broadcasted_iota(I32, (lc, LANES), 1)
    jcol = lax.broadcasted_iota(I32, (lc, 1), 0)
    causal = jj >= ss
    ninf = -jnp.inf

    c_prev = c_scr[...]
    c_prev_b = c_prev.astype(BF16)
    n_prev = n_scr[...]
    m_prev = m_scr[...]
    c_new = jnp.zeros_like(c_prev)
    carry_rows = jnp.zeros((hk, 1), F32)
    carry_lanes = jnp.zeros((1, hk), F32)
    ws_full = jnp.zeros((lc, hk), F32)
    m_out = m_prev
    srow_state = lax.broadcasted_iota(I32, (hk, 1), 0) // DK_D
    lane_state = lax.broadcasted_iota(I32, (1, hk), 1) // DK_D
    lane_m = lax.broadcasted_iota(I32, (1, LANES), 1)
    for h in range(H_D):
        mh = _head_mask(hk, h, DK_D)
        e_i = jnp.where(lane_g == h, 1.0, 0.0).astype(F32)
        e_f = jnp.where(lane_g == H_D + h, 1.0, 0.0).astype(F32)
        logi_col = jnp.where(jcol < l_valid, gates[:, h:h + 1], ninf)
        b_col = bcum[:, H_D + h:H_D + h + 1]
        i_row = lax.dot_general(e_i, gates, NT, precision=HI, preferred_element_type=F32)
        i_row = jnp.where(ss < l_valid, i_row, ninf)
        b_row = lax.dot_general(e_f, bcum, NT, precision=HI, preferred_element_type=F32)
        m_h = m_prev[:, h:h + 1]
        inter = b_col + m_h
        dmat = jnp.where(causal, b_col - b_row + i_row, ninf)
        m_j = jnp.maximum(inter, jnp.max(dmat, axis=-1, keepdims=True))
        qm = q * mh
        qmb = qm.astype(BF16)
        amat = jnp.exp(dmat - m_j) * lax.dot_general(qmb, kb, NT, preferred_element_type=F32)
        sc = jnp.exp(inter - m_j)
        vh = v_ref[0, :, h * DV_D:(h + 1) * DV_D]
        vhb = vh.astype(BF16)
        num = jnp.dot(amat.astype(BF16), vhb, preferred_element_type=F32) \
            + sc * jnp.dot(qmb, c_prev_b, preferred_element_type=F32)
        den = jnp.sum(amat, axis=-1, keepdims=True) + sc * jnp.sum(qm * n_prev, axis=-1, keepdims=True)
        hh = num / jnp.maximum(jnp.abs(den), jnp.exp(-m_j))
        ogh = og_ref[0, :, h * DV_D:(h + 1) * DV_D]
        y_ref[0, :, h * DV_D:(h + 1) * DV_D] = _head_norm128(hh, gn_ref[h:h + 1, :]) * jax.nn.sigmoid(ogh)
        b_last = b_col[l_valid - 1:l_valid, :]
        gs = b_last - b_col + logi_col
        m_new = jnp.maximum(b_last + m_h, jnp.max(gs, axis=0, keepdims=True))
        ws = jnp.exp(gs - m_new)
        carry = jnp.exp(b_last + m_h - m_new)
        c_new = c_new + lax.dot_general((k * mh).astype(BF16), (vh * ws).astype(BF16), TN,
                                        preferred_element_type=F32)
        carry_rows = jnp.where(srow_state == h, carry, carry_rows)
        carry_lanes = jnp.where(lane_state == h, carry, carry_lanes)
        ws_full = ws_full + ws * mh
        m_out = jnp.where(lane_m == h, m_new, m_out)
    c_fin = carry_rows * c_prev + c_new
    n_fin = carry_lanes * n_prev + jnp.sum(ws_full * k, axis=0, keepdims=True)
    c_scr[...] = c_fin
    n_scr[...] = n_fin
    m_scr[...] = m_out

    @pl.when(c == pl.num_programs(1) - 1)
    def _():
        cout_ref[0] = c_fin
        nout_ref[0] = n_fin
        mout_ref[0] = m_out


def mlstm_call(zb, zc, conv_buf, conv_w, conv_b, gate_bias, gn, c0, n0, m0, l_valid):
    b, t, _ = zb.shape
    lc = _row_tile(t, 128)
    assert l_valid == lc or t == lc
    hk, hv = H_D * DK_D, H_D * DV_D
    blk = lambda j: pl.BlockSpec((1, lc, 2 * hk), lambda i, c, j=j: (i, c, j))
    cst = lambda s: pl.BlockSpec(s, lambda i, c: (0,) * len(s))
    per_b = lambda s: pl.BlockSpec((1,) + s, lambda i, c: (i,) + (0,) * len(s))
    return pl.pallas_call(
        functools.partial(_mlstm_body, l_valid=l_valid),
        grid=(b, t // lc),
        in_specs=[blk(0), blk(1), blk(2),
                  pl.BlockSpec((1, lc, LANES), lambda i, c: (i, c, 0)),
                  per_b((SUBLANES, 2 * hk)),
                  cst((CONV_W, 2 * hk)), cst((1, 2 * hk)), cst((1, LANES)), cst((H_D, DV_D)),
                  per_b((hk, DV_D)), per_b((1, hk)), per_b((1, LANES))],
        out_specs=[pl.BlockSpec((1, lc, hv), lambda i, c: (i, c, 0)),
                   per_b((hk, DV_D)), per_b((1, hk)), per_b((1, LANES))],
        out_shape=[jax.ShapeDtypeStruct((b, t, hv), F32),
                   jax.ShapeDtypeStruct((b, hk, DV_D), F32),
                   jax.ShapeDtypeStruct((b, 1, hk), F32),
                   jax.ShapeDtypeStruct((b, 1, LANES), F32)],
        scratch_shapes=[pltpu.VMEM((SUBLANES + lc, 2 * hk), F32),
                        pltpu.VMEM((hk, DV_D), F32),
                        pltpu.VMEM((1, hk), F32),
                        pltpu.VMEM((1, LANES), F32)],
        compiler_params=_cparams(("parallel", "arbitrary")),
        name="mlstm",
    )(zb, zb, zb, zc, conv_buf, conv_w, conv_b, gate_bias, gn, c0, n0, m0)


def _pad_cols(w, n):
    return jnp.pad(w, ((0, 0), (0, n - w.shape[1])))


def _pad_time(a, tp):
    t = a.shape[1]
    if t == tp:
        return a
    return jnp.pad(a, ((0, 0), (0, tp - t)) + ((0, 0),) * (a.ndim - 2))


def _heads_major(a, h):
    b, t, _ = a.shape
    return a.reshape(b, t, h, -1).transpose(0, 2, 1, 3)


def _ab_mixer(xf, b, t, e, g_pre, P, prompt, shift_prev, s0, cache):
    m = b * t
    hd = H_A * DH_A
    w = P['ab_w_in'][e]
    o_qi = 3 * hd
    o_ki = o_qi + H_IDX * D_IDX
    o_wi = o_ki + D_IDX
    o_zb = o_wi + H_IDX
    qkv = proj_in_call(xf, g_pre, w[:, :o_qi].astype(BF16))
    qi = proj_in_call(xf, g_pre, w[:, o_qi:o_ki].astype(BF16), out_dtype=BF16)
    kw, ki2 = kiwi_call(xf, g_pre, w[:, o_ki:o_wi], w[:, o_wi:o_zb], P['kidx_g'][e])
    zb = proj_in_call(xf, g_pre, w[:, o_zb:].astype(BF16))
    q3 = qkv[:, :hd].reshape(b, t, hd)
    k3 = qkv[:, hd:2 * hd].reshape(b, t, hd)
    v3 = qkv[:, 2 * hd:].reshape(b, t, hd)
    ki3 = kw[:, :D_IDX].reshape(b, t, D_IDX)
    wi3 = kw[:, D_IDX:D_IDX + H_IDX].reshape(b, t, H_IDX)

    if prompt:
        ya = dsa_prompt_call(qkv.reshape(b, t, 3 * hd), qi.reshape(b, t, H_IDX * D_IDX),
                             ki2.reshape(b, t, LANES), kw.reshape(b, t, LANES)).reshape(m, hd)
    else:
        cache_k, cache_v, cache_kidx, page_table = cache
        n_pool = cache_k.shape[1]
        nq = SUBLANES
        qi_s = _pad_time(qi.reshape(b, t, H_IDX * D_IDX), nq).reshape(b, nq * H_IDX, D_IDX)
        wi_s = _pad_time(wi3, nq).reshape(b, 1, nq * H_IDX)
        isc = dsa_sidx_call(page_table, qi_s, wi_s, _pad_time(ki3, PAGE_SIZE), cache_kidx[e])
        ya = dsa_satt_call(page_table, isc, _pad_time(q3, nq), _pad_time(k3, PAGE_SIZE), _pad_time(v3, PAGE_SIZE),
                           cache_k[e].reshape(n_pool, PAGE_SIZE, hd), cache_v[e].reshape(n_pool, PAGE_SIZE, hd), t)
        ya = ya[:, :t].reshape(m, hd)

    tp = -(-t // SUBLANES) * SUBLANES
    hb = H_B * DH_B
    zb3 = zb.reshape(b, t, D_B_IN)
    zpad = jnp.zeros((R_DECAY, hb), F32)
    w2p = jnp.concatenate([P['rwkv_w2'][e], zpad], axis=0).astype(BF16)
    a2p = jnp.concatenate([zpad, P['rwkv_a2'][e]], axis=0).astype(BF16)
    row = lambda a: a.reshape(1, -1)
    r, dec, k2, v, an, bb, g = rwkv_prep_call(
        _pad_time(zb3, tp), shift_prev.reshape(b, 1, D_B_IN), row(P['rwkv_mu'][e]), row(P['rwkv_w0'][e]),
        row(P['rwkv_a0'][e]), w2p, a2p, P['rwkv_g2'][e].astype(BF16), row(P['rwkv_kk'][e]), row(P['rwkv_ka'][e]))
    s0p = s0.reshape(b, RWKV_PAIRS, 2, DH_B, DH_B).transpose(0, 1, 3, 2, 4).reshape(b, RWKV_PAIRS, DH_B, LANES)
    y, s_fin = rwkv_scan_call(r, dec, k2, v, an, bb, s0p, t)
    fl = lambda a: a.reshape(b * tp, hb)
    rk = jnp.broadcast_to(P['rwkv_rk'][e], (H_B, DH_B)).reshape(1, hb)
    yb = rwkv_post_call(fl(y), fl(r), fl(k2), fl(v), fl(g), row(P['rwkv_lnx_g'][e]), row(P['rwkv_lnx_b'][e]), rk)
    yb = yb.reshape(b, tp, hb)[:, :t].reshape(m, hb)
    s_new = s_fin.reshape(b, RWKV_PAIRS, DH_B, 2, DH_B).transpose(0, 1, 3, 2, 4).reshape(b, H_B, DH_B, DH_B)
    st = (k3.reshape(b, t, H_A, DH_A), v3.reshape(b, t, H_A, DH_A), ki3, s_new, zb3[:, t - 1])
    return ya, yb, st


def _cd_mixer(xf, b, t, o, g_pre, pos, P, ret_s, m_c, m_n, m_m, conv_buf):
    w = P['cd_w_in'][o]
    hk, hv = H_C * DK_C, H_C * DV_C
    o_g = 2 * hk + 2 * hv
    o_vd = o_g + 2 * H_D * DK_D
    o_ig = o_vd + H_D * DV_D
    o_og = o_ig + 2 * H_D
    za = proj_in_call(xf, g_pre, w[:, :o_g].astype(BF16))
    zb = proj_in_call(xf, g_pre, jnp.concatenate([w[:, o_g:o_ig], w[:, o_og:]], axis=1).astype(BF16))
    zc = proj_in_call(xf, g_pre, _pad_cols(w[:, o_ig:o_og], LANES).astype(BF16))
    tp = -(-t // SUBLANES) * SUBLANES
    l_valid = t if tp != t else _row_tile(t, 128)
    za3 = _pad_time(za.reshape(b, t, -1), tp)
    zb3 = zb.reshape(b, t, -1)
    zc3 = _pad_time(zc.reshape(b, t, -1), tp)

    half = DK_C // 2
    inv = ROPE_BASE ** (-jnp.arange(half, dtype=F32) / half)
    ang = _pad_time(pos.astype(F32)[None], tp)[0][:, None] * inv[None, :]
    cos = jnp.tile(jnp.cos(ang), (1, 2 * H_C))
    sin = jnp.tile(jnp.sin(ang), (1, 2 * H_C))
    yc, ret_new = retention_call(za3, cos, sin, P['ret_gn'][o], ret_s.astype(F32).reshape(b, hk, DV_C), l_valid)

    hkd = H_D * DK_D
    cb = jnp.pad(conv_buf.astype(F32), ((0, 0), (SUBLANES - (CONV_W - 1), 0), (0, 0)))
    gate_bias = jnp.pad(P['mlstm_if_b'][o].astype(F32).reshape(1, 2 * H_D), ((0, 0), (0, LANES - 2 * H_D)))
    c0 = m_c.astype(F32).transpose(0, 1, 3, 2).reshape(b, hkd, DV_D)
    n0 = m_n.astype(F32).reshape(b, 1, hkd)
    m0 = jnp.pad(m_m.astype(F32), ((0, 0), (0, LANES - H_D))).reshape(b, 1, LANES)
    yd, c_new, n_new, m_new = mlstm_call(_pad_time(zb3, tp), zc3, cb, P['conv_w'][o], P['conv_b'][o].reshape(1, -1),
                                         gate_bias, P['mlstm_gn'][o], c0, n0, m0, l_valid)
    conv_new = jnp.concatenate([conv_buf.astype(F32), zb3[:, :, :2 * hkd]], axis=1)[:, -(CONV_W - 1):]
    st = (ret_new.reshape(b, H_C, DK_C, DV_C),
          c_new.reshape(b, H_D, DK_D, DV_D).transpose(0, 1, 3, 2),
          n_new.reshape(b, H_D, DK_D),
          m_new.reshape(b, LANES)[:, :H_D],
          conv_new)
    m = b * t
    return yc[:, :t].reshape(m, hv), yd[:, :t].reshape(m, H_D * DV_D), st


def _trunk(x, pos, prompt, ab_init, cd_init, cache, P):
    b, t, d = x.shape
    xf = x.reshape(b * t, d)
    depth = P['norm_g'].shape[0]
    ab_new, cd_new = [], []
    for l in range(depth):
        g = P['norm_g'][l]
        bf = lambda a: a.astype(BF16)
        xf = ffn_call(xf, g[0], g[1], bf(P['ffn_wg'][l, 0]), bf(P['ffn_wu'][l, 0]), bf(P['ffn_wd'][l, 0]))
        if l % 2 == 0:
            e = l // 2
            shift_prev, s0 = ab_init(e)
            ya, yb, st = _ab_mixer(xf, b, t, e, g[2], P, prompt, shift_prev, s0, cache)
            ab_new.append(st)
            wo = bf(P['ab_w_out'][e])
            xf = proj_out_call(ya, yb, xf, g[3], wo[:H_A * DH_A], wo[H_A * DH_A:])
        else:
            o = l // 2
            yc, yd, st = _cd_mixer(xf, b, t, o, g[2], pos, P, *cd_init(o))
            cd_new.append(st)
            wo = bf(P['cd_w_out'][o])
            xf = proj_out_call(yc, yd, xf, g[3], wo[:H_C * DV_C], wo[H_C * DV_C:])
        xf = ffn_call(xf, g[4], g[5], bf(P['ffn_wg'][l, 1]), bf(P['ffn_wu'][l, 1]), bf(P['ffn_wd'][l, 1]))
    ab = tuple(jnp.stack(s) for s in zip(*ab_new))
    cd = tuple(jnp.stack(s) for s in zip(*cd_new))
    return xf.reshape(b, t, d), ab, cd


def kernel(x_prompt, x_sample, cache_k, cache_v, cache_kidx, state_rwkv, state_shift, state_ret, state_mlstm_C, state_mlstm_n, state_mlstm_m, state_conv, page_table, norm_g, ffn_wg, ffn_wu, ffn_wd, ab_w_in, ab_w_out, kidx_g, rwkv_mu, rwkv_w0, rwkv_w2, rwkv_a0, rwkv_a2, rwkv_g2, rwkv_kk, rwkv_ka, rwkv_rk, rwkv_lnx_g, rwkv_lnx_b, cd_w_in, cd_w_out, ret_gn, conv_w, conv_b, mlstm_if_b, mlstm_gn):
    P = dict(norm_g=norm_g, ffn_wg=ffn_wg, ffn_wu=ffn_wu, ffn_wd=ffn_wd, ab_w_in=ab_w_in, ab_w_out=ab_w_out,
             kidx_g=kidx_g, rwkv_mu=rwkv_mu, rwkv_w0=rwkv_w0, rwkv_w2=rwkv_w2, rwkv_a0=rwkv_a0, rwkv_a2=rwkv_a2,
             rwkv_g2=rwkv_g2, rwkv_kk=rwkv_kk, rwkv_ka=rwkv_ka, rwkv_rk=rwkv_rk, rwkv_lnx_g=rwkv_lnx_g,
             rwkv_lnx_b=rwkv_lnx_b, cd_w_in=cd_w_in, cd_w_out=cd_w_out, ret_gn=ret_gn, conv_w=conv_w,
             conv_b=conv_b, mlstm_if_b=mlstm_if_b, mlstm_gn=mlstm_gn)
    B, T, _ = x_prompt.shape
    DB, DS, _ = x_sample.shape
    past = page_table.shape[1] * PAGE_SIZE

    def ab_zero(e):
        return (jnp.zeros((B, D_B_IN), F32), jnp.zeros((B, H_B, DH_B, DH_B), F32))

    def cd_zero(o):
        return (jnp.zeros((B, H_C, DK_C, DV_C), F32), jnp.zeros((B, H_D, DV_D, DK_D), F32),
                jnp.zeros((B, H_D, DK_D), F32), jnp.zeros((B, H_D), F32),
                jnp.zeros((B, CONV_W - 1, 2 * H_D * DK_D), F32))

    def ab_cached(e):
        return (state_shift[e], state_rwkv[e])

    def cd_cached(o):
        return (state_ret[o], state_mlstm_C[o], state_mlstm_n[o], state_mlstm_m[o], state_conv[o])

    y_p, (kp, vp, kip, rwp, shp), (rtp, cp, nvp, mp, cvp) = _trunk(
        x_prompt, jnp.arange(T), True, ab_zero, cd_zero, None, P)
    y_s, (ks_, vs_, kis, rws, shs), (rts, cs, nvs, ms, cvs) = _trunk(
        x_sample, past + jnp.arange(DS), False, ab_cached, cd_cached,
        (cache_k, cache_v, cache_kidx, page_table), P)
    return (y_p, y_s, kp, vp, kip, rwp, shp, rtp, cp, nvp, mp, cvp,
            ks_, vs_, kis, rws, shs, rts, cs, nvs, ms, cvs)
```

```python
import functools
import math

import numpy as np
import jax
import jax.numpy as jnp
from jax import lax
from jax.experimental import pallas as pl
from jax.experimental.pallas import tpu as pltpu

F32 = jnp.float32
BF16 = jnp.bfloat16
I32 = jnp.int32
HI = lax.Precision.HIGHEST

LANES = 128
SUBLANES = 8
VMEM_LIMIT = 56 * 1024 * 1024

EPS = 1e-6
PAGE_SIZE = 128
H_A, DH_A, H_IDX, D_IDX, TOPK_MAX = 8, 64, 16, 64, 256
H_B, DH_B, R_DECAY, R_AAA, R_GATE = 8, 64, 64, 64, 128
D_B_IN = 3 * H_B * DH_B + R_DECAY + R_AAA + R_GATE
LNX_EPS = 64e-5
H_C, DK_C, DV_C, ROPE_BASE = 4, 64, 128, 10000.0
H_D, DK_D, DV_D, CONV_W = 4, 64, 128, 4
HN_EPS = 1e-5
NEG_BIG = -1e30
INT_MIN = -(2 ** 31)
INT_MAX = 2 ** 31 - 1
KEY_NEG_INF = (0xFF800000 ^ 0x7FFFFFFF) - 2 ** 32

NT = (((1,), (1,)), ((), ()))
TN = (((0,), (0,)), ((), ()))


def _cparams(sem):
    return pltpu.CompilerParams(dimension_semantics=sem, vmem_limit_bytes=VMEM_LIMIT)


def _rms(x, g):
    return x * lax.rsqrt(jnp.mean(x * x, axis=-1, keepdims=True) + EPS) * g


def _softplus(x):
    return jnp.maximum(x, 0.0) + jnp.log(1.0 + jnp.exp(-jnp.abs(x)))


def _row_tile(m, want):
    t = min(want, m)
    while m % t:
        t //= 2
    return t


def _col_tile(n, want):
    best = LANES
    for t in range(LANES, min(n, want) + 1, LANES):
        if n % t == 0:
            best = t
    return best


def _ffn_body(x_ref, g0_ref, g1_ref, wg_ref, wu_ref, wd_ref, o_ref, h_scr, acc_scr):
    j = pl.program_id(1)

    @pl.when(j == 0)
    def _():
        h_scr[...] = _rms(x_ref[...], g0_ref[...]).astype(BF16)
        acc_scr[...] = jnp.zeros_like(acc_scr)

    h = h_scr[...]
    g = jnp.dot(h, wg_ref[...], preferred_element_type=F32)
    u = jnp.dot(h, wu_ref[...], preferred_element_type=F32)
    a = g * jax.nn.sigmoid(g) * u
    acc_scr[...] += jnp.dot(a.astype(BF16), wd_ref[...], preferred_element_type=F32)

    @pl.when(j == pl.num_programs(1) - 1)
    def _():
        o_ref[...] = x_ref[...] + 0.5 * _rms(acc_scr[...], g1_ref[...])


def ffn_call(x, g0, g1, wg, wu, wd):
    m, d = x.shape
    ff = wg.shape[1]
    tm = _row_tile(m, 512)
    tf = _col_tile(ff, 1408)
    return pl.pallas_call(
        _ffn_body,
        grid=(m // tm, ff // tf),
        in_specs=[
            pl.BlockSpec((tm, d), lambda i, j: (i, 0)),
            pl.BlockSpec((1, d), lambda i, j: (0, 0)),
            pl.BlockSpec((1, d), lambda i, j: (0, 0)),
            pl.BlockSpec((d, tf), lambda i, j: (0, j)),
            pl.BlockSpec((d, tf), lambda i, j: (0, j)),
            pl.BlockSpec((tf, d), lambda i, j: (j, 0)),
        ],
        out_specs=pl.BlockSpec((tm, d), lambda i, j: (i, 0)),
        out_shape=jax.ShapeDtypeStruct((m, d), F32),
        scratch_shapes=[pltpu.VMEM((tm, d), BF16), pltpu.VMEM((tm, d), F32)],
        compiler_params=_cparams(("parallel", "arbitrary")),
        name="ffn",
    )(x, g0.reshape(1, d), g1.reshape(1, d), wg, wu, wd)


def _proj_in_body(x_ref, g_ref, w_ref, o_ref, h_scr):
    @pl.when(pl.program_id(1) == 0)
    def _():
        h_scr[...] = _rms(x_ref[...], g_ref[...]).astype(BF16)

    o_ref[...] = jnp.dot(h_scr[...], w_ref[...], preferred_element_type=F32).astype(o_ref.dtype)


def proj_in_call(x, g, w, out_dtype=F32):
    m, d = x.shape
    n = w.shape[1]
    tm = _row_tile(m, 512)
    tn = _col_tile(n, 2048)
    return pl.pallas_call(
        _proj_in_body,
        grid=(m // tm, n // tn),
        in_specs=[
            pl.BlockSpec((tm, d), lambda i, j: (i, 0)),
            pl.BlockSpec((1, d), lambda i, j: (0, 0)),
            pl.BlockSpec((d, tn), lambda i, j: (0, j)),
        ],
        out_specs=pl.BlockSpec((tm, tn), lambda i, j: (i, j)),
        out_shape=jax.ShapeDtypeStruct((m, n), out_dtype),
        scratch_shapes=[pltpu.VMEM((tm, d), BF16)],
        compiler_params=_cparams(("parallel", "arbitrary")),
        name="proj_in",
    )(x, g.reshape(1, d), w)


def _kiwi_body(x_ref, g_ref, w_ref, kg_ref, o_ref, k2_ref):
    h = _rms(x_ref[...], g_ref[...]).astype(BF16)
    z = jnp.dot(h, w_ref[...], preferred_element_type=F32)
    z1 = z[:, :LANES]
    lane = lax.broadcasted_iota(I32, z1.shape, 1)
    is_k = lane < D_IDX
    ms = jnp.sum(jnp.where(is_k, z1 * z1, 0.0), axis=-1, keepdims=True) * (1.0 / D_IDX)
    inv = lax.rsqrt(ms + EPS)
    kg = kg_ref[...]
    o_ref[...] = jnp.where(is_k, z1 * inv * kg[:, :LANES], z1 * (H_IDX * D_IDX) ** -0.5)
    k2_ref[...] = (z[:, LANES:] * inv * kg[:, LANES:]).astype(BF16)


def kiwi_call(x, g, w_ki, w_wi, kidx_g):
    m, d = x.shape
    tm = _row_tile(m, 512)
    zpad = jnp.zeros((d, LANES - D_IDX - H_IDX), F32)
    w = jnp.concatenate([w_ki, w_wi, zpad, w_ki, w_ki], axis=1).astype(BF16)
    gpad = jnp.zeros((LANES - D_IDX,), F32)
    kg = jnp.concatenate([kidx_g, gpad, kidx_g, kidx_g]).reshape(1, 2 * LANES)
    return pl.pallas_call(
        _kiwi_body,
        grid=(m // tm,),
        in_specs=[
            pl.BlockSpec((tm, d), lambda i: (i, 0)),
            pl.BlockSpec((1, d), lambda i: (0, 0)),
            pl.BlockSpec((d, 2 * LANES), lambda i: (0, 0)),
            pl.BlockSpec((1, 2 * LANES), lambda i: (0, 0)),
        ],
        out_specs=[pl.BlockSpec((tm, LANES), lambda i: (i, 0))] * 2,
        out_shape=[jax.ShapeDtypeStruct((m, LANES), F32), jax.ShapeDtypeStruct((m, LANES), BF16)],
        compiler_params=_cparams(("parallel",)),
        name="kiwi",
    )(x, g.reshape(1, d), w, kg)


def _proj_out_body(a1_ref, a2_ref, x_ref, g_ref, w1_ref, w2_ref, o_ref):
    y = jnp.dot(a1_ref[...].astype(BF16), w1_ref[...], preferred_element_type=F32)
    y = y + jnp.dot(a2_ref[...].astype(BF16), w2_ref[...], preferred_element_type=F32)
    o_ref[...] = x_ref[...] + _rms(y, g_ref[...])


def proj_out_call(a1, a2, x, g, w1, w2):
    m, d = x.shape
    k1, k2 = a1.shape[1], a2.shape[1]
    tm = _row_tile(m, 512)
    return pl.pallas_call(
        _proj_out_body,
        grid=(m // tm,),
        in_specs=[
            pl.BlockSpec((tm, k1), lambda i: (i, 0)),
            pl.BlockSpec((tm, k2), lambda i: (i, 0)),
            pl.BlockSpec((tm, d), lambda i: (i, 0)),
            pl.BlockSpec((1, d), lambda i: (0, 0)),
            pl.BlockSpec((k1, d), lambda i: (0, 0)),
            pl.BlockSpec((k2, d), lambda i: (0, 0)),
        ],
        out_specs=pl.BlockSpec((tm, d), lambda i: (i, 0)),
        out_shape=jax.ShapeDtypeStruct((m, d), F32),
        compiler_params=_cparams(("parallel",)),
        name="proj_out",
    )(a1, a2, x, g.reshape(1, d), w1, w2)


def _topk_bias(isc, col, topk, nbits_col):
    bits = pltpu.bitcast(isc, I32)
    key = jnp.where(bits < 0, bits ^ 0x7FFFFFFF, bits)
    kf = float(topk)

    def count(mask):
        return jnp.sum(jnp.where(mask, 1.0, 0.0), axis=-1, keepdims=True)

    prefix = jnp.where(count(key >= 0) >= kf, 0, INT_MIN).astype(I32)

    def bit_step(i, prefix):
        cand = prefix + jnp.left_shift(jnp.int32(1), 30 - i)
        return jnp.where(count(key >= cand) >= kf, cand, prefix)

    thr = lax.fori_loop(0, 31, bit_step, prefix)
    gt = key > thr
    eq = key == thr
    need = kf - count(gt)

    def col_step(i, y):
        cand = y + jnp.left_shift(jnp.int32(1), nbits_col - 1 - i)
        c = jnp.sum(jnp.where(eq, jnp.where(col < cand, 1.0, 0.0), 0.0), axis=-1, keepdims=True)
        return jnp.where(c < need, cand, y)

    tied = jnp.where(count(key >= thr) > kf, jnp.where(thr > KEY_NEG_INF, 1.0, 0.0), 0.0)
    y = lax.cond(jnp.max(tied) > 0.0,
                 lambda: lax.fori_loop(0, nbits_col, col_step, jnp.zeros_like(thr)),
                 lambda: jnp.full_like(thr, INT_MAX))
    ninf = -jnp.inf
    bias = jnp.where(gt, 0.0, jnp.where(eq, jnp.where(col <= y, 0.0, ninf), ninf))
    return jnp.where(jnp.abs(isc) < jnp.inf, bias, ninf)


def _dsa_prompt_body(qkv_q_ref, qkv_k_ref, qkv_v_ref, qi_ref, ki2_ref, kw_ref, o_ref, kb_scr, vb_scr, *, topk):
    qb = qkv_q_ref.shape[1]
    t = qkv_k_ref.shape[1]
    j = pl.program_id(1)
    t0 = j * qb

    @pl.when(j == 0)
    def _():
        kb_scr[...] = qkv_k_ref[0].astype(BF16)
        vb_scr[...] = qkv_v_ref[0].astype(BF16)

    lane = lax.broadcasted_iota(I32, (1, LANES), 1)
    half = (jnp.where(lane < DH_A, 1.0, 0.0), jnp.where(lane < DH_A, 0.0, 1.0))
    half_b = tuple(m.astype(BF16) for m in half)
    ki2 = ki2_ref[0]
    kw = kw_ref[0]
    isc = jnp.zeros((qb, t), F32)
    for hp in range(H_IDX // 2):
        qp = qi_ref[0, :, hp * LANES:(hp + 1) * LANES]
        for h2 in range(2):
            h = 2 * hp + h2
            s = lax.dot_general(qp * half_b[h2], ki2, NT, preferred_element_type=F32)
            isc = isc + kw[:, D_IDX + h:D_IDX + h + 1] * jnp.maximum(s, 0.0)
    row = lax.broadcasted_iota(I32, (qb, t), 0) + t0
    col = lax.broadcasted_iota(I32, (qb, t), 1)
    isc = jnp.where(col <= row, isc, -jnp.inf)
    bias = _topk_bias(isc, col, topk, max(1, (t - 1).bit_length()))
    scale = DH_A ** -0.5
    assert math.frexp(scale)[0] == 0.5
    for p in range(H_A // 2):
        sl = slice(p * LANES, (p + 1) * LANES)
        qp = qkv_q_ref[0, :, sl]
        kp = kb_scr[:, sl]
        vp = vb_scr[:, sl]
        outs = []
        for h2 in range(2):
            qm = (qp * (half[h2] * scale)).astype(BF16)
            lg = lax.dot_general(qm, kp, NT, preferred_element_type=F32) + bias
            mx = jnp.max(lg, axis=-1, keepdims=True)
            pr = jnp.exp(lg - mx)
            l = jnp.sum(pr, axis=-1, keepdims=True)
            outs.append(jnp.dot(pr.astype(BF16), vp, preferred_element_type=F32) / l)
        o_ref[0, :, sl] = jnp.where(lane < DH_A, outs[0], outs[1])


def dsa_prompt_call(qkv, qi, ki2, kw):
    b, t, _ = qkv.shape
    hd = H_A * DH_A
    qb = _row_tile(t, 256)
    topk = min(TOPK_MAX, t // 4)
    return pl.pallas_call(
        functools.partial(_dsa_prompt_body, topk=topk),
        grid=(b, t // qb),
        in_specs=[
            pl.BlockSpec((1, qb, hd), lambda i, j: (i, j, 0)),
            pl.BlockSpec((1, t, hd), lambda i, j: (i, 0, 1)),
            pl.BlockSpec((1, t, hd), lambda i, j: (i, 0, 2)),
            pl.BlockSpec((1, qb, H_IDX * D_IDX), lambda i, j: (i, j, 0)),
            pl.BlockSpec((1, t, LANES), lambda i, j: (i, 0, 0)),
            pl.BlockSpec((1, qb, LANES), lambda i, j: (i, j, 0)),
        ],
        out_specs=pl.BlockSpec((1, qb, hd), lambda i, j: (i, j, 0)),
        out_shape=jax.ShapeDtypeStruct((b, t, hd), F32),
        scratch_shapes=[pltpu.VMEM((t, hd), BF16), pltpu.VMEM((t, hd), BF16)],
        compiler_params=_cparams(("parallel", "arbitrary")),
        name="dsa_prompt",
    )(qkv, qkv, qkv, qi, ki2, kw)


DSA_PAGE_GROUP = 8


def _page_group(n_pages):
    return math.gcd(n_pages, DSA_PAGE_GROUP)


def _page_specs(shape, n_pages, group):
    zeros = (0,) * (len(shape) - 1)

    def spec(g):
        return pl.BlockSpec(shape, lambda b, s, pt: (pt[b, jnp.minimum(s * group + g, n_pages - 1)],) + zeros)
    return [spec(g) for g in range(group)]


def _page_cat(refs, new_ref, is_new):
    pages = [r[0] for r in refs]
    pages[0] = jnp.where(is_new, new_ref[0], pages[0])
    return jnp.concatenate([p.reshape(-1, PAGE_SIZE).astype(BF16) for p in pages], axis=1)


def _dsa_sidx_body(pt_ref, qi_ref, wi_ref, kin_ref, *rest, n_steps):
    kc_refs, o_ref = rest[:-1], rest[-1]
    ki = _page_cat(kc_refs, kin_ref, pl.program_id(1) == n_steps)
    s = jnp.dot(qi_ref[0], ki, preferred_element_type=F32)
    nq = o_ref.shape[1]
    r = lax.broadcasted_iota(I32, (nq, nq * H_IDX), 0)
    c = lax.broadcasted_iota(I32, (nq, nq * H_IDX), 1)
    lo = r * H_IDX
    wsel = jnp.where(c >= lo, jnp.where(c < lo + H_IDX, wi_ref[0], 0.0), 0.0)
    o_ref[0] = jnp.dot(wsel, jnp.maximum(s, 0.0), precision=HI, preferred_element_type=F32)


def dsa_sidx_call(page_table, qi, wi, ki_new, cache_kidx):
    db, n_pages = page_table.shape
    nq = qi.shape[1] // H_IDX
    group = _page_group(n_pages)
    n_steps = n_pages // group
    grid_spec = pltpu.PrefetchScalarGridSpec(
        num_scalar_prefetch=1,
        grid=(db, n_steps + 1),
        in_specs=[
            pl.BlockSpec((1, nq * H_IDX, D_IDX), lambda b, s, pt: (b, 0, 0)),
            pl.BlockSpec((1, 1, nq * H_IDX), lambda b, s, pt: (b, 0, 0)),
            pl.BlockSpec((1, D_IDX, PAGE_SIZE), lambda b, s, pt: (b, 0, 0)),
        ] + _page_specs((1, D_IDX, PAGE_SIZE), n_pages, group),
        out_specs=pl.BlockSpec((1, nq, group * PAGE_SIZE), lambda b, s, pt: (b, 0, s)),
    )
    return pl.pallas_call(
        functools.partial(_dsa_sidx_body, n_steps=n_steps),
        grid_spec=grid_spec,
        out_shape=jax.ShapeDtypeStruct((db, nq, (n_steps + 1) * group * PAGE_SIZE), F32),
        compiler_params=_cparams(("parallel", "arbitrary")),
        name="dsa_sample_idx",
    )(page_table, qi, wi, ki_new, *([cache_kidx] * group))


def _dsa_satt_body(pt_ref, isc_ref, q_ref, kn_ref, vn_ref, *rest, n_pages, n_steps, topk, n_new):
    group = n_pages // n_steps
    kc_refs, vc_refs = rest[:group], rest[group:2 * group]
    o_ref, bias_scr, qe_scr, m_scr, l_scr, acc_scr = rest[2 * group:]
    p = pl.program_id(1)
    nq = q_ref.shape[1]
    past = n_pages * PAGE_SIZE
    wcols = group * PAGE_SIZE
    ltot = (n_steps + 1) * wcols
    hd = H_A * DH_A
    scale = DH_A ** -0.5
    assert math.frexp(scale)[0] == 0.5

    @pl.when(p == 0)
    def _():
        isc = isc_ref[0]
        row = lax.broadcasted_iota(I32, (nq, ltot), 0)
        col = lax.broadcasted_iota(I32, (nq, ltot), 1)
        rel = col - past
        ninf = -jnp.inf
        isc = jnp.where(rel < 0, isc, jnp.where(rel < n_new, jnp.where(rel <= row, isc, ninf), ninf))
        bias = _topk_bias(isc, col, topk, max(1, (ltot - 1).bit_length()))
        for i in range(n_steps + 1):
            bias_scr[i] = bias[:, i * wcols:(i + 1) * wcols]
        q = q_ref[0] * scale
        lane = lax.broadcasted_iota(I32, (nq, hd), 1)
        qe_scr[...] = jnp.concatenate(
            [jnp.where(lane >= h * DH_A, jnp.where(lane < (h + 1) * DH_A, q, 0.0), 0.0) for h in range(H_A)],
            axis=0).astype(BF16)
        m_scr[...] = jnp.full_like(m_scr, NEG_BIG)
        l_scr[...] = jnp.zeros_like(l_scr)
        acc_scr[...] = jnp.zeros_like(acc_scr)

    is_new = p == n_steps
    kcat = _page_cat(kc_refs, kn_ref, is_new)
    vcat = _page_cat(vc_refs, vn_ref, is_new)
    lg = jnp.dot(qe_scr[...], kcat, preferred_element_type=F32)
    lg = lg + jnp.concatenate([bias_scr[p]] * H_A, axis=0)
    m_old = m_scr[...]
    m_new = jnp.maximum(m_old, jnp.max(lg, axis=-1, keepdims=True))
    alpha = jnp.exp(m_old - m_new)
    pr = jnp.exp(lg - m_new)
    l_scr[...] = alpha * l_scr[...] + jnp.sum(pr, axis=-1, keepdims=True)
    acc_scr[...] = alpha * acc_scr[...] + lax.dot_general(pr.astype(BF16), vcat, NT, preferred_element_type=F32)
    m_scr[...] = m_new

    @pl.when(is_new)
    def _():
        o = acc_scr[...] / l_scr[...]
        lane = lax.broadcasted_iota(I32, (nq, hd), 1)
        out = jnp.zeros((nq, hd), F32)
        for h in range(H_A):
            oh = o[h * nq:(h + 1) * nq]
            out = out + jnp.where(lane >= h * DH_A, jnp.where(lane < (h + 1) * DH_A, oh, 0.0), 0.0)
        o_ref[0] = out


def dsa_satt_call(page_table, isc, q, k_new, v_new, cache_k, cache_v, n_new):
    db, n_pages = page_table.shape
    nq = q.shape[1]
    hd = H_A * DH_A
    page = (1, H_A, DH_A, PAGE_SIZE)
    group = _page_group(n_pages)
    n_steps = n_pages // group
    wcols = group * PAGE_SIZE
    ltot = (n_steps + 1) * wcols
    assert isc.shape[2] == ltot
    topk = min(TOPK_MAX, (n_pages * PAGE_SIZE + n_new) // 4)
    grid_spec = pltpu.PrefetchScalarGridSpec(
        num_scalar_prefetch=1,
        grid=(db, n_steps + 1),
        in_specs=[
            pl.BlockSpec((1, nq, ltot), lambda b, p, pt: (b, 0, 0)),
            pl.BlockSpec((1, nq, hd), lambda b, p, pt: (b, 0, 0)),
            pl.BlockSpec(page, lambda b, p, pt: (b, 0, 0, 0)),
            pl.BlockSpec(page, lambda b, p, pt: (b, 0, 0, 0)),
        ] + _page_specs(page, n_pages, group) * 2,
        out_specs=pl.BlockSpec((1, nq, hd), lambda b, p, pt: (b, 0, 0)),
        scratch_shapes=[
            pltpu.VMEM((n_steps + 1, nq, wcols), F32),
            pltpu.VMEM((H_A * nq, hd), BF16),
            pltpu.VMEM((H_A * nq, 1), F32),
            pltpu.VMEM((H_A * nq, 1), F32),
            pltpu.VMEM((H_A * nq, hd), F32),
        ],
    )
    return pl.pallas_call(
        functools.partial(_dsa_satt_body, n_pages=n_pages, n_steps=n_steps, topk=topk, n_new=n_new),
        grid_spec=grid_spec,
        out_shape=jax.ShapeDtypeStruct((db, nq, hd), F32),
        compiler_params=_cparams(("parallel", "arbitrary")),
        name="dsa_sample_att",
    )(page_table, isc, q, k_new, v_new, *([cache_k] * group), *([cache_v] * group))


def _head_sum_matrix(n, group):
    r = lax.broadcasted_iota(I32, (n, n), 0) // group
    c = lax.broadcasted_iota(I32, (n, n), 1) // group
    return jnp.where(r == c, 1.0, 0.0).astype(F32)


def _rwkv_prep_body(zb_ref, sp_ref, mu_ref, w0_ref, a0_ref, w2_ref, a2_ref, g2_ref, kkp_ref, ka_ref,
                    r_ref, w_ref, k_ref, v_ref, an_ref, b_ref, g_ref, carry_scr):
    tc = pl.program_id(1)
    zb = zb_ref[0]
    tt = zb.shape[0]
    hd = H_B * DH_B

    @pl.when(tc == 0)
    def _():
        carry_scr[...] = sp_ref[0]

    rows = lax.broadcasted_iota(I32, zb.shape, 0)
    prev = jnp.where(rows == 0, carry_scr[...], pltpu.roll(zb, 1, 0))
    carry_scr[...] = zb[tt - 1:tt]
    z = zb + (prev - zb) * mu_ref[...]
    r = z[:, 0:hd]
    k = z[:, hd:2 * hd]
    v = z[:, 2 * hd:3 * hd]
    xwa = z[:, 3 * hd:3 * hd + R_DECAY + R_AAA]
    xg = z[:, 3 * hd + R_DECAY + R_AAA:]
    wl = w0_ref[...] + jnp.dot(jnp.tanh(xwa).astype(BF16), w2_ref[...], preferred_element_type=F32)
    w_log = -_softplus(-wl) - 0.5
    decay = jnp.exp(-jnp.exp(w_log))
    a = jax.nn.sigmoid(a0_ref[...] + jnp.dot(xwa.astype(BF16), a2_ref[...], preferred_element_type=F32))
    g = jnp.dot(jax.nn.sigmoid(xg).astype(BF16), g2_ref[...], preferred_element_type=F32)
    kk = k * kkp_ref[...]
    ss = jnp.dot(kk * kk, _head_sum_matrix(hd, DH_B), precision=HI, preferred_element_type=F32)
    kk = kk * lax.rsqrt(jnp.maximum(ss, 1e-24))
    r_ref[0] = r
    w_ref[0] = decay
    k_ref[0] = k * (1.0 + (a - 1.0) * ka_ref[...])
    v_ref[0] = v
    an_ref[0] = -kk
    b_ref[0] = kk * a
    g_ref[0] = g


def rwkv_prep_call(zb, shift_prev, mu, w0, a0, w2p, a2p, g2, kkp, ka):
    b, t, dz = zb.shape
    hd = H_B * DH_B
    tt = _row_tile(t, 256)
    row = lambda n: pl.BlockSpec((1, n), lambda i, j: (0, 0))
    full = lambda s: pl.BlockSpec(s, lambda i, j: (0, 0))
    out = pl.BlockSpec((1, tt, hd), lambda i, j: (i, j, 0))
    return pl.pallas_call(
        _rwkv_prep_body,
        grid=(b, t // tt),
        in_specs=[
            pl.BlockSpec((1, tt, dz), lambda i, j: (i, j, 0)),
            pl.BlockSpec((1, 1, dz), lambda i, j: (i, 0, 0)),
            row(dz), row(hd), row(hd),
            full((R_DECAY + R_AAA, hd)), full((R_DECAY + R_AAA, hd)), full((R_GATE, hd)),
            row(hd), row(hd),
        ],
        out_specs=[out] * 7,
        out_shape=[jax.ShapeDtypeStruct((b, t, hd), F32)] * 7,
        scratch_shapes=[pltpu.VMEM((1, dz), F32)],
        compiler_params=_cparams(("parallel", "arbitrary")),
        name="rwkv_prep",
    )(zb, shift_prev, mu, w0, a0, w2p, a2p, g2, kkp, ka)


RWKV_PAIRS = H_B // 2
RWKV_STEPS = SUBLANES // RWKV_PAIRS
RWKV_NB = 8
RWKV_SUM_PIECES = 2


def _rwkv_scan_body(r_ref, w_ref, k_ref, v_ref, a_ref, b_ref, s0_ref, y_ref, sout_ref,
                    s_scr, pa_scr, pv_scr, *, t_valid):
    c = pl.program_id(1)
    nb, tc = r_ref.shape[0], r_ref.shape[1]
    chains = [(bi, p) for bi in range(nb) for p in range(RWKV_PAIRS)]
    rows = lambda n: slice(n * DH_B, (n + 1) * DH_B)

    @pl.when(c == 0)
    def _():
        for n, (bi, p) in enumerate(chains):
            s_scr[rows(n), :] = s0_ref[bi, p]

    if t_valid < tc * RWKV_STEPS:
        y_ref[...] = jnp.zeros_like(y_ref)

    sub = lax.broadcasted_iota(I32, (DH_B, LANES), 0)
    lane = lax.broadcasted_iota(I32, (DH_B, LANES), 1)
    isel = jnp.where((lane & (DH_B - 1)) == sub, 1.0, 0.0).astype(F32)
    qblk = _head_sum_matrix(LANES, DH_B).astype(BF16)

    def hsum(ref, pieces=3):
        x = ref[...]
        parts = []
        for _ in range(pieces - 1):
            h = x.astype(BF16)
            parts.append(h)
            x = x - h.astype(F32)
        parts.append(x.astype(BF16))
        return jnp.dot(jnp.concatenate(parts, axis=1), jnp.concatenate([qblk] * pieces, axis=0),
                       preferred_element_type=F32)

    def step(u, carry):
        tiles = [tuple(ref[bi, u] for ref in (a_ref, w_ref, k_ref, v_ref, b_ref, r_ref)) for bi in range(nb)]
        for i in range(RWKV_STEPS):
            for n, (bi, p) in enumerate(chains):
                at, _, _, vt, _, _ = tiles[bi]
                row = slice(i * RWKV_PAIRS + p, i * RWKV_PAIRS + p + 1)
                pa_scr[rows(n), :] = s_scr[rows(n), :] * at[row]
                pv_scr[rows(n), :] = isel * vt[row]
            sa = hsum(pa_scr, RWKV_SUM_PIECES)
            vc = hsum(pv_scr)
            for n, (bi, p) in enumerate(chains):
                _, wt, kt, _, bt, rt = tiles[bi]
                row = slice(i * RWKV_PAIRS + p, i * RWKV_PAIRS + p + 1)
                s = s_scr[rows(n), :] * wt[row] + sa[rows(n)] * bt[row] + vc[rows(n)] * kt[row]
                s_scr[rows(n), :] = s
                pa_scr[rows(n), :] = s * rt[row]
            yb = hsum(pa_scr, RWKV_SUM_PIECES)
            for n, (bi, p) in enumerate(chains):
                row = slice(i * RWKV_PAIRS + p, i * RWKV_PAIRS + p + 1)
                y_ref[bi, u, row, :] = jnp.sum(yb[rows(n)] * isel, axis=0, keepdims=True)
        return carry

    lax.fori_loop(0, min(t_valid, tc * RWKV_STEPS) // RWKV_STEPS, step, 0)

    @pl.when(c == pl.num_programs(1) - 1)
    def _():
        for n, (bi, p) in enumerate(chains):
            sout_ref[bi, p] = s_scr[rows(n), :]


def rwkv_scan_call(r, w, k, v, a, b, s0, t_valid):
    bsz, t, hd = r.shape
    nb = math.gcd(bsz, RWKV_NB)
    nt = t // RWKV_STEPS
    tc = _row_tile(nt, 32)
    assert t % RWKV_STEPS == 0 and t_valid % RWKV_STEPS == 0 and (t_valid == t or nt == tc)
    tiled = lambda x: x.reshape(bsz, nt, SUBLANES, LANES)
    seq = pl.BlockSpec((nb, tc, SUBLANES, LANES), lambda i, j: (i, j, 0, 0))
    st = pl.BlockSpec((nb, RWKV_PAIRS, DH_B, LANES), lambda i, j: (i, 0, 0, 0))
    y, s_fin = pl.pallas_call(
        functools.partial(_rwkv_scan_body, t_valid=t_valid),
        grid=(bsz // nb, nt // tc),
        in_specs=[seq] * 6 + [st],
        out_specs=[seq, st],
        out_shape=[jax.ShapeDtypeStruct((bsz, nt, SUBLANES, LANES), F32),
                   jax.ShapeDtypeStruct((bsz, RWKV_PAIRS, DH_B, LANES), F32)],
        scratch_shapes=[pltpu.VMEM((nb * RWKV_PAIRS * DH_B, LANES), F32)] * 3,
        compiler_params=_cparams(("parallel", "arbitrary")),
        name="rwkv_scan",
    )(tiled(r), tiled(w), tiled(k), tiled(v), tiled(a), tiled(b), s0)
    return y.reshape(bsz, t, hd), s_fin


def _rwkv_post_body(y_ref, r_ref, k_ref, v_ref, g_ref, lg_ref, lb_ref, rk_ref, o_ref):
    hd = H_B * DH_B
    avg = _head_sum_matrix(hd, DH_B) * (1.0 / DH_B)
    y = y_ref[...]
    mu = jnp.dot(y, avg, precision=HI, preferred_element_type=F32)
    d = y - mu
    var = jnp.dot(d * d, avg, precision=HI, preferred_element_type=F32)
    yn = d * lax.rsqrt(var + LNX_EPS) * lg_ref[...] + lb_ref[...]
    bonus = jnp.dot(r_ref[...] * k_ref[...] * rk_ref[...], avg * float(DH_B), precision=HI,
                    preferred_element_type=F32)
    o_ref[...] = (yn + bonus * v_ref[...]) * g_ref[...]


def rwkv_post_call(y, r, k, v, g, lnx_g, lnx_b, rk):
    m, hd = y.shape
    tm = _row_tile(m, 512)
    tok = pl.BlockSpec((tm, hd), lambda i: (i, 0))
    row = pl.BlockSpec((1, hd), lambda i: (0, 0))
    return pl.pallas_call(
        _rwkv_post_body,
        grid=(m // tm,),
        in_specs=[tok] * 5 + [row] * 3,
        out_specs=tok,
        out_shape=jax.ShapeDtypeStruct((m, hd), F32),
        compiler_params=_cparams(("parallel",)),
        name="rwkv_post",
    )(y, r, k, v, g, lnx_g, lnx_b, rk)


def _head_mask(n, h, width):
    lane = lax.broadcasted_iota(I32, (1, n), 1)
    return jnp.where(lane >= h * width, jnp.where(lane < (h + 1) * width, 1.0, 0.0), 0.0).astype(F32)


def _head_norm128(y, g):
    mu = jnp.mean(y, axis=-1, keepdims=True)
    d = y - mu
    var = jnp.mean(d * d, axis=-1, keepdims=True)
    return d * lax.rsqrt(var + HN_EPS) * g


def _ret_body(qk_ref, v_ref, gc_ref, cos_ref, sin_ref, gn_ref, s0_ref, y_ref, sout_ref, s_scr, *, l_valid):
    c = pl.program_id(1)
    lc = qk_ref.shape[1]
    hk = H_C * DK_C

    @pl.when(c == 0)
    def _():
        s_scr[...] = s0_ref[0]

    rr = lax.broadcasted_iota(I32, (hk, hk), 0)
    cc = lax.broadcasted_iota(I32, (hk, hk), 1)
    half = DK_C // 2
    same = (rr // DK_C) == (cc // DK_C)
    dr = rr & (DK_C - 1)
    dc = cc & (DK_C - 1)
    rot = jnp.where(same, jnp.where(dr == dc + half, -1.0, jnp.where(dr + half == dc, 1.0, 0.0)), 0.0).astype(F32)

    qk = qk_ref[0]
    cos = cos_ref[...]
    sin = sin_ref[...]

    def rope(x):
        return x * cos + jnp.dot(x, rot, precision=HI, preferred_element_type=F32) * sin

    qr = rope(qk[:, :hk])
    kr = rope(qk[:, hk:]) * DK_C ** -0.5
    krb = kr.astype(BF16)
    s_prev = s_scr[...]
    s_prev_b = s_prev.astype(BF16)

    jj = lax.broadcasted_iota(I32, (lc, lc), 0)
    ss = lax.broadcasted_iota(I32, (lc, lc), 1)
    diff = (jj - ss).astype(F32)
    jcol = lax.broadcasted_iota(I32, (lc, 1), 0).astype(F32)
    srow_state = lax.broadcasted_iota(I32, (hk, 1), 0) // DK_C
    s_new = jnp.zeros_like(s_prev)
    decay_rows = jnp.zeros((hk, 1), F32)
    for h in range(H_C):
        lg = math.log1p(-2.0 ** (-5.0 - h))
        mh = _head_mask(hk, h, DK_C)
        qm = (qr * mh).astype(BF16)
        dmask = jnp.where(diff >= 0, jnp.exp(jnp.maximum(diff, 0.0) * lg), 0.0)
        scores = lax.dot_general(qm, krb, NT, preferred_element_type=F32) * dmask
        vh = v_ref[0, :, h * DV_C:(h + 1) * DV_C]
        vhb = vh.astype(BF16)
        intra = jnp.dot(scores.astype(BF16), vhb, preferred_element_type=F32)
        inter = jnp.dot(qm, s_prev_b, preferred_element_type=F32) * jnp.exp((jcol + 1.0) * lg)
        yh = _head_norm128(intra + inter, gn_ref[h:h + 1, :])
        gch = gc_ref[0, :, h * DV_C:(h + 1) * DV_C]
        y_ref[0, :, h * DV_C:(h + 1) * DV_C] = yh * (gch * jax.nn.sigmoid(gch))
        w_s = jnp.where(jcol < l_valid, jnp.exp((l_valid - 1.0 - jcol) * lg), 0.0)
        kw = (kr * mh * w_s).astype(BF16)
        s_new = s_new + lax.dot_general(kw, vhb, TN, preferred_element_type=F32)
        decay_rows = jnp.where(srow_state == h, math.exp(l_valid * lg), decay_rows)
    s_fin = decay_rows * s_prev + s_new
    s_scr[...] = s_fin

    @pl.when(c == pl.num_programs(1) - 1)
    def _():
        sout_ref[0] = s_fin


def retention_call(za, cos, sin, gn, s0, l_valid):
    b, t, _ = za.shape
    lc = _row_tile(t, 128)
    assert l_valid == lc or t == lc
    hk, hv = H_C * DK_C, H_C * DV_C
    blk = lambda j: pl.BlockSpec((1, lc, 2 * hk), lambda i, c, j=j: (i, c, j))
    st = pl.BlockSpec((1, hk, DV_C), lambda i, c: (i, 0, 0))
    return pl.pallas_call(
        functools.partial(_ret_body, l_valid=l_valid),
        grid=(b, t // lc),
        in_specs=[blk(0), blk(1), blk(2),
                  pl.BlockSpec((lc, hk), lambda i, c: (c, 0)),
                  pl.BlockSpec((lc, hk), lambda i, c: (c, 0)),
                  pl.BlockSpec((H_C, DV_C), lambda i, c: (0, 0)),
                  st],
        out_specs=[pl.BlockSpec((1, lc, hv), lambda i, c: (i, c, 0)), st],
        out_shape=[jax.ShapeDtypeStruct((b, t, hv), F32), jax.ShapeDtypeStruct((b, hk, DV_C), F32)],
        scratch_shapes=[pltpu.VMEM((hk, DV_C), F32)],
        compiler_params=_cparams(("parallel", "arbitrary")),
        name="retention",
    )(za, za, za, cos, sin, gn, s0)


def _mlstm_body(qk_ref, v_ref, og_ref, gt_ref, cb_ref, cw_ref, cbias_ref, gbias_ref, gn_ref,
                c0_ref, n0_ref, m0_ref, y_ref, cout_ref, nout_ref, mout_ref,
                ext_scr, c_scr, n_scr, m_scr, *, l_valid):
    c = pl.program_id(1)
    lc = qk_ref.shape[1]
    hk = H_D * DK_D
    pad = SUBLANES

    @pl.when(c == 0)
    def _():
        ext_scr[0:pad, :] = cb_ref[0]
        c_scr[...] = c0_ref[0]
        n_scr[...] = n0_ref[0]
        m_scr[...] = m0_ref[0]

    u = qk_ref[0]
    ext_scr[pad:pad + lc, :] = u
    acc = cbias_ref[...] + u * cw_ref[CONV_W - 1:CONV_W, :]
    for i in range(CONV_W - 1):
        sh = CONV_W - 1 - i
        acc = acc + ext_scr[pad - sh:pad - sh + lc, :] * cw_ref[i:i + 1, :]
    ext_scr[0:pad, :] = u[lc - pad:lc, :]
    qkc = acc * jax.nn.sigmoid(acc)
    q = qkc[:, :hk]
    k = qkc[:, hk:] * DK_D ** -0.5
    kb = k.astype(BF16)

    gates = gt_ref[0] + gbias_ref[...]
    logsig = -_softplus(-gates)
    rowi = lax.broadcasted_iota(I32, (lc, LANES), 0)
    logsig = jnp.where(rowi < l_valid, logsig, 0.0)
    jj = lax.broadcasted_iota(I32, (lc, lc), 0)
    ss = lax.broadcasted_iota(I32, (lc, lc), 1)
    tril = jnp.where(jj >= ss, 1.0, 0.0).astype(F32)
    bcum = jnp.dot(tril, logsig, precision=HI, preferred_element_type=F32)
    lane_g = lax.broadcasted_iota(I32, (lc, LANES), 1)
    jcol = lax.broadcasted_iota(I32, (lc, 1), 0)
    causal = jj >= ss
    ninf = -jnp.inf

    c_prev = c_scr[...]
    c_prev_b = c_prev.astype(BF16)
    n_prev = n_scr[...]
    m_prev = m_scr[...]
    c_new = jnp.zeros_like(c_prev)
    carry_rows = jnp.zeros((hk, 1), F32)
    carry_lanes = jnp.zeros((1, hk), F32)
    ws_full = jnp.zeros((lc, hk), F32)
    m_out = m_prev
    srow_state = lax.broadcasted_iota(I32, (hk, 1), 0) // DK_D
    lane_state = lax.broadcasted_iota(I32, (1, hk), 1) // DK_D
    lane_m = lax.broadcasted_iota(I32, (1, LANES), 1)
    for h in range(H_D):
        mh = _head_mask(hk, h, DK_D)
        e_i = jnp.where(lane_g == h, 1.0, 0.0).astype(F32)
        e_f = jnp.where(lane_g == H_D + h, 1.0, 0.0).astype(F32)
        logi_col = jnp.where(jcol < l_valid, gates[:, h:h + 1], ninf)
        b_col = bcum[:, H_D + h:H_D + h + 1]
        i_row = lax.dot_general(e_i, gates, NT, precision=HI, preferred_element_type=F32)
        i_row = jnp.where(ss < l_valid, i_row, ninf)
        b_row = lax.dot_general(e_f, bcum, NT, precision=HI, preferred_element_type=F32)
        m_h = m_prev[:, h:h + 1]
        inter = b_col + m_h
        dmat = jnp.where(causal, b_col - b_row + i_row, ninf)
        m_j = jnp.maximum(inter, jnp.max(dmat, axis=-1, keepdims=True))
        qm = q * mh
        qmb = qm.astype(BF16)
        amat = jnp.exp(dmat - m_j) * lax.dot_general(qmb, kb, NT, preferred_element_type=F32)
        sc = jnp.exp(inter - m_j)
        vh = v_ref[0, :, h * DV_D:(h + 1) * DV_D]
        vhb = vh.astype(BF16)
        num = jnp.dot(amat.astype(BF16), vhb, preferred_element_type=F32) \
            + sc * jnp.dot(qmb, c_prev_b, preferred_element_type=F32)
        den = jnp.sum(amat, axis=-1, keepdims=True) + sc * jnp.sum(qm * n_prev, axis=-1, keepdims=True)
        hh = num / jnp.maximum(jnp.abs(den), jnp.exp(-m_j))
        ogh = og_ref[0, :, h * DV_D:(h + 1) * DV_D]
        y_ref[0, :, h * DV_D:(h + 1) * DV_D] = _head_norm128(hh, gn_ref[h:h + 1, :]) * jax.nn.sigmoid(ogh)
        b_last = b_col[l_valid - 1:l_valid, :]
        gs = b_last - b_col + logi_col
        m_new = jnp.maximum(b_last + m_h, jnp.max(gs, axis=0, keepdims=True))
        ws = jnp.exp(gs - m_new)
        carry = jnp.exp(b_last + m_h - m_new)
        c_new = c_new + lax.dot_general((k * mh).astype(BF16), (vh * ws).astype(BF16), TN,
                                        preferred_element_type=F32)
        carry_rows = jnp.where(srow_state == h, carry, carry_rows)
        carry_lanes = jnp.where(lane_state == h, carry, carry_lanes)
        ws_full = ws_full + ws * mh
        m_out = jnp.where(lane_m == h, m_new, m_out)
    c_fin = carry_rows * c_prev + c_new
    n_fin = carry_lanes * n_prev + jnp.sum(ws_full * k, axis=0, keepdims=True)
    c_scr[...] = c_fin
    n_scr[...] = n_fin
    m_scr[...] = m_out

    @pl.when(c == pl.num_programs(1) - 1)
    def _():
        cout_ref[0] = c_fin
        nout_ref[0] = n_fin
        mout_ref[0] = m_out


def mlstm_call(zb, zc, conv_buf, conv_w, conv_b, gate_bias, gn, c0, n0, m0, l_valid):
    b, t, _ = zb.shape
    lc = _row_tile(t, 128)
    assert l_valid == lc or t == lc
    hk, hv = H_D * DK_D, H_D * DV_D
    blk = lambda j: pl.BlockSpec((1, lc, 2 * hk), lambda i, c, j=j: (i, c, j))
    cst = lambda s: pl.BlockSpec(s, lambda i, c: (0,) * len(s))
    per_b = lambda s: pl.BlockSpec((1,) + s, lambda i, c: (i,) + (0,) * len(s))
    return pl.pallas_call(
        functools.partial(_mlstm_body, l_valid=l_valid),
        grid=(b, t // lc),
        in_specs=[blk(0), blk(1), blk(2),
                  pl.BlockSpec((1, lc, LANES), lambda i, c: (i, c, 0)),
                  per_b((SUBLANES, 2 * hk)),
                  cst((CONV_W, 2 * hk)), cst((1, 2 * hk)), cst((1, LANES)), cst((H_D, DV_D)),
                  per_b((hk, DV_D)), per_b((1, hk)), per_b((1, LANES))],
        out_specs=[pl.BlockSpec((1, lc, hv), lambda i, c: (i, c, 0)),
                   per_b((hk, DV_D)), per_b((1, hk)), per_b((1, LANES))],
        out_shape=[jax.ShapeDtypeStruct((b, t, hv), F32),
                   jax.ShapeDtypeStruct((b, hk, DV_D), F32),
                   jax.ShapeDtypeStruct((b, 1, hk), F32),
                   jax.ShapeDtypeStruct((b, 1, LANES), F32)],
        scratch_shapes=[pltpu.VMEM((SUBLANES + lc, 2 * hk), F32),
                        pltpu.VMEM((hk, DV_D), F32),
                        pltpu.VMEM((1, hk), F32),
                        pltpu.VMEM((1, LANES), F32)],
        compiler_params=_cparams(("parallel", "arbitrary")),
        name="mlstm",
    )(zb, zb, zb, zc, conv_buf, conv_w, conv_b, gate_bias, gn, c0, n0, m0)


def _pad_cols(w, n):
    return jnp.pad(w, ((0, 0), (0, n - w.shape[1])))


def _pad_time(a, tp):
    t = a.shape[1]
    if t == tp:
        return a
    return jnp.pad(a, ((0, 0), (0, tp - t)) + ((0, 0),) * (a.ndim - 2))


def _heads_major(a, h):
    b, t, _ = a.shape
    return a.reshape(b, t, h, -1).transpose(0, 2, 1, 3)


def _ab_mixer(xf, b, t, e, g_pre, P, prompt, shift_prev, s0, cache):
    m = b * t
    hd = H_A * DH_A
    w = P['ab_w_in'][e]
    o_qi = 3 * hd
    o_ki = o_qi + H_IDX * D_IDX
    o_wi = o_ki + D_IDX
    o_zb = o_wi + H_IDX
    qkv = proj_in_call(xf, g_pre, w[:, :o_qi].astype(BF16))
    qi = proj_in_call(xf, g_pre, w[:, o_qi:o_ki].astype(BF16), out_dtype=BF16)
    kw, ki2 = kiwi_call(xf, g_pre, w[:, o_ki:o_wi], w[:, o_wi:o_zb], P['kidx_g'][e])
    zb = proj_in_call(xf, g_pre, w[:, o_zb:].astype(BF16))
    q3 = qkv[:, :hd].reshape(b, t, hd)
    k3 = qkv[:, hd:2 * hd].reshape(b, t, hd)
    v3 = qkv[:, 2 * hd:].reshape(b, t, hd)
    ki3 = kw[:, :D_IDX].reshape(b, t, D_IDX)
    wi3 = kw[:, D_IDX:D_IDX + H_IDX].reshape(b, t, H_IDX)

    if prompt:
        ya = dsa_prompt_call(qkv.reshape(b, t, 3 * hd), qi.reshape(b, t, H_IDX * D_IDX),
                             ki2.reshape(b, t, LANES), kw.reshape(b, t, LANES)).reshape(m, hd)
    else:
        cache_k, cache_v, cache_kidx, page_table = cache
        n_pool = cache_k.shape[1]
        nq = SUBLANES
        qi_s = _pad_time(qi.reshape(b, t, H_IDX * D_IDX), nq).reshape(b, nq * H_IDX, D_IDX)
        wi_s = _pad_time(wi3, nq).reshape(b, 1, nq * H_IDX)
        keys_last = lambda a: jnp.moveaxis(a, 1, -1)
        new_page = lambda a: jnp.pad(keys_last(a), ((0, 0),) * (a.ndim - 1) + ((0, PAGE_SIZE - t),))
        isc = dsa_sidx_call(page_table, qi_s, wi_s, new_page(ki3), keys_last(cache_kidx[e]))
        ya = dsa_satt_call(page_table, isc, _pad_time(q3, nq),
                           new_page(k3.reshape(b, t, H_A, DH_A)), new_page(v3.reshape(b, t, H_A, DH_A)),
                           keys_last(cache_k[e]), keys_last(cache_v[e]), t)
        ya = ya[:, :t].reshape(m, hd)

    tp = -(-t // SUBLANES) * SUBLANES
    hb = H_B * DH_B
    zb3 = zb.reshape(b, t, D_B_IN)
    zpad = jnp.zeros((R_DECAY, hb), F32)
    w2p = jnp.concatenate([P['rwkv_w2'][e], zpad], axis=0).astype(BF16)
    a2p = jnp.concatenate([zpad, P['rwkv_a2'][e]], axis=0).astype(BF16)
    row = lambda a: a.reshape(1, -1)
    r, dec, k2, v, an, bb, g = rwkv_prep_call(
        _pad_time(zb3, tp), shift_prev.reshape(b, 1, D_B_IN), row(P['rwkv_mu'][e]), row(P['rwkv_w0'][e]),
        row(P['rwkv_a0'][e]), w2p, a2p, P['rwkv_g2'][e].astype(BF16), row(P['rwkv_kk'][e]), row(P['rwkv_ka'][e]))
    s0p = s0.reshape(b, RWKV_PAIRS, 2, DH_B, DH_B).transpose(0, 1, 3, 2, 4).reshape(b, RWKV_PAIRS, DH_B, LANES)
    y, s_fin = rwkv_scan_call(r, dec, k2, v, an, bb, s0p, t)
    fl = lambda a: a.reshape(b * tp, hb)
    rk = jnp.broadcast_to(P['rwkv_rk'][e], (H_B, DH_B)).reshape(1, hb)
    yb = rwkv_post_call(fl(y), fl(r), fl(k2), fl(v), fl(g), row(P['rwkv_lnx_g'][e]), row(P['rwkv_lnx_b'][e]), rk)
    yb = yb.reshape(b, tp, hb)[:, :t].reshape(m, hb)
    s_new = s_fin.reshape(b, RWKV_PAIRS, DH_B, 2, DH_B).transpose(0, 1, 3, 2, 4).reshape(b, H_B, DH_B, DH_B)
    st = (k3.reshape(b, t, H_A, DH_A), v3.reshape(b, t, H_A, DH_A), ki3, s_new, zb3[:, t - 1])
    return ya, yb, st


def _cd_mixer(xf, b, t, o, g_pre, pos, P, ret_s, m_c, m_n, m_m, conv_buf):
    w = P['cd_w_in'][o]
    hk, hv = H_C * DK_C, H_C * DV_C
    o_g = 2 * hk + 2 * hv
    o_vd = o_g + 2 * H_D * DK_D
    o_ig = o_vd + H_D * DV_D
    o_og = o_ig + 2 * H_D
    za = proj_in_call(xf, g_pre, w[:, :o_g].astype(BF16))
    zb = proj_in_call(xf, g_pre, jnp.concatenate([w[:, o_g:o_ig], w[:, o_og:]], axis=1).astype(BF16))
    zc = proj_in_call(xf, g_pre, _pad_cols(w[:, o_ig:o_og], LANES).astype(BF16))
    tp = -(-t // SUBLANES) * SUBLANES
    l_valid = t if tp != t else _row_tile(t, 128)
    za3 = _pad_time(za.reshape(b, t, -1), tp)
    zb3 = zb.reshape(b, t, -1)
    zc3 = _pad_time(zc.reshape(b, t, -1), tp)

    half = DK_C // 2
    inv = ROPE_BASE ** (-jnp.arange(half, dtype=F32) / half)
    ang = _pad_time(pos.astype(F32)[None], tp)[0][:, None] * inv[None, :]
    cos = jnp.tile(jnp.cos(ang), (1, 2 * H_C))
    sin = jnp.tile(jnp.sin(ang), (1, 2 * H_C))
    yc, ret_new = retention_call(za3, cos, sin, P['ret_gn'][o], ret_s.astype(F32).reshape(b, hk, DV_C), l_valid)

    hkd = H_D * DK_D
    cb = jnp.pad(conv_buf.astype(F32), ((0, 0), (SUBLANES - (CONV_W - 1), 0), (0, 0)))
    gate_bias = jnp.pad(P['mlstm_if_b'][o].astype(F32).reshape(1, 2 * H_D), ((0, 0), (0, LANES - 2 * H_D)))
    c0 = m_c.astype(F32).transpose(0, 1, 3, 2).reshape(b, hkd, DV_D)
    n0 = m_n.astype(F32).reshape(b, 1, hkd)
    m0 = jnp.pad(m_m.astype(F32), ((0, 0), (0, LANES - H_D))).reshape(b, 1, LANES)
    yd, c_new, n_new, m_new = mlstm_call(_pad_time(zb3, tp), zc3, cb, P['conv_w'][o], P['conv_b'][o].reshape(1, -1),
                                         gate_bias, P['mlstm_gn'][o], c0, n0, m0, l_valid)
    conv_new = jnp.concatenate([conv_buf.astype(F32), zb3[:, :, :2 * hkd]], axis=1)[:, -(CONV_W - 1):]
    st = (ret_new.reshape(b, H_C, DK_C, DV_C),
          c_new.reshape(b, H_D, DK_D, DV_D).transpose(0, 1, 3, 2),
          n_new.reshape(b, H_D, DK_D),
          m_new.reshape(b, LANES)[:, :H_D],
          conv_new)
    m = b * t
    return yc[:, :t].reshape(m, hv), yd[:, :t].reshape(m, H_D * DV_D), st


def _trunk(x, pos, prompt, ab_init, cd_init, cache, P):
    b, t, d = x.shape
    xf = x.reshape(b * t, d)
    depth = P['norm_g'].shape[0]
    ab_new, cd_new = [], []
    for l in range(depth):
        g = P['norm_g'][l]
        bf = lambda a: a.astype(BF16)
        xf = ffn_call(xf, g[0], g[1], bf(P['ffn_wg'][l, 0]), bf(P['ffn_wu'][l, 0]), bf(P['ffn_wd'][l, 0]))
        if l % 2 == 0:
            e = l // 2
            shift_prev, s0 = ab_init(e)
            ya, yb, st = _ab_mixer(xf, b, t, e, g[2], P, prompt, shift_prev, s0, cache)
            ab_new.append(st)
            wo = bf(P['ab_w_out'][e])
            xf = proj_out_call(ya, yb, xf, g[3], wo[:H_A * DH_A], wo[H_A * DH_A:])
        else:
            o = l // 2
            yc, yd, st = _cd_mixer(xf, b, t, o, g[2], pos, P, *cd_init(o))
            cd_new.append(st)
            wo = bf(P['cd_w_out'][o])
            xf = proj_out_call(yc, yd, xf, g[3], wo[:H_C * DV_C], wo[H_C * DV_C:])
        xf = ffn_call(xf, g[4], g[5], bf(P['ffn_wg'][l, 1]), bf(P['ffn_wu'][l, 1]), bf(P['ffn_wd'][l, 1]))
    ab = tuple(jnp.stack(s) for s in zip(*ab_new))
    cd = tuple(jnp.stack(s) for s in zip(*cd_new))
    return xf.reshape(b, t, d), ab, cd


def kernel(x_prompt, x_sample, cache_k, cache_v, cache_kidx, state_rwkv, state_shift, state_ret, state_mlstm_C, state_mlstm_n, state_mlstm_m, state_conv, page_table, norm_g, ffn_wg, ffn_wu, ffn_wd, ab_w_in, ab_w_out, kidx_g, rwkv_mu, rwkv_w0, rwkv_w2, rwkv_a0, rwkv_a2, rwkv_g2, rwkv_kk, rwkv_ka, rwkv_rk, rwkv_lnx_g, rwkv_lnx_b, cd_w_in, cd_w_out, ret_gn, conv_w, conv_b, mlstm_if_b, mlstm_gn):
    P = dict(norm_g=norm_g, ffn_wg=ffn_wg, ffn_wu=ffn_wu, ffn_wd=ffn_wd, ab_w_in=ab_w_in, ab_w_out=ab_w_out,
             kidx_g=kidx_g, rwkv_mu=rwkv_mu, rwkv_w0=rwkv_w0, rwkv_w2=rwkv_w2, rwkv_a0=rwkv_a0, rwkv_a2=rwkv_a2,
             rwkv_g2=rwkv_g2, rwkv_kk=rwkv_kk, rwkv_ka=rwkv_ka, rwkv_rk=rwkv_rk, rwkv_lnx_g=rwkv_lnx_g,
             rwkv_lnx_b=rwkv_lnx_b, cd_w_in=cd_w_in, cd_w_out=cd_w_out, ret_gn=ret_gn, conv_w=conv_w,
             conv_b=conv_b, mlstm_if_b=mlstm_if_b, mlstm_gn=mlstm_gn)
    B, T, _ = x_prompt.shape
    DB, DS, _ = x_sample.shape
    past = page_table.shape[1] * PAGE_SIZE

    def ab_zero(e):
        return (jnp.zeros((B, D_B_IN), F32), jnp.zeros((B, H_B, DH_B, DH_B), F32))

    def cd_zero(o):
        return (jnp.zeros((B, H_C, DK_C, DV_C), F32), jnp.zeros((B, H_D, DV_D, DK_D), F32),
                jnp.zeros((B, H_D, DK_D), F32), jnp.zeros((B, H_D), F32),
                jnp.zeros((B, CONV_W - 1, 2 * H_D * DK_D), F32))

    def ab_cached(e):
        return (state_shift[e], state_rwkv[e])

    def cd_cached(o):
        return (state_ret[o], state_mlstm_C[o], state_mlstm_n[o], state_mlstm_m[o], state_conv[o])

    y_p, (kp, vp, kip, rwp, shp), (rtp, cp, nvp, mp, cvp) = _trunk(
        x_prompt, jnp.arange(T), True, ab_zero, cd_zero, None, P)
    y_s, (ks_, vs_, kis, rws, shs), (rts, cs, nvs, ms, cvs) = _trunk(
        x_sample, past + jnp.arange(DS), False, ab_cached, cd_cached,
        (cache_k, cache_v, cache_kidx, page_table), P)
    return (y_p, y_s, kp, vp, kip, rwp, shp, rtp, cp, nvp, mp, cvp,
            ks_, vs_, kis, rws, shs, rts, cs, nvs, ms, cvs)
```

```python
import functools
import math

import numpy as np
import jax
import jax.numpy as jnp
from jax import lax
from jax.experimental import pallas as pl
from jax.experimental.pallas import tpu as pltpu

F32 = jnp.float32
BF16 = jnp.bfloat16
I32 = jnp.int32
HI = lax.Precision.HIGHEST

LANES = 128
SUBLANES = 8
VMEM_LIMIT = 56 * 1024 * 1024

EPS = 1e-6
PAGE_SIZE = 128
H_A, DH_A, H_IDX, D_IDX, TOPK_MAX = 8, 64, 16, 64, 256
H_B, DH_B, R_DECAY, R_AAA, R_GATE = 8, 64, 64, 64, 128
D_B_IN = 3 * H_B * DH_B + R_DECAY + R_AAA + R_GATE
LNX_EPS = 64e-5
H_C, DK_C, DV_C, ROPE_BASE = 4, 64, 128, 10000.0
H_D, DK_D, DV_D, CONV_W = 4, 64, 128, 4
HN_EPS = 1e-5
NEG_BIG = -1e30
INT_MIN = -(2 ** 31)
INT_MAX = 2 ** 31 - 1
KEY_NEG_INF = (0xFF800000 ^ 0x7FFFFFFF) - 2 ** 32

NT = (((1,), (1,)), ((), ()))
TN = (((0,), (0,)), ((), ()))


def _cparams(sem):
    return pltpu.CompilerParams(dimension_semantics=sem, vmem_limit_bytes=VMEM_LIMIT)


def _rms(x, g):
    return x * lax.rsqrt(jnp.mean(x * x, axis=-1, keepdims=True) + EPS) * g


def _softplus(x):
    return jnp.maximum(x, 0.0) + jnp.log(1.0 + jnp.exp(-jnp.abs(x)))


def _row_tile(m, want):
    t = min(want, m)
    while m % t:
        t //= 2
    return t


def _col_tile(n, want):
    best = LANES
    for t in range(LANES, min(n, want) + 1, LANES):
        if n % t == 0:
            best = t
    return best


def _ffn_body(x_ref, g0_ref, g1_ref, wg_ref, wu_ref, wd_ref, o_ref, h_scr, acc_scr):
    j = pl.program_id(1)

    @pl.when(j == 0)
    def _():
        h_scr[...] = _rms(x_ref[...], g0_ref[...]).astype(BF16)
        acc_scr[...] = jnp.zeros_like(acc_scr)

    h = h_scr[...]
    g = jnp.dot(h, wg_ref[...], preferred_element_type=F32)
    u = jnp.dot(h, wu_ref[...], preferred_element_type=F32)
    a = g * jax.nn.sigmoid(g) * u
    acc_scr[...] += jnp.dot(a.astype(BF16), wd_ref[...], preferred_element_type=F32)

    @pl.when(j == pl.num_programs(1) - 1)
    def _():
        o_ref[...] = x_ref[...] + 0.5 * _rms(acc_scr[...], g1_ref[...])


def ffn_call(x, g0, g1, wg, wu, wd):
    m, d = x.shape
    ff = wg.shape[1]
    tm = _row_tile(m, 512)
    tf = _col_tile(ff, 1408)
    return pl.pallas_call(
        _ffn_body,
        grid=(m // tm, ff // tf),
        in_specs=[
            pl.BlockSpec((tm, d), lambda i, j: (i, 0)),
            pl.BlockSpec((1, d), lambda i, j: (0, 0)),
            pl.BlockSpec((1, d), lambda i, j: (0, 0)),
            pl.BlockSpec((d, tf), lambda i, j: (0, j)),
            pl.BlockSpec((d, tf), lambda i, j: (0, j)),
            pl.BlockSpec((tf, d), lambda i, j: (j, 0)),
        ],
        out_specs=pl.BlockSpec((tm, d), lambda i, j: (i, 0)),
        out_shape=jax.ShapeDtypeStruct((m, d), F32),
        scratch_shapes=[pltpu.VMEM((tm, d), BF16), pltpu.VMEM((tm, d), F32)],
        compiler_params=_cparams(("parallel", "arbitrary")),
        name="ffn",
    )(x, g0.reshape(1, d), g1.reshape(1, d), wg, wu, wd)


def _proj_in_body(x_ref, g_ref, w_ref, o_ref, h_scr):
    @pl.when(pl.program_id(1) == 0)
    def _():
        h_scr[...] = _rms(x_ref[...], g_ref[...]).astype(BF16)

    o_ref[...] = jnp.dot(h_scr[...], w_ref[...], preferred_element_type=F32).astype(o_ref.dtype)


def proj_in_call(x, g, w, out_dtype=F32):
    m, d = x.shape
    n = w.shape[1]
    tm = _row_tile(m, 512)
    tn = _col_tile(n, 2048)
    return pl.pallas_call(
        _proj_in_body,
        grid=(m // tm, n // tn),
        in_specs=[
            pl.BlockSpec((tm, d), lambda i, j: (i, 0)),
            pl.BlockSpec((1, d), lambda i, j: (0, 0)),
            pl.BlockSpec((d, tn), lambda i, j: (0, j)),
        ],
        out_specs=pl.BlockSpec((tm, tn), lambda i, j: (i, j)),
        out_shape=jax.ShapeDtypeStruct((m, n), out_dtype),
        scratch_shapes=[pltpu.VMEM((tm, d), BF16)],
        compiler_params=_cparams(("parallel", "arbitrary")),
        name="proj_in",
    )(x, g.reshape(1, d), w)


def _kiwi_body(x_ref, g_ref, w_ref, kg_ref, o_ref, k2_ref):
    h = _rms(x_ref[...], g_ref[...]).astype(BF16)
    z = jnp.dot(h, w_ref[...], preferred_element_type=F32)
    z1 = z[:, :LANES]
    lane = lax.broadcasted_iota(I32, z1.shape, 1)
    is_k = lane < D_IDX
    ms = jnp.sum(jnp.where(is_k, z1 * z1, 0.0), axis=-1, keepdims=True) * (1.0 / D_IDX)
    inv = lax.rsqrt(ms + EPS)
    kg = kg_ref[...]
    o_ref[...] = jnp.where(is_k, z1 * inv * kg[:, :LANES], z1 * (H_IDX * D_IDX) ** -0.5)
    k2_ref[...] = (z[:, LANES:] * inv * kg[:, LANES:]).astype(BF16)


def kiwi_call(x, g, w_ki, w_wi, kidx_g):
    m, d = x.shape
    tm = _row_tile(m, 512)
    zpad = jnp.zeros((d, LANES - D_IDX - H_IDX), F32)
    w = jnp.concatenate([w_ki, w_wi, zpad, w_ki, w_ki], axis=1).astype(BF16)
    gpad = jnp.zeros((LANES - D_IDX,), F32)
    kg = jnp.concatenate([kidx_g, gpad, kidx_g, kidx_g]).reshape(1, 2 * LANES)
    return pl.pallas_call(
        _kiwi_body,
        grid=(m // tm,),
        in_specs=[
            pl.BlockSpec((tm, d), lambda i: (i, 0)),
            pl.BlockSpec((1, d), lambda i: (0, 0)),
            pl.BlockSpec((d, 2 * LANES), lambda i: (0, 0)),
            pl.BlockSpec((1, 2 * LANES), lambda i: (0, 0)),
        ],
        out_specs=[pl.BlockSpec((tm, LANES), lambda i: (i, 0))] * 2,
        out_shape=[jax.ShapeDtypeStruct((m, LANES), F32), jax.ShapeDtypeStruct((m, LANES), BF16)],
        compiler_params=_cparams(("parallel",)),
        name="kiwi",
    )(x, g.reshape(1, d), w, kg)


def _proj_out_body(a1_ref, a2_ref, x_ref, g_ref, w1_ref, w2_ref, o_ref):
    y = jnp.dot(a1_ref[...].astype(BF16), w1_ref[...], preferred_element_type=F32)
    y = y + jnp.dot(a2_ref[...].astype(BF16), w2_ref[...], preferred_element_type=F32)
    o_ref[...] = x_ref[...] + _rms(y, g_ref[...])


def proj_out_call(a1, a2, x, g, w1, w2):
    m, d = x.shape
    k1, k2 = a1.shape[1], a2.shape[1]
    tm = _row_tile(m, 512)
    return pl.pallas_call(
        _proj_out_body,
        grid=(m // tm,),
        in_specs=[
            pl.BlockSpec((tm, k1), lambda i: (i, 0)),
            pl.BlockSpec((tm, k2), lambda i: (i, 0)),
            pl.BlockSpec((tm, d), lambda i: (i, 0)),
            pl.BlockSpec((1, d), lambda i: (0, 0)),
            pl.BlockSpec((k1, d), lambda i: (0, 0)),
            pl.BlockSpec((k2, d), lambda i: (0, 0)),
        ],
        out_specs=pl.BlockSpec((tm, d), lambda i: (i, 0)),
        out_shape=jax.ShapeDtypeStruct((m, d), F32),
        compiler_params=_cparams(("parallel",)),
        name="proj_out",
    )(a1, a2, x, g.reshape(1, d), w1, w2)


def _topk_bias(isc, col, topk, nbits_col):
    bits = pltpu.bitcast(isc, I32)
    key = jnp.where(bits < 0, bits ^ 0x7FFFFFFF, bits)
    kf = float(topk)

    def count(mask):
        return jnp.sum(jnp.where(mask, 1.0, 0.0), axis=-1, keepdims=True)

    prefix = jnp.where(count(key >= 0) >= kf, 0, INT_MIN).astype(I32)

    def bit_step(i, prefix):
        cand = prefix + jnp.left_shift(jnp.int32(1), 30 - i)
        return jnp.where(count(key >= cand) >= kf, cand, prefix)

    thr = lax.fori_loop(0, 31, bit_step, prefix)
    gt = key > thr
    eq = key == thr
    need = kf - count(gt)

    def col_step(i, y):
        cand = y + jnp.left_shift(jnp.int32(1), nbits_col - 1 - i)
        c = jnp.sum(jnp.where(eq, jnp.where(col < cand, 1.0, 0.0), 0.0), axis=-1, keepdims=True)
        return jnp.where(c < need, cand, y)

    tied = jnp.where(count(key >= thr) > kf, jnp.where(thr > KEY_NEG_INF, 1.0, 0.0), 0.0)
    y = lax.cond(jnp.max(tied) > 0.0,
                 lambda: lax.fori_loop(0, nbits_col, col_step, jnp.zeros_like(thr)),
                 lambda: jnp.full_like(thr, INT_MAX))
    ninf = -jnp.inf
    bias = jnp.where(gt, 0.0, jnp.where(eq, jnp.where(col <= y, 0.0, ninf), ninf))
    return jnp.where(jnp.abs(isc) < jnp.inf, bias, ninf)


def _dsa_prompt_body(qkv_q_ref, qkv_k_ref, qkv_v_ref, qi_ref, ki2_ref, kw_ref, o_ref, kb_scr, vb_scr, *, topk):
    qb = qkv_q_ref.shape[1]
    t = qkv_k_ref.shape[1]
    j = pl.program_id(1)
    t0 = j * qb

    @pl.when(j == 0)
    def _():
        kb_scr[...] = qkv_k_ref[0].astype(BF16)
        vb_scr[...] = qkv_v_ref[0].astype(BF16)

    lane = lax.broadcasted_iota(I32, (1, LANES), 1)
    half = (jnp.where(lane < DH_A, 1.0, 0.0), jnp.where(lane < DH_A, 0.0, 1.0))
    half_b = tuple(m.astype(BF16) for m in half)
    ki2 = ki2_ref[0]
    kw = kw_ref[0]
    isc = jnp.zeros((qb, t), F32)
    for hp in range(H_IDX // 2):
        qp = qi_ref[0, :, hp * LANES:(hp + 1) * LANES]
        for h2 in range(2):
            h = 2 * hp + h2
            s = lax.dot_general(qp * half_b[h2], ki2, NT, preferred_element_type=F32)
            isc = isc + kw[:, D_IDX + h:D_IDX + h + 1] * jnp.maximum(s, 0.0)
    row = lax.broadcasted_iota(I32, (qb, t), 0) + t0
    col = lax.broadcasted_iota(I32, (qb, t), 1)
    isc = jnp.where(col <= row, isc, -jnp.inf)
    bias = _topk_bias(isc, col, topk, max(1, (t - 1).bit_length()))
    scale = DH_A ** -0.5
    assert math.frexp(scale)[0] == 0.5
    for p in range(H_A // 2):
        sl = slice(p * LANES, (p + 1) * LANES)
        qp = qkv_q_ref[0, :, sl]
        kp = kb_scr[:, sl]
        vp = vb_scr[:, sl]
        outs = []
        for h2 in range(2):
            qm = (qp * (half[h2] * scale)).astype(BF16)
            lg = lax.dot_general(qm, kp, NT, preferred_element_type=F32) + bias
            mx = jnp.max(lg, axis=-1, keepdims=True)
            pr = jnp.exp(lg - mx)
            l = jnp.sum(pr, axis=-1, keepdims=True)
            outs.append(jnp.dot(pr.astype(BF16), vp, preferred_element_type=F32) / l)
        o_ref[0, :, sl] = jnp.where(lane < DH_A, outs[0], outs[1])


def dsa_prompt_call(qkv, qi, ki2, kw):
    b, t, _ = qkv.shape
    hd = H_A * DH_A
    qb = _row_tile(t, 256)
    topk = min(TOPK_MAX, t // 4)
    return pl.pallas_call(
        functools.partial(_dsa_prompt_body, topk=topk),
        grid=(b, t // qb),
        in_specs=[
            pl.BlockSpec((1, qb, hd), lambda i, j: (i, j, 0)),
            pl.BlockSpec((1, t, hd), lambda i, j: (i, 0, 1)),
            pl.BlockSpec((1, t, hd), lambda i, j: (i, 0, 2)),
            pl.BlockSpec((1, qb, H_IDX * D_IDX), lambda i, j: (i, j, 0)),
            pl.BlockSpec((1, t, LANES), lambda i, j: (i, 0, 0)),
            pl.BlockSpec((1, qb, LANES), lambda i, j: (i, j, 0)),
        ],
        out_specs=pl.BlockSpec((1, qb, hd), lambda i, j: (i, j, 0)),
        out_shape=jax.ShapeDtypeStruct((b, t, hd), F32),
        scratch_shapes=[pltpu.VMEM((t, hd), BF16), pltpu.VMEM((t, hd), BF16)],
        compiler_params=_cparams(("parallel", "arbitrary")),
        name="dsa_prompt",
    )(qkv, qkv, qkv, qi, ki2, kw)


DSA_PAGE_GROUP = 8


def _page_group(n_pages):
    return math.gcd(n_pages, DSA_PAGE_GROUP)


def _page_specs(shape, n_pages, group):
    zeros = (0,) * (len(shape) - 1)

    def spec(g):
        return pl.BlockSpec(shape, lambda b, s, pt: (pt[b, jnp.minimum(s * group + g, n_pages - 1)],) + zeros)
    return [spec(g) for g in range(group)]


def _page_cat(refs, new_ref, is_new):
    pages = [r[0] for r in refs]
    pages[0] = jnp.where(is_new, new_ref[0], pages[0])
    return jnp.concatenate([p.reshape(-1, PAGE_SIZE).astype(BF16) for p in pages], axis=1)


def _dsa_sidx_body(pt_ref, qi_ref, wi_ref, kin_ref, *rest, n_steps):
    kc_refs, o_ref = rest[:-1], rest[-1]
    ki = _page_cat(kc_refs, kin_ref, pl.program_id(1) == n_steps)
    s = jnp.dot(qi_ref[0], ki, preferred_element_type=F32)
    nq = o_ref.shape[1]
    r = lax.broadcasted_iota(I32, (nq, nq * H_IDX), 0)
    c = lax.broadcasted_iota(I32, (nq, nq * H_IDX), 1)
    lo = r * H_IDX
    wsel = jnp.where(c >= lo, jnp.where(c < lo + H_IDX, wi_ref[0], 0.0), 0.0)
    o_ref[0] = jnp.dot(wsel, jnp.maximum(s, 0.0), precision=HI, preferred_element_type=F32)


def dsa_sidx_call(page_table, qi, wi, ki_new, cache_kidx):
    db, n_pages = page_table.shape
    nq = qi.shape[1] // H_IDX
    group = _page_group(n_pages)
    n_steps = n_pages // group
    grid_spec = pltpu.PrefetchScalarGridSpec(
        num_scalar_prefetch=1,
        grid=(db, n_steps + 1),
        in_specs=[
            pl.BlockSpec((1, nq * H_IDX, D_IDX), lambda b, s, pt: (b, 0, 0)),
            pl.BlockSpec((1, 1, nq * H_IDX), lambda b, s, pt: (b, 0, 0)),
            pl.BlockSpec((1, D_IDX, PAGE_SIZE), lambda b, s, pt: (b, 0, 0)),
        ] + _page_specs((1, D_IDX, PAGE_SIZE), n_pages, group),
        out_specs=pl.BlockSpec((1, nq, group * PAGE_SIZE), lambda b, s, pt: (b, 0, s)),
    )
    return pl.pallas_call(
        functools.partial(_dsa_sidx_body, n_steps=n_steps),
        grid_spec=grid_spec,
        out_shape=jax.ShapeDtypeStruct((db, nq, (n_steps + 1) * group * PAGE_SIZE), F32),
        compiler_params=_cparams(("parallel", "arbitrary")),
        name="dsa_sample_idx",
    )(page_table, qi, wi, ki_new, *([cache_kidx] * group))


def _dsa_satt_body(pt_ref, isc_ref, q_ref, kn_ref, vn_ref, *rest, n_pages, n_steps, topk, n_new):
    group = n_pages // n_steps
    kc_refs, vc_refs = rest[:group], rest[group:2 * group]
    o_ref, bias_scr, qe_scr, m_scr, l_scr, acc_scr = rest[2 * group:]
    p = pl.program_id(1)
    nq = q_ref.shape[1]
    past = n_pages * PAGE_SIZE
    wcols = group * PAGE_SIZE
    ltot = (n_steps + 1) * wcols
    hd = H_A * DH_A
    scale = DH_A ** -0.5
    assert math.frexp(scale)[0] == 0.5

    @pl.when(p == 0)
    def _():
        isc = isc_ref[0]
        row = lax.broadcasted_iota(I32, (nq, ltot), 0)
        col = lax.broadcasted_iota(I32, (nq, ltot), 1)
        rel = col - past
        ninf = -jnp.inf
        isc = jnp.where(rel < 0, isc, jnp.where(rel < n_new, jnp.where(rel <= row, isc, ninf), ninf))
        bias = _topk_bias(isc, col, topk, max(1, (ltot - 1).bit_length()))
        for i in range(n_steps + 1):
            bias_scr[i] = bias[:, i * wcols:(i + 1) * wcols]
        q = q_ref[0] * scale
        lane = lax.broadcasted_iota(I32, (nq, hd), 1)
        qe_scr[...] = jnp.concatenate(
            [jnp.where(lane >= h * DH_A, jnp.where(lane < (h + 1) * DH_A, q, 0.0), 0.0) for h in range(H_A)],
            axis=0).astype(BF16)
        m_scr[...] = jnp.full_like(m_scr, NEG_BIG)
        l_scr[...] = jnp.zeros_like(l_scr)
        acc_scr[...] = jnp.zeros_like(acc_scr)

    is_new = p == n_steps
    kcat = _page_cat(kc_refs, kn_ref, is_new)
    vcat = _page_cat(vc_refs, vn_ref, is_new)
    lg = jnp.dot(qe_scr[...], kcat, preferred_element_type=F32)
    lg = lg + jnp.concatenate([bias_scr[p]] * H_A, axis=0)
    m_old = m_scr[...]
    m_new = jnp.maximum(m_old, jnp.max(lg, axis=-1, keepdims=True))
    alpha = jnp.exp(m_old - m_new)
    pr = jnp.exp(lg - m_new)
    l_scr[...] = alpha * l_scr[...] + jnp.sum(pr, axis=-1, keepdims=True)
    acc_scr[...] = alpha * acc_scr[...] + lax.dot_general(pr.astype(BF16), vcat, NT, preferred_element_type=F32)
    m_scr[...] = m_new

    @pl.when(is_new)
    def _():
        o = acc_scr[...] / l_scr[...]
        lane = lax.broadcasted_iota(I32, (nq, hd), 1)
        out = jnp.zeros((nq, hd), F32)
        for h in range(H_A):
            oh = o[h * nq:(h + 1) * nq]
            out = out + jnp.where(lane >= h * DH_A, jnp.where(lane < (h + 1) * DH_A, oh, 0.0), 0.0)
        o_ref[0] = out


def dsa_satt_call(page_table, isc, q, k_new, v_new, cache_k, cache_v, n_new):
    db, n_pages = page_table.shape
    nq = q.shape[1]
    hd = H_A * DH_A
    page = (1, H_A, DH_A, PAGE_SIZE)
    group = _page_group(n_pages)
    n_steps = n_pages // group
    wcols = group * PAGE_SIZE
    ltot = (n_steps + 1) * wcols
    assert isc.shape[2] == ltot
    topk = min(TOPK_MAX, (n_pages * PAGE_SIZE + n_new) // 4)
    grid_spec = pltpu.PrefetchScalarGridSpec(
        num_scalar_prefetch=1,
        grid=(db, n_steps + 1),
        in_specs=[
            pl.BlockSpec((1, nq, ltot), lambda b, p, pt: (b, 0, 0)),
            pl.BlockSpec((1, nq, hd), lambda b, p, pt: (b, 0, 0)),
            pl.BlockSpec(page, lambda b, p, pt: (b, 0, 0, 0)),
            pl.BlockSpec(page, lambda b, p, pt: (b, 0, 0, 0)),
        ] + _page_specs(page, n_pages, group) * 2,
        out_specs=pl.BlockSpec((1, nq, hd), lambda b, p, pt: (b, 0, 0)),
        scratch_shapes=[
            pltpu.VMEM((n_steps + 1, nq, wcols), F32),
            pltpu.VMEM((H_A * nq, hd), BF16),
            pltpu.VMEM((H_A * nq, 1), F32),
            pltpu.VMEM((H_A * nq, 1), F32),
            pltpu.VMEM((H_A * nq, hd), F32),
        ],
    )
    return pl.pallas_call(
        functools.partial(_dsa_satt_body, n_pages=n_pages, n_steps=n_steps, topk=topk, n_new=n_new),
        grid_spec=grid_spec,
        out_shape=jax.ShapeDtypeStruct((db, nq, hd), F32),
        compiler_params=_cparams(("parallel", "arbitrary")),
        name="dsa_sample_att",
    )(page_table, isc, q, k_new, v_new, *([cache_k] * group), *([cache_v] * group))


def _head_sum_matrix(n, group):
    r = lax.broadcasted_iota(I32, (n, n), 0) // group
    c = lax.broadcasted_iota(I32, (n, n), 1) // group
    return jnp.where(r == c, 1.0, 0.0).astype(F32)


def _rwkv_prep_body(zb_ref, sp_ref, mu_ref, w0_ref, a0_ref, w2_ref, a2_ref, g2_ref, kkp_ref, ka_ref,
                    r_ref, w_ref, k_ref, v_ref, an_ref, b_ref, g_ref, carry_scr):
    tc = pl.program_id(1)
    zb = zb_ref[0]
    tt = zb.shape[0]
    hd = H_B * DH_B

    @pl.when(tc == 0)
    def _():
        carry_scr[...] = sp_ref[0]

    rows = lax.broadcasted_iota(I32, zb.shape, 0)
    prev = jnp.where(rows == 0, carry_scr[...], pltpu.roll(zb, 1, 0))
    carry_scr[...] = zb[tt - 1:tt]
    z = zb + (prev - zb) * mu_ref[...]
    r = z[:, 0:hd]
    k = z[:, hd:2 * hd]
    v = z[:, 2 * hd:3 * hd]
    xwa = z[:, 3 * hd:3 * hd + R_DECAY + R_AAA]
    xg = z[:, 3 * hd + R_DECAY + R_AAA:]
    wl = w0_ref[...] + jnp.dot(jnp.tanh(xwa).astype(BF16), w2_ref[...], preferred_element_type=F32)
    w_log = -_softplus(-wl) - 0.5
    log_decay = -jnp.exp(w_log)
    a = jax.nn.sigmoid(a0_ref[...] + jnp.dot(xwa.astype(BF16), a2_ref[...], preferred_element_type=F32))
    g = jnp.dot(jax.nn.sigmoid(xg).astype(BF16), g2_ref[...], preferred_element_type=F32)
    kk = k * kkp_ref[...]
    ss = jnp.dot(kk * kk, _head_sum_matrix(hd, DH_B), precision=HI, preferred_element_type=F32)
    kk = kk * lax.rsqrt(jnp.maximum(ss, 1e-24))
    r_ref[0] = r
    w_ref[0] = log_decay
    k_ref[0] = k * (1.0 + (a - 1.0) * ka_ref[...])
    v_ref[0] = v
    an_ref[0] = -kk
    b_ref[0] = kk * a
    g_ref[0] = g


def rwkv_prep_call(zb, shift_prev, mu, w0, a0, w2p, a2p, g2, kkp, ka):
    b, t, dz = zb.shape
    hd = H_B * DH_B
    tt = _row_tile(t, 256)
    row = lambda n: pl.BlockSpec((1, n), lambda i, j: (0, 0))
    full = lambda s: pl.BlockSpec(s, lambda i, j: (0, 0))
    out = pl.BlockSpec((1, tt, hd), lambda i, j: (i, j, 0))
    return pl.pallas_call(
        _rwkv_prep_body,
        grid=(b, t // tt),
        in_specs=[
            pl.BlockSpec((1, tt, dz), lambda i, j: (i, j, 0)),
            pl.BlockSpec((1, 1, dz), lambda i, j: (i, 0, 0)),
            row(dz), row(hd), row(hd),
            full((R_DECAY + R_AAA, hd)), full((R_DECAY + R_AAA, hd)), full((R_GATE, hd)),
            row(hd), row(hd),
        ],
        out_specs=[out] * 7,
        out_shape=[jax.ShapeDtypeStruct((b, t, hd), F32)] * 7,
        scratch_shapes=[pltpu.VMEM((1, dz), F32)],
        compiler_params=_cparams(("parallel", "arbitrary")),
        name="rwkv_prep",
    )(zb, shift_prev, mu, w0, a0, w2p, a2p, g2, kkp, ka)


RWKV_PAIRS = H_B // 2
RWKV_STEPS = SUBLANES // RWKV_PAIRS
RWKV_NB = 8
RWKV_SUM_PIECES = 2


def _rwkv_scan_body(r_ref, w_ref, k_ref, v_ref, a_ref, b_ref, s0_ref, y_ref, sout_ref,
                    s_scr, pa_scr, pv_scr, *, t_valid):
    c = pl.program_id(1)
    nb, tc = r_ref.shape[0], r_ref.shape[1]
    chains = [(bi, p) for bi in range(nb) for p in range(RWKV_PAIRS)]
    rows = lambda n: slice(n * DH_B, (n + 1) * DH_B)

    @pl.when(c == 0)
    def _():
        for n, (bi, p) in enumerate(chains):
            s_scr[rows(n), :] = s0_ref[bi, p]

    if t_valid < tc * RWKV_STEPS:
        y_ref[...] = jnp.zeros_like(y_ref)

    sub = lax.broadcasted_iota(I32, (DH_B, LANES), 0)
    lane = lax.broadcasted_iota(I32, (DH_B, LANES), 1)
    isel = jnp.where((lane & (DH_B - 1)) == sub, 1.0, 0.0).astype(F32)
    qblk = _head_sum_matrix(LANES, DH_B).astype(BF16)

    def hsum(ref, pieces=3):
        x = ref[...]
        parts = []
        for _ in range(pieces - 1):
            h = x.astype(BF16)
            parts.append(h)
            x = x - h.astype(F32)
        parts.append(x.astype(BF16))
        return jnp.dot(jnp.concatenate(parts, axis=1), jnp.concatenate([qblk] * pieces, axis=0),
                       preferred_element_type=F32)

    def step(u, carry):
        tiles = [(a_ref[bi, u], jnp.exp(w_ref[bi, u]), k_ref[bi, u], v_ref[bi, u], b_ref[bi, u], r_ref[bi, u])
                 for bi in range(nb)]
        for i in range(RWKV_STEPS):
            for n, (bi, p) in enumerate(chains):
                at, _, _, vt, _, _ = tiles[bi]
                row = slice(i * RWKV_PAIRS + p, i * RWKV_PAIRS + p + 1)
                pa_scr[rows(n), :] = s_scr[rows(n), :] * at[row]
                pv_scr[rows(n), :] = isel * vt[row]
            sa = hsum(pa_scr, RWKV_SUM_PIECES)
            vc = hsum(pv_scr)
            for n, (bi, p) in enumerate(chains):
                _, wt, kt, _, bt, rt = tiles[bi]
                row = slice(i * RWKV_PAIRS + p, i * RWKV_PAIRS + p + 1)
                s = s_scr[rows(n), :] * wt[row] + sa[rows(n)] * bt[row] + vc[rows(n)] * kt[row]
                s_scr[rows(n), :] = s
                pa_scr[rows(n), :] = s * rt[row]
            yb = hsum(pa_scr, RWKV_SUM_PIECES)
            for n, (bi, p) in enumerate(chains):
                row = slice(i * RWKV_PAIRS + p, i * RWKV_PAIRS + p + 1)
                y_ref[bi, u, row, :] = jnp.sum(yb[rows(n)] * isel, axis=0, keepdims=True)
        return carry

    lax.fori_loop(0, min(t_valid, tc * RWKV_STEPS) // RWKV_STEPS, step, 0)

    @pl.when(c == pl.num_programs(1) - 1)
    def _():
        for n, (bi, p) in enumerate(chains):
            sout_ref[bi, p] = s_scr[rows(n), :]


def rwkv_scan_call(r, w, k, v, a, b, s0, t_valid):
    bsz, t, hd = r.shape
    nb = math.gcd(bsz, RWKV_NB)
    nt = t // RWKV_STEPS
    tc = _row_tile(nt, 32)
    assert t % RWKV_STEPS == 0 and t_valid % RWKV_STEPS == 0 and (t_valid == t or nt == tc)
    tiled = lambda x: x.reshape(bsz, nt, SUBLANES, LANES)
    seq = pl.BlockSpec((nb, tc, SUBLANES, LANES), lambda i, j: (i, j, 0, 0))
    st = pl.BlockSpec((nb, RWKV_PAIRS, DH_B, LANES), lambda i, j: (i, 0, 0, 0))
    y, s_fin = pl.pallas_call(
        functools.partial(_rwkv_scan_body, t_valid=t_valid),
        grid=(bsz // nb, nt // tc),
        in_specs=[seq] * 6 + [st],
        out_specs=[seq, st],
        out_shape=[jax.ShapeDtypeStruct((bsz, nt, SUBLANES, LANES), F32),
                   jax.ShapeDtypeStruct((bsz, RWKV_PAIRS, DH_B, LANES), F32)],
        scratch_shapes=[pltpu.VMEM((nb * RWKV_PAIRS * DH_B, LANES), F32)] * 3,
        compiler_params=_cparams(("parallel", "arbitrary")),
        name="rwkv_scan",
    )(tiled(r), tiled(w), tiled(k), tiled(v), tiled(a), tiled(b), s0)
    return y.reshape(bsz, t, hd), s_fin


RWKV_CHUNK = 64
RWKV_HG = 4
RWKV_CHUNK_SEQS = 1


def _rwkv_chunk_body(r_ref, lw_ref, k_ref, v_ref, a_ref, b_ref, s0_ref, y_ref, sout_ref, s_scr):
    c = pl.program_id(1)
    cs = r_ref.shape[1]
    ng = s_scr.shape[1]
    gw = RWKV_HG * DH_B
    n = RWKV_HG * cs
    assert cs & (cs - 1) == 0

    @pl.when(c == 0)
    def _():
        s_scr[...] = s0_ref[...]

    tr = lax.broadcasted_iota(I32, (cs, cs), 0)
    tcol = lax.broadcasted_iota(I32, (cs, cs), 1)
    tril = jnp.where(tr >= tcol, 1.0, 0.0).astype(F32)
    row = lax.broadcasted_iota(I32, (n, n), 0)
    col = lax.broadcasted_iota(I32, (n, n), 1)
    same = (row // cs) == (col // cs)
    tt = row & (cs - 1)
    ss = col & (cs - 1)
    strict = jnp.where(same, jnp.where(tt > ss, 1.0, 0.0), 0.0).astype(F32)
    incl = jnp.where(same, jnp.where(tt >= ss, 1.0, 0.0), 0.0).astype(F32)
    eye = jnp.where(row == col, 1.0, 0.0).astype(F32)
    lane_h = lax.broadcasted_iota(I32, (1, gw), 1) // DH_B
    hmask = [jnp.where(lane_h == h, 1.0, 0.0).astype(F32) for h in range(RWKV_HG)]
    sblk = jnp.where(lax.broadcasted_iota(I32, (gw, gw), 0) // DH_B == lax.broadcasted_iota(I32, (gw, gw), 1) // DH_B,
                     1.0, 0.0).astype(F32)
    nn = (((1,), (0,)), ((), ()))

    def stack(x):
        return jnp.concatenate([x * m for m in hmask], axis=0)

    def fold(x):
        out = x[0:cs]
        for h in range(1, RWKV_HG):
            out = out + x[h * cs:(h + 1) * cs]
        return out

    def bdot(x, y, dims=nn):
        return lax.dot_general(x.astype(BF16), y.astype(BF16), dims, preferred_element_type=F32)

    for bi, g in [(bi, g) for bi in range(r_ref.shape[0]) for g in range(ng)]:
        sl = slice(g * gw, (g + 1) * gw)
        lw = lw_ref[bi, :, sl]
        cl = jnp.dot(tril, lw, precision=HI, preferred_element_type=F32)
        e_in = jnp.exp(cl)
        e_inv = jnp.exp(-cl)
        e_tail = jnp.exp(cl[cs - 1:cs] - cl)
        at = a_ref[bi, :, sl] * jnp.exp(cl - lw)
        bt = b_ref[bi, :, sl] * e_inv
        kt = k_ref[bi, :, sl] * e_inv
        rt = r_ref[bi, :, sl] * e_in
        bw = b_ref[bi, :, sl] * e_tail
        kw = k_ref[bi, :, sl] * e_tail
        v = v_ref[bi, :, sl]
        s_old = s_scr[bi, g]
        a_s, r_s, b_s, k_s, v_s = (stack(x) for x in (at, rt, bt, kt, v))
        lab = bdot(a_s, b_s, NT) * strict
        lak = bdot(a_s, k_s, NT) * strict
        lrb = bdot(r_s, b_s, NT) * incl
        lrk = bdot(r_s, k_s, NT) * incl
        rhs = stack(bdot(at, s_old, NT)) + bdot(lak, v_s)
        tm = eye + lab
        lp = lab
        for _ in range(cs.bit_length() - 2):
            lp = bdot(lp, lp)
            tm = tm + bdot(lp, tm)
        u_s = bdot(tm, rhs)
        y_s = stack(bdot(rt, s_old, NT)) + bdot(lrb, u_s) + bdot(lrk, v_s)
        y_ref[bi, :, sl] = fold(y_s)
        u = fold(u_s)
        s_scr[bi, g] = s_old * e_in[cs - 1:cs] + (bdot(u, bw, TN) + bdot(v, kw, TN)) * sblk

    @pl.when(c == pl.num_programs(1) - 1)
    def _():
        sout_ref[...] = s_scr[...]


def rwkv_chunk_call(r, lw, k, v, a, b, s0):
    bsz, t, hd = r.shape
    cs = RWKV_CHUNK
    ng = H_B // RWKV_HG
    gw = RWKV_HG * DH_B
    eye = jnp.eye(RWKV_HG, dtype=F32)
    s0g = jnp.einsum('bghij,hk->bghikj', s0.astype(F32).reshape(bsz, ng, RWKV_HG, DH_B, DH_B), eye)
    nb = math.gcd(bsz, RWKV_CHUNK_SEQS)
    seq = pl.BlockSpec((nb, cs, hd), lambda i, j: (i, j, 0))
    st = pl.BlockSpec((nb, ng, gw, gw), lambda i, j: (i, 0, 0, 0))
    y, s_fin = pl.pallas_call(
        _rwkv_chunk_body,
        grid=(bsz // nb, t // cs),
        in_specs=[seq] * 6 + [st],
        out_specs=[seq, st],
        out_shape=[jax.ShapeDtypeStruct((bsz, t, hd), F32), jax.ShapeDtypeStruct((bsz, ng, gw, gw), F32)],
        scratch_shapes=[pltpu.VMEM((nb, ng, gw, gw), F32)],
        compiler_params=_cparams(("parallel", "arbitrary")),
        name="rwkv_chunk",
    )(r, lw, k, v, a, b, s0g.reshape(bsz, ng, gw, gw))
    s_fin = jnp.einsum('bghihj->bghij', s_fin.reshape(bsz, ng, RWKV_HG, DH_B, RWKV_HG, DH_B))
    return y, s_fin.reshape(bsz, H_B, DH_B, DH_B)


def _rwkv_post_body(y_ref, r_ref, k_ref, v_ref, g_ref, lg_ref, lb_ref, rk_ref, o_ref):
    hd = H_B * DH_B
    avg = _head_sum_matrix(hd, DH_B) * (1.0 / DH_B)
    y = y_ref[...]
    mu = jnp.dot(y, avg, precision=HI, preferred_element_type=F32)
    d = y - mu
    var = jnp.dot(d * d, avg, precision=HI, preferred_element_type=F32)
    yn = d * lax.rsqrt(var + LNX_EPS) * lg_ref[...] + lb_ref[...]
    bonus = jnp.dot(r_ref[...] * k_ref[...] * rk_ref[...], avg * float(DH_B), precision=HI,
                    preferred_element_type=F32)
    o_ref[...] = (yn + bonus * v_ref[...]) * g_ref[...]


def rwkv_post_call(y, r, k, v, g, lnx_g, lnx_b, rk):
    m, hd = y.shape
    tm = _row_tile(m, 512)
    tok = pl.BlockSpec((tm, hd), lambda i: (i, 0))
    row = pl.BlockSpec((1, hd), lambda i: (0, 0))
    return pl.pallas_call(
        _rwkv_post_body,
        grid=(m // tm,),
        in_specs=[tok] * 5 + [row] * 3,
        out_specs=tok,
        out_shape=jax.ShapeDtypeStruct((m, hd), F32),
        compiler_params=_cparams(("parallel",)),
        name="rwkv_post",
    )(y, r, k, v, g, lnx_g, lnx_b, rk)


def _head_mask(n, h, width):
    lane = lax.broadcasted_iota(I32, (1, n), 1)
    return jnp.where(lane >= h * width, jnp.where(lane < (h + 1) * width, 1.0, 0.0), 0.0).astype(F32)


def _head_norm128(y, g):
    mu = jnp.mean(y, axis=-1, keepdims=True)
    d = y - mu
    var = jnp.mean(d * d, axis=-1, keepdims=True)
    return d * lax.rsqrt(var + HN_EPS) * g


def _ret_body(qk_ref, v_ref, gc_ref, cos_ref, sin_ref, gn_ref, s0_ref, y_ref, sout_ref, s_scr, *, l_valid):
    c = pl.program_id(1)
    lc = qk_ref.shape[1]
    hk = H_C * DK_C

    @pl.when(c == 0)
    def _():
        s_scr[...] = s0_ref[0]

    rr = lax.broadcasted_iota(I32, (hk, hk), 0)
    cc = lax.broadcasted_iota(I32, (hk, hk), 1)
    half = DK_C // 2
    same = (rr // DK_C) == (cc // DK_C)
    dr = rr & (DK_C - 1)
    dc = cc & (DK_C - 1)
    rot = jnp.where(same, jnp.where(dr == dc + half, -1.0, jnp.where(dr + half == dc, 1.0, 0.0)), 0.0).astype(F32)

    qk = qk_ref[0]
    cos = cos_ref[...]
    sin = sin_ref[...]

    def rope(x):
        return x * cos + jnp.dot(x, rot, precision=HI, preferred_element_type=F32) * sin

    qr = rope(qk[:, :hk])
    kr = rope(qk[:, hk:]) * DK_C ** -0.5
    krb = kr.astype(BF16)
    s_prev = s_scr[...]
    s_prev_b = s_prev.astype(BF16)

    jj = lax.broadcasted_iota(I32, (lc, lc), 0)
    ss = lax.broadcasted_iota(I32, (lc, lc), 1)
    diff = (jj - ss).astype(F32)
    jcol = lax.broadcasted_iota(I32, (lc, 1), 0).astype(F32)
    srow_state = lax.broadcasted_iota(I32, (hk, 1), 0) // DK_C
    s_new = jnp.zeros_like(s_prev)
    decay_rows = jnp.zeros((hk, 1), F32)
    for h in range(H_C):
        lg = math.log1p(-2.0 ** (-5.0 - h))
        mh = _head_mask(hk, h, DK_C)
        qm = (qr * mh).astype(BF16)
        dmask = jnp.where(diff >= 0, jnp.exp(jnp.maximum(diff, 0.0) * lg), 0.0)
        scores = lax.dot_general(qm, krb, NT, preferred_element_type=F32) * dmask
        vh = v_ref[0, :, h * DV_C:(h + 1) * DV_C]
        vhb = vh.astype(BF16)
        intra = jnp.dot(scores.astype(BF16), vhb, preferred_element_type=F32)
        inter = jnp.dot(qm, s_prev_b, preferred_element_type=F32) * jnp.exp((jcol + 1.0) * lg)
        yh = _head_norm128(intra + inter, gn_ref[h:h + 1, :])
        gch = gc_ref[0, :, h * DV_C:(h + 1) * DV_C]
        y_ref[0, :, h * DV_C:(h + 1) * DV_C] = yh * (gch * jax.nn.sigmoid(gch))
        w_s = jnp.where(jcol < l_valid, jnp.exp((l_valid - 1.0 - jcol) * lg), 0.0)
        kw = (kr * mh * w_s).astype(BF16)
        s_new = s_new + lax.dot_general(kw, vhb, TN, preferred_element_type=F32)
        decay_rows = jnp.where(srow_state == h, math.exp(l_valid * lg), decay_rows)
    s_fin = decay_rows * s_prev + s_new
    s_scr[...] = s_fin

    @pl.when(c == pl.num_programs(1) - 1)
    def _():
        sout_ref[0] = s_fin


def retention_call(za, cos, sin, gn, s0, l_valid):
    b, t, _ = za.shape
    lc = _row_tile(t, 128)
    assert l_valid == lc or t == lc
    hk, hv = H_C * DK_C, H_C * DV_C
    blk = lambda j: pl.BlockSpec((1, lc, 2 * hk), lambda i, c, j=j: (i, c, j))
    st = pl.BlockSpec((1, hk, DV_C), lambda i, c: (i, 0, 0))
    return pl.pallas_call(
        functools.partial(_ret_body, l_valid=l_valid),
        grid=(b, t // lc),
        in_specs=[blk(0), blk(1), blk(2),
                  pl.BlockSpec((lc, hk), lambda i, c: (c, 0)),
                  pl.BlockSpec((lc, hk), lambda i, c: (c, 0)),
                  pl.BlockSpec((H_C, DV_C), lambda i, c: (0, 0)),
                  st],
        out_specs=[pl.BlockSpec((1, lc, hv), lambda i, c: (i, c, 0)), st],
        out_shape=[jax.ShapeDtypeStruct((b, t, hv), F32), jax.ShapeDtypeStruct((b, hk, DV_C), F32)],
        scratch_shapes=[pltpu.VMEM((hk, DV_C), F32)],
        compiler_params=_cparams(("parallel", "arbitrary")),
        name="retention",
    )(za, za, za, cos, sin, gn, s0)


def _mlstm_body(qk_ref, v_ref, og_ref, gt_ref, cb_ref, cw_ref, cbias_ref, gbias_ref, gn_ref,
                c0_ref, n0_ref, m0_ref, y_ref, cout_ref, nout_ref, mout_ref,
                ext_scr, c_scr, n_scr, m_scr, *, l_valid):
    c = pl.program_id(1)
    lc = qk_ref.shape[1]
    hk = H_D * DK_D
    pad = SUBLANES

    @pl.when(c == 0)
    def _():
        ext_scr[0:pad, :] = cb_ref[0]
        c_scr[...] = c0_ref[0]
        n_scr[...] = n0_ref[0]
        m_scr[...] = m0_ref[0]

    u = qk_ref[0]
    ext_scr[pad:pad + lc, :] = u
    acc = cbias_ref[...] + u * cw_ref[CONV_W - 1:CONV_W, :]
    for i in range(CONV_W - 1):
        sh = CONV_W - 1 - i
        acc = acc + ext_scr[pad - sh:pad - sh + lc, :] * cw_ref[i:i + 1, :]
    ext_scr[0:pad, :] = u[lc - pad:lc, :]
    qkc = acc * jax.nn.sigmoid(acc)
    q = qkc[:, :hk]
    k = qkc[:, hk:] * DK_D ** -0.5
    kb = k.astype(BF16)

    gates = gt_ref[0] + gbias_ref[...]
    logsig = -_softplus(-gates)
    rowi = lax.broadcasted_iota(I32, (lc, LANES), 0)
    logsig = jnp.where(rowi < l_valid, logsig, 0.0)
    jj = lax.broadcasted_iota(I32, (lc, lc), 0)
    ss = lax.broadcasted_iota(I32, (lc, lc), 1)
    tril = jnp.where(jj >= ss, 1.0, 0.0).astype(F32)
    bcum = jnp.dot(tril, logsig, precision=HI, preferred_element_type=F32)
    lane_g = lax.broadcasted_iota(I32, (lc, LANES), 1)
    jcol = lax.broadcasted_iota(I32, (lc, 1), 0)
    causal = jj >= ss
    ninf = -jnp.inf

    c_prev = c_scr[...]
    c_prev_b = c_prev.astype(BF16)
    n_prev = n_scr[...]
    m_prev = m_scr[...]
    c_new = jnp.zeros_like(c_prev)
    carry_rows = jnp.zeros((hk, 1), F32)
    carry_lanes = jnp.zeros((1, hk), F32)
    ws_full = jnp.zeros((lc, hk), F32)
    m_out = m_prev
    srow_state = lax.broadcasted_iota(I32, (hk, 1), 0) // DK_D
    lane_state = lax.broadcasted_iota(I32, (1, hk), 1) // DK_D
    lane_m = lax.broadcasted_iota(I32, (1, LANES), 1)
    for h in range(H_D):
        mh = _head_mask(hk, h, DK_D)
        e_i = jnp.where(lane_g == h, 1.0, 0.0).astype(F32)
        e_f = jnp.where(lane_g == H_D + h, 1.0, 0.0).astype(F32)
        logi_col = jnp.where(jcol < l_valid, gates[:, h:h + 1], ninf)
        b_col = bcum[:, H_D + h:H_D + h + 1]
        i_row = lax.dot_general(e_i, gates, NT, precision=HI, preferred_element_type=F32)
        i_row = jnp.where(ss < l_valid, i_row, ninf)
        b_row = lax.dot_general(e_f, bcum, NT, precision=HI, preferred_element_type=F32)
        m_h = m_prev[:, h:h + 1]
        inter = b_col + m_h
        dmat = jnp.where(causal, b_col - b_row + i_row, ninf)
        m_j = jnp.maximum(inter, jnp.max(dmat, axis=-1, keepdims=True))
        qm = q * mh
        qmb = qm.astype(BF16)
        amat = jnp.exp(dmat - m_j) * lax.dot_general(qmb, kb, NT, preferred_element_type=F32)
        sc = jnp.exp(inter - m_j)
        vh = v_ref[0, :, h * DV_D:(h + 1) * DV_D]
        vhb = vh.astype(BF16)
        num = jnp.dot(amat.astype(BF16), vhb, preferred_element_type=F32) \
            + sc * jnp.dot(qmb, c_prev_b, preferred_element_type=F32)
        den = jnp.sum(amat, axis=-1, keepdims=True) + sc * jnp.sum(qm * n_prev, axis=-1, keepdims=True)
        hh = num / jnp.maximum(jnp.abs(den), jnp.exp(-m_j))
        ogh = og_ref[0, :, h * DV_D:(h + 1) * DV_D]
        y_ref[0, :, h * DV_D:(h + 1) * DV_D] = _head_norm128(hh, gn_ref[h:h + 1, :]) * jax.nn.sigmoid(ogh)
        b_last = b_col[l_valid - 1:l_valid, :]
        gs = b_last - b_col + logi_col
        m_new = jnp.maximum(b_last + m_h, jnp.max(gs, axis=0, keepdims=True))
        ws = jnp.exp(gs - m_new)
        carry = jnp.exp(b_last + m_h - m_new)
        c_new = c_new + lax.dot_general((k * mh).astype(BF16), (vh * ws).astype(BF16), TN,
                                        preferred_element_type=F32)
        carry_rows = jnp.where(srow_state == h, carry, carry_rows)
        carry_lanes = jnp.where(lane_state == h, carry, carry_lanes)
        ws_full = ws_full + ws * mh
        m_out = jnp.where(lane_m == h, m_new, m_out)
    c_fin = carry_rows * c_prev + c_new
    n_fin = carry_lanes * n_prev + jnp.sum(ws_full * k, axis=0, keepdims=True)
    c_scr[...] = c_fin
    n_scr[...] = n_fin
    m_scr[...] = m_out

    @pl.when(c == pl.num_programs(1) - 1)
    def _():
        cout_ref[0] = c_fin
        nout_ref[0] = n_fin
        mout_ref[0] = m_out


def mlstm_call(zb, zc, conv_buf, conv_w, conv_b, gate_bias, gn, c0, n0, m0, l_valid):
    b, t, _ = zb.shape
    lc = _row_tile(t, 128)
    assert l_valid == lc or t == lc
    hk, hv = H_D * DK_D, H_D * DV_D
    blk = lambda j: pl.BlockSpec((1, lc, 2 * hk), lambda i, c, j=j: (i, c, j))
    cst = lambda s: pl.BlockSpec(s, lambda i, c: (0,) * len(s))
    per_b = lambda s: pl.BlockSpec((1,) + s, lambda i, c: (i,) + (0,) * len(s))
    return pl.pallas_call(
        functools.partial(_mlstm_body, l_valid=l_valid),
        grid=(b, t // lc),
        in_specs=[blk(0), blk(1), blk(2),
                  pl.BlockSpec((1, lc, LANES), lambda i, c: (i, c, 0)),
                  per_b((SUBLANES, 2 * hk)),
                  cst((CONV_W, 2 * hk)), cst((1, 2 * hk)), cst((1, LANES)), cst((H_D, DV_D)),
                  per_b((hk, DV_D)), per_b((1, hk)), per_b((1, LANES))],
        out_specs=[pl.BlockSpec((1, lc, hv), lambda i, c: (i, c, 0)),
                   per_b((hk, DV_D)), per_b((1, hk)), per_b((1, LANES))],
        out_shape=[jax.ShapeDtypeStruct((b, t, hv), F32),
                   jax.ShapeDtypeStruct((b, hk, DV_D), F32),
                   jax.ShapeDtypeStruct((b, 1, hk), F32),
                   jax.ShapeDtypeStruct((b, 1, LANES), F32)],
        scratch_shapes=[pltpu.VMEM((SUBLANES + lc, 2 * hk), F32),
                        pltpu.VMEM((hk, DV_D), F32),
                        pltpu.VMEM((1, hk), F32),
                        pltpu.VMEM((1, LANES), F32)],
        compiler_params=_cparams(("parallel", "arbitrary")),
        name="mlstm",
    )(zb, zb, zb, zc, conv_buf, conv_w, conv_b, gate_bias, gn, c0, n0, m0)


def _pad_cols(w, n):
    return jnp.pad(w, ((0, 0), (0, n - w.shape[1])))


def _pad_time(a, tp):
    t = a.shape[1]
    if t == tp:
        return a
    return jnp.pad(a, ((0, 0), (0, tp - t)) + ((0, 0),) * (a.ndim - 2))


def _heads_major(a, h):
    b, t, _ = a.shape
    return a.reshape(b, t, h, -1).transpose(0, 2, 1, 3)


def _ab_mixer(xf, b, t, e, g_pre, P, prompt, shift_prev, s0, cache):
    m = b * t
    hd = H_A * DH_A
    w = P['ab_w_in'][e]
    o_qi = 3 * hd
    o_ki = o_qi + H_IDX * D_IDX
    o_wi = o_ki + D_IDX
    o_zb = o_wi + H_IDX
    qkv = proj_in_call(xf, g_pre, w[:, :o_qi].astype(BF16))
    qi = proj_in_call(xf, g_pre, w[:, o_qi:o_ki].astype(BF16), out_dtype=BF16)
    kw, ki2 = kiwi_call(xf, g_pre, w[:, o_ki:o_wi], w[:, o_wi:o_zb], P['kidx_g'][e])
    zb = proj_in_call(xf, g_pre, w[:, o_zb:].astype(BF16))
    q3 = qkv[:, :hd].reshape(b, t, hd)
    k3 = qkv[:, hd:2 * hd].reshape(b, t, hd)
    v3 = qkv[:, 2 * hd:].reshape(b, t, hd)
    ki3 = kw[:, :D_IDX].reshape(b, t, D_IDX)
    wi3 = kw[:, D_IDX:D_IDX + H_IDX].reshape(b, t, H_IDX)

    if prompt:
        ya = dsa_prompt_call(qkv.reshape(b, t, 3 * hd), qi.reshape(b, t, H_IDX * D_IDX),
                             ki2.reshape(b, t, LANES), kw.reshape(b, t, LANES)).reshape(m, hd)
    else:
        cache_k, cache_v, cache_kidx, page_table = cache
        n_pool = cache_k.shape[1]
        nq = SUBLANES
        qi_s = _pad_time(qi.reshape(b, t, H_IDX * D_IDX), nq).reshape(b, nq * H_IDX, D_IDX)
        wi_s = _pad_time(wi3, nq).reshape(b, 1, nq * H_IDX)
        keys_last = lambda a: jnp.moveaxis(a, 1, -1)
        new_page = lambda a: jnp.pad(keys_last(a), ((0, 0),) * (a.ndim - 1) + ((0, PAGE_SIZE - t),))
        isc = dsa_sidx_call(page_table, qi_s, wi_s, new_page(ki3), keys_last(cache_kidx[e]))
        ya = dsa_satt_call(page_table, isc, _pad_time(q3, nq),
                           new_page(k3.reshape(b, t, H_A, DH_A)), new_page(v3.reshape(b, t, H_A, DH_A)),
                           keys_last(cache_k[e]), keys_last(cache_v[e]), t)
        ya = ya[:, :t].reshape(m, hd)

    tp = -(-t // SUBLANES) * SUBLANES
    hb = H_B * DH_B
    zb3 = zb.reshape(b, t, D_B_IN)
    zpad = jnp.zeros((R_DECAY, hb), F32)
    w2p = jnp.concatenate([P['rwkv_w2'][e], zpad], axis=0).astype(BF16)
    a2p = jnp.concatenate([zpad, P['rwkv_a2'][e]], axis=0).astype(BF16)
    row = lambda a: a.reshape(1, -1)
    r, dec, k2, v, an, bb, g = rwkv_prep_call(
        _pad_time(zb3, tp), shift_prev.reshape(b, 1, D_B_IN), row(P['rwkv_mu'][e]), row(P['rwkv_w0'][e]),
        row(P['rwkv_a0'][e]), w2p, a2p, P['rwkv_g2'][e].astype(BF16), row(P['rwkv_kk'][e]), row(P['rwkv_ka'][e]))
    if t % RWKV_CHUNK == 0:
        y, s_new = rwkv_chunk_call(r, dec, k2, v, an, bb, s0)
    else:
        s0p = s0.reshape(b, RWKV_PAIRS, 2, DH_B, DH_B).transpose(0, 1, 3, 2, 4).reshape(b, RWKV_PAIRS, DH_B, LANES)
        y, s_fin = rwkv_scan_call(r, dec, k2, v, an, bb, s0p, t)
        s_new = s_fin.reshape(b, RWKV_PAIRS, DH_B, 2, DH_B).transpose(0, 1, 3, 2, 4).reshape(b, H_B, DH_B, DH_B)
    fl = lambda a: a.reshape(b * tp, hb)
    rk = jnp.broadcast_to(P['rwkv_rk'][e], (H_B, DH_B)).reshape(1, hb)
    yb = rwkv_post_call(fl(y), fl(r), fl(k2), fl(v), fl(g), row(P['rwkv_lnx_g'][e]), row(P['rwkv_lnx_b'][e]), rk)
    yb = yb.reshape(b, tp, hb)[:, :t].reshape(m, hb)
    st = (k3.reshape(b, t, H_A, DH_A), v3.reshape(b, t, H_A, DH_A), ki3, s_new, zb3[:, t - 1])
    return ya, yb, st


def _cd_mixer(xf, b, t, o, g_pre, pos, P, ret_s, m_c, m_n, m_m, conv_buf):
    w = P['cd_w_in'][o]
    hk, hv = H_C * DK_C, H_C * DV_C
    o_g = 2 * hk + 2 * hv
    o_vd = o_g + 2 * H_D * DK_D
    o_ig = o_vd + H_D * DV_D
    o_og = o_ig + 2 * H_D
    za = proj_in_call(xf, g_pre, w[:, :o_g].astype(BF16))
    zb = proj_in_call(xf, g_pre, jnp.concatenate([w[:, o_g:o_ig], w[:, o_og:]], axis=1).astype(BF16))
    zc = proj_in_call(xf, g_pre, _pad_cols(w[:, o_ig:o_og], LANES).astype(BF16))
    tp = -(-t // SUBLANES) * SUBLANES
    l_valid = t if tp != t else _row_tile(t, 128)
    za3 = _pad_time(za.reshape(b, t, -1), tp)
    zb3 = zb.reshape(b, t, -1)
    zc3 = _pad_time(zc.reshape(b, t, -1), tp)

    half = DK_C // 2
    inv = ROPE_BASE ** (-jnp.arange(half, dtype=F32) / half)
    ang = _pad_time(pos.astype(F32)[None], tp)[0][:, None] * inv[None, :]
    cos = jnp.tile(jnp.cos(ang), (1, 2 * H_C))
    sin = jnp.tile(jnp.sin(ang), (1, 2 * H_C))
    yc, ret_new = retention_call(za3, cos, sin, P['ret_gn'][o], ret_s.astype(F32).reshape(b, hk, DV_C), l_valid)

    hkd = H_D * DK_D
    cb = jnp.pad(conv_buf.astype(F32), ((0, 0), (SUBLANES - (CONV_W - 1), 0), (0, 0)))
    gate_bias = jnp.pad(P['mlstm_if_b'][o].astype(F32).reshape(1, 2 * H_D), ((0, 0), (0, LANES - 2 * H_D)))
    c0 = m_c.astype(F32).transpose(0, 1, 3, 2).reshape(b, hkd, DV_D)
    n0 = m_n.astype(F32).reshape(b, 1, hkd)
    m0 = jnp.pad(m_m.astype(F32), ((0, 0), (0, LANES - H_D))).reshape(b, 1, LANES)
    yd, c_new, n_new, m_new = mlstm_call(_pad_time(zb3, tp), zc3, cb, P['conv_w'][o], P['conv_b'][o].reshape(1, -1),
                                         gate_bias, P['mlstm_gn'][o], c0, n0, m0, l_valid)
    conv_new = jnp.concatenate([conv_buf.astype(F32), zb3[:, :, :2 * hkd]], axis=1)[:, -(CONV_W - 1):]
    st = (ret_new.reshape(b, H_C, DK_C, DV_C),
          c_new.reshape(b, H_D, DK_D, DV_D).transpose(0, 1, 3, 2),
          n_new.reshape(b, H_D, DK_D),
          m_new.reshape(b, LANES)[:, :H_D],
          conv_new)
    m = b * t
    return yc[:, :t].reshape(m, hv), yd[:, :t].reshape(m, H_D * DV_D), st


def _trunk(x, pos, prompt, ab_init, cd_init, cache, P):
    b, t, d = x.shape
    xf = x.reshape(b * t, d)
    depth = P['norm_g'].shape[0]
    ab_new, cd_new = [], []
    for l in range(depth):
        g = P['norm_g'][l]
        bf = lambda a: a.astype(BF16)
        xf = ffn_call(xf, g[0], g[1], bf(P['ffn_wg'][l, 0]), bf(P['ffn_wu'][l, 0]), bf(P['ffn_wd'][l, 0]))
        if l % 2 == 0:
            e = l // 2
            shift_prev, s0 = ab_init(e)
            ya, yb, st = _ab_mixer(xf, b, t, e, g[2], P, prompt, shift_prev, s0, cache)
            ab_new.append(st)
            wo = bf(P['ab_w_out'][e])
            xf = proj_out_call(ya, yb, xf, g[3], wo[:H_A * DH_A], wo[H_A * DH_A:])
        else:
            o = l // 2
            yc, yd, st = _cd_mixer(xf, b, t, o, g[2], pos, P, *cd_init(o))
            cd_new.append(st)
            wo = bf(P['cd_w_out'][o])
            xf = proj_out_call(yc, yd, xf, g[3], wo[:H_C * DV_C], wo[H_C * DV_C:])
        xf = ffn_call(xf, g[4], g[5], bf(P['ffn_wg'][l, 1]), bf(P['ffn_wu'][l, 1]), bf(P['ffn_wd'][l, 1]))
    ab = tuple(jnp.stack(s) for s in zip(*ab_new))
    cd = tuple(jnp.stack(s) for s in zip(*cd_new))
    return xf.reshape(b, t, d), ab, cd


def kernel(x_prompt, x_sample, cache_k, cache_v, cache_kidx, state_rwkv, state_shift, state_ret, state_mlstm_C, state_mlstm_n, state_mlstm_m, state_conv, page_table, norm_g, ffn_wg, ffn_wu, ffn_wd, ab_w_in, ab_w_out, kidx_g, rwkv_mu, rwkv_w0, rwkv_w2, rwkv_a0, rwkv_a2, rwkv_g2, rwkv_kk, rwkv_ka, rwkv_rk, rwkv_lnx_g, rwkv_lnx_b, cd_w_in, cd_w_out, ret_gn, conv_w, conv_b, mlstm_if_b, mlstm_gn):
    P = dict(norm_g=norm_g, ffn_wg=ffn_wg, ffn_wu=ffn_wu, ffn_wd=ffn_wd, ab_w_in=ab_w_in, ab_w_out=ab_w_out,
             kidx_g=kidx_g, rwkv_mu=rwkv_mu, rwkv_w0=rwkv_w0, rwkv_w2=rwkv_w2, rwkv_a0=rwkv_a0, rwkv_a2=rwkv_a2,
             rwkv_g2=rwkv_g2, rwkv_kk=rwkv_kk, rwkv_ka=rwkv_ka, rwkv_rk=rwkv_rk, rwkv_lnx_g=rwkv_lnx_g,
             rwkv_lnx_b=rwkv_lnx_b, cd_w_in=cd_w_in, cd_w_out=cd_w_out, ret_gn=ret_gn, conv_w=conv_w,
             conv_b=conv_b, mlstm_if_b=mlstm_if_b, mlstm_gn=mlstm_gn)
    B, T, _ = x_prompt.shape
    DB, DS, _ = x_sample.shape
    past = page_table.shape[1] * PAGE_SIZE

    def ab_zero(e):
        return (jnp.zeros((B, D_B_IN), F32), jnp.zeros((B, H_B, DH_B, DH_B), F32))

    def cd_zero(o):
        return (jnp.zeros((B, H_C, DK_C, DV_C), F32), jnp.zeros((B, H_D, DV_D, DK_D), F32),
                jnp.zeros((B, H_D, DK_D), F32), jnp.zeros((B, H_D), F32),
                jnp.zeros((B, CONV_W - 1, 2 * H_D * DK_D), F32))

    def ab_cached(e):
        return (state_shift[e], state_rwkv[e])

    def cd_cached(o):
        return (state_ret[o], state_mlstm_C[o], state_mlstm_n[o], state_mlstm_m[o], state_conv[o])

    y_p, (kp, vp, kip, rwp, shp), (rtp, cp, nvp, mp, cvp) = _trunk(
        x_prompt, jnp.arange(T), True, ab_zero, cd_zero, None, P)
    y_s, (ks_, vs_, kis, rws, shs), (rts, cs, nvs, ms, cvs) = _trunk(
        x_sample, past + jnp.arange(DS), False, ab_cached, cd_cached,
        (cache_k, cache_v, cache_kidx, page_table), P)
    return (y_p, y_s, kp, vp, kip, rwp, shp, rtp, cp, nvp, mp, cvp,
            ks_, vs_, kis, rws, shs, rts, cs, nvs, ms, cvs)
```

```python
import functools
import math

import numpy as np
import jax
import jax.numpy as jnp
from jax import lax
from jax.experimental import pallas as pl
from jax.experimental.pallas import tpu as pltpu

F32 = jnp.float32
BF16 = jnp.bfloat16
I32 = jnp.int32
HI = lax.Precision.HIGHEST

LANES = 128
SUBLANES = 8
VMEM_LIMIT = 56 * 1024 * 1024

EPS = 1e-6
PAGE_SIZE = 128
H_A, DH_A, H_IDX, D_IDX, TOPK_MAX = 8, 64, 16, 64, 256
H_B, DH_B, R_DECAY, R_AAA, R_GATE = 8, 64, 64, 64, 128
D_B_IN = 3 * H_B * DH_B + R_DECAY + R_AAA + R_GATE
LNX_EPS = 64e-5
H_C, DK_C, DV_C, ROPE_BASE = 4, 64, 128, 10000.0
H_D, DK_D, DV_D, CONV_W = 4, 64, 128, 4
HN_EPS = 1e-5
NEG_BIG = -1e30
INT_MIN = -(2 ** 31)
INT_MAX = 2 ** 31 - 1
KEY_NEG_INF = (0xFF800000 ^ 0x7FFFFFFF) - 2 ** 32

NT = (((1,), (1,)), ((), ()))
TN = (((0,), (0,)), ((), ()))


def _cparams(sem):
    return pltpu.CompilerParams(dimension_semantics=sem, vmem_limit_bytes=VMEM_LIMIT)


def _rms(x, g):
    return x * lax.rsqrt(jnp.mean(x * x, axis=-1, keepdims=True) + EPS) * g


def _softplus(x):
    return jnp.maximum(x, 0.0) + jnp.log(1.0 + jnp.exp(-jnp.abs(x)))


def _row_tile(m, want):
    t = min(want, m)
    while m % t:
        t //= 2
    return t


def _col_tile(n, want):
    best = LANES
    for t in range(LANES, min(n, want) + 1, LANES):
        if n % t == 0:
            best = t
    return best


def _ffn_body(x_ref, g0_ref, g1_ref, wg_ref, wu_ref, wd_ref, o_ref, h_scr, acc_scr):
    j = pl.program_id(1)

    @pl.when(j == 0)
    def _():
        h_scr[...] = _rms(x_ref[...], g0_ref[...]).astype(BF16)
        acc_scr[...] = jnp.zeros_like(acc_scr)

    h = h_scr[...]
    g = jnp.dot(h, wg_ref[...], preferred_element_type=F32)
    u = jnp.dot(h, wu_ref[...], preferred_element_type=F32)
    a = g * jax.nn.sigmoid(g) * u
    acc_scr[...] += jnp.dot(a.astype(BF16), wd_ref[...], preferred_element_type=F32)

    @pl.when(j == pl.num_programs(1) - 1)
    def _():
        o_ref[...] = x_ref[...] + 0.5 * _rms(acc_scr[...], g1_ref[...])


def ffn_call(x, g0, g1, wg, wu, wd):
    m, d = x.shape
    ff = wg.shape[1]
    tm = _row_tile(m, 512)
    tf = _col_tile(ff, 1408)
    return pl.pallas_call(
        _ffn_body,
        grid=(m // tm, ff // tf),
        in_specs=[
            pl.BlockSpec((tm, d), lambda i, j: (i, 0)),
            pl.BlockSpec((1, d), lambda i, j: (0, 0)),
            pl.BlockSpec((1, d), lambda i, j: (0, 0)),
            pl.BlockSpec((d, tf), lambda i, j: (0, j)),
            pl.BlockSpec((d, tf), lambda i, j: (0, j)),
            pl.BlockSpec((tf, d), lambda i, j: (j, 0)),
        ],
        out_specs=pl.BlockSpec((tm, d), lambda i, j: (i, 0)),
        out_shape=jax.ShapeDtypeStruct((m, d), F32),
        scratch_shapes=[pltpu.VMEM((tm, d), BF16), pltpu.VMEM((tm, d), F32)],
        compiler_params=_cparams(("parallel", "arbitrary")),
        name="ffn",
    )(x, g0.reshape(1, d), g1.reshape(1, d), wg, wu, wd)


def _proj_in_body(x_ref, g_ref, w_ref, o_ref, h_scr):
    @pl.when(pl.program_id(1) == 0)
    def _():
        h_scr[...] = _rms(x_ref[...], g_ref[...]).astype(BF16)

    o_ref[...] = jnp.dot(h_scr[...], w_ref[...], preferred_element_type=F32).astype(o_ref.dtype)


def proj_in_call(x, g, w, out_dtype=F32):
    m, d = x.shape
    n = w.shape[1]
    tm = _row_tile(m, 512)
    tn = _col_tile(n, 2048)
    return pl.pallas_call(
        _proj_in_body,
        grid=(m // tm, n // tn),
        in_specs=[
            pl.BlockSpec((tm, d), lambda i, j: (i, 0)),
            pl.BlockSpec((1, d), lambda i, j: (0, 0)),
            pl.BlockSpec((d, tn), lambda i, j: (0, j)),
        ],
        out_specs=pl.BlockSpec((tm, tn), lambda i, j: (i, j)),
        out_shape=jax.ShapeDtypeStruct((m, n), out_dtype),
        scratch_shapes=[pltpu.VMEM((tm, d), BF16)],
        compiler_params=_cparams(("parallel", "arbitrary")),
        name="proj_in",
    )(x, g.reshape(1, d), w)


def _kiwi_body(x_ref, g_ref, w_ref, kg_ref, o_ref, k2_ref):
    h = _rms(x_ref[...], g_ref[...]).astype(BF16)
    z = jnp.dot(h, w_ref[...], preferred_element_type=F32)
    z1 = z[:, :LANES]
    lane = lax.broadcasted_iota(I32, z1.shape, 1)
    is_k = lane < D_IDX
    ms = jnp.sum(jnp.where(is_k, z1 * z1, 0.0), axis=-1, keepdims=True) * (1.0 / D_IDX)
    inv = lax.rsqrt(ms + EPS)
    kg = kg_ref[...]
    o_ref[...] = jnp.where(is_k, z1 * inv * kg[:, :LANES], z1 * (H_IDX * D_IDX) ** -0.5)
    k2_ref[...] = (z[:, LANES:] * inv * kg[:, LANES:]).astype(BF16)


def kiwi_call(x, g, w_ki, w_wi, kidx_g):
    m, d = x.shape
    tm = _row_tile(m, 512)
    zpad = jnp.zeros((d, LANES - D_IDX - H_IDX), F32)
    w = jnp.concatenate([w_ki, w_wi, zpad, w_ki, w_ki], axis=1).astype(BF16)
    gpad = jnp.zeros((LANES - D_IDX,), F32)
    kg = jnp.concatenate([kidx_g, gpad, kidx_g, kidx_g]).reshape(1, 2 * LANES)
    return pl.pallas_call(
        _kiwi_body,
        grid=(m // tm,),
        in_specs=[
            pl.BlockSpec((tm, d), lambda i: (i, 0)),
            pl.BlockSpec((1, d), lambda i: (0, 0)),
            pl.BlockSpec((d, 2 * LANES), lambda i: (0, 0)),
            pl.BlockSpec((1, 2 * LANES), lambda i: (0, 0)),
        ],
        out_specs=[pl.BlockSpec((tm, LANES), lambda i: (i, 0))] * 2,
        out_shape=[jax.ShapeDtypeStruct((m, LANES), F32), jax.ShapeDtypeStruct((m, LANES), BF16)],
        compiler_params=_cparams(("parallel",)),
        name="kiwi",
    )(x, g.reshape(1, d), w, kg)


def _proj_out_body(a1_ref, a2_ref, x_ref, g_ref, w1_ref, w2_ref, o_ref):
    y = jnp.dot(a1_ref[...].astype(BF16), w1_ref[...], preferred_element_type=F32)
    y = y + jnp.dot(a2_ref[...].astype(BF16), w2_ref[...], preferred_element_type=F32)
    o_ref[...] = x_ref[...] + _rms(y, g_ref[...])


def proj_out_call(a1, a2, x, g, w1, w2):
    m, d = x.shape
    k1, k2 = a1.shape[1], a2.shape[1]
    tm = _row_tile(m, 512)
    return pl.pallas_call(
        _proj_out_body,
        grid=(m // tm,),
        in_specs=[
            pl.BlockSpec((tm, k1), lambda i: (i, 0)),
            pl.BlockSpec((tm, k2), lambda i: (i, 0)),
            pl.BlockSpec((tm, d), lambda i: (i, 0)),
            pl.BlockSpec((1, d), lambda i: (0, 0)),
            pl.BlockSpec((k1, d), lambda i: (0, 0)),
            pl.BlockSpec((k2, d), lambda i: (0, 0)),
        ],
        out_specs=pl.BlockSpec((tm, d), lambda i: (i, 0)),
        out_shape=jax.ShapeDtypeStruct((m, d), F32),
        compiler_params=_cparams(("parallel",)),
        name="proj_out",
    )(a1, a2, x, g.reshape(1, d), w1, w2)


def _topk_bias(isc, col, topk, nbits_col):
    bits = pltpu.bitcast(isc, I32)
    key = jnp.where(bits < 0, bits ^ 0x7FFFFFFF, bits)
    kf = float(topk)

    def count(mask):
        return jnp.sum(jnp.where(mask, 1.0, 0.0), axis=-1, keepdims=True)

    prefix = jnp.where(count(key >= 0) >= kf, 0, INT_MIN).astype(I32)

    def bit_step(i, prefix):
        cand = prefix + jnp.left_shift(jnp.int32(1), 30 - i)
        return jnp.where(count(key >= cand) >= kf, cand, prefix)

    thr = lax.fori_loop(0, 31, bit_step, prefix)
    gt = key > thr
    eq = key == thr
    need = kf - count(gt)

    def col_step(i, y):
        cand = y + jnp.left_shift(jnp.int32(1), nbits_col - 1 - i)
        c = jnp.sum(jnp.where(eq, jnp.where(col < cand, 1.0, 0.0), 0.0), axis=-1, keepdims=True)
        return jnp.where(c < need, cand, y)

    tied = jnp.where(count(key >= thr) > kf, jnp.where(thr > KEY_NEG_INF, 1.0, 0.0), 0.0)
    y = lax.cond(jnp.max(tied) > 0.0,
                 lambda: lax.fori_loop(0, nbits_col, col_step, jnp.zeros_like(thr)),
                 lambda: jnp.full_like(thr, INT_MAX))
    ninf = -jnp.inf
    bias = jnp.where(gt, 0.0, jnp.where(eq, jnp.where(col <= y, 0.0, ninf), ninf))
    return jnp.where(jnp.abs(isc) < jnp.inf, bias, ninf)


DSA_CAUSAL_BANDS = 4


def _dsa_prompt_body(qkv_q_ref, qkv_k_ref, qkv_v_ref, qi_ref, ki2_ref, kw_ref, o_ref, kb_scr, vb_scr, *,
                     topk, q_first):
    qb = qkv_q_ref.shape[1]
    t = qkv_k_ref.shape[1]
    j = pl.program_id(1)
    t0 = (j + q_first) * qb

    @pl.when(j == 0)
    def _():
        kb_scr[...] = qkv_k_ref[0].astype(BF16)
        vb_scr[...] = qkv_v_ref[0].astype(BF16)

    lane = lax.broadcasted_iota(I32, (1, LANES), 1)
    half = (jnp.where(lane < DH_A, 1.0, 0.0), jnp.where(lane < DH_A, 0.0, 1.0))
    half_b = tuple(m.astype(BF16) for m in half)
    ki2 = ki2_ref[0]
    kw = kw_ref[0]
    isc = jnp.zeros((qb, t), F32)
    for hp in range(H_IDX // 2):
        qp = qi_ref[0, :, hp * LANES:(hp + 1) * LANES]
        for h2 in range(2):
            h = 2 * hp + h2
            s = lax.dot_general(qp * half_b[h2], ki2, NT, preferred_element_type=F32)
            isc = isc + kw[:, D_IDX + h:D_IDX + h + 1] * jnp.maximum(s, 0.0)
    row = lax.broadcasted_iota(I32, (qb, t), 0) + t0
    col = lax.broadcasted_iota(I32, (qb, t), 1)
    isc = jnp.where(col <= row, isc, -jnp.inf)
    bias = _topk_bias(isc, col, topk, max(1, (t - 1).bit_length()))
    scale = DH_A ** -0.5
    assert math.frexp(scale)[0] == 0.5
    for p in range(H_A // 2):
        sl = slice(p * LANES, (p + 1) * LANES)
        qp = qkv_q_ref[0, :, sl]
        kp = kb_scr[:, sl]
        vp = vb_scr[:, sl]
        outs = []
        for h2 in range(2):
            qm = (qp * (half[h2] * scale)).astype(BF16)
            lg = lax.dot_general(qm, kp, NT, preferred_element_type=F32) + bias
            mx = jnp.max(lg, axis=-1, keepdims=True)
            pr = jnp.exp(lg - mx)
            l = jnp.sum(pr, axis=-1, keepdims=True)
            outs.append(jnp.dot(pr.astype(BF16), vp, preferred_element_type=F32) / l)
        o_ref[0, :, sl] = jnp.where(lane < DH_A, outs[0], outs[1])


def dsa_prompt_call(qkv, qi, ki2, kw):
    b, t, _ = qkv.shape
    hd = H_A * DH_A
    qb = _row_tile(t, 256)
    topk = min(TOPK_MAX, t // 4)
    n_qb = t // qb
    per_band = n_qb // math.gcd(n_qb, DSA_CAUSAL_BANDS)
    outs = []
    for q_first in range(0, n_qb, per_band):
        tk = (q_first + per_band) * qb
        qmap = lambda i, j, q_first=q_first: (i, j + q_first, 0)
        outs.append(pl.pallas_call(
            functools.partial(_dsa_prompt_body, topk=topk, q_first=q_first),
            grid=(b, per_band),
            in_specs=[
                pl.BlockSpec((1, qb, hd), qmap),
                pl.BlockSpec((1, tk, hd), lambda i, j: (i, 0, 1)),
                pl.BlockSpec((1, tk, hd), lambda i, j: (i, 0, 2)),
                pl.BlockSpec((1, qb, H_IDX * D_IDX), qmap),
                pl.BlockSpec((1, tk, LANES), lambda i, j: (i, 0, 0)),
                pl.BlockSpec((1, qb, LANES), qmap),
            ],
            out_specs=pl.BlockSpec((1, qb, hd), lambda i, j: (i, j, 0)),
            out_shape=jax.ShapeDtypeStruct((b, per_band * qb, hd), F32),
            scratch_shapes=[pltpu.VMEM((tk, hd), BF16), pltpu.VMEM((tk, hd), BF16)],
            compiler_params=_cparams(("parallel", "arbitrary")),
            name="dsa_prompt",
        )(qkv, qkv, qkv, qi, ki2, kw))
    return outs[0] if len(outs) == 1 else jnp.concatenate(outs, axis=1)


DSA_PAGE_GROUP = 16


def _page_group(n_pages):
    return math.gcd(n_pages, DSA_PAGE_GROUP)


def _page_specs(shape, n_pages, group):
    zeros = (0,) * (len(shape) - 1)

    def spec(g):
        return pl.BlockSpec(shape, lambda b, s, pt: (pt[b, jnp.minimum(s * group + g, n_pages - 1)],) + zeros)
    return [spec(g) for g in range(group)]


def _page_cat(refs):
    return jnp.concatenate([r[0].reshape(-1, PAGE_SIZE).astype(BF16) for r in refs], axis=1)


def _dsa_sidx_body(pt_ref, qi_ref, wi_ref, kin_ref, *rest):
    kc_refs, o_ref, onew_ref = rest[:-2], rest[-2], rest[-1]
    nq = o_ref.shape[1]
    r = lax.broadcasted_iota(I32, (nq, nq * H_IDX), 0)
    c = lax.broadcasted_iota(I32, (nq, nq * H_IDX), 1)
    lo = r * H_IDX
    wsel = jnp.where(c >= lo, jnp.where(c < lo + H_IDX, wi_ref[0], 0.0), 0.0)

    def scores(ki):
        s = jnp.dot(qi_ref[0], ki, preferred_element_type=F32)
        return jnp.dot(wsel, jnp.maximum(s, 0.0), precision=HI, preferred_element_type=F32)

    @pl.when(pl.program_id(1) == 0)
    def _():
        onew_ref[0] = scores(kin_ref[0].astype(BF16))

    o_ref[0] = scores(_page_cat(kc_refs))


def dsa_sidx_call(page_table, qi, wi, ki_new, cache_kidx):
    db, n_pages = page_table.shape
    nq = qi.shape[1] // H_IDX
    group = _page_group(n_pages)
    grid_spec = pltpu.PrefetchScalarGridSpec(
        num_scalar_prefetch=1,
        grid=(db, n_pages // group),
        in_specs=[
            pl.BlockSpec((1, nq * H_IDX, D_IDX), lambda b, s, pt: (b, 0, 0)),
            pl.BlockSpec((1, 1, nq * H_IDX), lambda b, s, pt: (b, 0, 0)),
            pl.BlockSpec((1, D_IDX, PAGE_SIZE), lambda b, s, pt: (b, 0, 0)),
        ] + _page_specs((1, D_IDX, PAGE_SIZE), n_pages, group),
        out_specs=[pl.BlockSpec((1, nq, group * PAGE_SIZE), lambda b, s, pt: (b, 0, s)),
                   pl.BlockSpec((1, nq, PAGE_SIZE), lambda b, s, pt: (b, 0, 0))],
    )
    return pl.pallas_call(
        _dsa_sidx_body,
        grid_spec=grid_spec,
        out_shape=[jax.ShapeDtypeStruct((db, nq, n_pages * PAGE_SIZE), F32),
                   jax.ShapeDtypeStruct((db, nq, PAGE_SIZE), F32)],
        compiler_params=_cparams(("parallel", "arbitrary")),
        name="dsa_sample_idx",
    )(page_table, qi, wi, ki_new, *([cache_kidx] * group))


def _dsa_satt_body(pt_ref, isc_ref, iscn_ref, q_ref, kn_ref, vn_ref, *rest, n_pages, n_steps, topk, n_new):
    group = n_pages // n_steps
    kc_refs, vc_refs = rest[:group], rest[group:2 * group]
    o_ref, bias_scr, qe_scr, m_scr, l_scr, acc_scr = rest[2 * group:]
    p = pl.program_id(1)
    nq = q_ref.shape[1]
    past = n_pages * PAGE_SIZE
    wcols = group * PAGE_SIZE
    ltot = past + PAGE_SIZE
    hd = H_A * DH_A
    scale = DH_A ** -0.5
    assert math.frexp(scale)[0] == 0.5

    def attend(kcat, vcat, bias):
        lg = jnp.dot(qe_scr[...], kcat, preferred_element_type=F32)
        lg = lg + jnp.concatenate([bias] * H_A, axis=0)
        m_old = m_scr[...]
        m_new = jnp.maximum(m_old, jnp.max(lg, axis=-1, keepdims=True))
        alpha = jnp.exp(m_old - m_new)
        pr = jnp.exp(lg - m_new)
        l_scr[...] = alpha * l_scr[...] + jnp.sum(pr, axis=-1, keepdims=True)
        acc_scr[...] = alpha * acc_scr[...] + lax.dot_general(pr.astype(BF16), vcat, NT, preferred_element_type=F32)
        m_scr[...] = m_new

    @pl.when(p == 0)
    def _():
        isc = jnp.concatenate([isc_ref[0], iscn_ref[0]], axis=1)
        row = lax.broadcasted_iota(I32, (nq, ltot), 0)
        col = lax.broadcasted_iota(I32, (nq, ltot), 1)
        rel = col - past
        ninf = -jnp.inf
        isc = jnp.where(rel < 0, isc, jnp.where(rel < n_new, jnp.where(rel <= row, isc, ninf), ninf))
        bias = _topk_bias(isc, col, topk, max(1, (ltot - 1).bit_length()))
        for i in range(n_steps):
            bias_scr[i] = bias[:, i * wcols:(i + 1) * wcols]
        q = q_ref[0] * scale
        lane = lax.broadcasted_iota(I32, (nq, hd), 1)
        qe_scr[...] = jnp.concatenate(
            [jnp.where(lane >= h * DH_A, jnp.where(lane < (h + 1) * DH_A, q, 0.0), 0.0) for h in range(H_A)],
            axis=0).astype(BF16)
        m_scr[...] = jnp.full_like(m_scr, NEG_BIG)
        l_scr[...] = jnp.zeros_like(l_scr)
        acc_scr[...] = jnp.zeros_like(acc_scr)
        attend(kn_ref[0].reshape(hd, PAGE_SIZE).astype(BF16), vn_ref[0].reshape(hd, PAGE_SIZE).astype(BF16),
               bias[:, past:])

    attend(_page_cat(kc_refs), _page_cat(vc_refs), bias_scr[p])

    @pl.when(p == n_steps - 1)
    def _():
        o = acc_scr[...] / l_scr[...]
        lane = lax.broadcasted_iota(I32, (nq, hd), 1)
        out = jnp.zeros((nq, hd), F32)
        for h in range(H_A):
            oh = o[h * nq:(h + 1) * nq]
            out = out + jnp.where(lane >= h * DH_A, jnp.where(lane < (h + 1) * DH_A, oh, 0.0), 0.0)
        o_ref[0] = out


def dsa_satt_call(page_table, isc, isc_new, q, k_new, v_new, cache_k, cache_v, n_new):
    db, n_pages = page_table.shape
    nq = q.shape[1]
    hd = H_A * DH_A
    page = (1, H_A, DH_A, PAGE_SIZE)
    group = _page_group(n_pages)
    n_steps = n_pages // group
    wcols = group * PAGE_SIZE
    topk = min(TOPK_MAX, (n_pages * PAGE_SIZE + n_new) // 4)
    grid_spec = pltpu.PrefetchScalarGridSpec(
        num_scalar_prefetch=1,
        grid=(db, n_steps),
        in_specs=[
            pl.BlockSpec((1, nq, n_pages * PAGE_SIZE), lambda b, p, pt: (b, 0, 0)),
            pl.BlockSpec((1, nq, PAGE_SIZE), lambda b, p, pt: (b, 0, 0)),
            pl.BlockSpec((1, nq, hd), lambda b, p, pt: (b, 0, 0)),
            pl.BlockSpec(page, lambda b, p, pt: (b, 0, 0, 0)),
            pl.BlockSpec(page, lambda b, p, pt: (b, 0, 0, 0)),
        ] + _page_specs(page, n_pages, group) * 2,
        out_specs=pl.BlockSpec((1, nq, hd), lambda b, p, pt: (b, 0, 0)),
        scratch_shapes=[
            pltpu.VMEM((n_steps, nq, wcols), F32),
            pltpu.VMEM((H_A * nq, hd), BF16),
            pltpu.VMEM((H_A * nq, 1), F32),
            pltpu.VMEM((H_A * nq, 1), F32),
            pltpu.VMEM((H_A * nq, hd), F32),
        ],
    )
    return pl.pallas_call(
        functools.partial(_dsa_satt_body, n_pages=n_pages, n_steps=n_steps, topk=topk, n_new=n_new),
        grid_spec=grid_spec,
        out_shape=jax.ShapeDtypeStruct((db, nq, hd), F32),
        compiler_params=_cparams(("parallel", "arbitrary")),
        name="dsa_sample_att",
    )(page_table, isc, isc_new, q, k_new, v_new, *([cache_k] * group), *([cache_v] * group))


def _head_sum_matrix(n, group):
    r = lax.broadcasted_iota(I32, (n, n), 0) // group
    c = lax.broadcasted_iota(I32, (n, n), 1) // group
    return jnp.where(r == c, 1.0, 0.0).astype(F32)


def _rwkv_prep_body(zb_ref, sp_ref, mu_ref, w0_ref, a0_ref, w2_ref, a2_ref, g2_ref, kkp_ref, ka_ref,
                    r_ref, w_ref, k_ref, v_ref, an_ref, b_ref, g_ref, carry_scr):
    tc = pl.program_id(1)
    zb = zb_ref[0]
    tt = zb.shape[0]
    hd = H_B * DH_B

    @pl.when(tc == 0)
    def _():
        carry_scr[...] = sp_ref[0]

    rows = lax.broadcasted_iota(I32, zb.shape, 0)
    prev = jnp.where(rows == 0, carry_scr[...], pltpu.roll(zb, 1, 0))
    carry_scr[...] = zb[tt - 1:tt]
    z = zb + (prev - zb) * mu_ref[...]
    r = z[:, 0:hd]
    k = z[:, hd:2 * hd]
    v = z[:, 2 * hd:3 * hd]
    xwa = z[:, 3 * hd:3 * hd + R_DECAY + R_AAA]
    xg = z[:, 3 * hd + R_DECAY + R_AAA:]
    wl = w0_ref[...] + jnp.dot(jnp.tanh(xwa).astype(BF16), w2_ref[...], preferred_element_type=F32)
    w_log = -_softplus(-wl) - 0.5
    log_decay = -jnp.exp(w_log)
    a = jax.nn.sigmoid(a0_ref[...] + jnp.dot(xwa.astype(BF16), a2_ref[...], preferred_element_type=F32))
    g = jnp.dot(jax.nn.sigmoid(xg).astype(BF16), g2_ref[...], preferred_element_type=F32)
    kk = k * kkp_ref[...]
    ss = jnp.dot(kk * kk, _head_sum_matrix(hd, DH_B), precision=HI, preferred_element_type=F32)
    kk = kk * lax.rsqrt(jnp.maximum(ss, 1e-24))
    r_ref[0] = r
    w_ref[0] = log_decay
    k_ref[0] = k * (1.0 + (a - 1.0) * ka_ref[...])
    v_ref[0] = v
    an_ref[0] = -kk
    b_ref[0] = kk * a
    g_ref[0] = g


def rwkv_prep_call(zb, shift_prev, mu, w0, a0, w2p, a2p, g2, kkp, ka):
    b, t, dz = zb.shape
    hd = H_B * DH_B
    tt = _row_tile(t, 256)
    row = lambda n: pl.BlockSpec((1, n), lambda i, j: (0, 0))
    full = lambda s: pl.BlockSpec(s, lambda i, j: (0, 0))
    out = pl.BlockSpec((1, tt, hd), lambda i, j: (i, j, 0))
    return pl.pallas_call(
        _rwkv_prep_body,
        grid=(b, t // tt),
        in_specs=[
            pl.BlockSpec((1, tt, dz), lambda i, j: (i, j, 0)),
            pl.BlockSpec((1, 1, dz), lambda i, j: (i, 0, 0)),
            row(dz), row(hd), row(hd),
            full((R_DECAY + R_AAA, hd)), full((R_DECAY + R_AAA, hd)), full((R_GATE, hd)),
            row(hd), row(hd),
        ],
        out_specs=[out] * 7,
        out_shape=[jax.ShapeDtypeStruct((b, t, hd), F32)] * 7,
        scratch_shapes=[pltpu.VMEM((1, dz), F32)],
        compiler_params=_cparams(("parallel", "arbitrary")),
        name="rwkv_prep",
    )(zb, shift_prev, mu, w0, a0, w2p, a2p, g2, kkp, ka)


RWKV_PAIRS = H_B // 2
RWKV_STEPS = SUBLANES // RWKV_PAIRS
RWKV_NB = 8
RWKV_SUM_PIECES = 2


def _rwkv_scan_body(r_ref, w_ref, k_ref, v_ref, a_ref, b_ref, s0_ref, y_ref, sout_ref,
                    s_scr, pa_scr, pv_scr, *, t_valid):
    c = pl.program_id(1)
    nb, tc = r_ref.shape[0], r_ref.shape[1]
    chains = [(bi, p) for bi in range(nb) for p in range(RWKV_PAIRS)]
    rows = lambda n: slice(n * DH_B, (n + 1) * DH_B)

    @pl.when(c == 0)
    def _():
        for n, (bi, p) in enumerate(chains):
            s_scr[rows(n), :] = s0_ref[bi, p]

    if t_valid < tc * RWKV_STEPS:
        y_ref[...] = jnp.zeros_like(y_ref)

    sub = lax.broadcasted_iota(I32, (DH_B, LANES), 0)
    lane = lax.broadcasted_iota(I32, (DH_B, LANES), 1)
    isel = jnp.where((lane & (DH_B - 1)) == sub, 1.0, 0.0).astype(F32)
    qblk = _head_sum_matrix(LANES, DH_B).astype(BF16)

    def hsum(ref, pieces=3):
        x = ref[...]
        parts = []
        for _ in range(pieces - 1):
            h = x.astype(BF16)
            parts.append(h)
            x = x - h.astype(F32)
        parts.append(x.astype(BF16))
        return jnp.dot(jnp.concatenate(parts, axis=1), jnp.concatenate([qblk] * pieces, axis=0),
                       preferred_element_type=F32)

    def step(u, carry):
        tiles = [(a_ref[bi, u], jnp.exp(w_ref[bi, u]), k_ref[bi, u], v_ref[bi, u], b_ref[bi, u], r_ref[bi, u])
                 for bi in range(nb)]
        for i in range(RWKV_STEPS):
            for n, (bi, p) in enumerate(chains):
                at, _, _, vt, _, _ = tiles[bi]
                row = slice(i * RWKV_PAIRS + p, i * RWKV_PAIRS + p + 1)
                pa_scr[rows(n), :] = s_scr[rows(n), :] * at[row]
                pv_scr[rows(n), :] = isel * vt[row]
            sa = hsum(pa_scr, RWKV_SUM_PIECES)
            vc = hsum(pv_scr)
            for n, (bi, p) in enumerate(chains):
                _, wt, kt, _, bt, rt = tiles[bi]
                row = slice(i * RWKV_PAIRS + p, i * RWKV_PAIRS + p + 1)
                s = s_scr[rows(n), :] * wt[row] + sa[rows(n)] * bt[row] + vc[rows(n)] * kt[row]
                s_scr[rows(n), :] = s
                pa_scr[rows(n), :] = s * rt[row]
            yb = hsum(pa_scr, RWKV_SUM_PIECES)
            for n, (bi, p) in enumerate(chains):
                row = slice(i * RWKV_PAIRS + p, i * RWKV_PAIRS + p + 1)
                y_ref[bi, u, row, :] = jnp.sum(yb[rows(n)] * isel, axis=0, keepdims=True)
        return carry

    lax.fori_loop(0, min(t_valid, tc * RWKV_STEPS) // RWKV_STEPS, step, 0)

    @pl.when(c == pl.num_programs(1) - 1)
    def _():
        for n, (bi, p) in enumerate(chains):
            sout_ref[bi, p] = s_scr[rows(n), :]


def rwkv_scan_call(r, w, k, v, a, b, s0, t_valid):
    bsz, t, hd = r.shape
    nb = math.gcd(bsz, RWKV_NB)
    nt = t // RWKV_STEPS
    tc = _row_tile(nt, 32)
    assert t % RWKV_STEPS == 0 and t_valid % RWKV_STEPS == 0 and (t_valid == t or nt == tc)
    tiled = lambda x: x.reshape(bsz, nt, SUBLANES, LANES)
    seq = pl.BlockSpec((nb, tc, SUBLANES, LANES), lambda i, j: (i, j, 0, 0))
    st = pl.BlockSpec((nb, RWKV_PAIRS, DH_B, LANES), lambda i, j: (i, 0, 0, 0))
    y, s_fin = pl.pallas_call(
        functools.partial(_rwkv_scan_body, t_valid=t_valid),
        grid=(bsz // nb, nt // tc),
        in_specs=[seq] * 6 + [st],
        out_specs=[seq, st],
        out_shape=[jax.ShapeDtypeStruct((bsz, nt, SUBLANES, LANES), F32),
                   jax.ShapeDtypeStruct((bsz, RWKV_PAIRS, DH_B, LANES), F32)],
        scratch_shapes=[pltpu.VMEM((nb * RWKV_PAIRS * DH_B, LANES), F32)] * 3,
        compiler_params=_cparams(("parallel", "arbitrary")),
        name="rwkv_scan",
    )(tiled(r), tiled(w), tiled(k), tiled(v), tiled(a), tiled(b), s0)
    return y.reshape(bsz, t, hd), s_fin


RWKV_CHUNK = 64
RWKV_HG = 4
RWKV_CHUNK_SEQS = 2


def _rwkv_chunk_body(r_ref, lw_ref, k_ref, v_ref, a_ref, b_ref, s0_ref, y_ref, sout_ref, s_scr):
    c = pl.program_id(1)
    cs = r_ref.shape[1]
    ng = s_scr.shape[1]
    gw = RWKV_HG * DH_B
    n = RWKV_HG * cs
    assert cs & (cs - 1) == 0

    @pl.when(c == 0)
    def _():
        s_scr[...] = s0_ref[...]

    tr = lax.broadcasted_iota(I32, (cs, cs), 0)
    tcol = lax.broadcasted_iota(I32, (cs, cs), 1)
    tril = jnp.where(tr >= tcol, 1.0, 0.0).astype(F32)
    row = lax.broadcasted_iota(I32, (n, n), 0)
    col = lax.broadcasted_iota(I32, (n, n), 1)
    same = (row // cs) == (col // cs)
    tt = row & (cs - 1)
    ss = col & (cs - 1)
    strict = jnp.where(same, jnp.where(tt > ss, 1.0, 0.0), 0.0).astype(F32)
    incl = jnp.where(same, jnp.where(tt >= ss, 1.0, 0.0), 0.0).astype(F32)
    eye = jnp.where(row == col, 1.0, 0.0).astype(F32)
    lane_h = lax.broadcasted_iota(I32, (1, gw), 1) // DH_B
    hmask = [jnp.where(lane_h == h, 1.0, 0.0).astype(F32) for h in range(RWKV_HG)]
    sblk = jnp.where(lax.broadcasted_iota(I32, (gw, gw), 0) // DH_B == lax.broadcasted_iota(I32, (gw, gw), 1) // DH_B,
                     1.0, 0.0).astype(F32)
    nn = (((1,), (0,)), ((), ()))

    def stack(x):
        return jnp.concatenate([x * m for m in hmask], axis=0)

    def fold(x):
        out = x[0:cs]
        for h in range(1, RWKV_HG):
            out = out + x[h * cs:(h + 1) * cs]
        return out

    def bdot(x, y, dims=nn):
        return lax.dot_general(x.astype(BF16), y.astype(BF16), dims, preferred_element_type=F32)

    jobs = []
    for bi in range(r_ref.shape[0]):
        for g in range(ng):
            sl = slice(g * gw, (g + 1) * gw)
            lw = lw_ref[bi, :, sl]
            cl = jnp.dot(tril, lw, precision=HI, preferred_element_type=F32)
            e_in = jnp.exp(cl)
            e_inv = jnp.exp(-cl)
            e_tail = jnp.exp(cl[cs - 1:cs] - cl)
            jobs.append(dict(
                bi=bi, g=g, sl=sl, p_last=e_in[cs - 1:cs],
                at=a_ref[bi, :, sl] * jnp.exp(cl - lw),
                bt=b_ref[bi, :, sl] * e_inv, kt=k_ref[bi, :, sl] * e_inv, rt=r_ref[bi, :, sl] * e_in,
                bw=b_ref[bi, :, sl] * e_tail, kw=k_ref[bi, :, sl] * e_tail, v=v_ref[bi, :, sl],
                s_old=s_scr[bi, g]))
    for j in jobs:
        j['a_s'], j['r_s'], j['b_s'], j['k_s'], j['v_s'] = (stack(j[x]) for x in ('at', 'rt', 'bt', 'kt', 'v'))
    for j in jobs:
        j['lab'] = bdot(j['a_s'], j['b_s'], NT) * strict
        j['lak'] = bdot(j['a_s'], j['k_s'], NT) * strict
        j['lrb'] = bdot(j['r_s'], j['b_s'], NT) * incl
        j['lrk'] = bdot(j['r_s'], j['k_s'], NT) * incl
    for j in jobs:
        j['rhs'] = stack(bdot(j['at'], j['s_old'], NT)) + bdot(j['lak'], j['v_s'])
        j['tm'] = eye + j['lab']
        j['lp'] = j['lab']
    for _ in range(cs.bit_length() - 2):
        for j in jobs:
            j['lp'] = bdot(j['lp'], j['lp'])
        for j in jobs:
            j['tm'] = j['tm'] + bdot(j['lp'], j['tm'])
    for j in jobs:
        j['u_s'] = bdot(j['tm'], j['rhs'])
    for j in jobs:
        j['y'] = fold(stack(bdot(j['rt'], j['s_old'], NT)) + bdot(j['lrb'], j['u_s']) + bdot(j['lrk'], j['v_s']))
        u = fold(j['u_s'])
        j['s_new'] = j['s_old'] * j['p_last'] + (bdot(u, j['bw'], TN) + bdot(j['v'], j['kw'], TN)) * sblk
    for j in jobs:
        y_ref[j['bi'], :, j['sl']] = j['y']
        s_scr[j['bi'], j['g']] = j['s_new']

    @pl.when(c == pl.num_programs(1) - 1)
    def _():
        sout_ref[...] = s_scr[...]


def rwkv_chunk_call(r, lw, k, v, a, b, s0):
    bsz, t, hd = r.shape
    cs = RWKV_CHUNK
    ng = H_B // RWKV_HG
    gw = RWKV_HG * DH_B
    eye = jnp.eye(RWKV_HG, dtype=F32)
    s0g = jnp.einsum('bghij,hk->bghikj', s0.astype(F32).reshape(bsz, ng, RWKV_HG, DH_B, DH_B), eye)
    nb = math.gcd(bsz, RWKV_CHUNK_SEQS)
    seq = pl.BlockSpec((nb, cs, hd), lambda i, j: (i, j, 0))
    st = pl.BlockSpec((nb, ng, gw, gw), lambda i, j: (i, 0, 0, 0))
    y, s_fin = pl.pallas_call(
        _rwkv_chunk_body,
        grid=(bsz // nb, t // cs),
        in_specs=[seq] * 6 + [st],
        out_specs=[seq, st],
        out_shape=[jax.ShapeDtypeStruct((bsz, t, hd), F32), jax.ShapeDtypeStruct((bsz, ng, gw, gw), F32)],
        scratch_shapes=[pltpu.VMEM((nb, ng, gw, gw), F32)],
        compiler_params=_cparams(("parallel", "arbitrary")),
        name="rwkv_chunk",
    )(r, lw, k, v, a, b, s0g.reshape(bsz, ng, gw, gw))
    s_fin = jnp.einsum('bghihj->bghij', s_fin.reshape(bsz, ng, RWKV_HG, DH_B, RWKV_HG, DH_B))
    return y, s_fin.reshape(bsz, H_B, DH_B, DH_B)


def _rwkv_post_body(y_ref, r_ref, k_ref, v_ref, g_ref, lg_ref, lb_ref, rk_ref, o_ref):
    hd = H_B * DH_B
    avg = _head_sum_matrix(hd, DH_B) * (1.0 / DH_B)
    y = y_ref[...]
    mu = jnp.dot(y, avg, precision=HI, preferred_element_type=F32)
    d = y - mu
    var = jnp.dot(d * d, avg, precision=HI, preferred_element_type=F32)
    yn = d * lax.rsqrt(var + LNX_EPS) * lg_ref[...] + lb_ref[...]
    bonus = jnp.dot(r_ref[...] * k_ref[...] * rk_ref[...], avg * float(DH_B), precision=HI,
                    preferred_element_type=F32)
    o_ref[...] = (yn + bonus * v_ref[...]) * g_ref[...]


def rwkv_post_call(y, r, k, v, g, lnx_g, lnx_b, rk):
    m, hd = y.shape
    tm = _row_tile(m, 512)
    tok = pl.BlockSpec((tm, hd), lambda i: (i, 0))
    row = pl.BlockSpec((1, hd), lambda i: (0, 0))
    return pl.pallas_call(
        _rwkv_post_body,
        grid=(m // tm,),
        in_specs=[tok] * 5 + [row] * 3,
        out_specs=tok,
        out_shape=jax.ShapeDtypeStruct((m, hd), F32),
        compiler_params=_cparams(("parallel",)),
        name="rwkv_post",
    )(y, r, k, v, g, lnx_g, lnx_b, rk)


def _head_mask(n, h, width):
    lane = lax.broadcasted_iota(I32, (1, n), 1)
    return jnp.where(lane >= h * width, jnp.where(lane < (h + 1) * width, 1.0, 0.0), 0.0).astype(F32)


def _head_norm128(y, g):
    mu = jnp.mean(y, axis=-1, keepdims=True)
    d = y - mu
    var = jnp.mean(d * d, axis=-1, keepdims=True)
    return d * lax.rsqrt(var + HN_EPS) * g


def _ret_body(qk_ref, v_ref, gc_ref, cos_ref, sin_ref, gn_ref, s0_ref, y_ref, sout_ref, s_scr, *, l_valid):
    c = pl.program_id(1)
    lc = qk_ref.shape[1]
    hk = H_C * DK_C

    @pl.when(c == 0)
    def _():
        s_scr[...] = s0_ref[0]

    rr = lax.broadcasted_iota(I32, (hk, hk), 0)
    cc = lax.broadcasted_iota(I32, (hk, hk), 1)
    half = DK_C // 2
    same = (rr // DK_C) == (cc // DK_C)
    dr = rr & (DK_C - 1)
    dc = cc & (DK_C - 1)
    rot = jnp.where(same, jnp.where(dr == dc + half, -1.0, jnp.where(dr + half == dc, 1.0, 0.0)), 0.0).astype(F32)

    qk = qk_ref[0]
    cos = cos_ref[...]
    sin = sin_ref[...]

    def rope(x):
        return x * cos + jnp.dot(x, rot, precision=HI, preferred_element_type=F32) * sin

    qr = rope(qk[:, :hk])
    kr = rope(qk[:, hk:]) * DK_C ** -0.5
    krb = kr.astype(BF16)
    s_prev = s_scr[...]
    s_prev_b = s_prev.astype(BF16)

    jj = lax.broadcasted_iota(I32, (lc, lc), 0)
    ss = lax.broadcasted_iota(I32, (lc, lc), 1)
    diff = (jj - ss).astype(F32)
    jcol = lax.broadcasted_iota(I32, (lc, 1), 0).astype(F32)
    srow_state = lax.broadcasted_iota(I32, (hk, 1), 0) // DK_C
    s_new = jnp.zeros_like(s_prev)
    decay_rows = jnp.zeros((hk, 1), F32)
    for h in range(H_C):
        lg = math.log1p(-2.0 ** (-5.0 - h))
        mh = _head_mask(hk, h, DK_C)
        qm = (qr * mh).astype(BF16)
        dmask = jnp.where(diff >= 0, jnp.exp(jnp.maximum(diff, 0.0) * lg), 0.0)
        scores = lax.dot_general(qm, krb, NT, preferred_element_type=F32) * dmask
        vh = v_ref[0, :, h * DV_C:(h + 1) * DV_C]
        vhb = vh.astype(BF16)
        intra = jnp.dot(scores.astype(BF16), vhb, preferred_element_type=F32)
        inter = jnp.dot(qm, s_prev_b, preferred_element_type=F32) * jnp.exp((jcol + 1.0) * lg)
        yh = _head_norm128(intra + inter, gn_ref[h:h + 1, :])
        gch = gc_ref[0, :, h * DV_C:(h + 1) * DV_C]
        y_ref[0, :, h * DV_C:(h + 1) * DV_C] = yh * (gch * jax.nn.sigmoid(gch))
        w_s = jnp.where(jcol < l_valid, jnp.exp((l_valid - 1.0 - jcol) * lg), 0.0)
        kw = (kr * mh * w_s).astype(BF16)
        s_new = s_new + lax.dot_general(kw, vhb, TN, preferred_element_type=F32)
        decay_rows = jnp.where(srow_state == h, math.exp(l_valid * lg), decay_rows)
    s_fin = decay_rows * s_prev + s_new
    s_scr[...] = s_fin

    @pl.when(c == pl.num_programs(1) - 1)
    def _():
        sout_ref[0] = s_fin


def retention_call(za, cos, sin, gn, s0, l_valid):
    b, t, _ = za.shape
    lc = _row_tile(t, 128)
    assert l_valid == lc or t == lc
    hk, hv = H_C * DK_C, H_C * DV_C
    blk = lambda j: pl.BlockSpec((1, lc, 2 * hk), lambda i, c, j=j: (i, c, j))
    st = pl.BlockSpec((1, hk, DV_C), lambda i, c: (i, 0, 0))
    return pl.pallas_call(
        functools.partial(_ret_body, l_valid=l_valid),
        grid=(b, t // lc),
        in_specs=[blk(0), blk(1), blk(2),
                  pl.BlockSpec((lc, hk), lambda i, c: (c, 0)),
                  pl.BlockSpec((lc, hk), lambda i, c: (c, 0)),
                  pl.BlockSpec((H_C, DV_C), lambda i, c: (0, 0)),
                  st],
        out_specs=[pl.BlockSpec((1, lc, hv), lambda i, c: (i, c, 0)), st],
        out_shape=[jax.ShapeDtypeStruct((b, t, hv), F32), jax.ShapeDtypeStruct((b, hk, DV_C), F32)],
        scratch_shapes=[pltpu.VMEM((hk, DV_C), F32)],
        compiler_params=_cparams(("parallel", "arbitrary")),
        name="retention",
    )(za, za, za, cos, sin, gn, s0)


def _mlstm_body(qk_ref, v_ref, og_ref, gt_ref, cb_ref, cw_ref, cbias_ref, gbias_ref, gn_ref,
                c0_ref, n0_ref, m0_ref, y_ref, cout_ref, nout_ref, mout_ref,
                ext_scr, c_scr, n_scr, m_scr, *, l_valid):
    c = pl.program_id(1)
    lc = qk_ref.shape[1]
    hk = H_D * DK_D
    pad = SUBLANES

    @pl.when(c == 0)
    def _():
        ext_scr[0:pad, :] = cb_ref[0]
        c_scr[...] = c0_ref[0]
        n_scr[...] = n0_ref[0]
        m_scr[...] = m0_ref[0]

    u = qk_ref[0]
    ext_scr[pad:pad + lc, :] = u
    acc = cbias_ref[...] + u * cw_ref[CONV_W - 1:CONV_W, :]
    for i in range(CONV_W - 1):
        sh = CONV_W - 1 - i
        acc = acc + ext_scr[pad - sh:pad - sh + lc, :] * cw_ref[i:i + 1, :]
    ext_scr[0:pad, :] = u[lc - pad:lc, :]
    qkc = acc * jax.nn.sigmoid(acc)
    q = qkc[:, :hk]
    k = qkc[:, hk:] * DK_D ** -0.5
    kb = k.astype(BF16)

    gates = gt_ref[0] + gbias_ref[...]
    logsig = -_softplus(-gates)
    rowi = lax.broadcasted_iota(I32, (lc, LANES), 0)
    logsig = jnp.where(rowi < l_valid, logsig, 0.0)
    jj = lax.broadcasted_iota(I32, (lc, lc), 0)
    ss = lax.broadcasted_iota(I32, (lc, lc), 1)
    tril = jnp.where(jj >= ss, 1.0, 0.0).astype(F32)
    bcum = jnp.dot(tril, logsig, precision=HI, preferred_element_type=F32)
    lane_g = lax.broadcasted_iota(I32, (lc, LANES), 1)
    jcol = lax.broadcasted_iota(I32, (lc, 1), 0)
    causal = jj >= ss
    ninf = -jnp.inf

    c_prev = c_scr[...]
    c_prev_b = c_prev.astype(BF16)
    n_prev = n_scr[...]
    m_prev = m_scr[...]
    c_new = jnp.zeros_like(c_prev)
    carry_rows = jnp.zeros((hk, 1), F32)
    carry_lanes = jnp.zeros((1, hk), F32)
    ws_full = jnp.zeros((lc, hk), F32)
    m_out = m_prev
    srow_state = lax.broadcasted_iota(I32, (hk, 1), 0) // DK_D
    lane_state = lax.broadcasted_iota(I32, (1, hk), 1) // DK_D
    lane_m = lax.broadcasted_iota(I32, (1, LANES), 1)
    for h in range(H_D):
        mh = _head_mask(hk, h, DK_D)
        e_i = jnp.where(lane_g == h, 1.0, 0.0).astype(F32)
        e_f = jnp.where(lane_g == H_D + h, 1.0, 0.0).astype(F32)
        logi_col = jnp.where(jcol < l_valid, gates[:, h:h + 1], ninf)
        b_col = bcum[:, H_D + h:H_D + h + 1]
        i_row = lax.dot_general(e_i, gates, NT, precision=HI, preferred_element_type=F32)
        i_row = jnp.where(ss < l_valid, i_row, ninf)
        b_row = lax.dot_general(e_f, bcum, NT, precision=HI, preferred_element_type=F32)
        m_h = m_prev[:, h:h + 1]
        inter = b_col + m_h
        dmat = jnp.where(causal, b_col - b_row + i_row, ninf)
        m_j = jnp.maximum(inter, jnp.max(dmat, axis=-1, keepdims=True))
        qm = q * mh
        qmb = qm.astype(BF16)
        amat = jnp.exp(dmat - m_j) * lax.dot_general(qmb, kb, NT, preferred_element_type=F32)
        sc = jnp.exp(inter - m_j)
        vh = v_ref[0, :, h * DV_D:(h + 1) * DV_D]
        vhb = vh.astype(BF16)
        num = jnp.dot(amat.astype(BF16), vhb, preferred_element_type=F32) \
            + sc * jnp.dot(qmb, c_prev_b, preferred_element_type=F32)
        den = jnp.sum(amat, axis=-1, keepdims=True) + sc * jnp.sum(qm * n_prev, axis=-1, keepdims=True)
        hh = num / jnp.maximum(jnp.abs(den), jnp.exp(-m_j))
        ogh = og_ref[0, :, h * DV_D:(h + 1) * DV_D]
        y_ref[0, :, h * DV_D:(h + 1) * DV_D] = _head_norm128(hh, gn_ref[h:h + 1, :]) * jax.nn.sigmoid(ogh)
        b_last = b_col[l_valid - 1:l_valid, :]
        gs = b_last - b_col + logi_col
        m_new = jnp.maximum(b_last + m_h, jnp.max(gs, axis=0, keepdims=True))
        ws = jnp.exp(gs - m_new)
        carry = jnp.exp(b_last + m_h - m_new)
        c_new = c_new + lax.dot_general((k * mh).astype(BF16), (vh * ws).astype(BF16), TN,
                                        preferred_element_type=F32)
        carry_rows = jnp.where(srow_state == h, carry, carry_rows)
        carry_lanes = jnp.where(lane_state == h, carry, carry_lanes)
        ws_full = ws_full + ws * mh
        m_out = jnp.where(lane_m == h, m_new, m_out)
    c_fin = carry_rows * c_prev + c_new
    n_fin = carry_lanes * n_prev + jnp.sum(ws_full * k, axis=0, keepdims=True)
    c_scr[...] = c_fin
    n_scr[...] = n_fin
    m_scr[...] = m_out

    @pl.when(c == pl.num_programs(1) - 1)
    def _():
        cout_ref[0] = c_fin
        nout_ref[0] = n_fin
        mout_ref[0] = m_out


def mlstm_call(zb, zc, conv_buf, conv_w, conv_b, gate_bias, gn, c0, n0, m0, l_valid):
    b, t, _ = zb.shape
    lc = _row_tile(t, 128)
    assert l_valid == lc or t == lc
    hk, hv = H_D * DK_D, H_D * DV_D
    blk = lambda j: pl.BlockSpec((1, lc, 2 * hk), lambda i, c, j=j: (i, c, j))
    cst = lambda s: pl.BlockSpec(s, lambda i, c: (0,) * len(s))
    per_b = lambda s: pl.BlockSpec((1,) + s, lambda i, c: (i,) + (0,) * len(s))
    return pl.pallas_call(
        functools.partial(_mlstm_body, l_valid=l_valid),
        grid=(b, t // lc),
        in_specs=[blk(0), blk(1), blk(2),
                  pl.BlockSpec((1, lc, LANES), lambda i, c: (i, c, 0)),
                  per_b((SUBLANES, 2 * hk)),
                  cst((CONV_W, 2 * hk)), cst((1, 2 * hk)), cst((1, LANES)), cst((H_D, DV_D)),
                  per_b((hk, DV_D)), per_b((1, hk)), per_b((1, LANES))],
        out_specs=[pl.BlockSpec((1, lc, hv), lambda i, c: (i, c, 0)),
                   per_b((hk, DV_D)), per_b((1, hk)), per_b((1, LANES))],
        out_shape=[jax.ShapeDtypeStruct((b, t, hv), F32),
                   jax.ShapeDtypeStruct((b, hk, DV_D), F32),
                   jax.ShapeDtypeStruct((b, 1, hk), F32),
                   jax.ShapeDtypeStruct((b, 1, LANES), F32)],
        scratch_shapes=[pltpu.VMEM((SUBLANES + lc, 2 * hk), F32),
                        pltpu.VMEM((hk, DV_D), F32),
                        pltpu.VMEM((1, hk), F32),
                        pltpu.VMEM((1, LANES), F32)],
        compiler_params=_cparams(("parallel", "arbitrary")),
        name="mlstm",
    )(zb, zb, zb, zc, conv_buf, conv_w, conv_b, gate_bias, gn, c0, n0, m0)


def _pad_cols(w, n):
    return jnp.pad(w, ((0, 0), (0, n - w.shape[1])))


def _pad_time(a, tp):
    t = a.shape[1]
    if t == tp:
        return a
    return jnp.pad(a, ((0, 0), (0, tp - t)) + ((0, 0),) * (a.ndim - 2))


def _heads_major(a, h):
    b, t, _ = a.shape
    return a.reshape(b, t, h, -1).transpose(0, 2, 1, 3)


def _ab_mixer(xf, b, t, e, g_pre, P, prompt, shift_prev, s0, cache):
    m = b * t
    hd = H_A * DH_A
    w = P['ab_w_in'][e]
    o_qi = 3 * hd
    o_ki = o_qi + H_IDX * D_IDX
    o_wi = o_ki + D_IDX
    o_zb = o_wi + H_IDX
    qkv = proj_in_call(xf, g_pre, w[:, :o_qi].astype(BF16))
    qi = proj_in_call(xf, g_pre, w[:, o_qi:o_ki].astype(BF16), out_dtype=BF16)
    kw, ki2 = kiwi_call(xf, g_pre, w[:, o_ki:o_wi], w[:, o_wi:o_zb], P['kidx_g'][e])
    zb = proj_in_call(xf, g_pre, w[:, o_zb:].astype(BF16))
    q3 = qkv[:, :hd].reshape(b, t, hd)
    k3 = qkv[:, hd:2 * hd].reshape(b, t, hd)
    v3 = qkv[:, 2 * hd:].reshape(b, t, hd)
    ki3 = kw[:, :D_IDX].reshape(b, t, D_IDX)
    wi3 = kw[:, D_IDX:D_IDX + H_IDX].reshape(b, t, H_IDX)

    if prompt:
        ya = dsa_prompt_call(qkv.reshape(b, t, 3 * hd), qi.reshape(b, t, H_IDX * D_IDX),
                             ki2.reshape(b, t, LANES), kw.reshape(b, t, LANES)).reshape(m, hd)
    else:
        cache_k, cache_v, cache_kidx, page_table = cache
        n_pool = cache_k.shape[1]
        nq = SUBLANES
        qi_s = _pad_time(qi.reshape(b, t, H_IDX * D_IDX), nq).reshape(b, nq * H_IDX, D_IDX)
        wi_s = _pad_time(wi3, nq).reshape(b, 1, nq * H_IDX)
        keys_last = lambda a: jnp.moveaxis(a, 1, -1)
        new_page = lambda a: jnp.pad(keys_last(a), ((0, 0),) * (a.ndim - 1) + ((0, PAGE_SIZE - t),))
        isc, isc_new = dsa_sidx_call(page_table, qi_s, wi_s, new_page(ki3), keys_last(cache_kidx[e]))
        ya = dsa_satt_call(page_table, isc, isc_new, _pad_time(q3, nq),
                           new_page(k3.reshape(b, t, H_A, DH_A)), new_page(v3.reshape(b, t, H_A, DH_A)),
                           keys_last(cache_k[e]), keys_last(cache_v[e]), t)
        ya = ya[:, :t].reshape(m, hd)

    tp = -(-t // SUBLANES) * SUBLANES
    hb = H_B * DH_B
    zb3 = zb.reshape(b, t, D_B_IN)
    zpad = jnp.zeros((R_DECAY, hb), F32)
    w2p = jnp.concatenate([P['rwkv_w2'][e], zpad], axis=0).astype(BF16)
    a2p = jnp.concatenate([zpad, P['rwkv_a2'][e]], axis=0).astype(BF16)
    row = lambda a: a.reshape(1, -1)
    r, dec, k2, v, an, bb, g = rwkv_prep_call(
        _pad_time(zb3, tp), shift_prev.reshape(b, 1, D_B_IN), row(P['rwkv_mu'][e]), row(P['rwkv_w0'][e]),
        row(P['rwkv_a0'][e]), w2p, a2p, P['rwkv_g2'][e].astype(BF16), row(P['rwkv_kk'][e]), row(P['rwkv_ka'][e]))
    if t % RWKV_CHUNK == 0:
        y, s_new = rwkv_chunk_call(r, dec, k2, v, an, bb, s0)
    else:
        s0p = s0.reshape(b, RWKV_PAIRS, 2, DH_B, DH_B).transpose(0, 1, 3, 2, 4).reshape(b, RWKV_PAIRS, DH_B, LANES)
        y, s_fin = rwkv_scan_call(r, dec, k2, v, an, bb, s0p, t)
        s_new = s_fin.reshape(b, RWKV_PAIRS, DH_B, 2, DH_B).transpose(0, 1, 3, 2, 4).reshape(b, H_B, DH_B, DH_B)
    fl = lambda a: a.reshape(b * tp, hb)
    rk = jnp.broadcast_to(P['rwkv_rk'][e], (H_B, DH_B)).reshape(1, hb)
    yb = rwkv_post_call(fl(y), fl(r), fl(k2), fl(v), fl(g), row(P['rwkv_lnx_g'][e]), row(P['rwkv_lnx_b'][e]), rk)
    yb = yb.reshape(b, tp, hb)[:, :t].reshape(m, hb)
    st = (k3.reshape(b, t, H_A, DH_A), v3.reshape(b, t, H_A, DH_A), ki3, s_new, zb3[:, t - 1])
    return ya, yb, st


def _cd_mixer(xf, b, t, o, g_pre, pos, P, ret_s, m_c, m_n, m_m, conv_buf):
    w = P['cd_w_in'][o]
    hk, hv = H_C * DK_C, H_C * DV_C
    o_g = 2 * hk + 2 * hv
    o_vd = o_g + 2 * H_D * DK_D
    o_ig = o_vd + H_D * DV_D
    o_og = o_ig + 2 * H_D
    za = proj_in_call(xf, g_pre, w[:, :o_g].astype(BF16))
    zb = proj_in_call(xf, g_pre, jnp.concatenate([w[:, o_g:o_ig], w[:, o_og:]], axis=1).astype(BF16))
    zc = proj_in_call(xf, g_pre, _pad_cols(w[:, o_ig:o_og], LANES).astype(BF16))
    tp = -(-t // SUBLANES) * SUBLANES
    l_valid = t if tp != t else _row_tile(t, 128)
    za3 = _pad_time(za.reshape(b, t, -1), tp)
    zb3 = zb.reshape(b, t, -1)
    zc3 = _pad_time(zc.reshape(b, t, -1), tp)

    half = DK_C // 2
    inv = ROPE_BASE ** (-jnp.arange(half, dtype=F32) / half)
    ang = _pad_time(pos.astype(F32)[None], tp)[0][:, None] * inv[None, :]
    cos = jnp.tile(jnp.cos(ang), (1, 2 * H_C))
    sin = jnp.tile(jnp.sin(ang), (1, 2 * H_C))
    yc, ret_new = retention_call(za3, cos, sin, P['ret_gn'][o], ret_s.astype(F32).reshape(b, hk, DV_C), l_valid)

    hkd = H_D * DK_D
    cb = jnp.pad(conv_buf.astype(F32), ((0, 0), (SUBLANES - (CONV_W - 1), 0), (0, 0)))
    gate_bias = jnp.pad(P['mlstm_if_b'][o].astype(F32).reshape(1, 2 * H_D), ((0, 0), (0, LANES - 2 * H_D)))
    c0 = m_c.astype(F32).transpose(0, 1, 3, 2).reshape(b, hkd, DV_D)
    n0 = m_n.astype(F32).reshape(b, 1, hkd)
    m0 = jnp.pad(m_m.astype(F32), ((0, 0), (0, LANES - H_D))).reshape(b, 1, LANES)
    yd, c_new, n_new, m_new = mlstm_call(_pad_time(zb3, tp), zc3, cb, P['conv_w'][o], P['conv_b'][o].reshape(1, -1),
                                         gate_bias, P['mlstm_gn'][o], c0, n0, m0, l_valid)
    conv_new = jnp.concatenate([conv_buf.astype(F32), zb3[:, :, :2 * hkd]], axis=1)[:, -(CONV_W - 1):]
    st = (ret_new.reshape(b, H_C, DK_C, DV_C),
          c_new.reshape(b, H_D, DK_D, DV_D).transpose(0, 1, 3, 2),
          n_new.reshape(b, H_D, DK_D),
          m_new.reshape(b, LANES)[:, :H_D],
          conv_new)
    m = b * t
    return yc[:, :t].reshape(m, hv), yd[:, :t].reshape(m, H_D * DV_D), st


def _trunk(x, pos, prompt, ab_init, cd_init, cache, P):
    b, t, d = x.shape
    xf = x.reshape(b * t, d)
    depth = P['norm_g'].shape[0]
    ab_new, cd_new = [], []
    for l in range(depth):
        g = P['norm_g'][l]
        bf = lambda a: a.astype(BF16)
        xf = ffn_call(xf, g[0], g[1], bf(P['ffn_wg'][l, 0]), bf(P['ffn_wu'][l, 0]), bf(P['ffn_wd'][l, 0]))
        if l % 2 == 0:
            e = l // 2
            shift_prev, s0 = ab_init(e)
            ya, yb, st = _ab_mixer(xf, b, t, e, g[2], P, prompt, shift_prev, s0, cache)
            ab_new.append(st)
            wo = bf(P['ab_w_out'][e])
            xf = proj_out_call(ya, yb, xf, g[3], wo[:H_A * DH_A], wo[H_A * DH_A:])
        else:
            o = l // 2
            yc, yd, st = _cd_mixer(xf, b, t, o, g[2], pos, P, *cd_init(o))
            cd_new.append(st)
            wo = bf(P['cd_w_out'][o])
            xf = proj_out_call(yc, yd, xf, g[3], wo[:H_C * DV_C], wo[H_C * DV_C:])
        xf = ffn_call(xf, g[4], g[5], bf(P['ffn_wg'][l, 1]), bf(P['ffn_wu'][l, 1]), bf(P['ffn_wd'][l, 1]))
    ab = tuple(jnp.stack(s) for s in zip(*ab_new))
    cd = tuple(jnp.stack(s) for s in zip(*cd_new))
    return xf.reshape(b, t, d), ab, cd


def kernel(x_prompt, x_sample, cache_k, cache_v, cache_kidx, state_rwkv, state_shift, state_ret, state_mlstm_C, state_mlstm_n, state_mlstm_m, state_conv, page_table, norm_g, ffn_wg, ffn_wu, ffn_wd, ab_w_in, ab_w_out, kidx_g, rwkv_mu, rwkv_w0, rwkv_w2, rwkv_a0, rwkv_a2, rwkv_g2, rwkv_kk, rwkv_ka, rwkv_rk, rwkv_lnx_g, rwkv_lnx_b, cd_w_in, cd_w_out, ret_gn, conv_w, conv_b, mlstm_if_b, mlstm_gn):
    P = dict(norm_g=norm_g, ffn_wg=ffn_wg, ffn_wu=ffn_wu, ffn_wd=ffn_wd, ab_w_in=ab_w_in, ab_w_out=ab_w_out,
             kidx_g=kidx_g, rwkv_mu=rwkv_mu, rwkv_w0=rwkv_w0, rwkv_w2=rwkv_w2, rwkv_a0=rwkv_a0, rwkv_a2=rwkv_a2,
             rwkv_g2=rwkv_g2, rwkv_kk=rwkv_kk, rwkv_ka=rwkv_ka, rwkv_rk=rwkv_rk, rwkv_lnx_g=rwkv_lnx_g,
             rwkv_lnx_b=rwkv_lnx_b, cd_w_in=cd_w_in, cd_w_out=cd_w_out, ret_gn=ret_gn, conv_w=conv_w,
             conv_b=conv_b, mlstm_if_b=mlstm_if_b, mlstm_gn=mlstm_gn)
    B, T, _ = x_prompt.shape
    DB, DS, _ = x_sample.shape
    past = page_table.shape[1] * PAGE_SIZE

    def ab_zero(e):
        return (jnp.zeros((B, D_B_IN), F32), jnp.zeros((B, H_B, DH_B, DH_B), F32))

    def cd_zero(o):
        return (jnp.zeros((B, H_C, DK_C, DV_C), F32), jnp.zeros((B, H_D, DV_D, DK_D), F32),
                jnp.zeros((B, H_D, DK_D), F32), jnp.zeros((B, H_D), F32),
                jnp.zeros((B, CONV_W - 1, 2 * H_D * DK_D), F32))

    def ab_cached(e):
        return (state_shift[e], state_rwkv[e])

    def cd_cached(o):
        return (state_ret[o], state_mlstm_C[o], state_mlstm_n[o], state_mlstm_m[o], state_conv[o])

    y_p, (kp, vp, kip, rwp, shp), (rtp, cp, nvp, mp, cvp) = _trunk(
        x_prompt, jnp.arange(T), True, ab_zero, cd_zero, None, P)
    y_s, (ks_, vs_, kis, rws, shs), (rts, cs, nvs, ms, cvs) = _trunk(
        x_sample, past + jnp.arange(DS), False, ab_cached, cd_cached,
        (cache_k, cache_v, cache_kidx, page_table), P)
    return (y_p, y_s, kp, vp, kip, rwp, shp, rtp, cp, nvp, mp, cvp,
            ks_, vs_, kis, rws, shs, rts, cs, nvs, ms, cvs)
```

```python
import functools
import math

import numpy as np
import jax
import jax.numpy as jnp
from jax import lax
from jax.experimental import pallas as pl
from jax.experimental.pallas import tpu as pltpu

F32 = jnp.float32
BF16 = jnp.bfloat16
I32 = jnp.int32
HI = lax.Precision.HIGHEST

LANES = 128
SUBLANES = 8
VMEM_LIMIT = 56 * 1024 * 1024

EPS = 1e-6
PAGE_SIZE = 128
H_A, DH_A, H_IDX, D_IDX, TOPK_MAX = 8, 64, 16, 64, 256
H_B, DH_B, R_DECAY, R_AAA, R_GATE = 8, 64, 64, 64, 128
D_B_IN = 3 * H_B * DH_B + R_DECAY + R_AAA + R_GATE
LNX_EPS = 64e-5
H_C, DK_C, DV_C, ROPE_BASE = 4, 64, 128, 10000.0
H_D, DK_D, DV_D, CONV_W = 4, 64, 128, 4
HN_EPS = 1e-5
NEG_BIG = -1e30
INT_MIN = -(2 ** 31)
INT_MAX = 2 ** 31 - 1
KEY_NEG_INF = (0xFF800000 ^ 0x7FFFFFFF) - 2 ** 32

NT = (((1,), (1,)), ((), ()))
TN = (((0,), (0,)), ((), ()))


def _cparams(sem):
    return pltpu.CompilerParams(dimension_semantics=sem, vmem_limit_bytes=VMEM_LIMIT)


def _rms(x, g):
    return x * lax.rsqrt(jnp.mean(x * x, axis=-1, keepdims=True) + EPS) * g


def _softplus(x):
    return jnp.maximum(x, 0.0) + jnp.log(1.0 + jnp.exp(-jnp.abs(x)))


def _bf16_pieces(x, pieces):
    parts = []
    for _ in range(pieces - 1):
        h = x.astype(BF16)
        parts.append(h)
        x = x - h.astype(F32)
    parts.append(x.astype(BF16))
    return parts


def _lane_group_dot(x, m):
    m3 = jnp.concatenate([m.astype(BF16)] * 3, axis=0)
    outs = []
    for p in range(x.shape[1] // LANES):
        xp = jnp.concatenate(_bf16_pieces(x[:, p * LANES:(p + 1) * LANES], 3), axis=1)
        outs.append(jnp.dot(xp, m3, preferred_element_type=F32))
    return jnp.concatenate(outs, axis=1)


def _row_tile(m, want):
    t = min(want, m)
    while m % t:
        t //= 2
    return t


def _col_tile(n, want):
    best = LANES
    for t in range(LANES, min(n, want) + 1, LANES):
        if n % t == 0:
            best = t
    return best


def _ffn_body(x_ref, g0_ref, g1_ref, wg_ref, wu_ref, wd_ref, o_ref, h_scr, acc_scr):
    j = pl.program_id(1)

    @pl.when(j == 0)
    def _():
        h_scr[...] = _rms(x_ref[...], g0_ref[...]).astype(BF16)
        acc_scr[...] = jnp.zeros_like(acc_scr)

    h = h_scr[...]
    g = jnp.dot(h, wg_ref[...], preferred_element_type=F32)
    u = jnp.dot(h, wu_ref[...], preferred_element_type=F32)
    a = g * jax.nn.sigmoid(g) * u
    acc_scr[...] += jnp.dot(a.astype(BF16), wd_ref[...], preferred_element_type=F32)

    @pl.when(j == pl.num_programs(1) - 1)
    def _():
        o_ref[...] = x_ref[...] + 0.5 * _rms(acc_scr[...], g1_ref[...])


def ffn_call(x, g0, g1, wg, wu, wd):
    m, d = x.shape
    ff = wg.shape[1]
    tm = _row_tile(m, 512)
    tf = _col_tile(ff, 1408)
    return pl.pallas_call(
        _ffn_body,
        grid=(m // tm, ff // tf),
        in_specs=[
            pl.BlockSpec((tm, d), lambda i, j: (i, 0)),
            pl.BlockSpec((1, d), lambda i, j: (0, 0)),
            pl.BlockSpec((1, d), lambda i, j: (0, 0)),
            pl.BlockSpec((d, tf), lambda i, j: (0, j)),
            pl.BlockSpec((d, tf), lambda i, j: (0, j)),
            pl.BlockSpec((tf, d), lambda i, j: (j, 0)),
        ],
        out_specs=pl.BlockSpec((tm, d), lambda i, j: (i, 0)),
        out_shape=jax.ShapeDtypeStruct((m, d), F32),
        scratch_shapes=[pltpu.VMEM((tm, d), BF16), pltpu.VMEM((tm, d), F32)],
        compiler_params=_cparams(("parallel", "arbitrary")),
        name="ffn",
    )(x, g0.reshape(1, d), g1.reshape(1, d), wg, wu, wd)


def _proj_in_body(x_ref, g_ref, w_ref, o_ref, h_scr):
    @pl.when(pl.program_id(1) == 0)
    def _():
        h_scr[...] = _rms(x_ref[...], g_ref[...]).astype(BF16)

    o_ref[...] = jnp.dot(h_scr[...], w_ref[...], preferred_element_type=F32).astype(o_ref.dtype)


def proj_in_call(x, g, w, out_dtype=F32):
    m, d = x.shape
    n = w.shape[1]
    tm = _row_tile(m, 512)
    tn = _col_tile(n, 2048)
    return pl.pallas_call(
        _proj_in_body,
        grid=(m // tm, n // tn),
        in_specs=[
            pl.BlockSpec((tm, d), lambda i, j: (i, 0)),
            pl.BlockSpec((1, d), lambda i, j: (0, 0)),
            pl.BlockSpec((d, tn), lambda i, j: (0, j)),
        ],
        out_specs=pl.BlockSpec((tm, tn), lambda i, j: (i, j)),
        out_shape=jax.ShapeDtypeStruct((m, n), out_dtype),
        scratch_shapes=[pltpu.VMEM((tm, d), BF16)],
        compiler_params=_cparams(("parallel", "arbitrary")),
        name="proj_in",
    )(x, g.reshape(1, d), w)


def _kiwi_body(x_ref, g_ref, w_ref, kg_ref, o_ref, k2_ref):
    h = _rms(x_ref[...], g_ref[...]).astype(BF16)
    z = jnp.dot(h, w_ref[...], preferred_element_type=F32)
    z1 = z[:, :LANES]
    lane = lax.broadcasted_iota(I32, z1.shape, 1)
    is_k = lane < D_IDX
    ms = jnp.sum(jnp.where(is_k, z1 * z1, 0.0), axis=-1, keepdims=True) * (1.0 / D_IDX)
    inv = lax.rsqrt(ms + EPS)
    kg = kg_ref[...]
    o_ref[...] = jnp.where(is_k, z1 * inv * kg[:, :LANES], z1 * (H_IDX * D_IDX) ** -0.5)
    k2_ref[...] = (z[:, LANES:] * inv * kg[:, LANES:]).astype(BF16)


def kiwi_call(x, g, w_ki, w_wi, kidx_g):
    m, d = x.shape
    tm = _row_tile(m, 512)
    zpad = jnp.zeros((d, LANES - D_IDX - H_IDX), F32)
    w = jnp.concatenate([w_ki, w_wi, zpad, w_ki, w_ki], axis=1).astype(BF16)
    gpad = jnp.zeros((LANES - D_IDX,), F32)
    kg = jnp.concatenate([kidx_g, gpad, kidx_g, kidx_g]).reshape(1, 2 * LANES)
    return pl.pallas_call(
        _kiwi_body,
        grid=(m // tm,),
        in_specs=[
            pl.BlockSpec((tm, d), lambda i: (i, 0)),
            pl.BlockSpec((1, d), lambda i: (0, 0)),
            pl.BlockSpec((d, 2 * LANES), lambda i: (0, 0)),
            pl.BlockSpec((1, 2 * LANES), lambda i: (0, 0)),
        ],
        out_specs=[pl.BlockSpec((tm, LANES), lambda i: (i, 0))] * 2,
        out_shape=[jax.ShapeDtypeStruct((m, LANES), F32), jax.ShapeDtypeStruct((m, LANES), BF16)],
        compiler_params=_cparams(("parallel",)),
        name="kiwi",
    )(x, g.reshape(1, d), w, kg)


def _proj_out_body(a1_ref, a2_ref, x_ref, g_ref, w1_ref, w2_ref, o_ref):
    y = jnp.dot(a1_ref[...].astype(BF16), w1_ref[...], preferred_element_type=F32)
    y = y + jnp.dot(a2_ref[...].astype(BF16), w2_ref[...], preferred_element_type=F32)
    o_ref[...] = x_ref[...] + _rms(y, g_ref[...])


def proj_out_call(a1, a2, x, g, w1, w2):
    m, d = x.shape
    k1, k2 = a1.shape[1], a2.shape[1]
    tm = _row_tile(m, 512)
    return pl.pallas_call(
        _proj_out_body,
        grid=(m // tm,),
        in_specs=[
            pl.BlockSpec((tm, k1), lambda i: (i, 0)),
            pl.BlockSpec((tm, k2), lambda i: (i, 0)),
            pl.BlockSpec((tm, d), lambda i: (i, 0)),
            pl.BlockSpec((1, d), lambda i: (0, 0)),
            pl.BlockSpec((k1, d), lambda i: (0, 0)),
            pl.BlockSpec((k2, d), lambda i: (0, 0)),
        ],
        out_specs=pl.BlockSpec((tm, d), lambda i: (i, 0)),
        out_shape=jax.ShapeDtypeStruct((m, d), F32),
        compiler_params=_cparams(("parallel",)),
        name="proj_out",
    )(a1, a2, x, g.reshape(1, d), w1, w2)


def _topk_bias(isc, col, topk, nbits_col):
    bits = pltpu.bitcast(isc, I32)
    key = jnp.where(bits < 0, bits ^ 0x7FFFFFFF, bits)
    kf = float(topk)

    def count(mask):
        return jnp.sum(jnp.where(mask, 1.0, 0.0), axis=-1, keepdims=True)

    prefix = jnp.where(count(key >= 0) >= kf, 0, INT_MIN).astype(I32)

    def bit_step(i, prefix):
        cand = prefix + jnp.left_shift(jnp.int32(1), 30 - i)
        return jnp.where(count(key >= cand) >= kf, cand, prefix)

    thr = lax.fori_loop(0, 31, bit_step, prefix)
    gt = key > thr
    eq = key == thr
    need = kf - count(gt)

    def col_step(i, y):
        cand = y + jnp.left_shift(jnp.int32(1), nbits_col - 1 - i)
        c = jnp.sum(jnp.where(eq, jnp.where(col < cand, 1.0, 0.0), 0.0), axis=-1, keepdims=True)
        return jnp.where(c < need, cand, y)

    tied = jnp.where(count(key >= thr) > kf, jnp.where(thr > KEY_NEG_INF, 1.0, 0.0), 0.0)
    y = lax.cond(jnp.max(tied) > 0.0,
                 lambda: lax.fori_loop(0, nbits_col, col_step, jnp.zeros_like(thr)),
                 lambda: jnp.full_like(thr, INT_MAX))
    ninf = -jnp.inf
    bias = jnp.where(gt, 0.0, jnp.where(eq, jnp.where(col <= y, 0.0, ninf), ninf))
    return jnp.where(jnp.abs(isc) < jnp.inf, bias, ninf)


DSA_CAUSAL_BANDS = 8


def _dsa_prompt_body(qkv_q_ref, qkv_k_ref, qkv_v_ref, qi_ref, ki2_ref, kw_ref, o_ref, kb_scr, vb_scr, *,
                     topk, q_first):
    qb = qkv_q_ref.shape[1]
    t = qkv_k_ref.shape[1]
    j = pl.program_id(1)
    t0 = (j + q_first) * qb

    @pl.when(j == 0)
    def _():
        kb_scr[...] = qkv_k_ref[0].astype(BF16)
        vb_scr[...] = qkv_v_ref[0].astype(BF16)

    lane = lax.broadcasted_iota(I32, (1, LANES), 1)
    half = (jnp.where(lane < DH_A, 1.0, 0.0), jnp.where(lane < DH_A, 0.0, 1.0))
    half_b = tuple(m.astype(BF16) for m in half)
    ki2 = ki2_ref[0]
    kw = kw_ref[0]
    isc = jnp.zeros((qb, t), F32)
    for hp in range(H_IDX // 2):
        qp = qi_ref[0, :, hp * LANES:(hp + 1) * LANES]
        for h2 in range(2):
            h = 2 * hp + h2
            s = lax.dot_general(qp * half_b[h2], ki2, NT, preferred_element_type=F32)
            isc = isc + kw[:, D_IDX + h:D_IDX + h + 1] * jnp.maximum(s, 0.0)
    row = lax.broadcasted_iota(I32, (qb, t), 0) + t0
    col = lax.broadcasted_iota(I32, (qb, t), 1)
    isc = jnp.where(col <= row, isc, -jnp.inf)
    bias = _topk_bias(isc, col, topk, max(1, (t - 1).bit_length()))
    scale = DH_A ** -0.5
    assert math.frexp(scale)[0] == 0.5
    for p in range(H_A // 2):
        sl = slice(p * LANES, (p + 1) * LANES)
        qp = qkv_q_ref[0, :, sl]
        kp = kb_scr[:, sl]
        vp = vb_scr[:, sl]
        outs = []
        for h2 in range(2):
            qm = (qp * (half[h2] * scale)).astype(BF16)
            lg = lax.dot_general(qm, kp, NT, preferred_element_type=F32) + bias
            mx = jnp.max(lg, axis=-1, keepdims=True)
            pr = jnp.exp(lg - mx)
            l = jnp.sum(pr, axis=-1, keepdims=True)
            outs.append(jnp.dot(pr.astype(BF16), vp, preferred_element_type=F32) / l)
        o_ref[0, :, sl] = jnp.where(lane < DH_A, outs[0], outs[1])


def dsa_prompt_call(qkv, qi, ki2, kw):
    b, t, _ = qkv.shape
    hd = H_A * DH_A
    qb = _row_tile(t, 256)
    topk = min(TOPK_MAX, t // 4)
    n_qb = t // qb
    per_band = n_qb // math.gcd(n_qb, DSA_CAUSAL_BANDS)
    outs = []
    for q_first in range(0, n_qb, per_band):
        tk = (q_first + per_band) * qb
        qmap = lambda i, j, q_first=q_first: (i, j + q_first, 0)
        outs.append(pl.pallas_call(
            functools.partial(_dsa_prompt_body, topk=topk, q_first=q_first),
            grid=(b, per_band),
            in_specs=[
                pl.BlockSpec((1, qb, hd), qmap),
                pl.BlockSpec((1, tk, hd), lambda i, j: (i, 0, 1)),
                pl.BlockSpec((1, tk, hd), lambda i, j: (i, 0, 2)),
                pl.BlockSpec((1, qb, H_IDX * D_IDX), qmap),
                pl.BlockSpec((1, tk, LANES), lambda i, j: (i, 0, 0)),
                pl.BlockSpec((1, qb, LANES), qmap),
            ],
            out_specs=pl.BlockSpec((1, qb, hd), lambda i, j: (i, j, 0)),
            out_shape=jax.ShapeDtypeStruct((b, per_band * qb, hd), F32),
            scratch_shapes=[pltpu.VMEM((tk, hd), BF16), pltpu.VMEM((tk, hd), BF16)],
            compiler_params=_cparams(("parallel", "arbitrary")),
            name="dsa_prompt",
        )(qkv, qkv, qkv, qi, ki2, kw))
    return outs[0] if len(outs) == 1 else jnp.concatenate(outs, axis=1)


DSA_PAGE_GROUP = 16


def _page_group(n_pages):
    return math.gcd(n_pages, DSA_PAGE_GROUP)


def _page_specs(shape, n_pages, group):
    zeros = (0,) * (len(shape) - 1)

    def spec(g):
        return pl.BlockSpec(shape, lambda b, s, pt: (pt[b, jnp.minimum(s * group + g, n_pages - 1)],) + zeros)
    return [spec(g) for g in range(group)]


def _page_cat(refs):
    return jnp.concatenate([r[0].reshape(-1, PAGE_SIZE).astype(BF16) for r in refs], axis=1)


def _dsa_sidx_body(pt_ref, qi_ref, wi_ref, kin_ref, *rest):
    kc_refs, o_ref, onew_ref = rest[:-2], rest[-2], rest[-1]
    nq = o_ref.shape[1]
    r = lax.broadcasted_iota(I32, (nq, nq * H_IDX), 0)
    c = lax.broadcasted_iota(I32, (nq, nq * H_IDX), 1)
    lo = r * H_IDX
    wsel = jnp.where(c >= lo, jnp.where(c < lo + H_IDX, wi_ref[0], 0.0), 0.0)

    def scores(ki):
        s = jnp.dot(qi_ref[0], ki, preferred_element_type=F32)
        return jnp.dot(wsel, jnp.maximum(s, 0.0), precision=HI, preferred_element_type=F32)

    @pl.when(pl.program_id(1) == 0)
    def _():
        onew_ref[0] = scores(kin_ref[0].astype(BF16))

    o_ref[0] = scores(_page_cat(kc_refs))


def dsa_sidx_call(page_table, qi, wi, ki_new, cache_kidx):
    db, n_pages = page_table.shape
    nq = qi.shape[1] // H_IDX
    group = _page_group(n_pages)
    grid_spec = pltpu.PrefetchScalarGridSpec(
        num_scalar_prefetch=1,
        grid=(db, n_pages // group),
        in_specs=[
            pl.BlockSpec((1, nq * H_IDX, D_IDX), lambda b, s, pt: (b, 0, 0)),
            pl.BlockSpec((1, 1, nq * H_IDX), lambda b, s, pt: (b, 0, 0)),
            pl.BlockSpec((1, D_IDX, PAGE_SIZE), lambda b, s, pt: (b, 0, 0)),
        ] + _page_specs((1, D_IDX, PAGE_SIZE), n_pages, group),
        out_specs=[pl.BlockSpec((1, nq, group * PAGE_SIZE), lambda b, s, pt: (b, 0, s)),
                   pl.BlockSpec((1, nq, PAGE_SIZE), lambda b, s, pt: (b, 0, 0))],
    )
    return pl.pallas_call(
        _dsa_sidx_body,
        grid_spec=grid_spec,
        out_shape=[jax.ShapeDtypeStruct((db, nq, n_pages * PAGE_SIZE), F32),
                   jax.ShapeDtypeStruct((db, nq, PAGE_SIZE), F32)],
        compiler_params=_cparams(("parallel", "arbitrary")),
        name="dsa_sample_idx",
    )(page_table, qi, wi, ki_new, *([cache_kidx] * group))


def _dsa_satt_body(pt_ref, isc_ref, iscn_ref, q_ref, kn_ref, vn_ref, *rest, n_pages, n_steps, topk, n_new):
    group = n_pages // n_steps
    kc_refs, vc_refs = rest[:group], rest[group:2 * group]
    o_ref, bias_scr, qe_scr, m_scr, l_scr, acc_scr = rest[2 * group:]
    p = pl.program_id(1)
    nq = q_ref.shape[1]
    past = n_pages * PAGE_SIZE
    wcols = group * PAGE_SIZE
    ltot = past + PAGE_SIZE
    hd = H_A * DH_A
    scale = DH_A ** -0.5
    assert math.frexp(scale)[0] == 0.5

    def attend(kcat, vcat, bias):
        lg = jnp.dot(qe_scr[...], kcat, preferred_element_type=F32)
        lg = lg + jnp.concatenate([bias] * H_A, axis=0)
        m_old = m_scr[...]
        m_new = jnp.maximum(m_old, jnp.max(lg, axis=-1, keepdims=True))
        alpha = jnp.exp(m_old - m_new)
        pr = jnp.exp(lg - m_new)
        l_scr[...] = alpha * l_scr[...] + jnp.sum(pr, axis=-1, keepdims=True)
        acc_scr[...] = alpha * acc_scr[...] + lax.dot_general(pr.astype(BF16), vcat, NT, preferred_element_type=F32)
        m_scr[...] = m_new

    @pl.when(p == 0)
    def _():
        isc = jnp.concatenate([isc_ref[0], iscn_ref[0]], axis=1)
        row = lax.broadcasted_iota(I32, (nq, ltot), 0)
        col = lax.broadcasted_iota(I32, (nq, ltot), 1)
        rel = col - past
        ninf = -jnp.inf
        isc = jnp.where(rel < 0, isc, jnp.where(rel < n_new, jnp.where(rel <= row, isc, ninf), ninf))
        bias = _topk_bias(isc, col, topk, max(1, (ltot - 1).bit_length()))
        for i in range(n_steps):
            bias_scr[i] = bias[:, i * wcols:(i + 1) * wcols]
        q = q_ref[0] * scale
        lane = lax.broadcasted_iota(I32, (nq, hd), 1)
        qe_scr[...] = jnp.concatenate(
            [jnp.where(lane >= h * DH_A, jnp.where(lane < (h + 1) * DH_A, q, 0.0), 0.0) for h in range(H_A)],
            axis=0).astype(BF16)
        m_scr[...] = jnp.full_like(m_scr, NEG_BIG)
        l_scr[...] = jnp.zeros_like(l_scr)
        acc_scr[...] = jnp.zeros_like(acc_scr)
        attend(kn_ref[0].reshape(hd, PAGE_SIZE).astype(BF16), vn_ref[0].reshape(hd, PAGE_SIZE).astype(BF16),
               bias[:, past:])

    attend(_page_cat(kc_refs), _page_cat(vc_refs), bias_scr[p])

    @pl.when(p == n_steps - 1)
    def _():
        o = acc_scr[...] / l_scr[...]
        lane = lax.broadcasted_iota(I32, (nq, hd), 1)
        out = jnp.zeros((nq, hd), F32)
        for h in range(H_A):
            oh = o[h * nq:(h + 1) * nq]
            out = out + jnp.where(lane >= h * DH_A, jnp.where(lane < (h + 1) * DH_A, oh, 0.0), 0.0)
        o_ref[0] = out


def dsa_satt_call(page_table, isc, isc_new, q, k_new, v_new, cache_k, cache_v, n_new):
    db, n_pages = page_table.shape
    nq = q.shape[1]
    hd = H_A * DH_A
    page = (1, H_A, DH_A, PAGE_SIZE)
    group = _page_group(n_pages)
    n_steps = n_pages // group
    wcols = group * PAGE_SIZE
    topk = min(TOPK_MAX, (n_pages * PAGE_SIZE + n_new) // 4)
    grid_spec = pltpu.PrefetchScalarGridSpec(
        num_scalar_prefetch=1,
        grid=(db, n_steps),
        in_specs=[
            pl.BlockSpec((1, nq, n_pages * PAGE_SIZE), lambda b, p, pt: (b, 0, 0)),
            pl.BlockSpec((1, nq, PAGE_SIZE), lambda b, p, pt: (b, 0, 0)),
            pl.BlockSpec((1, nq, hd), lambda b, p, pt: (b, 0, 0)),
            pl.BlockSpec(page, lambda b, p, pt: (b, 0, 0, 0)),
            pl.BlockSpec(page, lambda b, p, pt: (b, 0, 0, 0)),
        ] + _page_specs(page, n_pages, group) * 2,
        out_specs=pl.BlockSpec((1, nq, hd), lambda b, p, pt: (b, 0, 0)),
        scratch_shapes=[
            pltpu.VMEM((n_steps, nq, wcols), F32),
            pltpu.VMEM((H_A * nq, hd), BF16),
            pltpu.VMEM((H_A * nq, 1), F32),
            pltpu.VMEM((H_A * nq, 1), F32),
            pltpu.VMEM((H_A * nq, hd), F32),
        ],
    )
    return pl.pallas_call(
        functools.partial(_dsa_satt_body, n_pages=n_pages, n_steps=n_steps, topk=topk, n_new=n_new),
        grid_spec=grid_spec,
        out_shape=jax.ShapeDtypeStruct((db, nq, hd), F32),
        compiler_params=_cparams(("parallel", "arbitrary")),
        name="dsa_sample_att",
    )(page_table, isc, isc_new, q, k_new, v_new, *([cache_k] * group), *([cache_v] * group))


def _head_sum_matrix(n, group):
    r = lax.broadcasted_iota(I32, (n, n), 0) // group
    c = lax.broadcasted_iota(I32, (n, n), 1) // group
    return jnp.where(r == c, 1.0, 0.0).astype(F32)


def _rwkv_prep_math(zb, before, mu, w0, a0, w2p, a2p, g2, kkp, ka):
    hd = H_B * DH_B
    rows = lax.broadcasted_iota(I32, zb.shape, 0)
    prev = jnp.where(rows == 0, before, pltpu.roll(zb, 1, 0))
    z = zb + (prev - zb) * mu
    r = z[:, 0:hd]
    k = z[:, hd:2 * hd]
    v = z[:, 2 * hd:3 * hd]
    xwa = z[:, 3 * hd:3 * hd + R_DECAY + R_AAA]
    xg = z[:, 3 * hd + R_DECAY + R_AAA:]
    wl = w0 + jnp.dot(jnp.tanh(xwa).astype(BF16), w2p, preferred_element_type=F32)
    w_log = -_softplus(-wl) - 0.5
    log_decay = -jnp.exp(w_log)
    a = jax.nn.sigmoid(a0 + jnp.dot(xwa.astype(BF16), a2p, preferred_element_type=F32))
    g = jnp.dot(jax.nn.sigmoid(xg).astype(BF16), g2, preferred_element_type=F32)
    kk = k * kkp
    ss = _lane_group_dot(kk * kk, _head_sum_matrix(LANES, DH_B))
    kk = kk * lax.rsqrt(jnp.maximum(ss, 1e-24))
    return r, log_decay, k * (1.0 + (a - 1.0) * ka), v, -kk, kk * a, g


def _rwkv_post_math(y, r, k, v, g, lnx_g, lnx_b, rk):
    ones = _head_sum_matrix(LANES, DH_B)
    avg = ones * (1.0 / DH_B)
    mu = _lane_group_dot(y, avg)
    d = y - mu
    var = _lane_group_dot(d * d, avg)
    yn = d * lax.rsqrt(var + LNX_EPS) * lnx_g + lnx_b
    bonus = _lane_group_dot(r * k * rk, ones)
    return (yn + bonus * v) * g


def _rwkv_prep_body(zb_ref, sp_ref, mu_ref, w0_ref, a0_ref, w2_ref, a2_ref, g2_ref, kkp_ref, ka_ref,
                    r_ref, w_ref, k_ref, v_ref, an_ref, b_ref, g_ref, carry_scr):
    tc = pl.program_id(1)
    zb = zb_ref[0]
    tt = zb.shape[0]
    hd = H_B * DH_B

    @pl.when(tc == 0)
    def _():
        carry_scr[...] = sp_ref[0]

    r, log_decay, k2, v, an, bb, g = _rwkv_prep_math(
        zb, carry_scr[...], mu_ref[...], w0_ref[...], a0_ref[...], w2_ref[...], a2_ref[...], g2_ref[...],
        kkp_ref[...], ka_ref[...])
    carry_scr[...] = zb[tt - 1:tt]
    r_ref[0] = r
    w_ref[0] = log_decay
    k_ref[0] = k2
    v_ref[0] = v
    an_ref[0] = an
    b_ref[0] = bb
    g_ref[0] = g


def rwkv_prep_call(zb, shift_prev, mu, w0, a0, w2p, a2p, g2, kkp, ka):
    b, t, dz = zb.shape
    hd = H_B * DH_B
    tt = _row_tile(t, 256)
    row = lambda n: pl.BlockSpec((1, n), lambda i, j: (0, 0))
    full = lambda s: pl.BlockSpec(s, lambda i, j: (0, 0))
    out = pl.BlockSpec((1, tt, hd), lambda i, j: (i, j, 0))
    return pl.pallas_call(
        _rwkv_prep_body,
        grid=(b, t // tt),
        in_specs=[
            pl.BlockSpec((1, tt, dz), lambda i, j: (i, j, 0)),
            pl.BlockSpec((1, 1, dz), lambda i, j: (i, 0, 0)),
            row(dz), row(hd), row(hd),
            full((R_DECAY + R_AAA, hd)), full((R_DECAY + R_AAA, hd)), full((R_GATE, hd)),
            row(hd), row(hd),
        ],
        out_specs=[out] * 7,
        out_shape=[jax.ShapeDtypeStruct((b, t, hd), F32)] * 7,
        scratch_shapes=[pltpu.VMEM((1, dz), F32)],
        compiler_params=_cparams(("parallel", "arbitrary")),
        name="rwkv_prep",
    )(zb, shift_prev, mu, w0, a0, w2p, a2p, g2, kkp, ka)


RWKV_PAIRS = H_B // 2
RWKV_STEPS = SUBLANES // RWKV_PAIRS
RWKV_NB = 8
RWKV_SUM_PIECES = 2


def _rwkv_scan_body(r_ref, w_ref, k_ref, v_ref, a_ref, b_ref, s0_ref, y_ref, sout_ref,
                    s_scr, pa_scr, pv_scr, *, t_valid):
    c = pl.program_id(1)
    nb, tc = r_ref.shape[0], r_ref.shape[1]
    chains = [(bi, p) for bi in range(nb) for p in range(RWKV_PAIRS)]
    rows = lambda n: slice(n * DH_B, (n + 1) * DH_B)

    @pl.when(c == 0)
    def _():
        for n, (bi, p) in enumerate(chains):
            s_scr[rows(n), :] = s0_ref[bi, p]

    if t_valid < tc * RWKV_STEPS:
        y_ref[...] = jnp.zeros_like(y_ref)

    sub = lax.broadcasted_iota(I32, (DH_B, LANES), 0)
    lane = lax.broadcasted_iota(I32, (DH_B, LANES), 1)
    isel = jnp.where((lane & (DH_B - 1)) == sub, 1.0, 0.0).astype(F32)
    qblk = _head_sum_matrix(LANES, DH_B).astype(BF16)

    def hsum(ref, pieces=3):
        x = ref[...]
        parts = []
        for _ in range(pieces - 1):
            h = x.astype(BF16)
            parts.append(h)
            x = x - h.astype(F32)
        parts.append(x.astype(BF16))
        return jnp.dot(jnp.concatenate(parts, axis=1), jnp.concatenate([qblk] * pieces, axis=0),
                       preferred_element_type=F32)

    def step(u, carry):
        tiles = [(a_ref[bi, u], jnp.exp(w_ref[bi, u]), k_ref[bi, u], v_ref[bi, u], b_ref[bi, u], r_ref[bi, u])
                 for bi in range(nb)]
        for i in range(RWKV_STEPS):
            for n, (bi, p) in enumerate(chains):
                at, _, _, vt, _, _ = tiles[bi]
                row = slice(i * RWKV_PAIRS + p, i * RWKV_PAIRS + p + 1)
                pa_scr[rows(n), :] = s_scr[rows(n), :] * at[row]
                pv_scr[rows(n), :] = isel * vt[row]
            sa = hsum(pa_scr, RWKV_SUM_PIECES)
            vc = hsum(pv_scr)
            for n, (bi, p) in enumerate(chains):
                _, wt, kt, _, bt, rt = tiles[bi]
                row = slice(i * RWKV_PAIRS + p, i * RWKV_PAIRS + p + 1)
                s = s_scr[rows(n), :] * wt[row] + sa[rows(n)] * bt[row] + vc[rows(n)] * kt[row]
                s_scr[rows(n), :] = s
                pa_scr[rows(n), :] = s * rt[row]
            yb = hsum(pa_scr, RWKV_SUM_PIECES)
            for n, (bi, p) in enumerate(chains):
                row = slice(i * RWKV_PAIRS + p, i * RWKV_PAIRS + p + 1)
                y_ref[bi, u, row, :] = jnp.sum(yb[rows(n)] * isel, axis=0, keepdims=True)
        return carry

    lax.fori_loop(0, min(t_valid, tc * RWKV_STEPS) // RWKV_STEPS, step, 0)

    @pl.when(c == pl.num_programs(1) - 1)
    def _():
        for n, (bi, p) in enumerate(chains):
            sout_ref[bi, p] = s_scr[rows(n), :]


def rwkv_scan_call(r, w, k, v, a, b, s0, t_valid):
    bsz, t, hd = r.shape
    nb = math.gcd(bsz, RWKV_NB)
    nt = t // RWKV_STEPS
    tc = _row_tile(nt, 32)
    assert t % RWKV_STEPS == 0 and t_valid % RWKV_STEPS == 0 and (t_valid == t or nt == tc)
    tiled = lambda x: x.reshape(bsz, nt, SUBLANES, LANES)
    seq = pl.BlockSpec((nb, tc, SUBLANES, LANES), lambda i, j: (i, j, 0, 0))
    st = pl.BlockSpec((nb, RWKV_PAIRS, DH_B, LANES), lambda i, j: (i, 0, 0, 0))
    y, s_fin = pl.pallas_call(
        functools.partial(_rwkv_scan_body, t_valid=t_valid),
        grid=(bsz // nb, nt // tc),
        in_specs=[seq] * 6 + [st],
        out_specs=[seq, st],
        out_shape=[jax.ShapeDtypeStruct((bsz, nt, SUBLANES, LANES), F32),
                   jax.ShapeDtypeStruct((bsz, RWKV_PAIRS, DH_B, LANES), F32)],
        scratch_shapes=[pltpu.VMEM((nb * RWKV_PAIRS * DH_B, LANES), F32)] * 3,
        compiler_params=_cparams(("parallel", "arbitrary")),
        name="rwkv_scan",
    )(tiled(r), tiled(w), tiled(k), tiled(v), tiled(a), tiled(b), s0)
    return y.reshape(bsz, t, hd), s_fin


RWKV_CHUNK = 64
RWKV_HG = 4
RWKV_CHUNK_SEQS = 2


def _rwkv_chunk_body(zb_ref, sp_ref, mu_ref, w0_ref, a0_ref, w2_ref, a2_ref, g2_ref, kkp_ref, ka_ref,
                     lg_ref, lb_ref, rk_ref, s0_ref, yb_ref, sout_ref, s_scr, carry_scr):
    c = pl.program_id(1)
    nb, cs = zb_ref.shape[0], zb_ref.shape[1]
    ng = s_scr.shape[1]
    gw = RWKV_HG * DH_B
    n = RWKV_HG * cs
    assert cs & (cs - 1) == 0

    @pl.when(c == 0)
    def _():
        s_scr[...] = s0_ref[...]
        carry_scr[...] = sp_ref[...]

    seqs = []
    for bi in range(nb):
        zb = zb_ref[bi]
        seqs.append(_rwkv_prep_math(zb, carry_scr[bi], mu_ref[...], w0_ref[...], a0_ref[...], w2_ref[...],
                                    a2_ref[...], g2_ref[...], kkp_ref[...], ka_ref[...]))
        carry_scr[bi] = zb[cs - 1:cs]

    tr = lax.broadcasted_iota(I32, (cs, cs), 0)
    tcol = lax.broadcasted_iota(I32, (cs, cs), 1)
    tril = jnp.where(tr >= tcol, 1.0, 0.0).astype(F32)
    row = lax.broadcasted_iota(I32, (n, n), 0)
    col = lax.broadcasted_iota(I32, (n, n), 1)
    same = (row // cs) == (col // cs)
    tt = row & (cs - 1)
    ss = col & (cs - 1)
    strict = jnp.where(same, jnp.where(tt > ss, 1.0, 0.0), 0.0).astype(F32)
    incl = jnp.where(same, jnp.where(tt >= ss, 1.0, 0.0), 0.0).astype(F32)
    eye = jnp.where(row == col, 1.0, 0.0).astype(F32)
    lane_h = lax.broadcasted_iota(I32, (1, gw), 1) // DH_B
    hmask = [jnp.where(lane_h == h, 1.0, 0.0).astype(F32) for h in range(RWKV_HG)]
    sblk = jnp.where(lax.broadcasted_iota(I32, (gw, gw), 0) // DH_B == lax.broadcasted_iota(I32, (gw, gw), 1) // DH_B,
                     1.0, 0.0).astype(F32)
    nn = (((1,), (0,)), ((), ()))

    def stack(x):
        return jnp.concatenate([x * m for m in hmask], axis=0)

    def fold(x):
        out = x[0:cs]
        for h in range(1, RWKV_HG):
            out = out + x[h * cs:(h + 1) * cs]
        return out

    def bdot(x, y, dims=nn):
        return lax.dot_general(x.astype(BF16), y.astype(BF16), dims, preferred_element_type=F32)

    jobs = []
    for bi in range(nb):
        r_all, lw_all, k_all, v_all, a_all, b_all, _ = seqs[bi]
        for g in range(ng):
            sl = slice(g * gw, (g + 1) * gw)
            lw = lw_all[:, sl]
            cl = jnp.dot(tril, lw, precision=HI, preferred_element_type=F32)
            e_in = jnp.exp(cl)
            e_inv = jnp.exp(-cl)
            e_tail = jnp.exp(cl[cs - 1:cs] - cl)
            jobs.append(dict(
                bi=bi, g=g, p_last=e_in[cs - 1:cs],
                at=a_all[:, sl] * jnp.exp(cl - lw),
                bt=b_all[:, sl] * e_inv, kt=k_all[:, sl] * e_inv, rt=r_all[:, sl] * e_in,
                bw=b_all[:, sl] * e_tail, kw=k_all[:, sl] * e_tail, v=v_all[:, sl],
                s_old=s_scr[bi, g]))
    for j in jobs:
        j['a_s'], j['r_s'], j['b_s'], j['k_s'], j['v_s'] = (stack(j[x]) for x in ('at', 'rt', 'bt', 'kt', 'v'))
    for j in jobs:
        j['lab'] = bdot(j['a_s'], j['b_s'], NT) * strict
        j['lak'] = bdot(j['a_s'], j['k_s'], NT) * strict
        j['lrb'] = bdot(j['r_s'], j['b_s'], NT) * incl
        j['lrk'] = bdot(j['r_s'], j['k_s'], NT) * incl
    for j in jobs:
        j['rhs'] = stack(bdot(j['at'], j['s_old'], NT)) + bdot(j['lak'], j['v_s'])
        j['tm'] = eye + j['lab']
        j['lp'] = j['lab']
    for _ in range(cs.bit_length() - 2):
        for j in jobs:
            j['lp'] = bdot(j['lp'], j['lp'])
        for j in jobs:
            j['tm'] = j['tm'] + bdot(j['lp'], j['tm'])
    for j in jobs:
        j['u_s'] = bdot(j['tm'], j['rhs'])
    for j in jobs:
        j['y'] = fold(stack(bdot(j['rt'], j['s_old'], NT)) + bdot(j['lrb'], j['u_s']) + bdot(j['lrk'], j['v_s']))
        u = fold(j['u_s'])
        j['s_new'] = j['s_old'] * j['p_last'] + (bdot(u, j['bw'], TN) + bdot(j['v'], j['kw'], TN)) * sblk
    for bi in range(nb):
        r_all, _, k_all, v_all, _, _, g_all = seqs[bi]
        y = jnp.concatenate([j['y'] for j in jobs if j['bi'] == bi], axis=1)
        yb_ref[bi] = _rwkv_post_math(y, r_all, k_all, v_all, g_all, lg_ref[...], lb_ref[...], rk_ref[...])
    for j in jobs:
        s_scr[j['bi'], j['g']] = j['s_new']

    @pl.when(c == pl.num_programs(1) - 1)
    def _():
        sout_ref[...] = s_scr[...]


def rwkv_chunk_call(zb, shift_prev, mu, w0, a0, w2p, a2p, g2, kkp, ka, lnx_g, lnx_b, rk, s0):
    bsz, t, dz = zb.shape
    hd = H_B * DH_B
    cs = RWKV_CHUNK
    ng = H_B // RWKV_HG
    gw = RWKV_HG * DH_B
    eye = jnp.eye(RWKV_HG, dtype=F32)
    s0g = jnp.einsum('bghij,hk->bghikj', s0.astype(F32).reshape(bsz, ng, RWKV_HG, DH_B, DH_B), eye)
    nb = math.gcd(bsz, RWKV_CHUNK_SEQS)
    st = pl.BlockSpec((nb, ng, gw, gw), lambda i, j: (i, 0, 0, 0))
    row = lambda n: pl.BlockSpec((1, n), lambda i, j: (0, 0))
    full = lambda s: pl.BlockSpec(s, lambda i, j: (0, 0))
    y, s_fin = pl.pallas_call(
        _rwkv_chunk_body,
        grid=(bsz // nb, t // cs),
        in_specs=[
            pl.BlockSpec((nb, cs, dz), lambda i, j: (i, j, 0)),
            pl.BlockSpec((nb, 1, dz), lambda i, j: (i, 0, 0)),
            row(dz), row(hd), row(hd),
            full((R_DECAY + R_AAA, hd)), full((R_DECAY + R_AAA, hd)), full((R_GATE, hd)),
            row(hd), row(hd), row(hd), row(hd), row(hd), st],
        out_specs=[pl.BlockSpec((nb, cs, hd), lambda i, j: (i, j, 0)), st],
        out_shape=[jax.ShapeDtypeStruct((bsz, t, hd), F32), jax.ShapeDtypeStruct((bsz, ng, gw, gw), F32)],
        scratch_shapes=[pltpu.VMEM((nb, ng, gw, gw), F32), pltpu.VMEM((nb, 1, dz), F32)],
        compiler_params=_cparams(("parallel", "arbitrary")),
        name="rwkv_chunk",
    )(zb, shift_prev, mu, w0, a0, w2p, a2p, g2, kkp, ka, lnx_g, lnx_b, rk, s0g.reshape(bsz, ng, gw, gw))
    s_fin = jnp.einsum('bghihj->bghij', s_fin.reshape(bsz, ng, RWKV_HG, DH_B, RWKV_HG, DH_B))
    return y, s_fin.reshape(bsz, H_B, DH_B, DH_B)


def _rwkv_post_body(y_ref, r_ref, k_ref, v_ref, g_ref, lg_ref, lb_ref, rk_ref, o_ref):
    o_ref[...] = _rwkv_post_math(y_ref[...], r_ref[...], k_ref[...], v_ref[...], g_ref[...],
                                 lg_ref[...], lb_ref[...], rk_ref[...])


def rwkv_post_call(y, r, k, v, g, lnx_g, lnx_b, rk):
    m, hd = y.shape
    tm = _row_tile(m, 512)
    tok = pl.BlockSpec((tm, hd), lambda i: (i, 0))
    row = pl.BlockSpec((1, hd), lambda i: (0, 0))
    return pl.pallas_call(
        _rwkv_post_body,
        grid=(m // tm,),
        in_specs=[tok] * 5 + [row] * 3,
        out_specs=tok,
        out_shape=jax.ShapeDtypeStruct((m, hd), F32),
        compiler_params=_cparams(("parallel",)),
        name="rwkv_post",
    )(y, r, k, v, g, lnx_g, lnx_b, rk)


def _head_mask(n, h, width):
    lane = lax.broadcasted_iota(I32, (1, n), 1)
    return jnp.where(lane >= h * width, jnp.where(lane < (h + 1) * width, 1.0, 0.0), 0.0).astype(F32)


def _head_norm128(y, g):
    mu = jnp.mean(y, axis=-1, keepdims=True)
    d = y - mu
    var = jnp.mean(d * d, axis=-1, keepdims=True)
    return d * lax.rsqrt(var + HN_EPS) * g


def _ret_body(qk_ref, v_ref, gc_ref, cos_ref, sin_ref, gn_ref, s0_ref, y_ref, sout_ref, s_scr, *, l_valid):
    c = pl.program_id(1)
    lc = qk_ref.shape[1]
    hk = H_C * DK_C

    @pl.when(c == 0)
    def _():
        s_scr[...] = s0_ref[0]

    rr = lax.broadcasted_iota(I32, (LANES, LANES), 0)
    cc = lax.broadcasted_iota(I32, (LANES, LANES), 1)
    half = DK_C // 2
    same = (rr // DK_C) == (cc // DK_C)
    dr = rr & (DK_C - 1)
    dc = cc & (DK_C - 1)
    rot = jnp.where(same, jnp.where(dr == dc + half, -1.0, jnp.where(dr + half == dc, 1.0, 0.0)), 0.0).astype(F32)

    qk = qk_ref[0]
    cos = cos_ref[...]
    sin = sin_ref[...]

    def rope(x):
        return x * cos + _lane_group_dot(x, rot) * sin

    qr = rope(qk[:, :hk])
    kr = rope(qk[:, hk:]) * DK_C ** -0.5
    krb = kr.astype(BF16)
    s_prev = s_scr[...]
    s_prev_b = s_prev.astype(BF16)

    jj = lax.broadcasted_iota(I32, (lc, lc), 0)
    ss = lax.broadcasted_iota(I32, (lc, lc), 1)
    diff = (jj - ss).astype(F32)
    jcol = lax.broadcasted_iota(I32, (lc, 1), 0).astype(F32)
    srow_state = lax.broadcasted_iota(I32, (hk, 1), 0) // DK_C
    s_new = jnp.zeros_like(s_prev)
    decay_rows = jnp.zeros((hk, 1), F32)
    hs = []
    for h in range(H_C):
        lg = math.log1p(-2.0 ** (-5.0 - h))
        mh = _head_mask(hk, h, DK_C)
        qm = (qr * mh).astype(BF16)
        hs.append(dict(h=h, lg=lg, mh=mh, qm=qm, vhb=v_ref[0, :, h * DV_C:(h + 1) * DV_C].astype(BF16),
                       qk=lax.dot_general(qm, krb, NT, preferred_element_type=F32),
                       qs=jnp.dot(qm, s_prev_b, preferred_element_type=F32)))
    for d in hs:
        dmask = jnp.where(diff >= 0, jnp.exp(jnp.maximum(diff, 0.0) * d['lg']), 0.0)
        d['scores'] = (d['qk'] * dmask).astype(BF16)
    for d in hs:
        h = d['h']
        intra = jnp.dot(d['scores'], d['vhb'], preferred_element_type=F32)
        inter = d['qs'] * jnp.exp((jcol + 1.0) * d['lg'])
        yh = _head_norm128(intra + inter, gn_ref[h:h + 1, :])
        gch = gc_ref[0, :, h * DV_C:(h + 1) * DV_C]
        d['out'] = yh * (gch * jax.nn.sigmoid(gch))
    for d in hs:
        w_s = jnp.where(jcol < l_valid, jnp.exp((l_valid - 1.0 - jcol) * d['lg']), 0.0)
        kw = (kr * d['mh'] * w_s).astype(BF16)
        s_new = s_new + lax.dot_general(kw, d['vhb'], TN, preferred_element_type=F32)
        decay_rows = jnp.where(srow_state == d['h'], math.exp(l_valid * d['lg']), decay_rows)
    for d in hs:
        y_ref[0, :, d['h'] * DV_C:(d['h'] + 1) * DV_C] = d['out']
    s_fin = decay_rows * s_prev + s_new
    s_scr[...] = s_fin

    @pl.when(c == pl.num_programs(1) - 1)
    def _():
        sout_ref[0] = s_fin


def retention_call(za, cos, sin, gn, s0, l_valid):
    b, t, _ = za.shape
    lc = _row_tile(t, 128)
    assert l_valid == lc or t == lc
    hk, hv = H_C * DK_C, H_C * DV_C
    blk = lambda j: pl.BlockSpec((1, lc, 2 * hk), lambda i, c, j=j: (i, c, j))
    st = pl.BlockSpec((1, hk, DV_C), lambda i, c: (i, 0, 0))
    return pl.pallas_call(
        functools.partial(_ret_body, l_valid=l_valid),
        grid=(b, t // lc),
        in_specs=[blk(0), blk(1), blk(2),
                  pl.BlockSpec((lc, hk), lambda i, c: (c, 0)),
                  pl.BlockSpec((lc, hk), lambda i, c: (c, 0)),
                  pl.BlockSpec((H_C, DV_C), lambda i, c: (0, 0)),
                  st],
        out_specs=[pl.BlockSpec((1, lc, hv), lambda i, c: (i, c, 0)), st],
        out_shape=[jax.ShapeDtypeStruct((b, t, hv), F32), jax.ShapeDtypeStruct((b, hk, DV_C), F32)],
        scratch_shapes=[pltpu.VMEM((hk, DV_C), F32)],
        compiler_params=_cparams(("parallel", "arbitrary")),
        name="retention",
    )(za, za, za, cos, sin, gn, s0)


def _mlstm_body(qk_ref, v_ref, og_ref, gt_ref, cb_ref, cw_ref, cbias_ref, gbias_ref, gn_ref,
                c0_ref, n0_ref, m0_ref, y_ref, cout_ref, nout_ref, mout_ref,
                ext_scr, c_scr, n_scr, m_scr, *, l_valid):
    c = pl.program_id(1)
    lc = qk_ref.shape[1]
    hk = H_D * DK_D
    pad = SUBLANES

    @pl.when(c == 0)
    def _():
        ext_scr[0:pad, :] = cb_ref[0]
        c_scr[...] = c0_ref[0]
        n_scr[...] = n0_ref[0]
        m_scr[...] = m0_ref[0]

    u = qk_ref[0]
    ext_scr[pad:pad + lc, :] = u
    acc = cbias_ref[...] + u * cw_ref[CONV_W - 1:CONV_W, :]
    for i in range(CONV_W - 1):
        sh = CONV_W - 1 - i
        acc = acc + ext_scr[pad - sh:pad - sh + lc, :] * cw_ref[i:i + 1, :]
    ext_scr[0:pad, :] = u[lc - pad:lc, :]
    qkc = acc * jax.nn.sigmoid(acc)
    q = qkc[:, :hk]
    k = qkc[:, hk:] * DK_D ** -0.5
    kb = k.astype(BF16)

    gates = gt_ref[0] + gbias_ref[...]
    logsig = -_softplus(-gates)
    rowi = lax.broadcasted_iota(I32, (lc, LANES), 0)
    logsig = jnp.where(rowi < l_valid, logsig, 0.0)
    jj = lax.broadcasted_iota(I32, (lc, lc), 0)
    ss = lax.broadcasted_iota(I32, (lc, lc), 1)
    tril = jnp.where(jj >= ss, 1.0, 0.0).astype(F32)
    bcum = jnp.dot(tril, logsig, precision=HI, preferred_element_type=F32)
    lane_g = lax.broadcasted_iota(I32, (lc, LANES), 1)
    jcol = lax.broadcasted_iota(I32, (lc, 1), 0)
    causal = jj >= ss
    ninf = -jnp.inf

    c_prev = c_scr[...]
    c_prev_b = c_prev.astype(BF16)
    n_prev = n_scr[...]
    m_prev = m_scr[...]
    c_new = jnp.zeros_like(c_prev)
    carry_rows = jnp.zeros((hk, 1), F32)
    carry_lanes = jnp.zeros((1, hk), F32)
    ws_full = jnp.zeros((lc, hk), F32)
    m_out = m_prev
    srow_state = lax.broadcasted_iota(I32, (hk, 1), 0) // DK_D
    lane_state = lax.broadcasted_iota(I32, (1, hk), 1) // DK_D
    lane_m = lax.broadcasted_iota(I32, (1, LANES), 1)
    hs = []
    for h in range(H_D):
        mh = _head_mask(hk, h, DK_D)
        e_i = jnp.where(lane_g == h, 1.0, 0.0).astype(F32)
        e_f = jnp.where(lane_g == H_D + h, 1.0, 0.0).astype(F32)
        qm = q * mh
        vh = v_ref[0, :, h * DV_D:(h + 1) * DV_D]
        hs.append(dict(
            h=h, mh=mh, qm=qm, qmb=qm.astype(BF16), vh=vh, vhb=vh.astype(BF16),
            logi_col=jnp.where(jcol < l_valid, gates[:, h:h + 1], ninf),
            b_col=bcum[:, H_D + h:H_D + h + 1],
            i_row=lax.dot_general(e_i, gates, NT, precision=HI, preferred_element_type=F32),
            b_row=lax.dot_general(e_f, bcum, NT, precision=HI, preferred_element_type=F32),
            m_h=m_prev[:, h:h + 1]))
    for d in hs:
        d['qk'] = lax.dot_general(d['qmb'], kb, NT, preferred_element_type=F32)
        d['qc'] = jnp.dot(d['qmb'], c_prev_b, preferred_element_type=F32)
    for d in hs:
        i_row = jnp.where(ss < l_valid, d['i_row'], ninf)
        d['inter'] = d['b_col'] + d['m_h']
        dmat = jnp.where(causal, d['b_col'] - d['b_row'] + i_row, ninf)
        d['m_j'] = jnp.maximum(d['inter'], jnp.max(dmat, axis=-1, keepdims=True))
        d['amat'] = jnp.exp(dmat - d['m_j']) * d['qk']
        d['sc'] = jnp.exp(d['inter'] - d['m_j'])
    for d in hs:
        num = jnp.dot(d['amat'].astype(BF16), d['vhb'], preferred_element_type=F32) + d['sc'] * d['qc']
        den = jnp.sum(d['amat'], axis=-1, keepdims=True) \
            + d['sc'] * jnp.sum(d['qm'] * n_prev, axis=-1, keepdims=True)
        hh = num / jnp.maximum(jnp.abs(den), jnp.exp(-d['m_j']))
        h = d['h']
        ogh = og_ref[0, :, h * DV_D:(h + 1) * DV_D]
        d['out'] = _head_norm128(hh, gn_ref[h:h + 1, :]) * jax.nn.sigmoid(ogh)
    for d in hs:
        h = d['h']
        b_last = d['b_col'][l_valid - 1:l_valid, :]
        gs = b_last - d['b_col'] + d['logi_col']
        m_new = jnp.maximum(b_last + d['m_h'], jnp.max(gs, axis=0, keepdims=True))
        ws = jnp.exp(gs - m_new)
        carry = jnp.exp(b_last + d['m_h'] - m_new)
        c_new = c_new + lax.dot_general((k * d['mh']).astype(BF16), (d['vh'] * ws).astype(BF16), TN,
                                        preferred_element_type=F32)
        carry_rows = jnp.where(srow_state == h, carry, carry_rows)
        carry_lanes = jnp.where(lane_state == h, carry, carry_lanes)
        ws_full = ws_full + ws * d['mh']
        m_out = jnp.where(lane_m == h, m_new, m_out)
    for d in hs:
        y_ref[0, :, d['h'] * DV_D:(d['h'] + 1) * DV_D] = d['out']
    c_fin = carry_rows * c_prev + c_new
    n_fin = carry_lanes * n_prev + jnp.sum(ws_full * k, axis=0, keepdims=True)
    c_scr[...] = c_fin
    n_scr[...] = n_fin
    m_scr[...] = m_out

    @pl.when(c == pl.num_programs(1) - 1)
    def _():
        cout_ref[0] = c_fin
        nout_ref[0] = n_fin
        mout_ref[0] = m_out


def mlstm_call(zb, zc, conv_buf, conv_w, conv_b, gate_bias, gn, c0, n0, m0, l_valid):
    b, t, _ = zb.shape
    lc = _row_tile(t, 128)
    assert l_valid == lc or t == lc
    hk, hv = H_D * DK_D, H_D * DV_D
    blk = lambda j: pl.BlockSpec((1, lc, 2 * hk), lambda i, c, j=j: (i, c, j))
    cst = lambda s: pl.BlockSpec(s, lambda i, c: (0,) * len(s))
    per_b = lambda s: pl.BlockSpec((1,) + s, lambda i, c: (i,) + (0,) * len(s))
    return pl.pallas_call(
        functools.partial(_mlstm_body, l_valid=l_valid),
        grid=(b, t // lc),
        in_specs=[blk(0), blk(1), blk(2),
                  pl.BlockSpec((1, lc, LANES), lambda i, c: (i, c, 0)),
                  per_b((SUBLANES, 2 * hk)),
                  cst((CONV_W, 2 * hk)), cst((1, 2 * hk)), cst((1, LANES)), cst((H_D, DV_D)),
                  per_b((hk, DV_D)), per_b((1, hk)), per_b((1, LANES))],
        out_specs=[pl.BlockSpec((1, lc, hv), lambda i, c: (i, c, 0)),
                   per_b((hk, DV_D)), per_b((1, hk)), per_b((1, LANES))],
        out_shape=[jax.ShapeDtypeStruct((b, t, hv), F32),
                   jax.ShapeDtypeStruct((b, hk, DV_D), F32),
                   jax.ShapeDtypeStruct((b, 1, hk), F32),
                   jax.ShapeDtypeStruct((b, 1, LANES), F32)],
        scratch_shapes=[pltpu.VMEM((SUBLANES + lc, 2 * hk), F32),
                        pltpu.VMEM((hk, DV_D), F32),
                        pltpu.VMEM((1, hk), F32),
                        pltpu.VMEM((1, LANES), F32)],
        compiler_params=_cparams(("parallel", "arbitrary")),
        name="mlstm",
    )(zb, zb, zb, zc, conv_buf, conv_w, conv_b, gate_bias, gn, c0, n0, m0)


def _pad_cols(w, n):
    return jnp.pad(w, ((0, 0), (0, n - w.shape[1])))


def _pad_time(a, tp):
    t = a.shape[1]
    if t == tp:
        return a
    return jnp.pad(a, ((0, 0), (0, tp - t)) + ((0, 0),) * (a.ndim - 2))


def _heads_major(a, h):
    b, t, _ = a.shape
    return a.reshape(b, t, h, -1).transpose(0, 2, 1, 3)


def _ab_mixer(xf, b, t, e, g_pre, P, prompt, shift_prev, s0, cache):
    m = b * t
    hd = H_A * DH_A
    w = P['ab_w_in'][e]
    o_qi = 3 * hd
    o_ki = o_qi + H_IDX * D_IDX
    o_wi = o_ki + D_IDX
    o_zb = o_wi + H_IDX
    qkv = proj_in_call(xf, g_pre, w[:, :o_qi].astype(BF16))
    qi = proj_in_call(xf, g_pre, w[:, o_qi:o_ki].astype(BF16), out_dtype=BF16)
    kw, ki2 = kiwi_call(xf, g_pre, w[:, o_ki:o_wi], w[:, o_wi:o_zb], P['kidx_g'][e])
    zb = proj_in_call(xf, g_pre, w[:, o_zb:].astype(BF16))
    q3 = qkv[:, :hd].reshape(b, t, hd)
    k3 = qkv[:, hd:2 * hd].reshape(b, t, hd)
    v3 = qkv[:, 2 * hd:].reshape(b, t, hd)
    ki3 = kw[:, :D_IDX].reshape(b, t, D_IDX)
    wi3 = kw[:, D_IDX:D_IDX + H_IDX].reshape(b, t, H_IDX)

    if prompt:
        ya = dsa_prompt_call(qkv.reshape(b, t, 3 * hd), qi.reshape(b, t, H_IDX * D_IDX),
                             ki2.reshape(b, t, LANES), kw.reshape(b, t, LANES)).reshape(m, hd)
    else:
        cache_k, cache_v, cache_kidx, page_table = cache
        n_pool = cache_k.shape[1]
        nq = SUBLANES
        qi_s = _pad_time(qi.reshape(b, t, H_IDX * D_IDX), nq).reshape(b, nq * H_IDX, D_IDX)
        wi_s = _pad_time(wi3, nq).reshape(b, 1, nq * H_IDX)
        keys_last = lambda a: jnp.moveaxis(a, 1, -1)
        new_page = lambda a: jnp.pad(keys_last(a), ((0, 0),) * (a.ndim - 1) + ((0, PAGE_SIZE - t),))
        isc, isc_new = dsa_sidx_call(page_table, qi_s, wi_s, new_page(ki3), keys_last(cache_kidx[e]))
        ya = dsa_satt_call(page_table, isc, isc_new, _pad_time(q3, nq),
                           new_page(k3.reshape(b, t, H_A, DH_A)), new_page(v3.reshape(b, t, H_A, DH_A)),
                           keys_last(cache_k[e]), keys_last(cache_v[e]), t)
        ya = ya[:, :t].reshape(m, hd)

    tp = -(-t // SUBLANES) * SUBLANES
    hb = H_B * DH_B
    zb3 = zb.reshape(b, t, D_B_IN)
    zpad = jnp.zeros((R_DECAY, hb), F32)
    w2p = jnp.concatenate([P['rwkv_w2'][e], zpad], axis=0).astype(BF16)
    a2p = jnp.concatenate([zpad, P['rwkv_a2'][e]], axis=0).astype(BF16)
    row = lambda a: a.reshape(1, -1)
    prep_args = (shift_prev.reshape(b, 1, D_B_IN), row(P['rwkv_mu'][e]), row(P['rwkv_w0'][e]),
                 row(P['rwkv_a0'][e]), w2p, a2p, P['rwkv_g2'][e].astype(BF16), row(P['rwkv_kk'][e]),
                 row(P['rwkv_ka'][e]))
    rk = jnp.broadcast_to(P['rwkv_rk'][e], (H_B, DH_B)).reshape(1, hb)
    post_args = (row(P['rwkv_lnx_g'][e]), row(P['rwkv_lnx_b'][e]), rk)
    if t % RWKV_CHUNK == 0:
        yb, s_new = rwkv_chunk_call(zb3, *prep_args, *post_args, s0)
        yb = yb.reshape(m, hb)
    else:
        r, dec, k2, v, an, bb, g = rwkv_prep_call(_pad_time(zb3, tp), *prep_args)
        s0p = s0.reshape(b, RWKV_PAIRS, 2, DH_B, DH_B).transpose(0, 1, 3, 2, 4).reshape(b, RWKV_PAIRS, DH_B, LANES)
        y, s_fin = rwkv_scan_call(r, dec, k2, v, an, bb, s0p, t)
        s_new = s_fin.reshape(b, RWKV_PAIRS, DH_B, 2, DH_B).transpose(0, 1, 3, 2, 4).reshape(b, H_B, DH_B, DH_B)
        fl = lambda a: a.reshape(b * tp, hb)
        yb = rwkv_post_call(fl(y), fl(r), fl(k2), fl(v), fl(g), *post_args)
        yb = yb.reshape(b, tp, hb)[:, :t].reshape(m, hb)
    st = (k3.reshape(b, t, H_A, DH_A), v3.reshape(b, t, H_A, DH_A), ki3, s_new, zb3[:, t - 1])
    return ya, yb, st


def _cd_mixer(xf, b, t, o, g_pre, pos, P, ret_s, m_c, m_n, m_m, conv_buf):
    w = P['cd_w_in'][o]
    hk, hv = H_C * DK_C, H_C * DV_C
    o_g = 2 * hk + 2 * hv
    o_vd = o_g + 2 * H_D * DK_D
    o_ig = o_vd + H_D * DV_D
    o_og = o_ig + 2 * H_D
    za = proj_in_call(xf, g_pre, w[:, :o_g].astype(BF16))
    zb = proj_in_call(xf, g_pre, jnp.concatenate([w[:, o_g:o_ig], w[:, o_og:]], axis=1).astype(BF16))
    zc = proj_in_call(xf, g_pre, _pad_cols(w[:, o_ig:o_og], LANES).astype(BF16))
    tp = -(-t // SUBLANES) * SUBLANES
    l_valid = t if tp != t else _row_tile(t, 128)
    za3 = _pad_time(za.reshape(b, t, -1), tp)
    zb3 = zb.reshape(b, t, -1)
    zc3 = _pad_time(zc.reshape(b, t, -1), tp)

    half = DK_C // 2
    inv = ROPE_BASE ** (-jnp.arange(half, dtype=F32) / half)
    ang = _pad_time(pos.astype(F32)[None], tp)[0][:, None] * inv[None, :]
    cos = jnp.tile(jnp.cos(ang), (1, 2 * H_C))
    sin = jnp.tile(jnp.sin(ang), (1, 2 * H_C))
    yc, ret_new = retention_call(za3, cos, sin, P['ret_gn'][o], ret_s.astype(F32).reshape(b, hk, DV_C), l_valid)

    hkd = H_D * DK_D
    cb = jnp.pad(conv_buf.astype(F32), ((0, 0), (SUBLANES - (CONV_W - 1), 0), (0, 0)))
    gate_bias = jnp.pad(P['mlstm_if_b'][o].astype(F32).reshape(1, 2 * H_D), ((0, 0), (0, LANES - 2 * H_D)))
    c0 = m_c.astype(F32).transpose(0, 1, 3, 2).reshape(b, hkd, DV_D)
    n0 = m_n.astype(F32).reshape(b, 1, hkd)
    m0 = jnp.pad(m_m.astype(F32), ((0, 0), (0, LANES - H_D))).reshape(b, 1, LANES)
    yd, c_new, n_new, m_new = mlstm_call(_pad_time(zb3, tp), zc3, cb, P['conv_w'][o], P['conv_b'][o].reshape(1, -1),
                                         gate_bias, P['mlstm_gn'][o], c0, n0, m0, l_valid)
    conv_new = jnp.concatenate([conv_buf.astype(F32), zb3[:, :, :2 * hkd]], axis=1)[:, -(CONV_W - 1):]
    st = (ret_new.reshape(b, H_C, DK_C, DV_C),
          c_new.reshape(b, H_D, DK_D, DV_D).transpose(0, 1, 3, 2),
          n_new.reshape(b, H_D, DK_D),
          m_new.reshape(b, LANES)[:, :H_D],
          conv_new)
    m = b * t
    return yc[:, :t].reshape(m, hv), yd[:, :t].reshape(m, H_D * DV_D), st


def _trunk(x, pos, prompt, ab_init, cd_init, cache, P):
    b, t, d = x.shape
    xf = x.reshape(b * t, d)
    depth = P['norm_g'].shape[0]
    ab_new, cd_new = [], []
    for l in range(depth):
        g = P['norm_g'][l]
        bf = lambda a: a.astype(BF16)
        xf = ffn_call(xf, g[0], g[1], bf(P['ffn_wg'][l, 0]), bf(P['ffn_wu'][l, 0]), bf(P['ffn_wd'][l, 0]))
        if l % 2 == 0:
            e = l // 2
            shift_prev, s0 = ab_init(e)
            ya, yb, st = _ab_mixer(xf, b, t, e, g[2], P, prompt, shift_prev, s0, cache)
            ab_new.append(st)
            wo = bf(P['ab_w_out'][e])
            xf = proj_out_call(ya, yb, xf, g[3], wo[:H_A * DH_A], wo[H_A * DH_A:])
        else:
            o = l // 2
            yc, yd, st = _cd_mixer(xf, b, t, o, g[2], pos, P, *cd_init(o))
            cd_new.append(st)
            wo = bf(P['cd_w_out'][o])
            xf = proj_out_call(yc, yd, xf, g[3], wo[:H_C * DV_C], wo[H_C * DV_C:])
        xf = ffn_call(xf, g[4], g[5], bf(P['ffn_wg'][l, 1]), bf(P['ffn_wu'][l, 1]), bf(P['ffn_wd'][l, 1]))
    ab = tuple(jnp.stack(s) for s in zip(*ab_new))
    cd = tuple(jnp.stack(s) for s in zip(*cd_new))
    return xf.reshape(b, t, d), ab, cd


def kernel(x_prompt, x_sample, cache_k, cache_v, cache_kidx, state_rwkv, state_shift, state_ret, state_mlstm_C, state_mlstm_n, state_mlstm_m, state_conv, page_table, norm_g, ffn_wg, ffn_wu, ffn_wd, ab_w_in, ab_w_out, kidx_g, rwkv_mu, rwkv_w0, rwkv_w2, rwkv_a0, rwkv_a2, rwkv_g2, rwkv_kk, rwkv_ka, rwkv_rk, rwkv_lnx_g, rwkv_lnx_b, cd_w_in, cd_w_out, ret_gn, conv_w, conv_b, mlstm_if_b, mlstm_gn):
    P = dict(norm_g=norm_g, ffn_wg=ffn_wg, ffn_wu=ffn_wu, ffn_wd=ffn_wd, ab_w_in=ab_w_in, ab_w_out=ab_w_out,
             kidx_g=kidx_g, rwkv_mu=rwkv_mu, rwkv_w0=rwkv_w0, rwkv_w2=rwkv_w2, rwkv_a0=rwkv_a0, rwkv_a2=rwkv_a2,
             rwkv_g2=rwkv_g2, rwkv_kk=rwkv_kk, rwkv_ka=rwkv_ka, rwkv_rk=rwkv_rk, rwkv_lnx_g=rwkv_lnx_g,
             rwkv_lnx_b=rwkv_lnx_b, cd_w_in=cd_w_in, cd_w_out=cd_w_out, ret_gn=ret_gn, conv_w=conv_w,
             conv_b=conv_b, mlstm_if_b=mlstm_if_b, mlstm_gn=mlstm_gn)
    B, T, _ = x_prompt.shape
    DB, DS, _ = x_sample.shape
    past = page_table.shape[1] * PAGE_SIZE

    def ab_zero(e):
        return (jnp.zeros((B, D_B_IN), F32), jnp.zeros((B, H_B, DH_B, DH_B), F32))

    def cd_zero(o):
        return (jnp.zeros((B, H_C, DK_C, DV_C), F32), jnp.zeros((B, H_D, DV_D, DK_D), F32),
                jnp.zeros((B, H_D, DK_D), F32), jnp.zeros((B, H_D), F32),
                jnp.zeros((B, CONV_W - 1, 2 * H_D * DK_D), F32))

    def ab_cached(e):
        return (state_shift[e], state_rwkv[e])

    def cd_cached(o):
        return (state_ret[o], state_mlstm_C[o], state_mlstm_n[o], state_mlstm_m[o], state_conv[o])

    y_p, (kp, vp, kip, rwp, shp), (rtp, cp, nvp, mp, cvp) = _trunk(
        x_prompt, jnp.arange(T), True, ab_zero, cd_zero, None, P)
    y_s, (ks_, vs_, kis, rws, shs), (rts, cs, nvs, ms, cvs) = _trunk(
        x_sample, past + jnp.arange(DS), False, ab_cached, cd_cached,
        (cache_k, cache_v, cache_kidx, page_table), P)
    return (y_p, y_s, kp, vp, kip, rwp, shp, rtp, cp, nvp, mp, cvp,
            ks_, vs_, kis, rws, shs, rts, cs, nvs, ms, cvs)
```

```python
import functools
import math

import numpy as np
import jax
import jax.numpy as jnp
from jax import lax
from jax.experimental import pallas as pl
from jax.experimental.pallas import tpu as pltpu

F32 = jnp.float32
BF16 = jnp.bfloat16
I32 = jnp.int32
HI = lax.Precision.HIGHEST

LANES = 128
SUBLANES = 8
VMEM_LIMIT = 56 * 1024 * 1024

EPS = 1e-6
PAGE_SIZE = 128
H_A, DH_A, H_IDX, D_IDX, TOPK_MAX = 8, 64, 16, 64, 256
H_B, DH_B, R_DECAY, R_AAA, R_GATE = 8, 64, 64, 64, 128
D_B_IN = 3 * H_B * DH_B + R_DECAY + R_AAA + R_GATE
LNX_EPS = 64e-5
H_C, DK_C, DV_C, ROPE_BASE = 4, 64, 128, 10000.0
H_D, DK_D, DV_D, CONV_W = 4, 64, 128, 4
HN_EPS = 1e-5
NEG_BIG = -1e30
INT_MIN = -(2 ** 31)
INT_MAX = 2 ** 31 - 1
KEY_NEG_INF = (0xFF800000 ^ 0x7FFFFFFF) - 2 ** 32

NT = (((1,), (1,)), ((), ()))
TN = (((0,), (0,)), ((), ()))


def _cparams(sem):
    return pltpu.CompilerParams(dimension_semantics=sem, vmem_limit_bytes=VMEM_LIMIT)


def _rms(x, g):
    return x * lax.rsqrt(jnp.mean(x * x, axis=-1, keepdims=True) + EPS) * g


def _softplus(x):
    return jnp.maximum(x, 0.0) + jnp.log(1.0 + jnp.exp(-jnp.abs(x)))


def _bf16_pieces(x, pieces):
    parts = []
    for _ in range(pieces - 1):
        h = x.astype(BF16)
        parts.append(h)
        x = x - h.astype(F32)
    parts.append(x.astype(BF16))
    return parts


def _lane_group_dot(x, m):
    m3 = jnp.concatenate([m.astype(BF16)] * 3, axis=0)
    outs = []
    for p in range(x.shape[1] // LANES):
        xp = jnp.concatenate(_bf16_pieces(x[:, p * LANES:(p + 1) * LANES], 3), axis=1)
        outs.append(jnp.dot(xp, m3, preferred_element_type=F32))
    return jnp.concatenate(outs, axis=1)


def _row_tile(m, want):
    t = min(want, m)
    while m % t:
        t //= 2
    return t


def _col_tile(n, want):
    best = LANES
    for t in range(LANES, min(n, want) + 1, LANES):
        if n % t == 0:
            best = t
    return best


def _ffn_body(x_ref, g0_ref, g1_ref, wg_ref, wu_ref, wd_ref, o_ref, h_scr, acc_scr):
    j = pl.program_id(1)

    @pl.when(j == 0)
    def _():
        h_scr[...] = _rms(x_ref[...], g0_ref[...]).astype(BF16)
        acc_scr[...] = jnp.zeros_like(acc_scr)

    h = h_scr[...]
    g = jnp.dot(h, wg_ref[...], preferred_element_type=F32)
    u = jnp.dot(h, wu_ref[...], preferred_element_type=F32)
    a = g * jax.nn.sigmoid(g) * u
    acc_scr[...] += jnp.dot(a.astype(BF16), wd_ref[...], preferred_element_type=F32)

    @pl.when(j == pl.num_programs(1) - 1)
    def _():
        o_ref[...] = x_ref[...] + 0.5 * _rms(acc_scr[...], g1_ref[...])


def ffn_call(x, g0, g1, wg, wu, wd, layer, slot):
    m, d = x.shape
    ff = wg.shape[3]
    tm = _row_tile(m, 512)
    tf = _col_tile(ff, 1408)
    return pl.pallas_call(
        _ffn_body,
        grid=(m // tm, ff // tf),
        in_specs=[
            pl.BlockSpec((tm, d), lambda i, j: (i, 0)),
            pl.BlockSpec((1, d), lambda i, j: (0, 0)),
            pl.BlockSpec((1, d), lambda i, j: (0, 0)),
            pl.BlockSpec((None, None, d, tf), lambda i, j: (layer, slot, 0, j)),
            pl.BlockSpec((None, None, d, tf), lambda i, j: (layer, slot, 0, j)),
            pl.BlockSpec((None, None, tf, d), lambda i, j: (layer, slot, j, 0)),
        ],
        out_specs=pl.BlockSpec((tm, d), lambda i, j: (i, 0)),
        out_shape=jax.ShapeDtypeStruct((m, d), F32),
        scratch_shapes=[pltpu.VMEM((tm, d), BF16), pltpu.VMEM((tm, d), F32)],
        compiler_params=_cparams(("parallel", "arbitrary")),
        name="ffn",
    )(x, g0.reshape(1, d), g1.reshape(1, d), wg, wu, wd)


def _proj_in_body(x_ref, g_ref, w_ref, o_ref, h_scr):
    @pl.when(pl.program_id(1) == 0)
    def _():
        h_scr[...] = _rms(x_ref[...], g_ref[...]).astype(BF16)

    o_ref[...] = jnp.dot(h_scr[...], w_ref[...], preferred_element_type=F32).astype(o_ref.dtype)


def _proj_split_body(x_ref, g_ref, w_ref, *o_refs):
    z = jnp.dot(_rms(x_ref[...], g_ref[...]).astype(BF16), w_ref[...], preferred_element_type=F32)
    width = o_refs[0].shape[1]
    for i, o_ref in enumerate(o_refs):
        o_ref[...] = z[:, i * width:(i + 1) * width]


def proj_split_call(x, g, w, parts):
    m, d = x.shape
    n = w.shape[1]
    width = n // parts
    assert width * parts == n and width % LANES == 0
    tm = _row_tile(m, 512)
    return pl.pallas_call(
        _proj_split_body,
        grid=(m // tm,),
        in_specs=[
            pl.BlockSpec((tm, d), lambda i: (i, 0)),
            pl.BlockSpec((1, d), lambda i: (0, 0)),
            pl.BlockSpec((d, n), lambda i: (0, 0)),
        ],
        out_specs=[pl.BlockSpec((tm, width), lambda i: (i, 0))] * parts,
        out_shape=[jax.ShapeDtypeStruct((m, width), F32)] * parts,
        compiler_params=_cparams(("parallel",)),
        name="proj_split",
    )(x, g.reshape(1, d), w)


def proj_in_call(x, g, w, out_dtype=F32):
    m, d = x.shape
    n = w.shape[1]
    tm = _row_tile(m, 512)
    tn = _col_tile(n, 2048)
    return pl.pallas_call(
        _proj_in_body,
        grid=(m // tm, n // tn),
        in_specs=[
            pl.BlockSpec((tm, d), lambda i, j: (i, 0)),
            pl.BlockSpec((1, d), lambda i, j: (0, 0)),
            pl.BlockSpec((d, tn), lambda i, j: (0, j)),
        ],
        out_specs=pl.BlockSpec((tm, tn), lambda i, j: (i, j)),
        out_shape=jax.ShapeDtypeStruct((m, n), out_dtype),
        scratch_shapes=[pltpu.VMEM((tm, d), BF16)],
        compiler_params=_cparams(("parallel", "arbitrary")),
        name="proj_in",
    )(x, g.reshape(1, d), w)


def _kiwi_body(x_ref, g_ref, w_ref, kg_ref, o_ref, k2_ref):
    h = _rms(x_ref[...], g_ref[...]).astype(BF16)
    z = jnp.dot(h, w_ref[...], preferred_element_type=F32)
    z1 = z[:, :LANES]
    lane = lax.broadcasted_iota(I32, z1.shape, 1)
    is_k = lane < D_IDX
    ms = jnp.sum(jnp.where(is_k, z1 * z1, 0.0), axis=-1, keepdims=True) * (1.0 / D_IDX)
    inv = lax.rsqrt(ms + EPS)
    kg = kg_ref[...]
    o_ref[...] = jnp.where(is_k, z1 * inv * kg[:, :LANES], z1 * (H_IDX * D_IDX) ** -0.5)
    k2_ref[...] = (z[:, LANES:] * inv * kg[:, LANES:]).astype(BF16)


def kiwi_call(x, g, w_ki, w_wi, kidx_g):
    m, d = x.shape
    tm = _row_tile(m, 512)
    zpad = jnp.zeros((d, LANES - D_IDX - H_IDX), F32)
    w = jnp.concatenate([w_ki, w_wi, zpad, w_ki, w_ki], axis=1).astype(BF16)
    gpad = jnp.zeros((LANES - D_IDX,), F32)
    kg = jnp.concatenate([kidx_g, gpad, kidx_g, kidx_g]).reshape(1, 2 * LANES)
    return pl.pallas_call(
        _kiwi_body,
        grid=(m // tm,),
        in_specs=[
            pl.BlockSpec((tm, d), lambda i: (i, 0)),
            pl.BlockSpec((1, d), lambda i: (0, 0)),
            pl.BlockSpec((d, 2 * LANES), lambda i: (0, 0)),
            pl.BlockSpec((1, 2 * LANES), lambda i: (0, 0)),
        ],
        out_specs=[pl.BlockSpec((tm, LANES), lambda i: (i, 0))] * 2,
        out_shape=[jax.ShapeDtypeStruct((m, LANES), F32), jax.ShapeDtypeStruct((m, LANES), BF16)],
        compiler_params=_cparams(("parallel",)),
        name="kiwi",
    )(x, g.reshape(1, d), w, kg)


def _proj_out_body(a1_ref, a2_ref, x_ref, g_ref, w1_ref, w2_ref, o_ref):
    y = jnp.dot(a1_ref[...].astype(BF16), w1_ref[...], preferred_element_type=F32)
    y = y + jnp.dot(a2_ref[...].astype(BF16), w2_ref[...], preferred_element_type=F32)
    o_ref[...] = x_ref[...] + _rms(y, g_ref[...])


def proj_out_call(a1, a2, x, g, w1, w2):
    m, d = x.shape
    k1, k2 = a1.shape[1], a2.shape[1]
    tm = _row_tile(m, 512)
    return pl.pallas_call(
        _proj_out_body,
        grid=(m // tm,),
        in_specs=[
            pl.BlockSpec((tm, k1), lambda i: (i, 0)),
            pl.BlockSpec((tm, k2), lambda i: (i, 0)),
            pl.BlockSpec((tm, d), lambda i: (i, 0)),
            pl.BlockSpec((1, d), lambda i: (0, 0)),
            pl.BlockSpec((k1, d), lambda i: (0, 0)),
            pl.BlockSpec((k2, d), lambda i: (0, 0)),
        ],
        out_specs=pl.BlockSpec((tm, d), lambda i: (i, 0)),
        out_shape=jax.ShapeDtypeStruct((m, d), F32),
        compiler_params=_cparams(("parallel",)),
        name="proj_out",
    )(a1, a2, x, g.reshape(1, d), w1, w2)


def _topk_bias(isc, col, topk, nbits_col):
    bits = pltpu.bitcast(isc, I32)
    key = jnp.where(bits < 0, bits ^ 0x7FFFFFFF, bits)
    kf = float(topk)

    def count(mask):
        return jnp.sum(jnp.where(mask, 1.0, 0.0), axis=-1, keepdims=True)

    prefix = jnp.where(count(key >= 0) >= kf, 0, INT_MIN).astype(I32)

    def bit_step(i, prefix):
        cand = prefix + jnp.left_shift(jnp.int32(1), 30 - i)
        return jnp.where(count(key >= cand) >= kf, cand, prefix)

    thr = lax.fori_loop(0, 31, bit_step, prefix)
    gt = key > thr
    eq = key == thr
    need = kf - count(gt)

    def col_step(i, y):
        cand = y + jnp.left_shift(jnp.int32(1), nbits_col - 1 - i)
        c = jnp.sum(jnp.where(eq, jnp.where(col < cand, 1.0, 0.0), 0.0), axis=-1, keepdims=True)
        return jnp.where(c < need, cand, y)

    tied = jnp.where(count(key >= thr) > kf, jnp.where(thr > KEY_NEG_INF, 1.0, 0.0), 0.0)
    y = lax.cond(jnp.max(tied) > 0.0,
                 lambda: lax.fori_loop(0, nbits_col, col_step, jnp.zeros_like(thr)),
                 lambda: jnp.full_like(thr, INT_MAX))
    ninf = -jnp.inf
    bias = jnp.where(gt, 0.0, jnp.where(eq, jnp.where(col <= y, 0.0, ninf), ninf))
    return jnp.where(jnp.abs(isc) < jnp.inf, bias, ninf)


DSA_CAUSAL_BANDS = 8


def _dsa_prompt_body(qkv_q_ref, qkv_k_ref, qkv_v_ref, qi_ref, ki2_ref, kw_ref, o_ref, kb_scr, vb_scr, *,
                     topk, q_first):
    qb = qkv_q_ref.shape[1]
    t = qkv_k_ref.shape[1]
    j = pl.program_id(1)
    t0 = (j + q_first) * qb

    @pl.when(j == 0)
    def _():
        kb_scr[...] = qkv_k_ref[0].astype(BF16)
        vb_scr[...] = qkv_v_ref[0].astype(BF16)

    lane = lax.broadcasted_iota(I32, (1, LANES), 1)
    half = (jnp.where(lane < DH_A, 1.0, 0.0), jnp.where(lane < DH_A, 0.0, 1.0))
    half_b = tuple(m.astype(BF16) for m in half)
    ki2 = ki2_ref[0]
    kw = kw_ref[0]
    isc = jnp.zeros((qb, t), F32)
    for hp in range(H_IDX // 2):
        qp = qi_ref[0, :, hp * LANES:(hp + 1) * LANES]
        for h2 in range(2):
            h = 2 * hp + h2
            s = lax.dot_general(qp * half_b[h2], ki2, NT, preferred_element_type=F32)
            isc = isc + kw[:, D_IDX + h:D_IDX + h + 1] * jnp.maximum(s, 0.0)
    row = lax.broadcasted_iota(I32, (qb, t), 0) + t0
    col = lax.broadcasted_iota(I32, (qb, t), 1)
    isc = jnp.where(col <= row, isc, -jnp.inf)
    bias = _topk_bias(isc, col, topk, max(1, (t - 1).bit_length()))
    scale = DH_A ** -0.5
    assert math.frexp(scale)[0] == 0.5
    for p in range(H_A // 2):
        sl = slice(p * LANES, (p + 1) * LANES)
        qp = qkv_q_ref[0, :, sl]
        kp = kb_scr[:, sl]
        vp = vb_scr[:, sl]
        outs = []
        for h2 in range(2):
            qm = (qp * (half[h2] * scale)).astype(BF16)
            lg = lax.dot_general(qm, kp, NT, preferred_element_type=F32) + bias
            mx = jnp.max(lg, axis=-1, keepdims=True)
            pr = jnp.exp(lg - mx)
            l = jnp.sum(pr, axis=-1, keepdims=True)
            outs.append(jnp.dot(pr.astype(BF16), vp, preferred_element_type=F32) / l)
        o_ref[0, :, sl] = jnp.where(lane < DH_A, outs[0], outs[1])


def dsa_prompt_call(q, k, v, qi, ki2, kw):
    b, t, hd = q.shape
    qb = _row_tile(t, 256)
    topk = min(TOPK_MAX, t // 4)
    n_qb = t // qb
    per_band = n_qb // math.gcd(n_qb, DSA_CAUSAL_BANDS)
    outs = []
    for q_first in range(0, n_qb, per_band):
        tk = (q_first + per_band) * qb
        qmap = lambda i, j, q_first=q_first: (i, j + q_first, 0)
        outs.append(pl.pallas_call(
            functools.partial(_dsa_prompt_body, topk=topk, q_first=q_first),
            grid=(b, per_band),
            in_specs=[
                pl.BlockSpec((1, qb, hd), qmap),
                pl.BlockSpec((1, tk, hd), lambda i, j: (i, 0, 0)),
                pl.BlockSpec((1, tk, hd), lambda i, j: (i, 0, 0)),
                pl.BlockSpec((1, qb, H_IDX * D_IDX), qmap),
                pl.BlockSpec((1, tk, LANES), lambda i, j: (i, 0, 0)),
                pl.BlockSpec((1, qb, LANES), qmap),
            ],
            out_specs=pl.BlockSpec((1, qb, hd), lambda i, j: (i, j, 0)),
            out_shape=jax.ShapeDtypeStruct((b, per_band * qb, hd), F32),
            scratch_shapes=[pltpu.VMEM((tk, hd), BF16), pltpu.VMEM((tk, hd), BF16)],
            compiler_params=_cparams(("parallel", "arbitrary")),
            name="dsa_prompt",
        )(q, k, v, qi, ki2, kw))
    return outs[0] if len(outs) == 1 else jnp.concatenate(outs, axis=1)


DSA_PAGE_GROUP = 16
DSA_IDX_PAGE_GROUP = 64


def _page_group(n_pages, want=DSA_PAGE_GROUP):
    return math.gcd(n_pages, want)


def _page_specs(shape, n_pages, group):
    zeros = (0,) * (len(shape) - 1)

    def spec(g):
        return pl.BlockSpec(shape, lambda b, s, pt: (pt[b, jnp.minimum(s * group + g, n_pages - 1)],) + zeros)
    return [spec(g) for g in range(group)]


def _page_cat(refs):
    return jnp.concatenate([r[0].reshape(-1, PAGE_SIZE).astype(BF16) for r in refs], axis=1)


def _dsa_sidx_body(pt_ref, qi_ref, wi_ref, kin_ref, *rest):
    kc_refs, o_ref, onew_ref = rest[:-2], rest[-2], rest[-1]
    nq = o_ref.shape[1]
    r = lax.broadcasted_iota(I32, (nq, nq * H_IDX), 0)
    c = lax.broadcasted_iota(I32, (nq, nq * H_IDX), 1)
    lo = r * H_IDX
    wsel = jnp.where(c >= lo, jnp.where(c < lo + H_IDX, wi_ref[0], 0.0), 0.0)

    def scores(ki):
        s = jnp.dot(qi_ref[0], ki, preferred_element_type=F32)
        return jnp.dot(wsel, jnp.maximum(s, 0.0), precision=HI, preferred_element_type=F32)

    @pl.when(pl.program_id(1) == 0)
    def _():
        onew_ref[0] = scores(kin_ref[0].astype(BF16))

    o_ref[0] = scores(_page_cat(kc_refs))


def dsa_sidx_call(page_table, qi, wi, ki_new, cache_kidx):
    db, n_pages = page_table.shape
    nq = qi.shape[1] // H_IDX
    group = _page_group(n_pages, DSA_IDX_PAGE_GROUP)
    grid_spec = pltpu.PrefetchScalarGridSpec(
        num_scalar_prefetch=1,
        grid=(db, n_pages // group),
        in_specs=[
            pl.BlockSpec((1, nq * H_IDX, D_IDX), lambda b, s, pt: (b, 0, 0)),
            pl.BlockSpec((1, 1, nq * H_IDX), lambda b, s, pt: (b, 0, 0)),
            pl.BlockSpec((1, D_IDX, PAGE_SIZE), lambda b, s, pt: (b, 0, 0)),
        ] + _page_specs((1, D_IDX, PAGE_SIZE), n_pages, group),
        out_specs=[pl.BlockSpec((1, nq, group * PAGE_SIZE), lambda b, s, pt: (b, 0, s)),
                   pl.BlockSpec((1, nq, PAGE_SIZE), lambda b, s, pt: (b, 0, 0))],
    )
    return pl.pallas_call(
        _dsa_sidx_body,
        grid_spec=grid_spec,
        out_shape=[jax.ShapeDtypeStruct((db, nq, n_pages * PAGE_SIZE), F32),
                   jax.ShapeDtypeStruct((db, nq, PAGE_SIZE), F32)],
        compiler_params=_cparams(("parallel", "arbitrary")),
        name="dsa_sample_idx",
    )(page_table, qi, wi, ki_new, *([cache_kidx] * group))


def _dsa_satt_body(pt_ref, isc_ref, iscn_ref, q_ref, kn_ref, vn_ref, *rest, n_pages, n_steps, topk, n_new):
    group = n_pages // n_steps
    kc_refs, vc_refs = rest[:group], rest[group:2 * group]
    o_ref, bias_scr, qe_scr, m_scr, l_scr, acc_scr = rest[2 * group:]
    p = pl.program_id(1)
    nq = q_ref.shape[1]
    past = n_pages * PAGE_SIZE
    wcols = group * PAGE_SIZE
    ltot = past + PAGE_SIZE
    hd = H_A * DH_A
    scale = DH_A ** -0.5
    assert math.frexp(scale)[0] == 0.5

    def attend(kcat, vcat, bias):
        lg = jnp.dot(qe_scr[...], kcat, preferred_element_type=F32)
        lg = lg + jnp.concatenate([bias] * H_A, axis=0)
        m_old = m_scr[...]
        m_new = jnp.maximum(m_old, jnp.max(lg, axis=-1, keepdims=True))
        alpha = jnp.exp(m_old - m_new)
        pr = jnp.exp(lg - m_new)
        l_scr[...] = alpha * l_scr[...] + jnp.sum(pr, axis=-1, keepdims=True)
        acc_scr[...] = alpha * acc_scr[...] + lax.dot_general(pr.astype(BF16), vcat, NT, preferred_element_type=F32)
        m_scr[...] = m_new

    @pl.when(p == 0)
    def _():
        isc = jnp.concatenate([isc_ref[0], iscn_ref[0]], axis=1)
        row = lax.broadcasted_iota(I32, (nq, ltot), 0)
        col = lax.broadcasted_iota(I32, (nq, ltot), 1)
        rel = col - past
        ninf = -jnp.inf
        isc = jnp.where(rel < 0, isc, jnp.where(rel < n_new, jnp.where(rel <= row, isc, ninf), ninf))
        bias = _topk_bias(isc, col, topk, max(1, (ltot - 1).bit_length()))
        for i in range(n_steps):
            bias_scr[i] = bias[:, i * wcols:(i + 1) * wcols]
        q = q_ref[0] * scale
        lane = lax.broadcasted_iota(I32, (nq, hd), 1)
        qe_scr[...] = jnp.concatenate(
            [jnp.where(lane >= h * DH_A, jnp.where(lane < (h + 1) * DH_A, q, 0.0), 0.0) for h in range(H_A)],
            axis=0).astype(BF16)
        m_scr[...] = jnp.full_like(m_scr, NEG_BIG)
        l_scr[...] = jnp.zeros_like(l_scr)
        acc_scr[...] = jnp.zeros_like(acc_scr)
        attend(kn_ref[0].reshape(hd, PAGE_SIZE).astype(BF16), vn_ref[0].reshape(hd, PAGE_SIZE).astype(BF16),
               bias[:, past:])

    attend(_page_cat(kc_refs), _page_cat(vc_refs), bias_scr[p])

    @pl.when(p == n_steps - 1)
    def _():
        o = acc_scr[...] / l_scr[...]
        lane = lax.broadcasted_iota(I32, (nq, hd), 1)
        out = jnp.zeros((nq, hd), F32)
        for h in range(H_A):
            oh = o[h * nq:(h + 1) * nq]
            out = out + jnp.where(lane >= h * DH_A, jnp.where(lane < (h + 1) * DH_A, oh, 0.0), 0.0)
        o_ref[0] = out


def dsa_satt_call(page_table, isc, isc_new, q, k_new, v_new, cache_k, cache_v, n_new):
    db, n_pages = page_table.shape
    nq = q.shape[1]
    hd = H_A * DH_A
    page = (1, H_A, DH_A, PAGE_SIZE)
    group = _page_group(n_pages)
    n_steps = n_pages // group
    wcols = group * PAGE_SIZE
    topk = min(TOPK_MAX, (n_pages * PAGE_SIZE + n_new) // 4)
    grid_spec = pltpu.PrefetchScalarGridSpec(
        num_scalar_prefetch=1,
        grid=(db, n_steps),
        in_specs=[
            pl.BlockSpec((1, nq, n_pages * PAGE_SIZE), lambda b, p, pt: (b, 0, 0)),
            pl.BlockSpec((1, nq, PAGE_SIZE), lambda b, p, pt: (b, 0, 0)),
            pl.BlockSpec((1, nq, hd), lambda b, p, pt: (b, 0, 0)),
            pl.BlockSpec(page, lambda b, p, pt: (b, 0, 0, 0)),
            pl.BlockSpec(page, lambda b, p, pt: (b, 0, 0, 0)),
        ] + _page_specs(page, n_pages, group) * 2,
        out_specs=pl.BlockSpec((1, nq, hd), lambda b, p, pt: (b, 0, 0)),
        scratch_shapes=[
            pltpu.VMEM((n_steps, nq, wcols), F32),
            pltpu.VMEM((H_A * nq, hd), BF16),
            pltpu.VMEM((H_A * nq, 1), F32),
            pltpu.VMEM((H_A * nq, 1), F32),
            pltpu.VMEM((H_A * nq, hd), F32),
        ],
    )
    return pl.pallas_call(
        functools.partial(_dsa_satt_body, n_pages=n_pages, n_steps=n_steps, topk=topk, n_new=n_new),
        grid_spec=grid_spec,
        out_shape=jax.ShapeDtypeStruct((db, nq, hd), F32),
        compiler_params=_cparams(("parallel", "arbitrary")),
        name="dsa_sample_att",
    )(page_table, isc, isc_new, q, k_new, v_new, *([cache_k] * group), *([cache_v] * group))


def _head_sum_matrix(n, group):
    r = lax.broadcasted_iota(I32, (n, n), 0) // group
    c = lax.broadcasted_iota(I32, (n, n), 1) // group
    return jnp.where(r == c, 1.0, 0.0).astype(F32)


def _rwkv_prep_math(zb, before, mu, w0, a0, w2p, a2p, g2, kkp, ka):
    hd = H_B * DH_B
    rows = lax.broadcasted_iota(I32, zb.shape, 0)
    prev = jnp.where(rows == 0, before, pltpu.roll(zb, 1, 0))
    z = zb + (prev - zb) * mu
    r = z[:, 0:hd]
    k = z[:, hd:2 * hd]
    v = z[:, 2 * hd:3 * hd]
    xwa = z[:, 3 * hd:3 * hd + R_DECAY + R_AAA]
    xg = z[:, 3 * hd + R_DECAY + R_AAA:]
    wl = w0 + jnp.dot(jnp.tanh(xwa).astype(BF16), w2p, preferred_element_type=F32)
    w_log = -_softplus(-wl) - 0.5
    log_decay = -jnp.exp(w_log)
    a = jax.nn.sigmoid(a0 + jnp.dot(xwa.astype(BF16), a2p, preferred_element_type=F32))
    g = jnp.dot(jax.nn.sigmoid(xg).astype(BF16), g2, preferred_element_type=F32)
    kk = k * kkp
    ss = _lane_group_dot(kk * kk, _head_sum_matrix(LANES, DH_B))
    kk = kk * lax.rsqrt(jnp.maximum(ss, 1e-24))
    return r, log_decay, k * (1.0 + (a - 1.0) * ka), v, -kk, kk * a, g


def _rwkv_post_math(y, r, k, v, g, lnx_g, lnx_b, rk):
    ones = _head_sum_matrix(LANES, DH_B)
    avg = ones * (1.0 / DH_B)
    mu = _lane_group_dot(y, avg)
    d = y - mu
    var = _lane_group_dot(d * d, avg)
    yn = d * lax.rsqrt(var + LNX_EPS) * lnx_g + lnx_b
    bonus = _lane_group_dot(r * k * rk, ones)
    return (yn + bonus * v) * g


def _rwkv_prep_body(zb_ref, sp_ref, mu_ref, w0_ref, a0_ref, w2_ref, a2_ref, g2_ref, kkp_ref, ka_ref,
                    r_ref, w_ref, k_ref, v_ref, an_ref, b_ref, g_ref, carry_scr):
    tc = pl.program_id(1)
    zb = zb_ref[0]
    tt = zb.shape[0]
    hd = H_B * DH_B

    @pl.when(tc == 0)
    def _():
        carry_scr[...] = sp_ref[0]

    r, log_decay, k2, v, an, bb, g = _rwkv_prep_math(
        zb, carry_scr[...], mu_ref[...], w0_ref[...], a0_ref[...], w2_ref[...], a2_ref[...], g2_ref[...],
        kkp_ref[...], ka_ref[...])
    carry_scr[...] = zb[tt - 1:tt]
    r_ref[0] = r
    w_ref[0] = log_decay
    k_ref[0] = k2
    v_ref[0] = v
    an_ref[0] = an
    b_ref[0] = bb
    g_ref[0] = g


def rwkv_prep_call(zb, shift_prev, mu, w0, a0, w2p, a2p, g2, kkp, ka):
    b, t, dz = zb.shape
    hd = H_B * DH_B
    tt = _row_tile(t, 256)
    row = lambda n: pl.BlockSpec((1, n), lambda i, j: (0, 0))
    full = lambda s: pl.BlockSpec(s, lambda i, j: (0, 0))
    out = pl.BlockSpec((1, tt, hd), lambda i, j: (i, j, 0))
    return pl.pallas_call(
        _rwkv_prep_body,
        grid=(b, t // tt),
        in_specs=[
            pl.BlockSpec((1, tt, dz), lambda i, j: (i, j, 0)),
            pl.BlockSpec((1, 1, dz), lambda i, j: (i, 0, 0)),
            row(dz), row(hd), row(hd),
            full((R_DECAY + R_AAA, hd)), full((R_DECAY + R_AAA, hd)), full((R_GATE, hd)),
            row(hd), row(hd),
        ],
        out_specs=[out] * 7,
        out_shape=[jax.ShapeDtypeStruct((b, t, hd), F32)] * 7,
        scratch_shapes=[pltpu.VMEM((1, dz), F32)],
        compiler_params=_cparams(("parallel", "arbitrary")),
        name="rwkv_prep",
    )(zb, shift_prev, mu, w0, a0, w2p, a2p, g2, kkp, ka)


RWKV_PAIRS = H_B // 2
RWKV_STEPS = SUBLANES // RWKV_PAIRS
RWKV_NB = 8
RWKV_SUM_PIECES = 2


def _rwkv_scan_body(r_ref, w_ref, k_ref, v_ref, a_ref, b_ref, s0_ref, y_ref, sout_ref,
                    s_scr, pa_scr, pv_scr, *, t_valid):
    c = pl.program_id(1)
    nb, tc = r_ref.shape[0], r_ref.shape[1]
    chains = [(bi, p) for bi in range(nb) for p in range(RWKV_PAIRS)]
    rows = lambda n: slice(n * DH_B, (n + 1) * DH_B)

    @pl.when(c == 0)
    def _():
        for n, (bi, p) in enumerate(chains):
            s_scr[rows(n), :] = s0_ref[bi, p]

    if t_valid < tc * RWKV_STEPS:
        y_ref[...] = jnp.zeros_like(y_ref)

    sub = lax.broadcasted_iota(I32, (DH_B, LANES), 0)
    lane = lax.broadcasted_iota(I32, (DH_B, LANES), 1)
    isel = jnp.where((lane & (DH_B - 1)) == sub, 1.0, 0.0).astype(F32)
    qblk = _head_sum_matrix(LANES, DH_B).astype(BF16)

    def hsum(ref, pieces=3):
        x = ref[...]
        parts = []
        for _ in range(pieces - 1):
            h = x.astype(BF16)
            parts.append(h)
            x = x - h.astype(F32)
        parts.append(x.astype(BF16))
        return jnp.dot(jnp.concatenate(parts, axis=1), jnp.concatenate([qblk] * pieces, axis=0),
                       preferred_element_type=F32)

    def step(u, carry):
        tiles = [(a_ref[bi, u], jnp.exp(w_ref[bi, u]), k_ref[bi, u], v_ref[bi, u], b_ref[bi, u], r_ref[bi, u])
                 for bi in range(nb)]
        for i in range(RWKV_STEPS):
            for n, (bi, p) in enumerate(chains):
                at, _, _, vt, _, _ = tiles[bi]
                row = slice(i * RWKV_PAIRS + p, i * RWKV_PAIRS + p + 1)
                pa_scr[rows(n), :] = s_scr[rows(n), :] * at[row]
                pv_scr[rows(n), :] = isel * vt[row]
            sa = hsum(pa_scr, RWKV_SUM_PIECES)
            vc = hsum(pv_scr)
            for n, (bi, p) in enumerate(chains):
                _, wt, kt, _, bt, rt = tiles[bi]
                row = slice(i * RWKV_PAIRS + p, i * RWKV_PAIRS + p + 1)
                s = s_scr[rows(n), :] * wt[row] + sa[rows(n)] * bt[row] + vc[rows(n)] * kt[row]
                s_scr[rows(n), :] = s
                pa_scr[rows(n), :] = s * rt[row]
            yb = hsum(pa_scr, RWKV_SUM_PIECES)
            for n, (bi, p) in enumerate(chains):
                row = slice(i * RWKV_PAIRS + p, i * RWKV_PAIRS + p + 1)
                y_ref[bi, u, row, :] = jnp.sum(yb[rows(n)] * isel, axis=0, keepdims=True)
        return carry

    lax.fori_loop(0, min(t_valid, tc * RWKV_STEPS) // RWKV_STEPS, step, 0)

    @pl.when(c == pl.num_programs(1) - 1)
    def _():
        for n, (bi, p) in enumerate(chains):
            sout_ref[bi, p] = s_scr[rows(n), :]


def rwkv_scan_call(r, w, k, v, a, b, s0, t_valid):
    bsz, t, hd = r.shape
    nb = math.gcd(bsz, RWKV_NB)
    nt = t // RWKV_STEPS
    tc = _row_tile(nt, 32)
    assert t % RWKV_STEPS == 0 and t_valid % RWKV_STEPS == 0 and (t_valid == t or nt == tc)
    tiled = lambda x: x.reshape(bsz, nt, SUBLANES, LANES)
    seq = pl.BlockSpec((nb, tc, SUBLANES, LANES), lambda i, j: (i, j, 0, 0))
    st = pl.BlockSpec((nb, RWKV_PAIRS, DH_B, LANES), lambda i, j: (i, 0, 0, 0))
    y, s_fin = pl.pallas_call(
        functools.partial(_rwkv_scan_body, t_valid=t_valid),
        grid=(bsz // nb, nt // tc),
        in_specs=[seq] * 6 + [st],
        out_specs=[seq, st],
        out_shape=[jax.ShapeDtypeStruct((bsz, nt, SUBLANES, LANES), F32),
                   jax.ShapeDtypeStruct((bsz, RWKV_PAIRS, DH_B, LANES), F32)],
        scratch_shapes=[pltpu.VMEM((nb * RWKV_PAIRS * DH_B, LANES), F32)] * 3,
        compiler_params=_cparams(("parallel", "arbitrary")),
        name="rwkv_scan",
    )(tiled(r), tiled(w), tiled(k), tiled(v), tiled(a), tiled(b), s0)
    return y.reshape(bsz, t, hd), s_fin


RWKV_CHUNK = 64
RWKV_HG = 4
RWKV_CHUNK_SEQS = 2


def _rwkv_chunk_body(zb_ref, sp_ref, mu_ref, w0_ref, a0_ref, w2_ref, a2_ref, g2_ref, kkp_ref, ka_ref,
                     lg_ref, lb_ref, rk_ref, s0_ref, yb_ref, sout_ref, s_scr, carry_scr):
    c = pl.program_id(1)
    nb, cs = zb_ref.shape[0], zb_ref.shape[1]
    ng = s_scr.shape[1]
    gw = RWKV_HG * DH_B
    n = RWKV_HG * cs
    assert cs & (cs - 1) == 0

    @pl.when(c == 0)
    def _():
        s_scr[...] = s0_ref[...]
        carry_scr[...] = sp_ref[...]

    seqs = []
    for bi in range(nb):
        zb = zb_ref[bi]
        seqs.append(_rwkv_prep_math(zb, carry_scr[bi], mu_ref[...], w0_ref[...], a0_ref[...], w2_ref[...],
                                    a2_ref[...], g2_ref[...], kkp_ref[...], ka_ref[...]))
        carry_scr[bi] = zb[cs - 1:cs]

    tr = lax.broadcasted_iota(I32, (cs, cs), 0)
    tcol = lax.broadcasted_iota(I32, (cs, cs), 1)
    tril = jnp.where(tr >= tcol, 1.0, 0.0).astype(F32)
    row = lax.broadcasted_iota(I32, (n, n), 0)
    col = lax.broadcasted_iota(I32, (n, n), 1)
    same = (row // cs) == (col // cs)
    tt = row & (cs - 1)
    ss = col & (cs - 1)
    strict = jnp.where(same, jnp.where(tt > ss, 1.0, 0.0), 0.0).astype(F32)
    incl = jnp.where(same, jnp.where(tt >= ss, 1.0, 0.0), 0.0).astype(F32)
    eye = jnp.where(row == col, 1.0, 0.0).astype(F32)
    lane_h = lax.broadcasted_iota(I32, (1, gw), 1) // DH_B
    hmask = [jnp.where(lane_h == h, 1.0, 0.0).astype(F32) for h in range(RWKV_HG)]
    sblk = jnp.where(lax.broadcasted_iota(I32, (gw, gw), 0) // DH_B == lax.broadcasted_iota(I32, (gw, gw), 1) // DH_B,
                     1.0, 0.0).astype(F32)
    nn = (((1,), (0,)), ((), ()))

    def stack(x):
        return jnp.concatenate([x * m for m in hmask], axis=0)

    def fold(x):
        out = x[0:cs]
        for h in range(1, RWKV_HG):
            out = out + x[h * cs:(h + 1) * cs]
        return out

    def bdot(x, y, dims=nn):
        return lax.dot_general(x.astype(BF16), y.astype(BF16), dims, preferred_element_type=F32)

    jobs = []
    for bi in range(nb):
        r_all, lw_all, k_all, v_all, a_all, b_all, _ = seqs[bi]
        for g in range(ng):
            sl = slice(g * gw, (g + 1) * gw)
            lw = lw_all[:, sl]
            cl = jnp.dot(tril, lw, precision=HI, preferred_element_type=F32)
            e_in = jnp.exp(cl)
            e_inv = jnp.exp(-cl)
            e_tail = jnp.exp(cl[cs - 1:cs] - cl)
            jobs.append(dict(
                bi=bi, g=g, p_last=e_in[cs - 1:cs],
                at=a_all[:, sl] * jnp.exp(cl - lw),
                bt=b_all[:, sl] * e_inv, kt=k_all[:, sl] * e_inv, rt=r_all[:, sl] * e_in,
                bw=b_all[:, sl] * e_tail, kw=k_all[:, sl] * e_tail, v=v_all[:, sl],
                s_old=s_scr[bi, g]))
    for j in jobs:
        j['a_s'], j['r_s'], j['b_s'], j['k_s'], j['v_s'] = (stack(j[x]) for x in ('at', 'rt', 'bt', 'kt', 'v'))
    for j in jobs:
        j['lab'] = bdot(j['a_s'], j['b_s'], NT) * strict
        j['lak'] = bdot(j['a_s'], j['k_s'], NT) * strict
        j['lrb'] = bdot(j['r_s'], j['b_s'], NT) * incl
        j['lrk'] = bdot(j['r_s'], j['k_s'], NT) * incl
    for j in jobs:
        j['rhs'] = stack(bdot(j['at'], j['s_old'], NT)) + bdot(j['lak'], j['v_s'])
        j['tm'] = eye + j['lab']
        j['lp'] = j['lab']
    for _ in range(cs.bit_length() - 2):
        for j in jobs:
            j['lp'] = bdot(j['lp'], j['lp'])
        for j in jobs:
            j['tm'] = j['tm'] + bdot(j['lp'], j['tm'])
    for j in jobs:
        j['u_s'] = bdot(j['tm'], j['rhs'])
    for j in jobs:
        j['y'] = fold(stack(bdot(j['rt'], j['s_old'], NT)) + bdot(j['lrb'], j['u_s']) + bdot(j['lrk'], j['v_s']))
        u = fold(j['u_s'])
        j['s_new'] = j['s_old'] * j['p_last'] + (bdot(u, j['bw'], TN) + bdot(j['v'], j['kw'], TN)) * sblk
    for bi in range(nb):
        r_all, _, k_all, v_all, _, _, g_all = seqs[bi]
        y = jnp.concatenate([j['y'] for j in jobs if j['bi'] == bi], axis=1)
        yb_ref[bi] = _rwkv_post_math(y, r_all, k_all, v_all, g_all, lg_ref[...], lb_ref[...], rk_ref[...])
    for j in jobs:
        s_scr[j['bi'], j['g']] = j['s_new']

    @pl.when(c == pl.num_programs(1) - 1)
    def _():
        sout_ref[...] = s_scr[...]


def rwkv_chunk_call(zb, shift_prev, mu, w0, a0, w2p, a2p, g2, kkp, ka, lnx_g, lnx_b, rk, s0):
    bsz, t, dz = zb.shape
    hd = H_B * DH_B
    cs = RWKV_CHUNK
    ng = H_B // RWKV_HG
    gw = RWKV_HG * DH_B
    eye = jnp.eye(RWKV_HG, dtype=F32)
    s0g = jnp.einsum('bghij,hk->bghikj', s0.astype(F32).reshape(bsz, ng, RWKV_HG, DH_B, DH_B), eye)
    nb = math.gcd(bsz, RWKV_CHUNK_SEQS)
    st = pl.BlockSpec((nb, ng, gw, gw), lambda i, j: (i, 0, 0, 0))
    row = lambda n: pl.BlockSpec((1, n), lambda i, j: (0, 0))
    full = lambda s: pl.BlockSpec(s, lambda i, j: (0, 0))
    y, s_fin = pl.pallas_call(
        _rwkv_chunk_body,
        grid=(bsz // nb, t // cs),
        in_specs=[
            pl.BlockSpec((nb, cs, dz), lambda i, j: (i, j, 0)),
            pl.BlockSpec((nb, 1, dz), lambda i, j: (i, 0, 0)),
            row(dz), row(hd), row(hd),
            full((R_DECAY + R_AAA, hd)), full((R_DECAY + R_AAA, hd)), full((R_GATE, hd)),
            row(hd), row(hd), row(hd), row(hd), row(hd), st],
        out_specs=[pl.BlockSpec((nb, cs, hd), lambda i, j: (i, j, 0)), st],
        out_shape=[jax.ShapeDtypeStruct((bsz, t, hd), F32), jax.ShapeDtypeStruct((bsz, ng, gw, gw), F32)],
        scratch_shapes=[pltpu.VMEM((nb, ng, gw, gw), F32), pltpu.VMEM((nb, 1, dz), F32)],
        compiler_params=_cparams(("parallel", "arbitrary")),
        name="rwkv_chunk",
    )(zb, shift_prev, mu, w0, a0, w2p, a2p, g2, kkp, ka, lnx_g, lnx_b, rk, s0g.reshape(bsz, ng, gw, gw))
    s_fin = jnp.einsum('bghihj->bghij', s_fin.reshape(bsz, ng, RWKV_HG, DH_B, RWKV_HG, DH_B))
    return y, s_fin.reshape(bsz, H_B, DH_B, DH_B)


def _rwkv_post_body(y_ref, r_ref, k_ref, v_ref, g_ref, lg_ref, lb_ref, rk_ref, o_ref):
    o_ref[...] = _rwkv_post_math(y_ref[...], r_ref[...], k_ref[...], v_ref[...], g_ref[...],
                                 lg_ref[...], lb_ref[...], rk_ref[...])


def rwkv_post_call(y, r, k, v, g, lnx_g, lnx_b, rk):
    m, hd = y.shape
    tm = _row_tile(m, 512)
    tok = pl.BlockSpec((tm, hd), lambda i: (i, 0))
    row = pl.BlockSpec((1, hd), lambda i: (0, 0))
    return pl.pallas_call(
        _rwkv_post_body,
        grid=(m // tm,),
        in_specs=[tok] * 5 + [row] * 3,
        out_specs=tok,
        out_shape=jax.ShapeDtypeStruct((m, hd), F32),
        compiler_params=_cparams(("parallel",)),
        name="rwkv_post",
    )(y, r, k, v, g, lnx_g, lnx_b, rk)


def _head_mask(n, h, width):
    lane = lax.broadcasted_iota(I32, (1, n), 1)
    return jnp.where(lane >= h * width, jnp.where(lane < (h + 1) * width, 1.0, 0.0), 0.0).astype(F32)


def _head_norm128(y, g):
    mu = jnp.mean(y, axis=-1, keepdims=True)
    d = y - mu
    var = jnp.mean(d * d, axis=-1, keepdims=True)
    return d * lax.rsqrt(var + HN_EPS) * g


def _ret_body(qk_ref, v_ref, gc_ref, cos_ref, sin_ref, gn_ref, s0_ref, y_ref, sout_ref, s_scr, *, l_valid):
    c = pl.program_id(1)
    lc = qk_ref.shape[1]
    hk = H_C * DK_C

    @pl.when(c == 0)
    def _():
        s_scr[...] = s0_ref[0]

    rr = lax.broadcasted_iota(I32, (LANES, LANES), 0)
    cc = lax.broadcasted_iota(I32, (LANES, LANES), 1)
    half = DK_C // 2
    same = (rr // DK_C) == (cc // DK_C)
    dr = rr & (DK_C - 1)
    dc = cc & (DK_C - 1)
    rot = jnp.where(same, jnp.where(dr == dc + half, -1.0, jnp.where(dr + half == dc, 1.0, 0.0)), 0.0).astype(F32)

    qk = qk_ref[0]
    cos = cos_ref[...]
    sin = sin_ref[...]

    def rope(x):
        return x * cos + _lane_group_dot(x, rot) * sin

    qr = rope(qk[:, :hk])
    kr = rope(qk[:, hk:]) * DK_C ** -0.5
    krb = kr.astype(BF16)
    s_prev = s_scr[...]
    s_prev_b = s_prev.astype(BF16)

    jj = lax.broadcasted_iota(I32, (lc, lc), 0)
    ss = lax.broadcasted_iota(I32, (lc, lc), 1)
    diff = (jj - ss).astype(F32)
    jcol = lax.broadcasted_iota(I32, (lc, 1), 0).astype(F32)
    srow_state = lax.broadcasted_iota(I32, (hk, 1), 0) // DK_C
    s_new = jnp.zeros_like(s_prev)
    decay_rows = jnp.zeros((hk, 1), F32)
    hs = []
    for h in range(H_C):
        lg = math.log1p(-2.0 ** (-5.0 - h))
        mh = _head_mask(hk, h, DK_C)
        qm = (qr * mh).astype(BF16)
        hs.append(dict(h=h, lg=lg, mh=mh, qm=qm, vhb=v_ref[0, :, h * DV_C:(h + 1) * DV_C].astype(BF16),
                       qk=lax.dot_general(qm, krb, NT, preferred_element_type=F32),
                       qs=jnp.dot(qm, s_prev_b, preferred_element_type=F32)))
    for d in hs:
        dmask = jnp.where(diff >= 0, jnp.exp(jnp.maximum(diff, 0.0) * d['lg']), 0.0)
        d['scores'] = (d['qk'] * dmask).astype(BF16)
    for d in hs:
        h = d['h']
        intra = jnp.dot(d['scores'], d['vhb'], preferred_element_type=F32)
        inter = d['qs'] * jnp.exp((jcol + 1.0) * d['lg'])
        yh = _head_norm128(intra + inter, gn_ref[h:h + 1, :])
        gch = gc_ref[0, :, h * DV_C:(h + 1) * DV_C]
        d['out'] = yh * (gch * jax.nn.sigmoid(gch))
    for d in hs:
        w_s = jnp.where(jcol < l_valid, jnp.exp((l_valid - 1.0 - jcol) * d['lg']), 0.0)
        kw = (kr * d['mh'] * w_s).astype(BF16)
        s_new = s_new + lax.dot_general(kw, d['vhb'], TN, preferred_element_type=F32)
        decay_rows = jnp.where(srow_state == d['h'], math.exp(l_valid * d['lg']), decay_rows)
    for d in hs:
        y_ref[0, :, d['h'] * DV_C:(d['h'] + 1) * DV_C] = d['out']
    s_fin = decay_rows * s_prev + s_new
    s_scr[...] = s_fin

    @pl.when(c == pl.num_programs(1) - 1)
    def _():
        sout_ref[0] = s_fin


def retention_call(za, cos, sin, gn, s0, l_valid):
    b, t, _ = za.shape
    lc = _row_tile(t, 128)
    assert l_valid == lc or t == lc
    hk, hv = H_C * DK_C, H_C * DV_C
    blk = lambda j: pl.BlockSpec((1, lc, 2 * hk), lambda i, c, j=j: (i, c, j))
    st = pl.BlockSpec((1, hk, DV_C), lambda i, c: (i, 0, 0))
    return pl.pallas_call(
        functools.partial(_ret_body, l_valid=l_valid),
        grid=(b, t // lc),
        in_specs=[blk(0), blk(1), blk(2),
                  pl.BlockSpec((lc, hk), lambda i, c: (c, 0)),
                  pl.BlockSpec((lc, hk), lambda i, c: (c, 0)),
                  pl.BlockSpec((H_C, DV_C), lambda i, c: (0, 0)),
                  st],
        out_specs=[pl.BlockSpec((1, lc, hv), lambda i, c: (i, c, 0)), st],
        out_shape=[jax.ShapeDtypeStruct((b, t, hv), F32), jax.ShapeDtypeStruct((b, hk, DV_C), F32)],
        scratch_shapes=[pltpu.VMEM((hk, DV_C), F32)],
        compiler_params=_cparams(("parallel", "arbitrary")),
        name="retention",
    )(za, za, za, cos, sin, gn, s0)


def _mlstm_body(qk_ref, v_ref, og_ref, gt_ref, cb_ref, cw_ref, cbias_ref, gbias_ref, gn_ref,
                c0_ref, n0_ref, m0_ref, y_ref, cout_ref, nout_ref, mout_ref,
                ext_scr, c_scr, n_scr, m_scr, *, l_valid):
    c = pl.program_id(1)
    lc = qk_ref.shape[1]
    hk = H_D * DK_D
    pad = SUBLANES

    @pl.when(c == 0)
    def _():
        ext_scr[0:pad, :] = cb_ref[0]
        c_scr[...] = c0_ref[0]
        n_scr[...] = n0_ref[0]
        m_scr[...] = m0_ref[0]

    u = qk_ref[0]
    ext_scr[pad:pad + lc, :] = u
    acc = cbias_ref[...] + u * cw_ref[CONV_W - 1:CONV_W, :]
    for i in range(CONV_W - 1):
        sh = CONV_W - 1 - i
        acc = acc + ext_scr[pad - sh:pad - sh + lc, :] * cw_ref[i:i + 1, :]
    ext_scr[0:pad, :] = u[lc - pad:lc, :]
    qkc = acc * jax.nn.sigmoid(acc)
    q = qkc[:, :hk]
    k = qkc[:, hk:] * DK_D ** -0.5
    kb = k.astype(BF16)

    gates = gt_ref[0] + gbias_ref[...]
    logsig = -_softplus(-gates)
    rowi = lax.broadcasted_iota(I32, (lc, LANES), 0)
    logsig = jnp.where(rowi < l_valid, logsig, 0.0)
    jj = lax.broadcasted_iota(I32, (lc, lc), 0)
    ss = lax.broadcasted_iota(I32, (lc, lc), 1)
    tril = jnp.where(jj >= ss, 1.0, 0.0).astype(F32)
    bcum = jnp.dot(tril, logsig, precision=HI, preferred_element_type=F32)
    lane_g = lax.broadcasted_iota(I32, (lc, LANES), 1)
    jcol = lax.broadcasted_iota(I32, (lc, 1), 0)
    causal = jj >= ss
    ninf = -jnp.inf

    c_prev = c_scr[...]
    c_prev_b = c_prev.astype(BF16)
    n_prev = n_scr[...]
    m_prev = m_scr[...]
    c_new = jnp.zeros_like(c_prev)
    carry_rows = jnp.zeros((hk, 1), F32)
    carry_lanes = jnp.zeros((1, hk), F32)
    ws_full = jnp.zeros((lc, hk), F32)
    m_out = m_prev
    srow_state = lax.broadcasted_iota(I32, (hk, 1), 0) // DK_D
    lane_state = lax.broadcasted_iota(I32, (1, hk), 1) // DK_D
    lane_m = lax.broadcasted_iota(I32, (1, LANES), 1)
    hs = []
    for h in range(H_D):
        mh = _head_mask(hk, h, DK_D)
        e_i = jnp.where(lane_g == h, 1.0, 0.0).astype(F32)
        e_f = jnp.where(lane_g == H_D + h, 1.0, 0.0).astype(F32)
        qm = q * mh
        vh = v_ref[0, :, h * DV_D:(h + 1) * DV_D]
        hs.append(dict(
            h=h, mh=mh, qm=qm, qmb=qm.astype(BF16), vh=vh, vhb=vh.astype(BF16),
            logi_col=jnp.where(jcol < l_valid, gates[:, h:h + 1], ninf),
            b_col=bcum[:, H_D + h:H_D + h + 1],
            i_row=lax.dot_general(e_i, gates, NT, precision=HI, preferred_element_type=F32),
            b_row=lax.dot_general(e_f, bcum, NT, precision=HI, preferred_element_type=F32),
            m_h=m_prev[:, h:h + 1]))
    for d in hs:
        d['qk'] = lax.dot_general(d['qmb'], kb, NT, preferred_element_type=F32)
        d['qc'] = jnp.dot(d['qmb'], c_prev_b, preferred_element_type=F32)
    for d in hs:
        i_row = jnp.where(ss < l_valid, d['i_row'], ninf)
        d['inter'] = d['b_col'] + d['m_h']
        dmat = jnp.where(causal, d['b_col'] - d['b_row'] + i_row, ninf)
        d['m_j'] = jnp.maximum(d['inter'], jnp.max(dmat, axis=-1, keepdims=True))
        d['amat'] = jnp.exp(dmat - d['m_j']) * d['qk']
        d['sc'] = jnp.exp(d['inter'] - d['m_j'])
    for d in hs:
        num = jnp.dot(d['amat'].astype(BF16), d['vhb'], preferred_element_type=F32) + d['sc'] * d['qc']
        den = jnp.sum(d['amat'], axis=-1, keepdims=True) \
            + d['sc'] * jnp.sum(d['qm'] * n_prev, axis=-1, keepdims=True)
        hh = num / jnp.maximum(jnp.abs(den), jnp.exp(-d['m_j']))
        h = d['h']
        ogh = og_ref[0, :, h * DV_D:(h + 1) * DV_D]
        d['out'] = _head_norm128(hh, gn_ref[h:h + 1, :]) * jax.nn.sigmoid(ogh)
    for d in hs:
        h = d['h']
        b_last = d['b_col'][l_valid - 1:l_valid, :]
        gs = b_last - d['b_col'] + d['logi_col']
        m_new = jnp.maximum(b_last + d['m_h'], jnp.max(gs, axis=0, keepdims=True))
        ws = jnp.exp(gs - m_new)
        carry = jnp.exp(b_last + d['m_h'] - m_new)
        c_new = c_new + lax.dot_general((k * d['mh']).astype(BF16), (d['vh'] * ws).astype(BF16), TN,
                                        preferred_element_type=F32)
        carry_rows = jnp.where(srow_state == h, carry, carry_rows)
        carry_lanes = jnp.where(lane_state == h, carry, carry_lanes)
        ws_full = ws_full + ws * d['mh']
        m_out = jnp.where(lane_m == h, m_new, m_out)
    for d in hs:
        y_ref[0, :, d['h'] * DV_D:(d['h'] + 1) * DV_D] = d['out']
    c_fin = carry_rows * c_prev + c_new
    n_fin = carry_lanes * n_prev + jnp.sum(ws_full * k, axis=0, keepdims=True)
    c_scr[...] = c_fin
    n_scr[...] = n_fin
    m_scr[...] = m_out

    @pl.when(c == pl.num_programs(1) - 1)
    def _():
        cout_ref[0] = c_fin
        nout_ref[0] = n_fin
        mout_ref[0] = m_out


def mlstm_call(zb, zc, conv_buf, conv_w, conv_b, gate_bias, gn, c0, n0, m0, l_valid):
    b, t, _ = zb.shape
    lc = _row_tile(t, 128)
    assert l_valid == lc or t == lc
    hk, hv = H_D * DK_D, H_D * DV_D
    blk = lambda j: pl.BlockSpec((1, lc, 2 * hk), lambda i, c, j=j: (i, c, j))
    cst = lambda s: pl.BlockSpec(s, lambda i, c: (0,) * len(s))
    per_b = lambda s: pl.BlockSpec((1,) + s, lambda i, c: (i,) + (0,) * len(s))
    return pl.pallas_call(
        functools.partial(_mlstm_body, l_valid=l_valid),
        grid=(b, t // lc),
        in_specs=[blk(0), blk(1), blk(2),
                  pl.BlockSpec((1, lc, LANES), lambda i, c: (i, c, 0)),
                  per_b((SUBLANES, 2 * hk)),
                  cst((CONV_W, 2 * hk)), cst((1, 2 * hk)), cst((1, LANES)), cst((H_D, DV_D)),
                  per_b((hk, DV_D)), per_b((1, hk)), per_b((1, LANES))],
        out_specs=[pl.BlockSpec((1, lc, hv), lambda i, c: (i, c, 0)),
                   per_b((hk, DV_D)), per_b((1, hk)), per_b((1, LANES))],
        out_shape=[jax.ShapeDtypeStruct((b, t, hv), F32),
                   jax.ShapeDtypeStruct((b, hk, DV_D), F32),
                   jax.ShapeDtypeStruct((b, 1, hk), F32),
                   jax.ShapeDtypeStruct((b, 1, LANES), F32)],
        scratch_shapes=[pltpu.VMEM((SUBLANES + lc, 2 * hk), F32),
                        pltpu.VMEM((hk, DV_D), F32),
                        pltpu.VMEM((1, hk), F32),
                        pltpu.VMEM((1, LANES), F32)],
        compiler_params=_cparams(("parallel", "arbitrary")),
        name="mlstm",
    )(zb, zb, zb, zc, conv_buf, conv_w, conv_b, gate_bias, gn, c0, n0, m0)


def _pad_cols(w, n):
    return jnp.pad(w, ((0, 0), (0, n - w.shape[1])))


def _pad_time(a, tp):
    t = a.shape[1]
    if t == tp:
        return a
    return jnp.pad(a, ((0, 0), (0, tp - t)) + ((0, 0),) * (a.ndim - 2))


def _heads_major(a, h):
    b, t, _ = a.shape
    return a.reshape(b, t, h, -1).transpose(0, 2, 1, 3)


def _ab_mixer(xf, b, t, e, g_pre, P, prompt, shift_prev, s0, cache):
    m = b * t
    hd = H_A * DH_A
    w = P['ab_w_in'][e]
    o_qi = 3 * hd
    o_ki = o_qi + H_IDX * D_IDX
    o_wi = o_ki + D_IDX
    o_zb = o_wi + H_IDX
    q3, k3, v3 = (a.reshape(b, t, hd) for a in proj_split_call(xf, g_pre, w[:, :o_qi].astype(BF16), 3))
    qi = proj_in_call(xf, g_pre, w[:, o_qi:o_ki].astype(BF16), out_dtype=BF16)
    kw, ki2 = kiwi_call(xf, g_pre, w[:, o_ki:o_wi], w[:, o_wi:o_zb], P['kidx_g'][e])
    zb = proj_in_call(xf, g_pre, w[:, o_zb:].astype(BF16))
    ki3 = kw[:, :D_IDX].reshape(b, t, D_IDX)
    wi3 = kw[:, D_IDX:D_IDX + H_IDX].reshape(b, t, H_IDX)

    if prompt:
        ya = dsa_prompt_call(q3, k3, v3, qi.reshape(b, t, H_IDX * D_IDX),
                             ki2.reshape(b, t, LANES), kw.reshape(b, t, LANES)).reshape(m, hd)
    else:
        cache_k, cache_v, cache_kidx, page_table = cache
        n_pool = cache_k.shape[1]
        nq = SUBLANES
        qi_s = _pad_time(qi.reshape(b, t, H_IDX * D_IDX), nq).reshape(b, nq * H_IDX, D_IDX)
        wi_s = _pad_time(wi3, nq).reshape(b, 1, nq * H_IDX)
        keys_last = lambda a: jnp.moveaxis(a, 1, -1)
        new_page = lambda a: jnp.pad(keys_last(a), ((0, 0),) * (a.ndim - 1) + ((0, PAGE_SIZE - t),))
        isc, isc_new = dsa_sidx_call(page_table, qi_s, wi_s, new_page(ki3), keys_last(cache_kidx[e]))
        ya = dsa_satt_call(page_table, isc, isc_new, _pad_time(q3, nq),
                           new_page(k3.reshape(b, t, H_A, DH_A)), new_page(v3.reshape(b, t, H_A, DH_A)),
                           keys_last(cache_k[e]), keys_last(cache_v[e]), t)
        ya = ya[:, :t].reshape(m, hd)

    tp = -(-t // SUBLANES) * SUBLANES
    hb = H_B * DH_B
    zb3 = zb.reshape(b, t, D_B_IN)
    zpad = jnp.zeros((R_DECAY, hb), F32)
    w2p = jnp.concatenate([P['rwkv_w2'][e], zpad], axis=0).astype(BF16)
    a2p = jnp.concatenate([zpad, P['rwkv_a2'][e]], axis=0).astype(BF16)
    row = lambda a: a.reshape(1, -1)
    prep_args = (shift_prev.reshape(b, 1, D_B_IN), row(P['rwkv_mu'][e]), row(P['rwkv_w0'][e]),
                 row(P['rwkv_a0'][e]), w2p, a2p, P['rwkv_g2'][e].astype(BF16), row(P['rwkv_kk'][e]),
                 row(P['rwkv_ka'][e]))
    rk = jnp.broadcast_to(P['rwkv_rk'][e], (H_B, DH_B)).reshape(1, hb)
    post_args = (row(P['rwkv_lnx_g'][e]), row(P['rwkv_lnx_b'][e]), rk)
    if t % RWKV_CHUNK == 0:
        yb, s_new = rwkv_chunk_call(zb3, *prep_args, *post_args, s0)
        yb = yb.reshape(m, hb)
    else:
        r, dec, k2, v, an, bb, g = rwkv_prep_call(_pad_time(zb3, tp), *prep_args)
        s0p = s0.reshape(b, RWKV_PAIRS, 2, DH_B, DH_B).transpose(0, 1, 3, 2, 4).reshape(b, RWKV_PAIRS, DH_B, LANES)
        y, s_fin = rwkv_scan_call(r, dec, k2, v, an, bb, s0p, t)
        s_new = s_fin.reshape(b, RWKV_PAIRS, DH_B, 2, DH_B).transpose(0, 1, 3, 2, 4).reshape(b, H_B, DH_B, DH_B)
        fl = lambda a: a.reshape(b * tp, hb)
        yb = rwkv_post_call(fl(y), fl(r), fl(k2), fl(v), fl(g), *post_args)
        yb = yb.reshape(b, tp, hb)[:, :t].reshape(m, hb)
    st = (k3.reshape(b, t, H_A, DH_A), v3.reshape(b, t, H_A, DH_A), ki3, s_new, zb3[:, t - 1])
    return ya, yb, st


def _cd_mixer(xf, b, t, o, g_pre, pos, P, ret_s, m_c, m_n, m_m, conv_buf):
    w = P['cd_w_in'][o]
    hk, hv = H_C * DK_C, H_C * DV_C
    o_g = 2 * hk + 2 * hv
    o_vd = o_g + 2 * H_D * DK_D
    o_ig = o_vd + H_D * DV_D
    o_og = o_ig + 2 * H_D
    za = proj_in_call(xf, g_pre, w[:, :o_g].astype(BF16))
    zb = proj_in_call(xf, g_pre, jnp.concatenate([w[:, o_g:o_ig], w[:, o_og:]], axis=1).astype(BF16))
    zc = proj_in_call(xf, g_pre, _pad_cols(w[:, o_ig:o_og], LANES).astype(BF16))
    tp = -(-t // SUBLANES) * SUBLANES
    l_valid = t if tp != t else _row_tile(t, 128)
    za3 = _pad_time(za.reshape(b, t, -1), tp)
    zb3 = zb.reshape(b, t, -1)
    zc3 = _pad_time(zc.reshape(b, t, -1), tp)

    half = DK_C // 2
    inv = ROPE_BASE ** (-jnp.arange(half, dtype=F32) / half)
    ang = _pad_time(pos.astype(F32)[None], tp)[0][:, None] * inv[None, :]
    cos = jnp.tile(jnp.cos(ang), (1, 2 * H_C))
    sin = jnp.tile(jnp.sin(ang), (1, 2 * H_C))
    yc, ret_new = retention_call(za3, cos, sin, P['ret_gn'][o], ret_s.astype(F32).reshape(b, hk, DV_C), l_valid)

    hkd = H_D * DK_D
    cb = jnp.pad(conv_buf.astype(F32), ((0, 0), (SUBLANES - (CONV_W - 1), 0), (0, 0)))
    gate_bias = jnp.pad(P['mlstm_if_b'][o].astype(F32).reshape(1, 2 * H_D), ((0, 0), (0, LANES - 2 * H_D)))
    c0 = m_c.astype(F32).transpose(0, 1, 3, 2).reshape(b, hkd, DV_D)
    n0 = m_n.astype(F32).reshape(b, 1, hkd)
    m0 = jnp.pad(m_m.astype(F32), ((0, 0), (0, LANES - H_D))).reshape(b, 1, LANES)
    yd, c_new, n_new, m_new = mlstm_call(_pad_time(zb3, tp), zc3, cb, P['conv_w'][o], P['conv_b'][o].reshape(1, -1),
                                         gate_bias, P['mlstm_gn'][o], c0, n0, m0, l_valid)
    conv_new = jnp.concatenate([conv_buf.astype(F32), zb3[:, :, :2 * hkd]], axis=1)[:, -(CONV_W - 1):]
    st = (ret_new.reshape(b, H_C, DK_C, DV_C),
          c_new.reshape(b, H_D, DK_D, DV_D).transpose(0, 1, 3, 2),
          n_new.reshape(b, H_D, DK_D),
          m_new.reshape(b, LANES)[:, :H_D],
          conv_new)
    m = b * t
    return yc[:, :t].reshape(m, hv), yd[:, :t].reshape(m, H_D * DV_D), st


def _trunk(x, pos, prompt, ab_init, cd_init, cache, P):
    b, t, d = x.shape
    xf = x.reshape(b * t, d)
    depth = P['norm_g'].shape[0]
    ab_new, cd_new = [], []
    bf = lambda a: a.astype(BF16)
    ffn_w = (bf(P['ffn_wg']), bf(P['ffn_wu']), bf(P['ffn_wd']))
    for l in range(depth):
        g = P['norm_g'][l]
        xf = ffn_call(xf, g[0], g[1], *ffn_w, l, 0)
        if l % 2 == 0:
            e = l // 2
            shift_prev, s0 = ab_init(e)
            ya, yb, st = _ab_mixer(xf, b, t, e, g[2], P, prompt, shift_prev, s0, cache)
            ab_new.append(st)
            wo = bf(P['ab_w_out'][e])
            xf = proj_out_call(ya, yb, xf, g[3], wo[:H_A * DH_A], wo[H_A * DH_A:])
        else:
            o = l // 2
            yc, yd, st = _cd_mixer(xf, b, t, o, g[2], pos, P, *cd_init(o))
            cd_new.append(st)
            wo = bf(P['cd_w_out'][o])
            xf = proj_out_call(yc, yd, xf, g[3], wo[:H_C * DV_C], wo[H_C * DV_C:])
        xf = ffn_call(xf, g[4], g[5], *ffn_w, l, 1)
    ab = tuple(jnp.stack(s) for s in zip(*ab_new))
    cd = tuple(jnp.stack(s) for s in zip(*cd_new))
    return xf.reshape(b, t, d), ab, cd


def kernel(x_prompt, x_sample, cache_k, cache_v, cache_kidx, state_rwkv, state_shift, state_ret, state_mlstm_C, state_mlstm_n, state_mlstm_m, state_conv, page_table, norm_g, ffn_wg, ffn_wu, ffn_wd, ab_w_in, ab_w_out, kidx_g, rwkv_mu, rwkv_w0, rwkv_w2, rwkv_a0, rwkv_a2, rwkv_g2, rwkv_kk, rwkv_ka, rwkv_rk, rwkv_lnx_g, rwkv_lnx_b, cd_w_in, cd_w_out, ret_gn, conv_w, conv_b, mlstm_if_b, mlstm_gn):
    P = dict(norm_g=norm_g, ffn_wg=ffn_wg, ffn_wu=ffn_wu, ffn_wd=ffn_wd, ab_w_in=ab_w_in, ab_w_out=ab_w_out,
             kidx_g=kidx_g, rwkv_mu=rwkv_mu, rwkv_w0=rwkv_w0, rwkv_w2=rwkv_w2, rwkv_a0=rwkv_a0, rwkv_a2=rwkv_a2,
             rwkv_g2=rwkv_g2, rwkv_kk=rwkv_kk, rwkv_ka=rwkv_ka, rwkv_rk=rwkv_rk, rwkv_lnx_g=rwkv_lnx_g,
             rwkv_lnx_b=rwkv_lnx_b, cd_w_in=cd_w_in, cd_w_out=cd_w_out, ret_gn=ret_gn, conv_w=conv_w,
             conv_b=conv_b, mlstm_if_b=mlstm_if_b, mlstm_gn=mlstm_gn)
    B, T, _ = x_prompt.shape
    DB, DS, _ = x_sample.shape
    past = page_table.shape[1] * PAGE_SIZE

    def ab_zero(e):
        return (jnp.zeros((B, D_B_IN), F32), jnp.zeros((B, H_B, DH_B, DH_B), F32))

    def cd_zero(o):
        return (jnp.zeros((B, H_C, DK_C, DV_C), F32), jnp.zeros((B, H_D, DV_D, DK_D), F32),
                jnp.zeros((B, H_D, DK_D), F32), jnp.zeros((B, H_D), F32),
                jnp.zeros((B, CONV_W - 1, 2 * H_D * DK_D), F32))

    def ab_cached(e):
        return (state_shift[e], state_rwkv[e])

    def cd_cached(o):
        return (state_ret[o], state_mlstm_C[o], state_mlstm_n[o], state_mlstm_m[o], state_conv[o])

    y_p, (kp, vp, kip, rwp, shp), (rtp, cp, nvp, mp, cvp) = _trunk(
        x_prompt, jnp.arange(T), True, ab_zero, cd_zero, None, P)
    y_s, (ks_, vs_, kis, rws, shs), (rts, cs, nvs, ms, cvs) = _trunk(
        x_sample, past + jnp.arange(DS), False, ab_cached, cd_cached,
        (cache_k, cache_v, cache_kidx, page_table), P)
    return (y_p, y_s, kp, vp, kip, rwp, shp, rtp, cp, nvp, mp, cvp,
            ks_, vs_, kis, rws, shs, rts, cs, nvs, ms, cvs)
```

```python
import functools
import itertools
import math

import numpy as np
import jax
import jax.numpy as jnp
from jax import lax
from jax.experimental import pallas as pl
from jax.experimental.pallas import tpu as pltpu

F32 = jnp.float32
BF16 = jnp.bfloat16
I32 = jnp.int32
HI = lax.Precision.HIGHEST

LANES = 128
SUBLANES = 8
VMEM_LIMIT = 56 * 1024 * 1024

EPS = 1e-6
PAGE_SIZE = 128
H_A, DH_A, H_IDX, D_IDX, TOPK_MAX = 8, 64, 16, 64, 256
H_B, DH_B, R_DECAY, R_AAA, R_GATE = 8, 64, 64, 64, 128
D_B_IN = 3 * H_B * DH_B + R_DECAY + R_AAA + R_GATE
LNX_EPS = 64e-5
H_C, DK_C, DV_C, ROPE_BASE = 4, 64, 128, 10000.0
H_D, DK_D, DV_D, CONV_W = 4, 64, 128, 4
HN_EPS = 1e-5
NEG_BIG = -1e30
INT_MIN = -(2 ** 31)
INT_MAX = 2 ** 31 - 1
KEY_NEG_INF = (0xFF800000 ^ 0x7FFFFFFF) - 2 ** 32

NT = (((1,), (1,)), ((), ()))
TN = (((0,), (0,)), ((), ()))


def _cparams(sem):
    return pltpu.CompilerParams(dimension_semantics=sem, vmem_limit_bytes=VMEM_LIMIT)


def _rms(x, g):
    return x * lax.rsqrt(jnp.mean(x * x, axis=-1, keepdims=True) + EPS) * g


def _softplus(x):
    return jnp.maximum(x, 0.0) + jnp.log(1.0 + jnp.exp(-jnp.abs(x)))


def _bf16_pieces(x, pieces):
    parts = []
    for _ in range(pieces - 1):
        h = x.astype(BF16)
        parts.append(h)
        x = x - h.astype(F32)
    parts.append(x.astype(BF16))
    return parts


def _lane_group_dot(x, m):
    m3 = jnp.concatenate([m.astype(BF16)] * 3, axis=0)
    outs = []
    for p in range(x.shape[1] // LANES):
        xp = jnp.concatenate(_bf16_pieces(x[:, p * LANES:(p + 1) * LANES], 3), axis=1)
        outs.append(jnp.dot(xp, m3, preferred_element_type=F32))
    return jnp.concatenate(outs, axis=1)


def _row_tile(m, want):
    t = min(want, m)
    while m % t:
        t //= 2
    return t


def _col_tile(n, want):
    best = LANES
    for t in range(LANES, min(n, want) + 1, LANES):
        if n % t == 0:
            best = t
    return best


def _ffn_body(x_ref, g0_ref, g1_ref, wg_ref, wu_ref, wd_ref, o_ref, h_scr, acc_scr):
    j = pl.program_id(1)

    @pl.when(j == 0)
    def _():
        h_scr[...] = _rms(x_ref[...], g0_ref[...]).astype(BF16)
        acc_scr[...] = jnp.zeros_like(acc_scr)

    h = h_scr[...]
    g = jnp.dot(h, wg_ref[...], preferred_element_type=F32)
    u = jnp.dot(h, wu_ref[...], preferred_element_type=F32)
    a = g * jax.nn.sigmoid(g) * u
    acc_scr[...] += jnp.dot(a.astype(BF16), wd_ref[...], preferred_element_type=F32)

    @pl.when(j == pl.num_programs(1) - 1)
    def _():
        o_ref[...] = x_ref[...] + 0.5 * _rms(acc_scr[...], g1_ref[...])


def ffn_call(x, g0, g1, wg, wu, wd, layer, slot):
    m, d = x.shape
    ff = wg.shape[3]
    tm = _row_tile(m, 1024)
    tf = _col_tile(ff, 1408)
    return pl.pallas_call(
        _ffn_body,
        grid=(m // tm, ff // tf),
        in_specs=[
            pl.BlockSpec((tm, d), lambda i, j: (i, 0)),
            pl.BlockSpec((1, d), lambda i, j: (0, 0)),
            pl.BlockSpec((1, d), lambda i, j: (0, 0)),
            pl.BlockSpec((None, None, d, tf), lambda i, j: (layer, slot, 0, j)),
            pl.BlockSpec((None, None, d, tf), lambda i, j: (layer, slot, 0, j)),
            pl.BlockSpec((None, None, tf, d), lambda i, j: (layer, slot, j, 0)),
        ],
        out_specs=pl.BlockSpec((tm, d), lambda i, j: (i, 0)),
        out_shape=jax.ShapeDtypeStruct((m, d), F32),
        scratch_shapes=[pltpu.VMEM((tm, d), BF16), pltpu.VMEM((tm, d), F32)],
        compiler_params=_cparams(("parallel", "arbitrary")),
        name="ffn",
    )(x, g0.reshape(1, d), g1.reshape(1, d), wg, wu, wd)


def _proj_in_body(x_ref, g_ref, w_ref, o_ref, h_scr):
    @pl.when(pl.program_id(1) == 0)
    def _():
        h_scr[...] = _rms(x_ref[...], g_ref[...]).astype(BF16)

    o_ref[...] = jnp.dot(h_scr[...], w_ref[...], preferred_element_type=F32).astype(o_ref.dtype)


def _proj_split_body(x_ref, g_ref, w_ref, *o_refs):
    z = jnp.dot(_rms(x_ref[...], g_ref[...]).astype(BF16), w_ref[...], preferred_element_type=F32)
    width = o_refs[0].shape[1]
    for i, o_ref in enumerate(o_refs):
        o_ref[...] = z[:, i * width:(i + 1) * width]


def proj_split_call(x, g, w, parts):
    m, d = x.shape
    n = w.shape[1]
    width = n // parts
    assert width * parts == n and width % LANES == 0
    tm = _row_tile(m, 512)
    return pl.pallas_call(
        _proj_split_body,
        grid=(m // tm,),
        in_specs=[
            pl.BlockSpec((tm, d), lambda i: (i, 0)),
            pl.BlockSpec((1, d), lambda i: (0, 0)),
            pl.BlockSpec((d, n), lambda i: (0, 0)),
        ],
        out_specs=[pl.BlockSpec((tm, width), lambda i: (i, 0))] * parts,
        out_shape=[jax.ShapeDtypeStruct((m, width), F32)] * parts,
        compiler_params=_cparams(("parallel",)),
        name="proj_split",
    )(x, g.reshape(1, d), w)


def proj_in_call(x, g, w, out_dtype=F32):
    m, d = x.shape
    n = w.shape[1]
    tm = _row_tile(m, 512)
    tn = _col_tile(n, 2048)
    return pl.pallas_call(
        _proj_in_body,
        grid=(m // tm, n // tn),
        in_specs=[
            pl.BlockSpec((tm, d), lambda i, j: (i, 0)),
            pl.BlockSpec((1, d), lambda i, j: (0, 0)),
            pl.BlockSpec((d, tn), lambda i, j: (0, j)),
        ],
        out_specs=pl.BlockSpec((tm, tn), lambda i, j: (i, j)),
        out_shape=jax.ShapeDtypeStruct((m, n), out_dtype),
        scratch_shapes=[pltpu.VMEM((tm, d), BF16)],
        compiler_params=_cparams(("parallel", "arbitrary")),
        name="proj_in",
    )(x, g.reshape(1, d), w)


def _kiwi_body(x_ref, g_ref, w_ref, kg_ref, o_ref, k2_ref):
    h = _rms(x_ref[...], g_ref[...]).astype(BF16)
    z = jnp.dot(h, w_ref[...], preferred_element_type=F32)
    z1 = z[:, :LANES]
    lane = lax.broadcasted_iota(I32, z1.shape, 1)
    is_k = lane < D_IDX
    ms = jnp.sum(jnp.where(is_k, z1 * z1, 0.0), axis=-1, keepdims=True) * (1.0 / D_IDX)
    inv = lax.rsqrt(ms + EPS)
    kg = kg_ref[...]
    o_ref[...] = jnp.where(is_k, z1 * inv * kg[:, :LANES], z1 * (H_IDX * D_IDX) ** -0.5)
    k2_ref[...] = (z[:, LANES:] * inv * kg[:, LANES:]).astype(BF16)


def kiwi_call(x, g, w_ki, w_wi, kidx_g):
    m, d = x.shape
    tm = _row_tile(m, 512)
    zpad = jnp.zeros((d, LANES - D_IDX - H_IDX), F32)
    w = jnp.concatenate([w_ki, w_wi, zpad, w_ki, w_ki], axis=1).astype(BF16)
    gpad = jnp.zeros((LANES - D_IDX,), F32)
    kg = jnp.concatenate([kidx_g, gpad, kidx_g, kidx_g]).reshape(1, 2 * LANES)
    return pl.pallas_call(
        _kiwi_body,
        grid=(m // tm,),
        in_specs=[
            pl.BlockSpec((tm, d), lambda i: (i, 0)),
            pl.BlockSpec((1, d), lambda i: (0, 0)),
            pl.BlockSpec((d, 2 * LANES), lambda i: (0, 0)),
            pl.BlockSpec((1, 2 * LANES), lambda i: (0, 0)),
        ],
        out_specs=[pl.BlockSpec((tm, LANES), lambda i: (i, 0))] * 2,
        out_shape=[jax.ShapeDtypeStruct((m, LANES), F32), jax.ShapeDtypeStruct((m, LANES), BF16)],
        compiler_params=_cparams(("parallel",)),
        name="kiwi",
    )(x, g.reshape(1, d), w, kg)


def _proj_out_body(a1_ref, a2_ref, x_ref, g_ref, w1_ref, w2_ref, o_ref):
    y = jnp.dot(a1_ref[...].astype(BF16), w1_ref[...], preferred_element_type=F32)
    y = y + jnp.dot(a2_ref[...].astype(BF16), w2_ref[...], preferred_element_type=F32)
    o_ref[...] = x_ref[...] + _rms(y, g_ref[...])


def proj_out_call(a1, a2, x, g, w1, w2):
    m, d = x.shape
    k1, k2 = a1.shape[1], a2.shape[1]
    tm = _row_tile(m, 512)
    return pl.pallas_call(
        _proj_out_body,
        grid=(m // tm,),
        in_specs=[
            pl.BlockSpec((tm, k1), lambda i: (i, 0)),
            pl.BlockSpec((tm, k2), lambda i: (i, 0)),
            pl.BlockSpec((tm, d), lambda i: (i, 0)),
            pl.BlockSpec((1, d), lambda i: (0, 0)),
            pl.BlockSpec((k1, d), lambda i: (0, 0)),
            pl.BlockSpec((k2, d), lambda i: (0, 0)),
        ],
        out_specs=pl.BlockSpec((tm, d), lambda i: (i, 0)),
        out_shape=jax.ShapeDtypeStruct((m, d), F32),
        compiler_params=_cparams(("parallel",)),
        name="proj_out",
    )(a1, a2, x, g.reshape(1, d), w1, w2)


def _topk_bias(isc, col, topk, nbits_col):
    bits = pltpu.bitcast(isc, I32)
    key = jnp.where(bits < 0, bits ^ 0x7FFFFFFF, bits)
    kf = float(topk)

    def count(mask):
        return jnp.sum(jnp.where(mask, 1.0, 0.0), axis=-1, keepdims=True)

    prefix = jnp.where(count(key >= 0) >= kf, 0, INT_MIN).astype(I32)

    def bit_step(i, prefix):
        cand = prefix + jnp.left_shift(jnp.int32(1), 30 - i)
        return jnp.where(count(key >= cand) >= kf, cand, prefix)

    thr = lax.fori_loop(0, 31, bit_step, prefix)
    gt = key > thr
    eq = key == thr
    need = kf - count(gt)

    def col_step(i, y):
        cand = y + jnp.left_shift(jnp.int32(1), nbits_col - 1 - i)
        c = jnp.sum(jnp.where(eq, jnp.where(col < cand, 1.0, 0.0), 0.0), axis=-1, keepdims=True)
        return jnp.where(c < need, cand, y)

    tied = jnp.where(count(key >= thr) > kf, jnp.where(thr > KEY_NEG_INF, 1.0, 0.0), 0.0)
    y = lax.cond(jnp.max(tied) > 0.0,
                 lambda: lax.fori_loop(0, nbits_col, col_step, jnp.zeros_like(thr)),
                 lambda: jnp.full_like(thr, INT_MAX))
    ninf = -jnp.inf
    bias = jnp.where(gt, 0.0, jnp.where(eq, jnp.where(col <= y, 0.0, ninf), ninf))
    return jnp.where(jnp.abs(isc) < jnp.inf, bias, ninf)


DSA_CAUSAL_BANDS = 8


def _dsa_prompt_body(qkv_q_ref, qkv_k_ref, qkv_v_ref, qi_ref, ki2_ref, kw_ref, o_ref, kb_scr, vb_scr, *,
                     topk, q_first):
    qb = qkv_q_ref.shape[1]
    t = qkv_k_ref.shape[1]
    j = pl.program_id(1)
    t0 = (j + q_first) * qb

    @pl.when(j == 0)
    def _():
        kb_scr[...] = qkv_k_ref[0].astype(BF16)
        vb_scr[...] = qkv_v_ref[0].astype(BF16)

    lane = lax.broadcasted_iota(I32, (1, LANES), 1)
    half = (jnp.where(lane < DH_A, 1.0, 0.0), jnp.where(lane < DH_A, 0.0, 1.0))
    half_b = tuple(m.astype(BF16) for m in half)
    ki2 = ki2_ref[0]
    kw = kw_ref[0]
    isc = jnp.zeros((qb, t), F32)
    for hp in range(H_IDX // 2):
        qp = qi_ref[0, :, hp * LANES:(hp + 1) * LANES]
        for h2 in range(2):
            h = 2 * hp + h2
            s = lax.dot_general(qp * half_b[h2], ki2, NT, preferred_element_type=F32)
            isc = isc + kw[:, D_IDX + h:D_IDX + h + 1] * jnp.maximum(s, 0.0)
    row = lax.broadcasted_iota(I32, (qb, t), 0) + t0
    col = lax.broadcasted_iota(I32, (qb, t), 1)
    isc = jnp.where(col <= row, isc, -jnp.inf)
    bias = _topk_bias(isc, col, topk, max(1, (t - 1).bit_length()))
    scale = DH_A ** -0.5
    assert math.frexp(scale)[0] == 0.5
    for p in range(H_A // 2):
        sl = slice(p * LANES, (p + 1) * LANES)
        qp = qkv_q_ref[0, :, sl]
        kp = kb_scr[:, sl]
        vp = vb_scr[:, sl]
        outs = []
        for h2 in range(2):
            qm = (qp * (half[h2] * scale)).astype(BF16)
            lg = lax.dot_general(qm, kp, NT, preferred_element_type=F32) + bias
            mx = jnp.max(lg, axis=-1, keepdims=True)
            pr = jnp.exp(lg - mx)
            l = jnp.sum(pr, axis=-1, keepdims=True)
            outs.append(jnp.dot(pr.astype(BF16), vp, preferred_element_type=F32) / l)
        o_ref[0, :, sl] = jnp.where(lane < DH_A, outs[0], outs[1])


def dsa_prompt_call(q, k, v, qi, ki2, kw):
    b, t, hd = q.shape
    qb = _row_tile(t, 256)
    topk = min(TOPK_MAX, t // 4)
    n_qb = t // qb
    per_band = n_qb // math.gcd(n_qb, DSA_CAUSAL_BANDS)
    outs = []
    for q_first in range(0, n_qb, per_band):
        tk = (q_first + per_band) * qb
        qmap = lambda i, j, q_first=q_first: (i, j + q_first, 0)
        outs.append(pl.pallas_call(
            functools.partial(_dsa_prompt_body, topk=topk, q_first=q_first),
            grid=(b, per_band),
            in_specs=[
                pl.BlockSpec((1, qb, hd), qmap),
                pl.BlockSpec((1, tk, hd), lambda i, j: (i, 0, 0)),
                pl.BlockSpec((1, tk, hd), lambda i, j: (i, 0, 0)),
                pl.BlockSpec((1, qb, H_IDX * D_IDX), qmap),
                pl.BlockSpec((1, tk, LANES), lambda i, j: (i, 0, 0)),
                pl.BlockSpec((1, qb, LANES), qmap),
            ],
            out_specs=pl.BlockSpec((1, qb, hd), lambda i, j: (i, j, 0)),
            out_shape=jax.ShapeDtypeStruct((b, per_band * qb, hd), F32),
            scratch_shapes=[pltpu.VMEM((tk, hd), BF16), pltpu.VMEM((tk, hd), BF16)],
            compiler_params=_cparams(("parallel", "arbitrary")),
            name="dsa_prompt",
        )(q, k, v, qi, ki2, kw))
    return outs[0] if len(outs) == 1 else jnp.concatenate(outs, axis=1)


DSA_PAGE_GROUP = 16
DSA_IDX_PAGE_GROUP = 64


def _page_group(n_pages, want=DSA_PAGE_GROUP):
    return math.gcd(n_pages, want)


def _page_specs(shape, n_pages, group):
    zeros = (0,) * (len(shape) - 1)

    def spec(g):
        return pl.BlockSpec(shape, lambda b, s, pt: (pt[b, jnp.minimum(s * group + g, n_pages - 1)],) + zeros)
    return [spec(g) for g in range(group)]


def _page_cat(refs):
    return jnp.concatenate([r[0].reshape(-1, PAGE_SIZE).astype(BF16) for r in refs], axis=1)


def _dsa_sidx_body(pt_ref, qi_ref, wi_ref, kin_ref, *rest):
    kc_refs, o_ref, onew_ref = rest[:-2], rest[-2], rest[-1]
    nq = o_ref.shape[1]
    r = lax.broadcasted_iota(I32, (nq, nq * H_IDX), 0)
    c = lax.broadcasted_iota(I32, (nq, nq * H_IDX), 1)
    lo = r * H_IDX
    wsel = jnp.where(c >= lo, jnp.where(c < lo + H_IDX, wi_ref[0], 0.0), 0.0)

    def scores(ki):
        s = jnp.dot(qi_ref[0], ki, preferred_element_type=F32)
        return jnp.dot(wsel, jnp.maximum(s, 0.0), precision=HI, preferred_element_type=F32)

    @pl.when(pl.program_id(1) == 0)
    def _():
        onew_ref[0] = scores(kin_ref[0].astype(BF16))

    o_ref[0] = scores(_page_cat(kc_refs))


def dsa_sidx_call(page_table, qi, wi, ki_new, cache_kidx):
    db, n_pages = page_table.shape
    nq = qi.shape[1] // H_IDX
    group = _page_group(n_pages, DSA_IDX_PAGE_GROUP)
    grid_spec = pltpu.PrefetchScalarGridSpec(
        num_scalar_prefetch=1,
        grid=(db, n_pages // group),
        in_specs=[
            pl.BlockSpec((1, nq * H_IDX, D_IDX), lambda b, s, pt: (b, 0, 0)),
            pl.BlockSpec((1, 1, nq * H_IDX), lambda b, s, pt: (b, 0, 0)),
            pl.BlockSpec((1, D_IDX, PAGE_SIZE), lambda b, s, pt: (b, 0, 0)),
        ] + _page_specs((1, D_IDX, PAGE_SIZE), n_pages, group),
        out_specs=[pl.BlockSpec((1, nq, group * PAGE_SIZE), lambda b, s, pt: (b, 0, s)),
                   pl.BlockSpec((1, nq, PAGE_SIZE), lambda b, s, pt: (b, 0, 0))],
    )
    return pl.pallas_call(
        _dsa_sidx_body,
        grid_spec=grid_spec,
        out_shape=[jax.ShapeDtypeStruct((db, nq, n_pages * PAGE_SIZE), F32),
                   jax.ShapeDtypeStruct((db, nq, PAGE_SIZE), F32)],
        compiler_params=_cparams(("parallel", "arbitrary")),
        name="dsa_sample_idx",
    )(page_table, qi, wi, ki_new, *([cache_kidx] * group))


def _dsa_satt_body(pt_ref, isc_ref, iscn_ref, q_ref, kn_ref, vn_ref, *rest, n_pages, n_steps, topk, n_new):
    group = n_pages // n_steps
    kc_refs, vc_refs = rest[:group], rest[group:2 * group]
    o_ref, bias_scr, qe_scr, m_scr, l_scr, acc_scr = rest[2 * group:]
    p = pl.program_id(1)
    nq = q_ref.shape[1]
    past = n_pages * PAGE_SIZE
    wcols = group * PAGE_SIZE
    ltot = past + PAGE_SIZE
    hd = H_A * DH_A
    scale = DH_A ** -0.5
    assert math.frexp(scale)[0] == 0.5

    def attend(kcat, vcat, bias):
        lg = jnp.dot(qe_scr[...], kcat, preferred_element_type=F32)
        lg = lg + jnp.concatenate([bias] * H_A, axis=0)
        m_old = m_scr[...]
        m_new = jnp.maximum(m_old, jnp.max(lg, axis=-1, keepdims=True))
        alpha = jnp.exp(m_old - m_new)
        pr = jnp.exp(lg - m_new)
        l_scr[...] = alpha * l_scr[...] + jnp.sum(pr, axis=-1, keepdims=True)
        acc_scr[...] = alpha * acc_scr[...] + lax.dot_general(pr.astype(BF16), vcat, NT, preferred_element_type=F32)
        m_scr[...] = m_new

    @pl.when(p == 0)
    def _():
        isc = jnp.concatenate([isc_ref[0], iscn_ref[0]], axis=1)
        row = lax.broadcasted_iota(I32, (nq, ltot), 0)
        col = lax.broadcasted_iota(I32, (nq, ltot), 1)
        rel = col - past
        ninf = -jnp.inf
        isc = jnp.where(rel < 0, isc, jnp.where(rel < n_new, jnp.where(rel <= row, isc, ninf), ninf))
        bias = _topk_bias(isc, col, topk, max(1, (ltot - 1).bit_length()))
        for i in range(n_steps):
            bias_scr[i] = bias[:, i * wcols:(i + 1) * wcols]
        q = q_ref[0] * scale
        lane = lax.broadcasted_iota(I32, (nq, hd), 1)
        qe_scr[...] = jnp.concatenate(
            [jnp.where(lane >= h * DH_A, jnp.where(lane < (h + 1) * DH_A, q, 0.0), 0.0) for h in range(H_A)],
            axis=0).astype(BF16)
        m_scr[...] = jnp.full_like(m_scr, NEG_BIG)
        l_scr[...] = jnp.zeros_like(l_scr)
        acc_scr[...] = jnp.zeros_like(acc_scr)
        attend(kn_ref[0].reshape(hd, PAGE_SIZE).astype(BF16), vn_ref[0].reshape(hd, PAGE_SIZE).astype(BF16),
               bias[:, past:])

    attend(_page_cat(kc_refs), _page_cat(vc_refs), bias_scr[p])

    @pl.when(p == n_steps - 1)
    def _():
        o = acc_scr[...] / l_scr[...]
        lane = lax.broadcasted_iota(I32, (nq, hd), 1)
        out = jnp.zeros((nq, hd), F32)
        for h in range(H_A):
            oh = o[h * nq:(h + 1) * nq]
            out = out + jnp.where(lane >= h * DH_A, jnp.where(lane < (h + 1) * DH_A, oh, 0.0), 0.0)
        o_ref[0] = out


def dsa_satt_call(page_table, isc, isc_new, q, k_new, v_new, cache_k, cache_v, n_new):
    db, n_pages = page_table.shape
    nq = q.shape[1]
    hd = H_A * DH_A
    page = (1, H_A, DH_A, PAGE_SIZE)
    group = _page_group(n_pages)
    n_steps = n_pages // group
    wcols = group * PAGE_SIZE
    topk = min(TOPK_MAX, (n_pages * PAGE_SIZE + n_new) // 4)
    grid_spec = pltpu.PrefetchScalarGridSpec(
        num_scalar_prefetch=1,
        grid=(db, n_steps),
        in_specs=[
            pl.BlockSpec((1, nq, n_pages * PAGE_SIZE), lambda b, p, pt: (b, 0, 0)),
            pl.BlockSpec((1, nq, PAGE_SIZE), lambda b, p, pt: (b, 0, 0)),
            pl.BlockSpec((1, nq, hd), lambda b, p, pt: (b, 0, 0)),
            pl.BlockSpec(page, lambda b, p, pt: (b, 0, 0, 0)),
            pl.BlockSpec(page, lambda b, p, pt: (b, 0, 0, 0)),
        ] + _page_specs(page, n_pages, group) * 2,
        out_specs=pl.BlockSpec((1, nq, hd), lambda b, p, pt: (b, 0, 0)),
        scratch_shapes=[
            pltpu.VMEM((n_steps, nq, wcols), F32),
            pltpu.VMEM((H_A * nq, hd), BF16),
            pltpu.VMEM((H_A * nq, 1), F32),
            pltpu.VMEM((H_A * nq, 1), F32),
            pltpu.VMEM((H_A * nq, hd), F32),
        ],
    )
    return pl.pallas_call(
        functools.partial(_dsa_satt_body, n_pages=n_pages, n_steps=n_steps, topk=topk, n_new=n_new),
        grid_spec=grid_spec,
        out_shape=jax.ShapeDtypeStruct((db, nq, hd), F32),
        compiler_params=_cparams(("parallel", "arbitrary")),
        name="dsa_sample_att",
    )(page_table, isc, isc_new, q, k_new, v_new, *([cache_k] * group), *([cache_v] * group))


def _head_sum_matrix(n, group):
    r = lax.broadcasted_iota(I32, (n, n), 0) // group
    c = lax.broadcasted_iota(I32, (n, n), 1) // group
    return jnp.where(r == c, 1.0, 0.0).astype(F32)


def _rwkv_prep_math(zb, before, mu, w0, a0, w2p, a2p, g2, kkp, ka):
    hd = H_B * DH_B
    rows = lax.broadcasted_iota(I32, zb.shape, 0)
    prev = jnp.where(rows == 0, before, pltpu.roll(zb, 1, 0))
    z = zb + (prev - zb) * mu
    r = z[:, 0:hd]
    k = z[:, hd:2 * hd]
    v = z[:, 2 * hd:3 * hd]
    xwa = z[:, 3 * hd:3 * hd + R_DECAY + R_AAA]
    xg = z[:, 3 * hd + R_DECAY + R_AAA:]
    wl = w0 + jnp.dot(jnp.tanh(xwa).astype(BF16), w2p, preferred_element_type=F32)
    w_log = -_softplus(-wl) - 0.5
    log_decay = -jnp.exp(w_log)
    a = jax.nn.sigmoid(a0 + jnp.dot(xwa.astype(BF16), a2p, preferred_element_type=F32))
    g = jnp.dot(jax.nn.sigmoid(xg).astype(BF16), g2, preferred_element_type=F32)
    kk = k * kkp
    ss = _lane_group_dot(kk * kk, _head_sum_matrix(LANES, DH_B))
    kk = kk * lax.rsqrt(jnp.maximum(ss, 1e-24))
    return r, log_decay, k * (1.0 + (a - 1.0) * ka), v, -kk, kk * a, g


def _rwkv_post_math(y, r, k, v, g, lnx_g, lnx_b, rk):
    ones = _head_sum_matrix(LANES, DH_B)
    avg = ones * (1.0 / DH_B)
    mu = _lane_group_dot(y, avg)
    d = y - mu
    var = _lane_group_dot(d * d, avg)
    yn = d * lax.rsqrt(var + LNX_EPS) * lnx_g + lnx_b
    bonus = _lane_group_dot(r * k * rk, ones)
    return (yn + bonus * v) * g


def _rwkv_prep_body(zb_ref, sp_ref, mu_ref, w0_ref, a0_ref, w2_ref, a2_ref, g2_ref, kkp_ref, ka_ref,
                    r_ref, w_ref, k_ref, v_ref, an_ref, b_ref, g_ref, carry_scr):
    tc = pl.program_id(1)
    zb = zb_ref[0]
    tt = zb.shape[0]
    hd = H_B * DH_B

    @pl.when(tc == 0)
    def _():
        carry_scr[...] = sp_ref[0]

    r, log_decay, k2, v, an, bb, g = _rwkv_prep_math(
        zb, carry_scr[...], mu_ref[...], w0_ref[...], a0_ref[...], w2_ref[...], a2_ref[...], g2_ref[...],
        kkp_ref[...], ka_ref[...])
    carry_scr[...] = zb[tt - 1:tt]
    r_ref[0] = r
    w_ref[0] = log_decay
    k_ref[0] = k2
    v_ref[0] = v
    an_ref[0] = an
    b_ref[0] = bb
    g_ref[0] = g


def rwkv_prep_call(zb, shift_prev, mu, w0, a0, w2p, a2p, g2, kkp, ka):
    b, t, dz = zb.shape
    hd = H_B * DH_B
    tt = _row_tile(t, 256)
    row = lambda n: pl.BlockSpec((1, n), lambda i, j: (0, 0))
    full = lambda s: pl.BlockSpec(s, lambda i, j: (0, 0))
    out = pl.BlockSpec((1, tt, hd), lambda i, j: (i, j, 0))
    return pl.pallas_call(
        _rwkv_prep_body,
        grid=(b, t // tt),
        in_specs=[
            pl.BlockSpec((1, tt, dz), lambda i, j: (i, j, 0)),
            pl.BlockSpec((1, 1, dz), lambda i, j: (i, 0, 0)),
            row(dz), row(hd), row(hd),
            full((R_DECAY + R_AAA, hd)), full((R_DECAY + R_AAA, hd)), full((R_GATE, hd)),
            row(hd), row(hd),
        ],
        out_specs=[out] * 7,
        out_shape=[jax.ShapeDtypeStruct((b, t, hd), F32)] * 7,
        scratch_shapes=[pltpu.VMEM((1, dz), F32)],
        compiler_params=_cparams(("parallel", "arbitrary")),
        name="rwkv_prep",
    )(zb, shift_prev, mu, w0, a0, w2p, a2p, g2, kkp, ka)


RWKV_PAIRS = H_B // 2
RWKV_STEPS = SUBLANES // RWKV_PAIRS
RWKV_NB = 8
RWKV_SUM_PIECES = 2


def _rwkv_scan_body(r_ref, w_ref, k_ref, v_ref, a_ref, b_ref, s0_ref, y_ref, sout_ref,
                    s_scr, pa_scr, pv_scr, *, t_valid):
    c = pl.program_id(1)
    nb, tc = r_ref.shape[0], r_ref.shape[1]
    chains = [(bi, p) for bi in range(nb) for p in range(RWKV_PAIRS)]
    rows = lambda n: slice(n * DH_B, (n + 1) * DH_B)

    @pl.when(c == 0)
    def _():
        for n, (bi, p) in enumerate(chains):
            s_scr[rows(n), :] = s0_ref[bi, p]

    if t_valid < tc * RWKV_STEPS:
        y_ref[...] = jnp.zeros_like(y_ref)

    sub = lax.broadcasted_iota(I32, (DH_B, LANES), 0)
    lane = lax.broadcasted_iota(I32, (DH_B, LANES), 1)
    isel = jnp.where((lane & (DH_B - 1)) == sub, 1.0, 0.0).astype(F32)
    qblk = _head_sum_matrix(LANES, DH_B).astype(BF16)

    def hsum(ref, pieces=3):
        x = ref[...]
        parts = []
        for _ in range(pieces - 1):
            h = x.astype(BF16)
            parts.append(h)
            x = x - h.astype(F32)
        parts.append(x.astype(BF16))
        return jnp.dot(jnp.concatenate(parts, axis=1), jnp.concatenate([qblk] * pieces, axis=0),
                       preferred_element_type=F32)

    def step(u, carry):
        tiles = [(a_ref[bi, u], jnp.exp(w_ref[bi, u]), k_ref[bi, u], v_ref[bi, u], b_ref[bi, u], r_ref[bi, u])
                 for bi in range(nb)]
        for i in range(RWKV_STEPS):
            for n, (bi, p) in enumerate(chains):
                at, _, _, vt, _, _ = tiles[bi]
                row = slice(i * RWKV_PAIRS + p, i * RWKV_PAIRS + p + 1)
                pa_scr[rows(n), :] = s_scr[rows(n), :] * at[row]
                pv_scr[rows(n), :] = isel * vt[row]
            sa = hsum(pa_scr, RWKV_SUM_PIECES)
            vc = hsum(pv_scr)
            for n, (bi, p) in enumerate(chains):
                _, wt, kt, _, bt, rt = tiles[bi]
                row = slice(i * RWKV_PAIRS + p, i * RWKV_PAIRS + p + 1)
                s = s_scr[rows(n), :] * wt[row] + sa[rows(n)] * bt[row] + vc[rows(n)] * kt[row]
                s_scr[rows(n), :] = s
                pa_scr[rows(n), :] = s * rt[row]
            yb = hsum(pa_scr, RWKV_SUM_PIECES)
            for n, (bi, p) in enumerate(chains):
                row = slice(i * RWKV_PAIRS + p, i * RWKV_PAIRS + p + 1)
                y_ref[bi, u, row, :] = jnp.sum(yb[rows(n)] * isel, axis=0, keepdims=True)
        return carry

    lax.fori_loop(0, min(t_valid, tc * RWKV_STEPS) // RWKV_STEPS, step, 0)

    @pl.when(c == pl.num_programs(1) - 1)
    def _():
        for n, (bi, p) in enumerate(chains):
            sout_ref[bi, p] = s_scr[rows(n), :]


def rwkv_scan_call(r, w, k, v, a, b, s0, t_valid):
    bsz, t, hd = r.shape
    nb = math.gcd(bsz, RWKV_NB)
    nt = t // RWKV_STEPS
    tc = _row_tile(nt, 32)
    assert t % RWKV_STEPS == 0 and t_valid % RWKV_STEPS == 0 and (t_valid == t or nt == tc)
    tiled = lambda x: x.reshape(bsz, nt, SUBLANES, LANES)
    seq = pl.BlockSpec((nb, tc, SUBLANES, LANES), lambda i, j: (i, j, 0, 0))
    st = pl.BlockSpec((nb, RWKV_PAIRS, DH_B, LANES), lambda i, j: (i, 0, 0, 0))
    y, s_fin = pl.pallas_call(
        functools.partial(_rwkv_scan_body, t_valid=t_valid),
        grid=(bsz // nb, nt // tc),
        in_specs=[seq] * 6 + [st],
        out_specs=[seq, st],
        out_shape=[jax.ShapeDtypeStruct((bsz, nt, SUBLANES, LANES), F32),
                   jax.ShapeDtypeStruct((bsz, RWKV_PAIRS, DH_B, LANES), F32)],
        scratch_shapes=[pltpu.VMEM((nb * RWKV_PAIRS * DH_B, LANES), F32)] * 3,
        compiler_params=_cparams(("parallel", "arbitrary")),
        name="rwkv_scan",
    )(tiled(r), tiled(w), tiled(k), tiled(v), tiled(a), tiled(b), s0)
    return y.reshape(bsz, t, hd), s_fin


RWKV_CHUNK = 64
RWKV_HG = 4
RWKV_CHUNK_SEQS = 2


def _rwkv_chunk_body(zb_ref, sp_ref, mu_ref, w0_ref, a0_ref, w2_ref, a2_ref, g2_ref, kkp_ref, ka_ref,
                     lg_ref, lb_ref, rk_ref, s0_ref, yb_ref, sout_ref, s_scr, carry_scr):
    c = pl.program_id(1)
    nb, cs = zb_ref.shape[0], zb_ref.shape[1]
    ng = s_scr.shape[1]
    gw = RWKV_HG * DH_B
    n = RWKV_HG * cs
    assert cs & (cs - 1) == 0

    @pl.when(c == 0)
    def _():
        s_scr[...] = s0_ref[...]
        carry_scr[...] = sp_ref[...]

    seqs = []
    for bi in range(nb):
        zb = zb_ref[bi]
        seqs.append(_rwkv_prep_math(zb, carry_scr[bi], mu_ref[...], w0_ref[...], a0_ref[...], w2_ref[...],
                                    a2_ref[...], g2_ref[...], kkp_ref[...], ka_ref[...]))
        carry_scr[bi] = zb[cs - 1:cs]

    tr = lax.broadcasted_iota(I32, (cs, cs), 0)
    tcol = lax.broadcasted_iota(I32, (cs, cs), 1)
    tril = jnp.where(tr >= tcol, 1.0, 0.0).astype(F32)
    row = lax.broadcasted_iota(I32, (n, n), 0)
    col = lax.broadcasted_iota(I32, (n, n), 1)
    same = (row // cs) == (col // cs)
    tt = row & (cs - 1)
    ss = col & (cs - 1)
    strict = jnp.where(same, jnp.where(tt > ss, 1.0, 0.0), 0.0).astype(F32)
    incl = jnp.where(same, jnp.where(tt >= ss, 1.0, 0.0), 0.0).astype(F32)
    eye = jnp.where(row == col, 1.0, 0.0).astype(F32)
    lane_h = lax.broadcasted_iota(I32, (1, gw), 1) // DH_B
    hmask = [jnp.where(lane_h == h, 1.0, 0.0).astype(F32) for h in range(RWKV_HG)]
    sblk = jnp.where(lax.broadcasted_iota(I32, (gw, gw), 0) // DH_B == lax.broadcasted_iota(I32, (gw, gw), 1) // DH_B,
                     1.0, 0.0).astype(F32)
    nn = (((1,), (0,)), ((), ()))

    def stack(x):
        return jnp.concatenate([x * m for m in hmask], axis=0)

    def fold(x):
        out = x[0:cs]
        for h in range(1, RWKV_HG):
            out = out + x[h * cs:(h + 1) * cs]
        return out

    def bdot(x, y, dims=nn):
        return lax.dot_general(x.astype(BF16), y.astype(BF16), dims, preferred_element_type=F32)

    jobs = []
    for bi in range(nb):
        r_all, lw_all, k_all, v_all, a_all, b_all, _ = seqs[bi]
        for g in range(ng):
            sl = slice(g * gw, (g + 1) * gw)
            lw = lw_all[:, sl]
            cl = jnp.dot(tril, lw, precision=HI, preferred_element_type=F32)
            e_in = jnp.exp(cl)
            e_inv = jnp.exp(-cl)
            e_tail = jnp.exp(cl[cs - 1:cs] - cl)
            jobs.append(dict(
                bi=bi, g=g, p_last=e_in[cs - 1:cs],
                at=a_all[:, sl] * jnp.exp(cl - lw),
                bt=b_all[:, sl] * e_inv, kt=k_all[:, sl] * e_inv, rt=r_all[:, sl] * e_in,
                bw=b_all[:, sl] * e_tail, kw=k_all[:, sl] * e_tail, v=v_all[:, sl],
                s_old=s_scr[bi, g]))
    for j in jobs:
        j['a_s'], j['r_s'], j['b_s'], j['k_s'], j['v_s'] = (stack(j[x]) for x in ('at', 'rt', 'bt', 'kt', 'v'))
    for j in jobs:
        j['lab'] = bdot(j['a_s'], j['b_s'], NT) * strict
        j['lak'] = bdot(j['a_s'], j['k_s'], NT) * strict
        j['lrb'] = bdot(j['r_s'], j['b_s'], NT) * incl
        j['lrk'] = bdot(j['r_s'], j['k_s'], NT) * incl
    for j in jobs:
        j['rhs'] = stack(bdot(j['at'], j['s_old'], NT)) + bdot(j['lak'], j['v_s'])
        j['tm'] = eye + j['lab']
        j['lp'] = j['lab']
    for _ in range(cs.bit_length() - 2):
        for j in jobs:
            j['lp'] = bdot(j['lp'], j['lp'])
        for j in jobs:
            j['tm'] = j['tm'] + bdot(j['lp'], j['tm'])
    for j in jobs:
        j['u_s'] = bdot(j['tm'], j['rhs'])
    for j in jobs:
        j['y'] = fold(stack(bdot(j['rt'], j['s_old'], NT)) + bdot(j['lrb'], j['u_s']) + bdot(j['lrk'], j['v_s']))
        u = fold(j['u_s'])
        j['s_new'] = j['s_old'] * j['p_last'] + (bdot(u, j['bw'], TN) + bdot(j['v'], j['kw'], TN)) * sblk
    for bi in range(nb):
        r_all, _, k_all, v_all, _, _, g_all = seqs[bi]
        y = jnp.concatenate([j['y'] for j in jobs if j['bi'] == bi], axis=1)
        yb_ref[bi] = _rwkv_post_math(y, r_all, k_all, v_all, g_all, lg_ref[...], lb_ref[...], rk_ref[...])
    for j in jobs:
        s_scr[j['bi'], j['g']] = j['s_new']

    @pl.when(c == pl.num_programs(1) - 1)
    def _():
        sout_ref[...] = s_scr[...]


def rwkv_chunk_call(zb, shift_prev, mu, w0, a0, w2p, a2p, g2, kkp, ka, lnx_g, lnx_b, rk, s0):
    bsz, t, dz = zb.shape
    hd = H_B * DH_B
    cs = RWKV_CHUNK
    ng = H_B // RWKV_HG
    gw = RWKV_HG * DH_B
    eye = jnp.eye(RWKV_HG, dtype=F32)
    s0g = jnp.einsum('bghij,hk->bghikj', s0.astype(F32).reshape(bsz, ng, RWKV_HG, DH_B, DH_B), eye)
    nb = math.gcd(bsz, RWKV_CHUNK_SEQS)
    st = pl.BlockSpec((nb, ng, gw, gw), lambda i, j: (i, 0, 0, 0))
    row = lambda n: pl.BlockSpec((1, n), lambda i, j: (0, 0))
    full = lambda s: pl.BlockSpec(s, lambda i, j: (0, 0))
    y, s_fin = pl.pallas_call(
        _rwkv_chunk_body,
        grid=(bsz // nb, t // cs),
        in_specs=[
            pl.BlockSpec((nb, cs, dz), lambda i, j: (i, j, 0)),
            pl.BlockSpec((nb, 1, dz), lambda i, j: (i, 0, 0)),
            row(dz), row(hd), row(hd),
            full((R_DECAY + R_AAA, hd)), full((R_DECAY + R_AAA, hd)), full((R_GATE, hd)),
            row(hd), row(hd), row(hd), row(hd), row(hd), st],
        out_specs=[pl.BlockSpec((nb, cs, hd), lambda i, j: (i, j, 0)), st],
        out_shape=[jax.ShapeDtypeStruct((bsz, t, hd), F32), jax.ShapeDtypeStruct((bsz, ng, gw, gw), F32)],
        scratch_shapes=[pltpu.VMEM((nb, ng, gw, gw), F32), pltpu.VMEM((nb, 1, dz), F32)],
        compiler_params=_cparams(("parallel", "arbitrary")),
        name="rwkv_chunk",
    )(zb, shift_prev, mu, w0, a0, w2p, a2p, g2, kkp, ka, lnx_g, lnx_b, rk, s0g.reshape(bsz, ng, gw, gw))
    s_fin = jnp.einsum('bghihj->bghij', s_fin.reshape(bsz, ng, RWKV_HG, DH_B, RWKV_HG, DH_B))
    return y, s_fin.reshape(bsz, H_B, DH_B, DH_B)


def _rwkv_post_body(y_ref, r_ref, k_ref, v_ref, g_ref, lg_ref, lb_ref, rk_ref, o_ref):
    o_ref[...] = _rwkv_post_math(y_ref[...], r_ref[...], k_ref[...], v_ref[...], g_ref[...],
                                 lg_ref[...], lb_ref[...], rk_ref[...])


def rwkv_post_call(y, r, k, v, g, lnx_g, lnx_b, rk):
    m, hd = y.shape
    tm = _row_tile(m, 512)
    tok = pl.BlockSpec((tm, hd), lambda i: (i, 0))
    row = pl.BlockSpec((1, hd), lambda i: (0, 0))
    return pl.pallas_call(
        _rwkv_post_body,
        grid=(m // tm,),
        in_specs=[tok] * 5 + [row] * 3,
        out_specs=tok,
        out_shape=jax.ShapeDtypeStruct((m, hd), F32),
        compiler_params=_cparams(("parallel",)),
        name="rwkv_post",
    )(y, r, k, v, g, lnx_g, lnx_b, rk)


def _head_mask(n, h, width):
    lane = lax.broadcasted_iota(I32, (1, n), 1)
    return jnp.where(lane >= h * width, jnp.where(lane < (h + 1) * width, 1.0, 0.0), 0.0).astype(F32)


def _head_norm128(y, g):
    mu = jnp.mean(y, axis=-1, keepdims=True)
    d = y - mu
    var = jnp.mean(d * d, axis=-1, keepdims=True)
    return d * lax.rsqrt(var + HN_EPS) * g


def _ret_body(qk_ref, v_ref, gc_ref, cos_ref, sin_ref, gn_ref, s0_ref, y_ref, sout_ref, s_scr, *, l_valid):
    c = pl.program_id(1)
    lc = qk_ref.shape[1]
    hk = H_C * DK_C

    @pl.when(c == 0)
    def _():
        s_scr[...] = s0_ref[0]

    rr = lax.broadcasted_iota(I32, (LANES, LANES), 0)
    cc = lax.broadcasted_iota(I32, (LANES, LANES), 1)
    half = DK_C // 2
    same = (rr // DK_C) == (cc // DK_C)
    dr = rr & (DK_C - 1)
    dc = cc & (DK_C - 1)
    rot = jnp.where(same, jnp.where(dr == dc + half, -1.0, jnp.where(dr + half == dc, 1.0, 0.0)), 0.0).astype(F32)

    qk = qk_ref[0]
    cos = cos_ref[...]
    sin = sin_ref[...]

    def rope(x):
        return x * cos + _lane_group_dot(x, rot) * sin

    qr = rope(qk[:, :hk])
    kr = rope(qk[:, hk:]) * DK_C ** -0.5
    krb = kr.astype(BF16)
    s_prev = s_scr[...]
    s_prev_b = s_prev.astype(BF16)

    jj = lax.broadcasted_iota(I32, (lc, lc), 0)
    ss = lax.broadcasted_iota(I32, (lc, lc), 1)
    diff = (jj - ss).astype(F32)
    jcol = lax.broadcasted_iota(I32, (lc, 1), 0).astype(F32)
    srow_state = lax.broadcasted_iota(I32, (hk, 1), 0) // DK_C
    s_new = jnp.zeros_like(s_prev)
    decay_rows = jnp.zeros((hk, 1), F32)
    hs = []
    for h in range(H_C):
        lg = math.log1p(-2.0 ** (-5.0 - h))
        mh = _head_mask(hk, h, DK_C)
        qm = (qr * mh).astype(BF16)
        hs.append(dict(h=h, lg=lg, mh=mh, qm=qm, vhb=v_ref[0, :, h * DV_C:(h + 1) * DV_C].astype(BF16),
                       qk=lax.dot_general(qm, krb, NT, preferred_element_type=F32),
                       qs=jnp.dot(qm, s_prev_b, preferred_element_type=F32)))
    for d in hs:
        dmask = jnp.where(diff >= 0, jnp.exp(jnp.maximum(diff, 0.0) * d['lg']), 0.0)
        d['scores'] = (d['qk'] * dmask).astype(BF16)
    for d in hs:
        h = d['h']
        intra = jnp.dot(d['scores'], d['vhb'], preferred_element_type=F32)
        inter = d['qs'] * jnp.exp((jcol + 1.0) * d['lg'])
        yh = _head_norm128(intra + inter, gn_ref[h:h + 1, :])
        gch = gc_ref[0, :, h * DV_C:(h + 1) * DV_C]
        d['out'] = yh * (gch * jax.nn.sigmoid(gch))
    for d in hs:
        w_s = jnp.where(jcol < l_valid, jnp.exp((l_valid - 1.0 - jcol) * d['lg']), 0.0)
        kw = (kr * d['mh'] * w_s).astype(BF16)
        s_new = s_new + lax.dot_general(kw, d['vhb'], TN, preferred_element_type=F32)
        decay_rows = jnp.where(srow_state == d['h'], math.exp(l_valid * d['lg']), decay_rows)
    for d in hs:
        y_ref[0, :, d['h'] * DV_C:(d['h'] + 1) * DV_C] = d['out']
    s_fin = decay_rows * s_prev + s_new
    s_scr[...] = s_fin

    @pl.when(c == pl.num_programs(1) - 1)
    def _():
        sout_ref[0] = s_fin


RET_CHUNK = 256
MLSTM_CHUNK = 128


def retention_call(za, cos, sin, gn, s0, valid):
    b, t, _ = za.shape
    lc = _row_tile(t, RET_CHUNK)
    l_valid = lc if valid is None else valid
    assert valid is None or t == lc
    hk, hv = H_C * DK_C, H_C * DV_C
    blk = lambda j: pl.BlockSpec((1, lc, 2 * hk), lambda i, c, j=j: (i, c, j))
    st = pl.BlockSpec((1, hk, DV_C), lambda i, c: (i, 0, 0))
    return pl.pallas_call(
        functools.partial(_ret_body, l_valid=l_valid),
        grid=(b, t // lc),
        in_specs=[blk(0), blk(1), blk(2),
                  pl.BlockSpec((lc, hk), lambda i, c: (c, 0)),
                  pl.BlockSpec((lc, hk), lambda i, c: (c, 0)),
                  pl.BlockSpec((H_C, DV_C), lambda i, c: (0, 0)),
                  st],
        out_specs=[pl.BlockSpec((1, lc, hv), lambda i, c: (i, c, 0)), st],
        out_shape=[jax.ShapeDtypeStruct((b, t, hv), F32), jax.ShapeDtypeStruct((b, hk, DV_C), F32)],
        scratch_shapes=[pltpu.VMEM((hk, DV_C), F32)],
        compiler_params=_cparams(("parallel", "arbitrary")),
        name="retention",
    )(za, za, za, cos, sin, gn, s0)


MLSTM_SEQS = 1


def _mlstm_body(*refs, l_valid):
    seqs = [_mlstm_seq(bi, *refs, l_valid=l_valid) for bi in range(refs[0].shape[0])]
    for _ in itertools.zip_longest(*seqs):
        pass


def _mlstm_seq(bi, qk_ref, v_ref, og_ref, gt_ref, cb_ref, cw_ref, cbias_ref, gbias_ref, gn_ref,
               c0_ref, n0_ref, m0_ref, y_ref, cout_ref, nout_ref, mout_ref,
               ext_scr, c_scr, n_scr, m_scr, *, l_valid):
    c = pl.program_id(1)
    lc = qk_ref.shape[1]
    hk = H_D * DK_D
    pad = SUBLANES

    @pl.when(c == 0)
    def _():
        ext_scr[bi, 0:pad, :] = cb_ref[bi]
        c_scr[bi] = c0_ref[bi]
        n_scr[bi] = n0_ref[bi]
        m_scr[bi] = m0_ref[bi]

    u = qk_ref[bi]
    ext_scr[bi, pad:pad + lc, :] = u
    acc = cbias_ref[...] + u * cw_ref[CONV_W - 1:CONV_W, :]
    for i in range(CONV_W - 1):
        sh = CONV_W - 1 - i
        acc = acc + ext_scr[bi, pad - sh:pad - sh + lc, :] * cw_ref[i:i + 1, :]
    ext_scr[bi, 0:pad, :] = u[lc - pad:lc, :]
    qkc = acc * jax.nn.sigmoid(acc)
    q = qkc[:, :hk]
    k = qkc[:, hk:] * DK_D ** -0.5
    kb = k.astype(BF16)

    yield
    gates = gt_ref[bi] + gbias_ref[...]
    logsig = -_softplus(-gates)
    rowi = lax.broadcasted_iota(I32, (lc, LANES), 0)
    logsig = jnp.where(rowi < l_valid, logsig, 0.0)
    jj = lax.broadcasted_iota(I32, (lc, lc), 0)
    ss = lax.broadcasted_iota(I32, (lc, lc), 1)
    tril = jnp.where(jj >= ss, 1.0, 0.0).astype(F32)
    bcum = jnp.dot(tril, logsig, precision=HI, preferred_element_type=F32)
    lane_g = lax.broadcasted_iota(I32, (lc, LANES), 1)
    jcol = lax.broadcasted_iota(I32, (lc, 1), 0)
    causal = jj >= ss
    ninf = -jnp.inf

    c_prev = c_scr[bi]
    c_prev_b = c_prev.astype(BF16)
    n_prev = n_scr[bi]
    m_prev = m_scr[bi]
    c_new = jnp.zeros_like(c_prev)
    carry_rows = jnp.zeros((hk, 1), F32)
    carry_lanes = jnp.zeros((1, hk), F32)
    ws_full = jnp.zeros((lc, hk), F32)
    m_out = m_prev
    srow_state = lax.broadcasted_iota(I32, (hk, 1), 0) // DK_D
    lane_state = lax.broadcasted_iota(I32, (1, hk), 1) // DK_D
    lane_m = lax.broadcasted_iota(I32, (1, LANES), 1)
    hs = []
    for h in range(H_D):
        mh = _head_mask(hk, h, DK_D)
        e_i = jnp.where(lane_g == h, 1.0, 0.0).astype(F32)
        e_f = jnp.where(lane_g == H_D + h, 1.0, 0.0).astype(F32)
        qm = q * mh
        vh = v_ref[bi, :, h * DV_D:(h + 1) * DV_D]
        hs.append(dict(
            h=h, mh=mh, qm=qm, qmb=qm.astype(BF16), vh=vh, vhb=vh.astype(BF16),
            logi_col=jnp.where(jcol < l_valid, gates[:, h:h + 1], ninf),
            b_col=bcum[:, H_D + h:H_D + h + 1],
            i_row=lax.dot_general(e_i, gates, NT, precision=HI, preferred_element_type=F32),
            b_row=lax.dot_general(e_f, bcum, NT, precision=HI, preferred_element_type=F32),
            m_h=m_prev[:, h:h + 1]))
    yield
    for d in hs:
        d['qk'] = lax.dot_general(d['qmb'], kb, NT, preferred_element_type=F32)
        d['qc'] = jnp.dot(d['qmb'], c_prev_b, preferred_element_type=F32)
    yield
    for d in hs:
        i_row = jnp.where(ss < l_valid, d['i_row'], ninf)
        d['inter'] = d['b_col'] + d['m_h']
        dmat = jnp.where(causal, d['b_col'] - d['b_row'] + i_row, ninf)
        d['m_j'] = jnp.maximum(d['inter'], jnp.max(dmat, axis=-1, keepdims=True))
        d['amat'] = jnp.exp(dmat - d['m_j']) * d['qk']
        d['sc'] = jnp.exp(d['inter'] - d['m_j'])
    yield
    for d in hs:
        num = jnp.dot(d['amat'].astype(BF16), d['vhb'], preferred_element_type=F32) + d['sc'] * d['qc']
        den = jnp.sum(d['amat'], axis=-1, keepdims=True) \
            + d['sc'] * jnp.sum(d['qm'] * n_prev, axis=-1, keepdims=True)
        hh = num / jnp.maximum(jnp.abs(den), jnp.exp(-d['m_j']))
        h = d['h']
        ogh = og_ref[bi, :, h * DV_D:(h + 1) * DV_D]
        d['out'] = _head_norm128(hh, gn_ref[h:h + 1, :]) * jax.nn.sigmoid(ogh)
    yield
    for d in hs:
        h = d['h']
        b_last = d['b_col'][l_valid - 1:l_valid, :]
        gs = b_last - d['b_col'] + d['logi_col']
        m_new = jnp.maximum(b_last + d['m_h'], jnp.max(gs, axis=0, keepdims=True))
        ws = jnp.exp(gs - m_new)
        carry = jnp.exp(b_last + d['m_h'] - m_new)
        c_new = c_new + lax.dot_general((k * d['mh']).astype(BF16), (d['vh'] * ws).astype(BF16), TN,
                                        preferred_element_type=F32)
        carry_rows = jnp.where(srow_state == h, carry, carry_rows)
        carry_lanes = jnp.where(lane_state == h, carry, carry_lanes)
        ws_full = ws_full + ws * d['mh']
        m_out = jnp.where(lane_m == h, m_new, m_out)
    c_fin = carry_rows * c_prev + c_new
    n_fin = carry_lanes * n_prev + jnp.sum(ws_full * k, axis=0, keepdims=True)
    yield
    for d in hs:
        y_ref[bi, :, d['h'] * DV_D:(d['h'] + 1) * DV_D] = d['out']
    c_scr[bi] = c_fin
    n_scr[bi] = n_fin
    m_scr[bi] = m_out

    @pl.when(c == pl.num_programs(1) - 1)
    def _():
        cout_ref[bi] = c_fin
        nout_ref[bi] = n_fin
        mout_ref[bi] = m_out


def mlstm_call(zb, zc, conv_buf, conv_w, conv_b, gate_bias, gn, c0, n0, m0, valid):
    b, t, _ = zb.shape
    lc = _row_tile(t, MLSTM_CHUNK)
    l_valid = lc if valid is None else valid
    assert valid is None or t == lc
    hk, hv = H_D * DK_D, H_D * DV_D
    nb = math.gcd(b, MLSTM_SEQS)
    blk = lambda j: pl.BlockSpec((nb, lc, 2 * hk), lambda i, c, j=j: (i, c, j))
    cst = lambda s: pl.BlockSpec(s, lambda i, c: (0,) * len(s))
    per_b = lambda s: pl.BlockSpec((nb,) + s, lambda i, c: (i,) + (0,) * len(s))
    return pl.pallas_call(
        functools.partial(_mlstm_body, l_valid=l_valid),
        grid=(b // nb, t // lc),
        in_specs=[blk(0), blk(1), blk(2),
                  pl.BlockSpec((nb, lc, LANES), lambda i, c: (i, c, 0)),
                  per_b((SUBLANES, 2 * hk)),
                  cst((CONV_W, 2 * hk)), cst((1, 2 * hk)), cst((1, LANES)), cst((H_D, DV_D)),
                  per_b((hk, DV_D)), per_b((1, hk)), per_b((1, LANES))],
        out_specs=[pl.BlockSpec((nb, lc, hv), lambda i, c: (i, c, 0)),
                   per_b((hk, DV_D)), per_b((1, hk)), per_b((1, LANES))],
        out_shape=[jax.ShapeDtypeStruct((b, t, hv), F32),
                   jax.ShapeDtypeStruct((b, hk, DV_D), F32),
                   jax.ShapeDtypeStruct((b, 1, hk), F32),
                   jax.ShapeDtypeStruct((b, 1, LANES), F32)],
        scratch_shapes=[pltpu.VMEM((nb, SUBLANES + lc, 2 * hk), F32),
                        pltpu.VMEM((nb, hk, DV_D), F32),
                        pltpu.VMEM((nb, 1, hk), F32),
                        pltpu.VMEM((nb, 1, LANES), F32)],
        compiler_params=_cparams(("parallel", "arbitrary")),
        name="mlstm",
    )(zb, zb, zb, zc, conv_buf, conv_w, conv_b, gate_bias, gn, c0, n0, m0)


def _pad_cols(w, n):
    return jnp.pad(w, ((0, 0), (0, n - w.shape[1])))


def _pad_time(a, tp):
    t = a.shape[1]
    if t == tp:
        return a
    return jnp.pad(a, ((0, 0), (0, tp - t)) + ((0, 0),) * (a.ndim - 2))


def _heads_major(a, h):
    b, t, _ = a.shape
    return a.reshape(b, t, h, -1).transpose(0, 2, 1, 3)


def _ab_mixer(xf, b, t, e, g_pre, P, prompt, shift_prev, s0, cache):
    m = b * t
    hd = H_A * DH_A
    w = P['ab_w_in'][e]
    o_qi = 3 * hd
    o_ki = o_qi + H_IDX * D_IDX
    o_wi = o_ki + D_IDX
    o_zb = o_wi + H_IDX
    q3, k3, v3 = (a.reshape(b, t, hd) for a in proj_split_call(xf, g_pre, w[:, :o_qi].astype(BF16), 3))
    qi = proj_in_call(xf, g_pre, w[:, o_qi:o_ki].astype(BF16), out_dtype=BF16)
    kw, ki2 = kiwi_call(xf, g_pre, w[:, o_ki:o_wi], w[:, o_wi:o_zb], P['kidx_g'][e])
    zb = proj_in_call(xf, g_pre, w[:, o_zb:].astype(BF16))
    ki3 = kw[:, :D_IDX].reshape(b, t, D_IDX)
    wi3 = kw[:, D_IDX:D_IDX + H_IDX].reshape(b, t, H_IDX)

    if prompt:
        ya = dsa_prompt_call(q3, k3, v3, qi.reshape(b, t, H_IDX * D_IDX),
                             ki2.reshape(b, t, LANES), kw.reshape(b, t, LANES)).reshape(m, hd)
    else:
        cache_k, cache_v, cache_kidx, page_table = cache
        n_pool = cache_k.shape[1]
        nq = SUBLANES
        qi_s = _pad_time(qi.reshape(b, t, H_IDX * D_IDX), nq).reshape(b, nq * H_IDX, D_IDX)
        wi_s = _pad_time(wi3, nq).reshape(b, 1, nq * H_IDX)
        keys_last = lambda a: jnp.moveaxis(a, 1, -1)
        new_page = lambda a: jnp.pad(keys_last(a), ((0, 0),) * (a.ndim - 1) + ((0, PAGE_SIZE - t),))
        isc, isc_new = dsa_sidx_call(page_table, qi_s, wi_s, new_page(ki3), keys_last(cache_kidx[e]))
        ya = dsa_satt_call(page_table, isc, isc_new, _pad_time(q3, nq),
                           new_page(k3.reshape(b, t, H_A, DH_A)), new_page(v3.reshape(b, t, H_A, DH_A)),
                           keys_last(cache_k[e]), keys_last(cache_v[e]), t)
        ya = ya[:, :t].reshape(m, hd)

    tp = -(-t // SUBLANES) * SUBLANES
    hb = H_B * DH_B
    zb3 = zb.reshape(b, t, D_B_IN)
    zpad = jnp.zeros((R_DECAY, hb), F32)
    w2p = jnp.concatenate([P['rwkv_w2'][e], zpad], axis=0).astype(BF16)
    a2p = jnp.concatenate([zpad, P['rwkv_a2'][e]], axis=0).astype(BF16)
    row = lambda a: a.reshape(1, -1)
    prep_args = (shift_prev.reshape(b, 1, D_B_IN), row(P['rwkv_mu'][e]), row(P['rwkv_w0'][e]),
                 row(P['rwkv_a0'][e]), w2p, a2p, P['rwkv_g2'][e].astype(BF16), row(P['rwkv_kk'][e]),
                 row(P['rwkv_ka'][e]))
    rk = jnp.broadcast_to(P['rwkv_rk'][e], (H_B, DH_B)).reshape(1, hb)
    post_args = (row(P['rwkv_lnx_g'][e]), row(P['rwkv_lnx_b'][e]), rk)
    if t % RWKV_CHUNK == 0:
        yb, s_new = rwkv_chunk_call(zb3, *prep_args, *post_args, s0)
        yb = yb.reshape(m, hb)
    else:
        r, dec, k2, v, an, bb, g = rwkv_prep_call(_pad_time(zb3, tp), *prep_args)
        s0p = s0.reshape(b, RWKV_PAIRS, 2, DH_B, DH_B).transpose(0, 1, 3, 2, 4).reshape(b, RWKV_PAIRS, DH_B, LANES)
        y, s_fin = rwkv_scan_call(r, dec, k2, v, an, bb, s0p, t)
        s_new = s_fin.reshape(b, RWKV_PAIRS, DH_B, 2, DH_B).transpose(0, 1, 3, 2, 4).reshape(b, H_B, DH_B, DH_B)
        fl = lambda a: a.reshape(b * tp, hb)
        yb = rwkv_post_call(fl(y), fl(r), fl(k2), fl(v), fl(g), *post_args)
        yb = yb.reshape(b, tp, hb)[:, :t].reshape(m, hb)
    st = (k3.reshape(b, t, H_A, DH_A), v3.reshape(b, t, H_A, DH_A), ki3, s_new, zb3[:, t - 1])
    return ya, yb, st


def _cd_mixer(xf, b, t, o, g_pre, pos, P, ret_s, m_c, m_n, m_m, conv_buf):
    w = P['cd_w_in'][o]
    hk, hv = H_C * DK_C, H_C * DV_C
    o_g = 2 * hk + 2 * hv
    o_vd = o_g + 2 * H_D * DK_D
    o_ig = o_vd + H_D * DV_D
    o_og = o_ig + 2 * H_D
    za = proj_in_call(xf, g_pre, w[:, :o_g].astype(BF16))
    zb = proj_in_call(xf, g_pre, jnp.concatenate([w[:, o_g:o_ig], w[:, o_og:]], axis=1).astype(BF16))
    zc = proj_in_call(xf, g_pre, _pad_cols(w[:, o_ig:o_og], LANES).astype(BF16))
    tp = -(-t // SUBLANES) * SUBLANES
    l_valid = t if tp != t else None
    za3 = _pad_time(za.reshape(b, t, -1), tp)
    zb3 = zb.reshape(b, t, -1)
    zc3 = _pad_time(zc.reshape(b, t, -1), tp)

    half = DK_C // 2
    inv = ROPE_BASE ** (-jnp.arange(half, dtype=F32) / half)
    ang = _pad_time(pos.astype(F32)[None], tp)[0][:, None] * inv[None, :]
    cos = jnp.tile(jnp.cos(ang), (1, 2 * H_C))
    sin = jnp.tile(jnp.sin(ang), (1, 2 * H_C))
    yc, ret_new = retention_call(za3, cos, sin, P['ret_gn'][o], ret_s.astype(F32).reshape(b, hk, DV_C), l_valid)

    hkd = H_D * DK_D
    cb = jnp.pad(conv_buf.astype(F32), ((0, 0), (SUBLANES - (CONV_W - 1), 0), (0, 0)))
    gate_bias = jnp.pad(P['mlstm_if_b'][o].astype(F32).reshape(1, 2 * H_D), ((0, 0), (0, LANES - 2 * H_D)))
    c0 = m_c.astype(F32).transpose(0, 1, 3, 2).reshape(b, hkd, DV_D)
    n0 = m_n.astype(F32).reshape(b, 1, hkd)
    m0 = jnp.pad(m_m.astype(F32), ((0, 0), (0, LANES - H_D))).reshape(b, 1, LANES)
    yd, c_new, n_new, m_new = mlstm_call(_pad_time(zb3, tp), zc3, cb, P['conv_w'][o], P['conv_b'][o].reshape(1, -1),
                                         gate_bias, P['mlstm_gn'][o], c0, n0, m0, l_valid)
    conv_new = jnp.concatenate([conv_buf.astype(F32), zb3[:, :, :2 * hkd]], axis=1)[:, -(CONV_W - 1):]
    st = (ret_new.reshape(b, H_C, DK_C, DV_C),
          c_new.reshape(b, H_D, DK_D, DV_D).transpose(0, 1, 3, 2),
          n_new.reshape(b, H_D, DK_D),
          m_new.reshape(b, LANES)[:, :H_D],
          conv_new)
    m = b * t
    return yc[:, :t].reshape(m, hv), yd[:, :t].reshape(m, H_D * DV_D), st


def _trunk(x, pos, prompt, ab_init, cd_init, cache, P):
    b, t, d = x.shape
    xf = x.reshape(b * t, d)
    depth = P['norm_g'].shape[0]
    ab_new, cd_new = [], []
    bf = lambda a: a.astype(BF16)
    ffn_w = (bf(P['ffn_wg']), bf(P['ffn_wu']), bf(P['ffn_wd']))
    for l in range(depth):
        g = P['norm_g'][l]
        xf = ffn_call(xf, g[0], g[1], *ffn_w, l, 0)
        if l % 2 == 0:
            e = l // 2
            shift_prev, s0 = ab_init(e)
            ya, yb, st = _ab_mixer(xf, b, t, e, g[2], P, prompt, shift_prev, s0, cache)
            ab_new.append(st)
            wo = bf(P['ab_w_out'][e])
            xf = proj_out_call(ya, yb, xf, g[3], wo[:H_A * DH_A], wo[H_A * DH_A:])
        else:
            o = l // 2
            yc, yd, st = _cd_mixer(xf, b, t, o, g[2], pos, P, *cd_init(o))
            cd_new.append(st)
            wo = bf(P['cd_w_out'][o])
            xf = proj_out_call(yc, yd, xf, g[3], wo[:H_C * DV_C], wo[H_C * DV_C:])
        xf = ffn_call(xf, g[4], g[5], *ffn_w, l, 1)
    ab = tuple(jnp.stack(s) for s in zip(*ab_new))
    cd = tuple(jnp.stack(s) for s in zip(*cd_new))
    return xf.reshape(b, t, d), ab, cd


def kernel(x_prompt, x_sample, cache_k, cache_v, cache_kidx, state_rwkv, state_shift, state_ret, state_mlstm_C, state_mlstm_n, state_mlstm_m, state_conv, page_table, norm_g, ffn_wg, ffn_wu, ffn_wd, ab_w_in, ab_w_out, kidx_g, rwkv_mu, rwkv_w0, rwkv_w2, rwkv_a0, rwkv_a2, rwkv_g2, rwkv_kk, rwkv_ka, rwkv_rk, rwkv_lnx_g, rwkv_lnx_b, cd_w_in, cd_w_out, ret_gn, conv_w, conv_b, mlstm_if_b, mlstm_gn):
    P = dict(norm_g=norm_g, ffn_wg=ffn_wg, ffn_wu=ffn_wu, ffn_wd=ffn_wd, ab_w_in=ab_w_in, ab_w_out=ab_w_out,
             kidx_g=kidx_g, rwkv_mu=rwkv_mu, rwkv_w0=rwkv_w0, rwkv_w2=rwkv_w2, rwkv_a0=rwkv_a0, rwkv_a2=rwkv_a2,
             rwkv_g2=rwkv_g2, rwkv_kk=rwkv_kk, rwkv_ka=rwkv_ka, rwkv_rk=rwkv_rk, rwkv_lnx_g=rwkv_lnx_g,
             rwkv_lnx_b=rwkv_lnx_b, cd_w_in=cd_w_in, cd_w_out=cd_w_out, ret_gn=ret_gn, conv_w=conv_w,
             conv_b=conv_b, mlstm_if_b=mlstm_if_b, mlstm_gn=mlstm_gn)
    B, T, _ = x_prompt.shape
    DB, DS, _ = x_sample.shape
    past = page_table.shape[1] * PAGE_SIZE

    def ab_zero(e):
        return (jnp.zeros((B, D_B_IN), F32), jnp.zeros((B, H_B, DH_B, DH_B), F32))

    def cd_zero(o):
        return (jnp.zeros((B, H_C, DK_C, DV_C), F32), jnp.zeros((B, H_D, DV_D, DK_D), F32),
                jnp.zeros((B, H_D, DK_D), F32), jnp.zeros((B, H_D), F32),
                jnp.zeros((B, CONV_W - 1, 2 * H_D * DK_D), F32))

    def ab_cached(e):
        return (state_shift[e], state_rwkv[e])

    def cd_cached(o):
        return (state_ret[o], state_mlstm_C[o], state_mlstm_n[o], state_mlstm_m[o], state_conv[o])

    y_p, (kp, vp, kip, rwp, shp), (rtp, cp, nvp, mp, cvp) = _trunk(
        x_prompt, jnp.arange(T), True, ab_zero, cd_zero, None, P)
    y_s, (ks_, vs_, kis, rws, shs), (rts, cs, nvs, ms, cvs) = _trunk(
        x_sample, past + jnp.arange(DS), False, ab_cached, cd_cached,
        (cache_k, cache_v, cache_kidx, page_table), P)
    return (y_p, y_s, kp, vp, kip, rwp, shp, rtp, cp, nvp, mp, cvp,
            ks_, vs_, kis, rws, shs, rts, cs, nvs, ms, cvs)
```

```python
import functools
import itertools
import math

import numpy as np
import jax
import jax.numpy as jnp
from jax import lax
from jax.experimental import pallas as pl
from jax.experimental.pallas import tpu as pltpu

F32 = jnp.float32
BF16 = jnp.bfloat16
I32 = jnp.int32
HI = lax.Precision.HIGHEST

LANES = 128
SUBLANES = 8
VMEM_LIMIT = 56 * 1024 * 1024

EPS = 1e-6
PAGE_SIZE = 128
H_A, DH_A, H_IDX, D_IDX, TOPK_MAX = 8, 64, 16, 64, 256
H_B, DH_B, R_DECAY, R_AAA, R_GATE = 8, 64, 64, 64, 128
D_B_IN = 3 * H_B * DH_B + R_DECAY + R_AAA + R_GATE
LNX_EPS = 64e-5
H_C, DK_C, DV_C, ROPE_BASE = 4, 64, 128, 10000.0
H_D, DK_D, DV_D, CONV_W = 4, 64, 128, 4
HN_EPS = 1e-5
NEG_BIG = -1e30
INT_MIN = -(2 ** 31)
INT_MAX = 2 ** 31 - 1
KEY_NEG_INF = (0xFF800000 ^ 0x7FFFFFFF) - 2 ** 32

NT = (((1,), (1,)), ((), ()))
TN = (((0,), (0,)), ((), ()))


def _cparams(sem):
    return pltpu.CompilerParams(dimension_semantics=sem, vmem_limit_bytes=VMEM_LIMIT)


def _rms(x, g):
    return x * lax.rsqrt(jnp.mean(x * x, axis=-1, keepdims=True) + EPS) * g


def _softplus(x):
    return jnp.maximum(x, 0.0) + jnp.log(1.0 + jnp.exp(-jnp.abs(x)))


def _bf16_pieces(x, pieces):
    parts = []
    for _ in range(pieces - 1):
        h = x.astype(BF16)
        parts.append(h)
        x = x - h.astype(F32)
    parts.append(x.astype(BF16))
    return parts


def _lane_group_dot(x, m):
    m3 = jnp.concatenate([m.astype(BF16)] * 3, axis=0)
    outs = []
    for p in range(x.shape[1] // LANES):
        xp = jnp.concatenate(_bf16_pieces(x[:, p * LANES:(p + 1) * LANES], 3), axis=1)
        outs.append(jnp.dot(xp, m3, preferred_element_type=F32))
    return jnp.concatenate(outs, axis=1)


def _row_tile(m, want):
    t = min(want, m)
    while m % t:
        t //= 2
    return t


def _col_tile(n, want):
    best = LANES
    for t in range(LANES, min(n, want) + 1, LANES):
        if n % t == 0:
            best = t
    return best


def _ffn_body(x_ref, g0_ref, g1_ref, wg_ref, wu_ref, wd_ref, o_ref, h_scr, acc_scr):
    j = pl.program_id(1)

    @pl.when(j == 0)
    def _():
        h_scr[...] = _rms(x_ref[...], g0_ref[...]).astype(BF16)
        acc_scr[...] = jnp.zeros_like(acc_scr)

    h = h_scr[...]
    g = jnp.dot(h, wg_ref[...], preferred_element_type=F32)
    u = jnp.dot(h, wu_ref[...], preferred_element_type=F32)
    a = g * jax.nn.sigmoid(g) * u
    acc_scr[...] += jnp.dot(a.astype(BF16), wd_ref[...], preferred_element_type=F32)

    @pl.when(j == pl.num_programs(1) - 1)
    def _():
        o_ref[...] = x_ref[...] + 0.5 * _rms(acc_scr[...], g1_ref[...])


def ffn_call(x, g0, g1, wg, wu, wd, layer, slot):
    m, d = x.shape
    ff = wg.shape[3]
    tm = _row_tile(m, 1024)
    tf = _col_tile(ff, 1408)
    return pl.pallas_call(
        _ffn_body,
        grid=(m // tm, ff // tf),
        in_specs=[
            pl.BlockSpec((tm, d), lambda i, j: (i, 0)),
            pl.BlockSpec((1, d), lambda i, j: (0, 0)),
            pl.BlockSpec((1, d), lambda i, j: (0, 0)),
            pl.BlockSpec((None, None, d, tf), lambda i, j: (layer, slot, 0, j)),
            pl.BlockSpec((None, None, d, tf), lambda i, j: (layer, slot, 0, j)),
            pl.BlockSpec((None, None, tf, d), lambda i, j: (layer, slot, j, 0)),
        ],
        out_specs=pl.BlockSpec((tm, d), lambda i, j: (i, 0)),
        out_shape=jax.ShapeDtypeStruct((m, d), F32),
        scratch_shapes=[pltpu.VMEM((tm, d), BF16), pltpu.VMEM((tm, d), F32)],
        compiler_params=_cparams(("parallel", "arbitrary")),
        name="ffn",
    )(x, g0.reshape(1, d), g1.reshape(1, d), wg, wu, wd)


def _proj_in_body(x_ref, g_ref, w_ref, o_ref, h_scr):
    @pl.when(pl.program_id(1) == 0)
    def _():
        h_scr[...] = _rms(x_ref[...], g_ref[...]).astype(BF16)

    o_ref[...] = jnp.dot(h_scr[...], w_ref[...], preferred_element_type=F32).astype(o_ref.dtype)


def _proj_split_body(x_ref, g_ref, w_ref, *o_refs):
    z = jnp.dot(_rms(x_ref[...], g_ref[...]).astype(BF16), w_ref[...], preferred_element_type=F32)
    width = o_refs[0].shape[1]
    for i, o_ref in enumerate(o_refs):
        o_ref[...] = z[:, i * width:(i + 1) * width]


def proj_split_call(x, g, w, parts):
    m, d = x.shape
    n = w.shape[1]
    width = n // parts
    assert width * parts == n and width % LANES == 0
    tm = _row_tile(m, 1024)
    return pl.pallas_call(
        _proj_split_body,
        grid=(m // tm,),
        in_specs=[
            pl.BlockSpec((tm, d), lambda i: (i, 0)),
            pl.BlockSpec((1, d), lambda i: (0, 0)),
            pl.BlockSpec((d, n), lambda i: (0, 0)),
        ],
        out_specs=[pl.BlockSpec((tm, width), lambda i: (i, 0))] * parts,
        out_shape=[jax.ShapeDtypeStruct((m, width), F32)] * parts,
        compiler_params=_cparams(("parallel",)),
        name="proj_split",
    )(x, g.reshape(1, d), w)


def proj_in_call(x, g, w, out_dtype=F32):
    m, d = x.shape
    n = w.shape[1]
    tm = _row_tile(m, 1024)
    tn = _col_tile(n, 2048)
    return pl.pallas_call(
        _proj_in_body,
        grid=(m // tm, n // tn),
        in_specs=[
            pl.BlockSpec((tm, d), lambda i, j: (i, 0)),
            pl.BlockSpec((1, d), lambda i, j: (0, 0)),
            pl.BlockSpec((d, tn), lambda i, j: (0, j)),
        ],
        out_specs=pl.BlockSpec((tm, tn), lambda i, j: (i, j)),
        out_shape=jax.ShapeDtypeStruct((m, n), out_dtype),
        scratch_shapes=[pltpu.VMEM((tm, d), BF16)],
        compiler_params=_cparams(("parallel", "arbitrary")),
        name="proj_in",
    )(x, g.reshape(1, d), w)


def _kiwi_body(x_ref, g_ref, w_ref, kg_ref, o_ref, k2_ref):
    h = _rms(x_ref[...], g_ref[...]).astype(BF16)
    z = jnp.dot(h, w_ref[...], preferred_element_type=F32)
    z1 = z[:, :LANES]
    lane = lax.broadcasted_iota(I32, z1.shape, 1)
    is_k = lane < D_IDX
    ms = jnp.sum(jnp.where(is_k, z1 * z1, 0.0), axis=-1, keepdims=True) * (1.0 / D_IDX)
    inv = lax.rsqrt(ms + EPS)
    kg = kg_ref[...]
    o_ref[...] = jnp.where(is_k, z1 * inv * kg[:, :LANES], z1 * (H_IDX * D_IDX) ** -0.5)
    k2_ref[...] = (z[:, LANES:] * inv * kg[:, LANES:]).astype(BF16)


def kiwi_call(x, g, w_ki, w_wi, kidx_g):
    m, d = x.shape
    tm = _row_tile(m, 512)
    zpad = jnp.zeros((d, LANES - D_IDX - H_IDX), F32)
    w = jnp.concatenate([w_ki, w_wi, zpad, w_ki, w_ki], axis=1).astype(BF16)
    gpad = jnp.zeros((LANES - D_IDX,), F32)
    kg = jnp.concatenate([kidx_g, gpad, kidx_g, kidx_g]).reshape(1, 2 * LANES)
    return pl.pallas_call(
        _kiwi_body,
        grid=(m // tm,),
        in_specs=[
            pl.BlockSpec((tm, d), lambda i: (i, 0)),
            pl.BlockSpec((1, d), lambda i: (0, 0)),
            pl.BlockSpec((d, 2 * LANES), lambda i: (0, 0)),
            pl.BlockSpec((1, 2 * LANES), lambda i: (0, 0)),
        ],
        out_specs=[pl.BlockSpec((tm, LANES), lambda i: (i, 0))] * 2,
        out_shape=[jax.ShapeDtypeStruct((m, LANES), F32), jax.ShapeDtypeStruct((m, LANES), BF16)],
        compiler_params=_cparams(("parallel",)),
        name="kiwi",
    )(x, g.reshape(1, d), w, kg)


def _proj_out_body(a1_ref, a2_ref, x_ref, g_ref, w1_ref, w2_ref, o_ref):
    y = jnp.dot(a1_ref[...].astype(BF16), w1_ref[...], preferred_element_type=F32)
    y = y + jnp.dot(a2_ref[...].astype(BF16), w2_ref[...], preferred_element_type=F32)
    o_ref[...] = x_ref[...] + _rms(y, g_ref[...])


def proj_out_call(a1, a2, x, g, w1, w2):
    m, d = x.shape
    k1, k2 = a1.shape[1], a2.shape[1]
    tm = _row_tile(m, 1024)
    return pl.pallas_call(
        _proj_out_body,
        grid=(m // tm,),
        in_specs=[
            pl.BlockSpec((tm, k1), lambda i: (i, 0)),
            pl.BlockSpec((tm, k2), lambda i: (i, 0)),
            pl.BlockSpec((tm, d), lambda i: (i, 0)),
            pl.BlockSpec((1, d), lambda i: (0, 0)),
            pl.BlockSpec((k1, d), lambda i: (0, 0)),
            pl.BlockSpec((k2, d), lambda i: (0, 0)),
        ],
        out_specs=pl.BlockSpec((tm, d), lambda i: (i, 0)),
        out_shape=jax.ShapeDtypeStruct((m, d), F32),
        compiler_params=_cparams(("parallel",)),
        name="proj_out",
    )(a1, a2, x, g.reshape(1, d), w1, w2)


def _topk_bias(isc, col, topk, nbits_col):
    bits = pltpu.bitcast(isc, I32)
    key = jnp.where(bits < 0, bits ^ 0x7FFFFFFF, bits)
    kf = float(topk)

    def count(mask):
        return jnp.sum(jnp.where(mask, 1.0, 0.0), axis=-1, keepdims=True)

    prefix = jnp.where(count(key >= 0) >= kf, 0, INT_MIN).astype(I32)

    def bit_step(i, prefix):
        cand = prefix + jnp.left_shift(jnp.int32(1), 30 - i)
        return jnp.where(count(key >= cand) >= kf, cand, prefix)

    thr = lax.fori_loop(0, 31, bit_step, prefix)
    gt = key > thr
    eq = key == thr
    need = kf - count(gt)

    def col_step(i, y):
        cand = y + jnp.left_shift(jnp.int32(1), nbits_col - 1 - i)
        c = jnp.sum(jnp.where(eq, jnp.where(col < cand, 1.0, 0.0), 0.0), axis=-1, keepdims=True)
        return jnp.where(c < need, cand, y)

    tied = jnp.where(count(key >= thr) > kf, jnp.where(thr > KEY_NEG_INF, 1.0, 0.0), 0.0)
    y = lax.cond(jnp.max(tied) > 0.0,
                 lambda: lax.fori_loop(0, nbits_col, col_step, jnp.zeros_like(thr)),
                 lambda: jnp.full_like(thr, INT_MAX))
    ninf = -jnp.inf
    bias = jnp.where(gt, 0.0, jnp.where(eq, jnp.where(col <= y, 0.0, ninf), ninf))
    return jnp.where(jnp.abs(isc) < jnp.inf, bias, ninf)


DSA_CAUSAL_BANDS = 8


def _dsa_prompt_body(qkv_q_ref, qkv_k_ref, qkv_v_ref, qi_ref, ki2_ref, kw_ref, o_ref, kb_scr, vb_scr, *,
                     topk, q_first):
    qb = qkv_q_ref.shape[1]
    t = qkv_k_ref.shape[1]
    j = pl.program_id(1)
    t0 = (j + q_first) * qb

    @pl.when(j == 0)
    def _():
        kb_scr[...] = qkv_k_ref[0].astype(BF16)
        vb_scr[...] = qkv_v_ref[0].astype(BF16)

    lane = lax.broadcasted_iota(I32, (1, LANES), 1)
    half = (jnp.where(lane < DH_A, 1.0, 0.0), jnp.where(lane < DH_A, 0.0, 1.0))
    half_b = tuple(m.astype(BF16) for m in half)
    ki2 = ki2_ref[0]
    kw = kw_ref[0]
    isc = jnp.zeros((qb, t), F32)
    for hp in range(H_IDX // 2):
        qp = qi_ref[0, :, hp * LANES:(hp + 1) * LANES]
        for h2 in range(2):
            h = 2 * hp + h2
            s = lax.dot_general(qp * half_b[h2], ki2, NT, preferred_element_type=F32)
            isc = isc + kw[:, D_IDX + h:D_IDX + h + 1] * jnp.maximum(s, 0.0)
    row = lax.broadcasted_iota(I32, (qb, t), 0) + t0
    col = lax.broadcasted_iota(I32, (qb, t), 1)
    isc = jnp.where(col <= row, isc, -jnp.inf)
    bias = _topk_bias(isc, col, topk, max(1, (t - 1).bit_length()))
    scale = DH_A ** -0.5
    assert math.frexp(scale)[0] == 0.5
    for p in range(H_A // 2):
        sl = slice(p * LANES, (p + 1) * LANES)
        qp = qkv_q_ref[0, :, sl]
        kp = kb_scr[:, sl]
        vp = vb_scr[:, sl]
        outs = []
        for h2 in range(2):
            qm = (qp * (half[h2] * scale)).astype(BF16)
            lg = lax.dot_general(qm, kp, NT, preferred_element_type=F32) + bias
            mx = jnp.max(lg, axis=-1, keepdims=True)
            pr = jnp.exp(lg - mx)
            l = jnp.sum(pr, axis=-1, keepdims=True)
            outs.append(jnp.dot(pr.astype(BF16), vp, preferred_element_type=F32) / l)
        o_ref[0, :, sl] = jnp.where(lane < DH_A, outs[0], outs[1])


def dsa_prompt_call(q, k, v, qi, ki2, kw):
    b, t, hd = q.shape
    qb = _row_tile(t, 256)
    topk = min(TOPK_MAX, t // 4)
    n_qb = t // qb
    per_band = n_qb // math.gcd(n_qb, DSA_CAUSAL_BANDS)
    outs = []
    for q_first in range(0, n_qb, per_band):
        tk = (q_first + per_band) * qb
        qmap = lambda i, j, q_first=q_first: (i, j + q_first, 0)
        outs.append(pl.pallas_call(
            functools.partial(_dsa_prompt_body, topk=topk, q_first=q_first),
            grid=(b, per_band),
            in_specs=[
                pl.BlockSpec((1, qb, hd), qmap),
                pl.BlockSpec((1, tk, hd), lambda i, j: (i, 0, 0)),
                pl.BlockSpec((1, tk, hd), lambda i, j: (i, 0, 0)),
                pl.BlockSpec((1, qb, H_IDX * D_IDX), qmap),
                pl.BlockSpec((1, tk, LANES), lambda i, j: (i, 0, 0)),
                pl.BlockSpec((1, qb, LANES), qmap),
            ],
            out_specs=pl.BlockSpec((1, qb, hd), lambda i, j: (i, j, 0)),
            out_shape=jax.ShapeDtypeStruct((b, per_band * qb, hd), F32),
            scratch_shapes=[pltpu.VMEM((tk, hd), BF16), pltpu.VMEM((tk, hd), BF16)],
            compiler_params=_cparams(("parallel", "arbitrary")),
            name="dsa_prompt",
        )(q, k, v, qi, ki2, kw))
    return outs[0] if len(outs) == 1 else jnp.concatenate(outs, axis=1)


DSA_PAGE_GROUP = 16
DSA_IDX_PAGE_GROUP = 64


def _page_group(n_pages, want=DSA_PAGE_GROUP):
    return math.gcd(n_pages, want)


def _page_specs(shape, n_pages, group):
    zeros = (0,) * (len(shape) - 1)

    def spec(g):
        return pl.BlockSpec(shape, lambda b, s, pt: (pt[b, jnp.minimum(s * group + g, n_pages - 1)],) + zeros)
    return [spec(g) for g in range(group)]


def _page_cat(refs):
    return jnp.concatenate([r[0].reshape(-1, PAGE_SIZE).astype(BF16) for r in refs], axis=1)


def _dsa_sidx_body(pt_ref, qi_ref, wi_ref, kin_ref, *rest):
    kc_refs, o_ref, onew_ref = rest[:-2], rest[-2], rest[-1]
    nq = o_ref.shape[1]
    r = lax.broadcasted_iota(I32, (nq, nq * H_IDX), 0)
    c = lax.broadcasted_iota(I32, (nq, nq * H_IDX), 1)
    lo = r * H_IDX
    wsel = jnp.where(c >= lo, jnp.where(c < lo + H_IDX, wi_ref[0], 0.0), 0.0)

    def scores(ki):
        s = jnp.dot(qi_ref[0], ki, preferred_element_type=F32)
        return jnp.dot(wsel, jnp.maximum(s, 0.0), precision=HI, preferred_element_type=F32)

    @pl.when(pl.program_id(1) == 0)
    def _():
        onew_ref[0] = scores(kin_ref[0].astype(BF16))

    o_ref[0] = scores(_page_cat(kc_refs))


def dsa_sidx_call(page_table, qi, wi, ki_new, cache_kidx):
    db, n_pages = page_table.shape
    nq = qi.shape[1] // H_IDX
    group = _page_group(n_pages, DSA_IDX_PAGE_GROUP)
    grid_spec = pltpu.PrefetchScalarGridSpec(
        num_scalar_prefetch=1,
        grid=(db, n_pages // group),
        in_specs=[
            pl.BlockSpec((1, nq * H_IDX, D_IDX), lambda b, s, pt: (b, 0, 0)),
            pl.BlockSpec((1, 1, nq * H_IDX), lambda b, s, pt: (b, 0, 0)),
            pl.BlockSpec((1, D_IDX, PAGE_SIZE), lambda b, s, pt: (b, 0, 0)),
        ] + _page_specs((1, D_IDX, PAGE_SIZE), n_pages, group),
        out_specs=[pl.BlockSpec((1, nq, group * PAGE_SIZE), lambda b, s, pt: (b, 0, s)),
                   pl.BlockSpec((1, nq, PAGE_SIZE), lambda b, s, pt: (b, 0, 0))],
    )
    return pl.pallas_call(
        _dsa_sidx_body,
        grid_spec=grid_spec,
        out_shape=[jax.ShapeDtypeStruct((db, nq, n_pages * PAGE_SIZE), F32),
                   jax.ShapeDtypeStruct((db, nq, PAGE_SIZE), F32)],
        compiler_params=_cparams(("parallel", "arbitrary")),
        name="dsa_sample_idx",
    )(page_table, qi, wi, ki_new, *([cache_kidx] * group))


def _dsa_satt_body(pt_ref, isc_ref, iscn_ref, q_ref, kn_ref, vn_ref, *rest, n_pages, n_steps, topk, n_new):
    group = n_pages // n_steps
    kc_refs, vc_refs = rest[:group], rest[group:2 * group]
    o_ref, bias_scr, qe_scr, m_scr, l_scr, acc_scr = rest[2 * group:]
    p = pl.program_id(1)
    nq = q_ref.shape[1]
    past = n_pages * PAGE_SIZE
    wcols = group * PAGE_SIZE
    ltot = past + PAGE_SIZE
    hd = H_A * DH_A
    scale = DH_A ** -0.5
    assert math.frexp(scale)[0] == 0.5

    def attend(kcat, vcat, bias):
        lg = jnp.dot(qe_scr[...], kcat, preferred_element_type=F32)
        lg = lg + jnp.concatenate([bias] * H_A, axis=0)
        m_old = m_scr[...]
        m_new = jnp.maximum(m_old, jnp.max(lg, axis=-1, keepdims=True))
        alpha = jnp.exp(m_old - m_new)
        pr = jnp.exp(lg - m_new)
        l_scr[...] = alpha * l_scr[...] + jnp.sum(pr, axis=-1, keepdims=True)
        acc_scr[...] = alpha * acc_scr[...] + lax.dot_general(pr.astype(BF16), vcat, NT, preferred_element_type=F32)
        m_scr[...] = m_new

    @pl.when(p == 0)
    def _():
        isc = jnp.concatenate([isc_ref[0], iscn_ref[0]], axis=1)
        row = lax.broadcasted_iota(I32, (nq, ltot), 0)
        col = lax.broadcasted_iota(I32, (nq, ltot), 1)
        rel = col - past
        ninf = -jnp.inf
        isc = jnp.where(rel < 0, isc, jnp.where(rel < n_new, jnp.where(rel <= row, isc, ninf), ninf))
        bias = _topk_bias(isc, col, topk, max(1, (ltot - 1).bit_length()))
        for i in range(n_steps):
            bias_scr[i] = bias[:, i * wcols:(i + 1) * wcols]
        q = q_ref[0] * scale
        lane = lax.broadcasted_iota(I32, (nq, hd), 1)
        qe_scr[...] = jnp.concatenate(
            [jnp.where(lane >= h * DH_A, jnp.where(lane < (h + 1) * DH_A, q, 0.0), 0.0) for h in range(H_A)],
            axis=0).astype(BF16)
        m_scr[...] = jnp.full_like(m_scr, NEG_BIG)
        l_scr[...] = jnp.zeros_like(l_scr)
        acc_scr[...] = jnp.zeros_like(acc_scr)
        attend(kn_ref[0].reshape(hd, PAGE_SIZE).astype(BF16), vn_ref[0].reshape(hd, PAGE_SIZE).astype(BF16),
               bias[:, past:])

    attend(_page_cat(kc_refs), _page_cat(vc_refs), bias_scr[p])

    @pl.when(p == n_steps - 1)
    def _():
        o = acc_scr[...] / l_scr[...]
        lane = lax.broadcasted_iota(I32, (nq, hd), 1)
        out = jnp.zeros((nq, hd), F32)
        for h in range(H_A):
            oh = o[h * nq:(h + 1) * nq]
            out = out + jnp.where(lane >= h * DH_A, jnp.where(lane < (h + 1) * DH_A, oh, 0.0), 0.0)
        o_ref[0] = out


def dsa_satt_call(page_table, isc, isc_new, q, k_new, v_new, cache_k, cache_v, n_new):
    db, n_pages = page_table.shape
    nq = q.shape[1]
    hd = H_A * DH_A
    page = (1, H_A, DH_A, PAGE_SIZE)
    group = _page_group(n_pages)
    n_steps = n_pages // group
    wcols = group * PAGE_SIZE
    topk = min(TOPK_MAX, (n_pages * PAGE_SIZE + n_new) // 4)
    grid_spec = pltpu.PrefetchScalarGridSpec(
        num_scalar_prefetch=1,
        grid=(db, n_steps),
        in_specs=[
            pl.BlockSpec((1, nq, n_pages * PAGE_SIZE), lambda b, p, pt: (b, 0, 0)),
            pl.BlockSpec((1, nq, PAGE_SIZE), lambda b, p, pt: (b, 0, 0)),
            pl.BlockSpec((1, nq, hd), lambda b, p, pt: (b, 0, 0)),
            pl.BlockSpec(page, lambda b, p, pt: (b, 0, 0, 0)),
            pl.BlockSpec(page, lambda b, p, pt: (b, 0, 0, 0)),
        ] + _page_specs(page, n_pages, group) * 2,
        out_specs=pl.BlockSpec((1, nq, hd), lambda b, p, pt: (b, 0, 0)),
        scratch_shapes=[
            pltpu.VMEM((n_steps, nq, wcols), F32),
            pltpu.VMEM((H_A * nq, hd), BF16),
            pltpu.VMEM((H_A * nq, 1), F32),
            pltpu.VMEM((H_A * nq, 1), F32),
            pltpu.VMEM((H_A * nq, hd), F32),
        ],
    )
    return pl.pallas_call(
        functools.partial(_dsa_satt_body, n_pages=n_pages, n_steps=n_steps, topk=topk, n_new=n_new),
        grid_spec=grid_spec,
        out_shape=jax.ShapeDtypeStruct((db, nq, hd), F32),
        compiler_params=_cparams(("parallel", "arbitrary")),
        name="dsa_sample_att",
    )(page_table, isc, isc_new, q, k_new, v_new, *([cache_k] * group), *([cache_v] * group))


def _head_sum_matrix(n, group):
    r = lax.broadcasted_iota(I32, (n, n), 0) // group
    c = lax.broadcasted_iota(I32, (n, n), 1) // group
    return jnp.where(r == c, 1.0, 0.0).astype(F32)


def _rwkv_prep_math(zb, before, mu, w0, a0, w2p, a2p, g2, kkp, ka):
    hd = H_B * DH_B
    rows = lax.broadcasted_iota(I32, zb.shape, 0)
    prev = jnp.where(rows == 0, before, pltpu.roll(zb, 1, 0))
    z = zb + (prev - zb) * mu
    r = z[:, 0:hd]
    k = z[:, hd:2 * hd]
    v = z[:, 2 * hd:3 * hd]
    xwa = z[:, 3 * hd:3 * hd + R_DECAY + R_AAA]
    xg = z[:, 3 * hd + R_DECAY + R_AAA:]
    wl = w0 + jnp.dot(jnp.tanh(xwa).astype(BF16), w2p, preferred_element_type=F32)
    w_log = -_softplus(-wl) - 0.5
    log_decay = -jnp.exp(w_log)
    a = jax.nn.sigmoid(a0 + jnp.dot(xwa.astype(BF16), a2p, preferred_element_type=F32))
    g = jnp.dot(jax.nn.sigmoid(xg).astype(BF16), g2, preferred_element_type=F32)
    kk = k * kkp
    ss = _lane_group_dot(kk * kk, _head_sum_matrix(LANES, DH_B))
    kk = kk * lax.rsqrt(jnp.maximum(ss, 1e-24))
    return r, log_decay, k * (1.0 + (a - 1.0) * ka), v, -kk, kk * a, g


def _rwkv_post_math(y, r, k, v, g, lnx_g, lnx_b, rk):
    ones = _head_sum_matrix(LANES, DH_B)
    avg = ones * (1.0 / DH_B)
    mu = _lane_group_dot(y, avg)
    d = y - mu
    var = _lane_group_dot(d * d, avg)
    yn = d * lax.rsqrt(var + LNX_EPS) * lnx_g + lnx_b
    bonus = _lane_group_dot(r * k * rk, ones)
    return (yn + bonus * v) * g


def _rwkv_prep_body(zb_ref, sp_ref, mu_ref, w0_ref, a0_ref, w2_ref, a2_ref, g2_ref, kkp_ref, ka_ref,
                    r_ref, w_ref, k_ref, v_ref, an_ref, b_ref, g_ref, carry_scr):
    tc = pl.program_id(1)
    zb = zb_ref[0]
    tt = zb.shape[0]
    hd = H_B * DH_B

    @pl.when(tc == 0)
    def _():
        carry_scr[...] = sp_ref[0]

    r, log_decay, k2, v, an, bb, g = _rwkv_prep_math(
        zb, carry_scr[...], mu_ref[...], w0_ref[...], a0_ref[...], w2_ref[...], a2_ref[...], g2_ref[...],
        kkp_ref[...], ka_ref[...])
    carry_scr[...] = zb[tt - 1:tt]
    r_ref[0] = r
    w_ref[0] = log_decay
    k_ref[0] = k2
    v_ref[0] = v
    an_ref[0] = an
    b_ref[0] = bb
    g_ref[0] = g


def rwkv_prep_call(zb, shift_prev, mu, w0, a0, w2p, a2p, g2, kkp, ka):
    b, t, dz = zb.shape
    hd = H_B * DH_B
    tt = _row_tile(t, 256)
    row = lambda n: pl.BlockSpec((1, n), lambda i, j: (0, 0))
    full = lambda s: pl.BlockSpec(s, lambda i, j: (0, 0))
    out = pl.BlockSpec((1, tt, hd), lambda i, j: (i, j, 0))
    return pl.pallas_call(
        _rwkv_prep_body,
        grid=(b, t // tt),
        in_specs=[
            pl.BlockSpec((1, tt, dz), lambda i, j: (i, j, 0)),
            pl.BlockSpec((1, 1, dz), lambda i, j: (i, 0, 0)),
            row(dz), row(hd), row(hd),
            full((R_DECAY + R_AAA, hd)), full((R_DECAY + R_AAA, hd)), full((R_GATE, hd)),
            row(hd), row(hd),
        ],
        out_specs=[out] * 7,
        out_shape=[jax.ShapeDtypeStruct((b, t, hd), F32)] * 7,
        scratch_shapes=[pltpu.VMEM((1, dz), F32)],
        compiler_params=_cparams(("parallel", "arbitrary")),
        name="rwkv_prep",
    )(zb, shift_prev, mu, w0, a0, w2p, a2p, g2, kkp, ka)


RWKV_PAIRS = H_B // 2
RWKV_STEPS = SUBLANES // RWKV_PAIRS
RWKV_NB = 8
RWKV_SUM_PIECES = 2


def _rwkv_scan_body(r_ref, w_ref, k_ref, v_ref, a_ref, b_ref, s0_ref, y_ref, sout_ref,
                    s_scr, pa_scr, pv_scr, *, t_valid):
    c = pl.program_id(1)
    nb, tc = r_ref.shape[0], r_ref.shape[1]
    chains = [(bi, p) for bi in range(nb) for p in range(RWKV_PAIRS)]
    rows = lambda n: slice(n * DH_B, (n + 1) * DH_B)

    @pl.when(c == 0)
    def _():
        for n, (bi, p) in enumerate(chains):
            s_scr[rows(n), :] = s0_ref[bi, p]

    if t_valid < tc * RWKV_STEPS:
        y_ref[...] = jnp.zeros_like(y_ref)

    sub = lax.broadcasted_iota(I32, (DH_B, LANES), 0)
    lane = lax.broadcasted_iota(I32, (DH_B, LANES), 1)
    isel = jnp.where((lane & (DH_B - 1)) == sub, 1.0, 0.0).astype(F32)
    qblk = _head_sum_matrix(LANES, DH_B).astype(BF16)

    def hsum(ref, pieces=3):
        x = ref[...]
        parts = []
        for _ in range(pieces - 1):
            h = x.astype(BF16)
            parts.append(h)
            x = x - h.astype(F32)
        parts.append(x.astype(BF16))
        return jnp.dot(jnp.concatenate(parts, axis=1), jnp.concatenate([qblk] * pieces, axis=0),
                       preferred_element_type=F32)

    def step(u, carry):
        tiles = [(a_ref[bi, u], jnp.exp(w_ref[bi, u]), k_ref[bi, u], v_ref[bi, u], b_ref[bi, u], r_ref[bi, u])
                 for bi in range(nb)]
        for i in range(RWKV_STEPS):
            for n, (bi, p) in enumerate(chains):
                at, _, _, vt, _, _ = tiles[bi]
                row = slice(i * RWKV_PAIRS + p, i * RWKV_PAIRS + p + 1)
                pa_scr[rows(n), :] = s_scr[rows(n), :] * at[row]
                pv_scr[rows(n), :] = isel * vt[row]
            sa = hsum(pa_scr, RWKV_SUM_PIECES)
            vc = hsum(pv_scr)
            for n, (bi, p) in enumerate(chains):
                _, wt, kt, _, bt, rt = tiles[bi]
                row = slice(i * RWKV_PAIRS + p, i * RWKV_PAIRS + p + 1)
                s = s_scr[rows(n), :] * wt[row] + sa[rows(n)] * bt[row] + vc[rows(n)] * kt[row]
                s_scr[rows(n), :] = s
                pa_scr[rows(n), :] = s * rt[row]
            yb = hsum(pa_scr, RWKV_SUM_PIECES)
            for n, (bi, p) in enumerate(chains):
                row = slice(i * RWKV_PAIRS + p, i * RWKV_PAIRS + p + 1)
                y_ref[bi, u, row, :] = jnp.sum(yb[rows(n)] * isel, axis=0, keepdims=True)
        return carry

    lax.fori_loop(0, min(t_valid, tc * RWKV_STEPS) // RWKV_STEPS, step, 0)

    @pl.when(c == pl.num_programs(1) - 1)
    def _():
        for n, (bi, p) in enumerate(chains):
            sout_ref[bi, p] = s_scr[rows(n), :]


def rwkv_scan_call(r, w, k, v, a, b, s0, t_valid):
    bsz, t, hd = r.shape
    nb = math.gcd(bsz, RWKV_NB)
    nt = t // RWKV_STEPS
    tc = _row_tile(nt, 32)
    assert t % RWKV_STEPS == 0 and t_valid % RWKV_STEPS == 0 and (t_valid == t or nt == tc)
    tiled = lambda x: x.reshape(bsz, nt, SUBLANES, LANES)
    seq = pl.BlockSpec((nb, tc, SUBLANES, LANES), lambda i, j: (i, j, 0, 0))
    st = pl.BlockSpec((nb, RWKV_PAIRS, DH_B, LANES), lambda i, j: (i, 0, 0, 0))
    y, s_fin = pl.pallas_call(
        functools.partial(_rwkv_scan_body, t_valid=t_valid),
        grid=(bsz // nb, nt // tc),
        in_specs=[seq] * 6 + [st],
        out_specs=[seq, st],
        out_shape=[jax.ShapeDtypeStruct((bsz, nt, SUBLANES, LANES), F32),
                   jax.ShapeDtypeStruct((bsz, RWKV_PAIRS, DH_B, LANES), F32)],
        scratch_shapes=[pltpu.VMEM((nb * RWKV_PAIRS * DH_B, LANES), F32)] * 3,
        compiler_params=_cparams(("parallel", "arbitrary")),
        name="rwkv_scan",
    )(tiled(r), tiled(w), tiled(k), tiled(v), tiled(a), tiled(b), s0)
    return y.reshape(bsz, t, hd), s_fin


RWKV_CHUNK = 64
RWKV_HG = 4
RWKV_CHUNK_SEQS = 4


def _rwkv_chunk_body(zb_ref, sp_ref, mu_ref, w0_ref, a0_ref, w2_ref, a2_ref, g2_ref, kkp_ref, ka_ref,
                     lg_ref, lb_ref, rk_ref, s0_ref, yb_ref, sout_ref, s_scr, carry_scr):
    c = pl.program_id(1)
    nb, cs = zb_ref.shape[0], zb_ref.shape[1]
    ng = s_scr.shape[1]
    gw = RWKV_HG * DH_B
    n = RWKV_HG * cs
    assert cs & (cs - 1) == 0

    @pl.when(c == 0)
    def _():
        s_scr[...] = s0_ref[...]
        carry_scr[...] = sp_ref[...]

    seqs = []
    for bi in range(nb):
        zb = zb_ref[bi]
        seqs.append(_rwkv_prep_math(zb, carry_scr[bi], mu_ref[...], w0_ref[...], a0_ref[...], w2_ref[...],
                                    a2_ref[...], g2_ref[...], kkp_ref[...], ka_ref[...]))
        carry_scr[bi] = zb[cs - 1:cs]

    tr = lax.broadcasted_iota(I32, (cs, cs), 0)
    tcol = lax.broadcasted_iota(I32, (cs, cs), 1)
    tril = jnp.where(tr >= tcol, 1.0, 0.0).astype(F32)
    row = lax.broadcasted_iota(I32, (n, n), 0)
    col = lax.broadcasted_iota(I32, (n, n), 1)
    same = (row // cs) == (col // cs)
    tt = row & (cs - 1)
    ss = col & (cs - 1)
    strict = jnp.where(same, jnp.where(tt > ss, 1.0, 0.0), 0.0).astype(F32)
    incl = jnp.where(same, jnp.where(tt >= ss, 1.0, 0.0), 0.0).astype(F32)
    eye = jnp.where(row == col, 1.0, 0.0).astype(F32)
    lane_h = lax.broadcasted_iota(I32, (1, gw), 1) // DH_B
    hmask = [jnp.where(lane_h == h, 1.0, 0.0).astype(F32) for h in range(RWKV_HG)]
    sblk = jnp.where(lax.broadcasted_iota(I32, (gw, gw), 0) // DH_B == lax.broadcasted_iota(I32, (gw, gw), 1) // DH_B,
                     1.0, 0.0).astype(F32)
    nn = (((1,), (0,)), ((), ()))

    def stack(x):
        return jnp.concatenate([x * m for m in hmask], axis=0)

    def fold(x):
        out = x[0:cs]
        for h in range(1, RWKV_HG):
            out = out + x[h * cs:(h + 1) * cs]
        return out

    def bdot(x, y, dims=nn):
        return lax.dot_general(x.astype(BF16), y.astype(BF16), dims, preferred_element_type=F32)

    jobs = []
    for bi in range(nb):
        r_all, lw_all, k_all, v_all, a_all, b_all, _ = seqs[bi]
        for g in range(ng):
            sl = slice(g * gw, (g + 1) * gw)
            lw = lw_all[:, sl]
            cl = jnp.dot(tril, lw, precision=HI, preferred_element_type=F32)
            e_in = jnp.exp(cl)
            e_inv = jnp.exp(-cl)
            e_tail = jnp.exp(cl[cs - 1:cs] - cl)
            jobs.append(dict(
                bi=bi, g=g, p_last=e_in[cs - 1:cs],
                at=a_all[:, sl] * jnp.exp(cl - lw),
                bt=b_all[:, sl] * e_inv, kt=k_all[:, sl] * e_inv, rt=r_all[:, sl] * e_in,
                bw=b_all[:, sl] * e_tail, kw=k_all[:, sl] * e_tail, v=v_all[:, sl],
                s_old=s_scr[bi, g]))
    for j in jobs:
        j['a_s'], j['r_s'], j['b_s'], j['k_s'], j['v_s'] = (stack(j[x]) for x in ('at', 'rt', 'bt', 'kt', 'v'))
    for j in jobs:
        j['lab'] = bdot(j['a_s'], j['b_s'], NT) * strict
        j['lak'] = bdot(j['a_s'], j['k_s'], NT) * strict
        j['lrb'] = bdot(j['r_s'], j['b_s'], NT) * incl
        j['lrk'] = bdot(j['r_s'], j['k_s'], NT) * incl
    for j in jobs:
        j['rhs'] = stack(bdot(j['at'], j['s_old'], NT)) + bdot(j['lak'], j['v_s'])
        j['tm'] = eye + j['lab']
        j['lp'] = j['lab']
    for _ in range(cs.bit_length() - 2):
        for j in jobs:
            j['lp'] = bdot(j['lp'], j['lp'])
        for j in jobs:
            j['tm'] = j['tm'] + bdot(j['lp'], j['tm'])
    for j in jobs:
        j['u_s'] = bdot(j['tm'], j['rhs'])
    for j in jobs:
        j['y'] = fold(stack(bdot(j['rt'], j['s_old'], NT)) + bdot(j['lrb'], j['u_s']) + bdot(j['lrk'], j['v_s']))
        u = fold(j['u_s'])
        j['s_new'] = j['s_old'] * j['p_last'] + (bdot(u, j['bw'], TN) + bdot(j['v'], j['kw'], TN)) * sblk
    for bi in range(nb):
        r_all, _, k_all, v_all, _, _, g_all = seqs[bi]
        y = jnp.concatenate([j['y'] for j in jobs if j['bi'] == bi], axis=1)
        yb_ref[bi] = _rwkv_post_math(y, r_all, k_all, v_all, g_all, lg_ref[...], lb_ref[...], rk_ref[...])
    for j in jobs:
        s_scr[j['bi'], j['g']] = j['s_new']

    @pl.when(c == pl.num_programs(1) - 1)
    def _():
        sout_ref[...] = s_scr[...]


def rwkv_chunk_call(zb, shift_prev, mu, w0, a0, w2p, a2p, g2, kkp, ka, lnx_g, lnx_b, rk, s0):
    bsz, t, dz = zb.shape
    hd = H_B * DH_B
    cs = RWKV_CHUNK
    ng = H_B // RWKV_HG
    gw = RWKV_HG * DH_B
    eye = jnp.eye(RWKV_HG, dtype=F32)
    s0g = jnp.einsum('bghij,hk->bghikj', s0.astype(F32).reshape(bsz, ng, RWKV_HG, DH_B, DH_B), eye)
    nb = math.gcd(bsz, RWKV_CHUNK_SEQS)
    st = pl.BlockSpec((nb, ng, gw, gw), lambda i, j: (i, 0, 0, 0))
    row = lambda n: pl.BlockSpec((1, n), lambda i, j: (0, 0))
    full = lambda s: pl.BlockSpec(s, lambda i, j: (0, 0))
    y, s_fin = pl.pallas_call(
        _rwkv_chunk_body,
        grid=(bsz // nb, t // cs),
        in_specs=[
            pl.BlockSpec((nb, cs, dz), lambda i, j: (i, j, 0)),
            pl.BlockSpec((nb, 1, dz), lambda i, j: (i, 0, 0)),
            row(dz), row(hd), row(hd),
            full((R_DECAY + R_AAA, hd)), full((R_DECAY + R_AAA, hd)), full((R_GATE, hd)),
            row(hd), row(hd), row(hd), row(hd), row(hd), st],
        out_specs=[pl.BlockSpec((nb, cs, hd), lambda i, j: (i, j, 0)), st],
        out_shape=[jax.ShapeDtypeStruct((bsz, t, hd), F32), jax.ShapeDtypeStruct((bsz, ng, gw, gw), F32)],
        scratch_shapes=[pltpu.VMEM((nb, ng, gw, gw), F32), pltpu.VMEM((nb, 1, dz), F32)],
        compiler_params=_cparams(("parallel", "arbitrary")),
        name="rwkv_chunk",
    )(zb, shift_prev, mu, w0, a0, w2p, a2p, g2, kkp, ka, lnx_g, lnx_b, rk, s0g.reshape(bsz, ng, gw, gw))
    s_fin = jnp.einsum('bghihj->bghij', s_fin.reshape(bsz, ng, RWKV_HG, DH_B, RWKV_HG, DH_B))
    return y, s_fin.reshape(bsz, H_B, DH_B, DH_B)


def _rwkv_post_body(y_ref, r_ref, k_ref, v_ref, g_ref, lg_ref, lb_ref, rk_ref, o_ref):
    o_ref[...] = _rwkv_post_math(y_ref[...], r_ref[...], k_ref[...], v_ref[...], g_ref[...],
                                 lg_ref[...], lb_ref[...], rk_ref[...])


def rwkv_post_call(y, r, k, v, g, lnx_g, lnx_b, rk):
    m, hd = y.shape
    tm = _row_tile(m, 512)
    tok = pl.BlockSpec((tm, hd), lambda i: (i, 0))
    row = pl.BlockSpec((1, hd), lambda i: (0, 0))
    return pl.pallas_call(
        _rwkv_post_body,
        grid=(m // tm,),
        in_specs=[tok] * 5 + [row] * 3,
        out_specs=tok,
        out_shape=jax.ShapeDtypeStruct((m, hd), F32),
        compiler_params=_cparams(("parallel",)),
        name="rwkv_post",
    )(y, r, k, v, g, lnx_g, lnx_b, rk)


def _head_mask(n, h, width):
    lane = lax.broadcasted_iota(I32, (1, n), 1)
    return jnp.where(lane >= h * width, jnp.where(lane < (h + 1) * width, 1.0, 0.0), 0.0).astype(F32)


def _head_norm128(y, g):
    mu = jnp.mean(y, axis=-1, keepdims=True)
    d = y - mu
    var = jnp.mean(d * d, axis=-1, keepdims=True)
    return d * lax.rsqrt(var + HN_EPS) * g


def _ret_body(qk_ref, v_ref, gc_ref, cos_ref, sin_ref, gn_ref, s0_ref, y_ref, sout_ref, s_scr, *, l_valid):
    c = pl.program_id(1)
    lc = qk_ref.shape[1]
    hk = H_C * DK_C

    @pl.when(c == 0)
    def _():
        s_scr[...] = s0_ref[0]

    rr = lax.broadcasted_iota(I32, (LANES, LANES), 0)
    cc = lax.broadcasted_iota(I32, (LANES, LANES), 1)
    half = DK_C // 2
    same = (rr // DK_C) == (cc // DK_C)
    dr = rr & (DK_C - 1)
    dc = cc & (DK_C - 1)
    rot = jnp.where(same, jnp.where(dr == dc + half, -1.0, jnp.where(dr + half == dc, 1.0, 0.0)), 0.0).astype(F32)

    qk = qk_ref[0]
    cos = cos_ref[...]
    sin = sin_ref[...]

    def rope(x):
        return x * cos + _lane_group_dot(x, rot) * sin

    qr = rope(qk[:, :hk])
    kr = rope(qk[:, hk:]) * DK_C ** -0.5
    krb = kr.astype(BF16)
    s_prev = s_scr[...]
    s_prev_b = s_prev.astype(BF16)

    jj = lax.broadcasted_iota(I32, (lc, lc), 0)
    ss = lax.broadcasted_iota(I32, (lc, lc), 1)
    diff = (jj - ss).astype(F32)
    jcol = lax.broadcasted_iota(I32, (lc, 1), 0).astype(F32)
    srow_state = lax.broadcasted_iota(I32, (hk, 1), 0) // DK_C
    s_new = jnp.zeros_like(s_prev)
    decay_rows = jnp.zeros((hk, 1), F32)
    hs = []
    for h in range(H_C):
        lg = math.log1p(-2.0 ** (-5.0 - h))
        mh = _head_mask(hk, h, DK_C)
        qm = (qr * mh).astype(BF16)
        hs.append(dict(h=h, lg=lg, mh=mh, qm=qm, vhb=v_ref[0, :, h * DV_C:(h + 1) * DV_C].astype(BF16),
                       qk=lax.dot_general(qm, krb, NT, preferred_element_type=F32),
                       qs=jnp.dot(qm, s_prev_b, preferred_element_type=F32)))
    for d in hs:
        dmask = jnp.where(diff >= 0, jnp.exp(jnp.maximum(diff, 0.0) * d['lg']), 0.0)
        d['scores'] = (d['qk'] * dmask).astype(BF16)
    for d in hs:
        h = d['h']
        intra = jnp.dot(d['scores'], d['vhb'], preferred_element_type=F32)
        inter = d['qs'] * jnp.exp((jcol + 1.0) * d['lg'])
        yh = _head_norm128(intra + inter, gn_ref[h:h + 1, :])
        gch = gc_ref[0, :, h * DV_C:(h + 1) * DV_C]
        d['out'] = yh * (gch * jax.nn.sigmoid(gch))
    for d in hs:
        w_s = jnp.where(jcol < l_valid, jnp.exp((l_valid - 1.0 - jcol) * d['lg']), 0.0)
        kw = (kr * d['mh'] * w_s).astype(BF16)
        s_new = s_new + lax.dot_general(kw, d['vhb'], TN, preferred_element_type=F32)
        decay_rows = jnp.where(srow_state == d['h'], math.exp(l_valid * d['lg']), decay_rows)
    for d in hs:
        y_ref[0, :, d['h'] * DV_C:(d['h'] + 1) * DV_C] = d['out']
    s_fin = decay_rows * s_prev + s_new
    s_scr[...] = s_fin

    @pl.when(c == pl.num_programs(1) - 1)
    def _():
        sout_ref[0] = s_fin


RET_CHUNK = 256
MLSTM_CHUNK = 128


def retention_call(za, cos, sin, gn, s0, valid):
    b, t, _ = za.shape
    lc = _row_tile(t, RET_CHUNK)
    l_valid = lc if valid is None else valid
    assert valid is None or t == lc
    hk, hv = H_C * DK_C, H_C * DV_C
    blk = lambda j: pl.BlockSpec((1, lc, 2 * hk), lambda i, c, j=j: (i, c, j))
    st = pl.BlockSpec((1, hk, DV_C), lambda i, c: (i, 0, 0))
    return pl.pallas_call(
        functools.partial(_ret_body, l_valid=l_valid),
        grid=(b, t // lc),
        in_specs=[blk(0), blk(1), blk(2),
                  pl.BlockSpec((lc, hk), lambda i, c: (c, 0)),
                  pl.BlockSpec((lc, hk), lambda i, c: (c, 0)),
                  pl.BlockSpec((H_C, DV_C), lambda i, c: (0, 0)),
                  st],
        out_specs=[pl.BlockSpec((1, lc, hv), lambda i, c: (i, c, 0)), st],
        out_shape=[jax.ShapeDtypeStruct((b, t, hv), F32), jax.ShapeDtypeStruct((b, hk, DV_C), F32)],
        scratch_shapes=[pltpu.VMEM((hk, DV_C), F32)],
        compiler_params=_cparams(("parallel", "arbitrary")),
        name="retention",
    )(za, za, za, cos, sin, gn, s0)


MLSTM_SEQS = 1


def _mlstm_body(*refs, l_valid):
    seqs = [_mlstm_seq(bi, *refs, l_valid=l_valid) for bi in range(refs[0].shape[0])]
    for _ in itertools.zip_longest(*seqs):
        pass


def _mlstm_seq(bi, qk_ref, v_ref, og_ref, gt_ref, cb_ref, cw_ref, cbias_ref, gbias_ref, gn_ref,
               c0_ref, n0_ref, m0_ref, y_ref, cout_ref, nout_ref, mout_ref,
               ext_scr, c_scr, n_scr, m_scr, *, l_valid):
    c = pl.program_id(1)
    lc = qk_ref.shape[1]
    hk = H_D * DK_D
    pad = SUBLANES

    @pl.when(c == 0)
    def _():
        ext_scr[bi, 0:pad, :] = cb_ref[bi]
        c_scr[bi] = c0_ref[bi]
        n_scr[bi] = n0_ref[bi]
        m_scr[bi] = m0_ref[bi]

    u = qk_ref[bi]
    ext_scr[bi, pad:pad + lc, :] = u
    acc = cbias_ref[...] + u * cw_ref[CONV_W - 1:CONV_W, :]
    for i in range(CONV_W - 1):
        sh = CONV_W - 1 - i
        acc = acc + ext_scr[bi, pad - sh:pad - sh + lc, :] * cw_ref[i:i + 1, :]
    ext_scr[bi, 0:pad, :] = u[lc - pad:lc, :]
    qkc = acc * jax.nn.sigmoid(acc)
    q = qkc[:, :hk]
    k = qkc[:, hk:] * DK_D ** -0.5
    kb = k.astype(BF16)

    yield
    gates = gt_ref[bi] + gbias_ref[...]
    logsig = -_softplus(-gates)
    rowi = lax.broadcasted_iota(I32, (lc, LANES), 0)
    logsig = jnp.where(rowi < l_valid, logsig, 0.0)
    jj = lax.broadcasted_iota(I32, (lc, lc), 0)
    ss = lax.broadcasted_iota(I32, (lc, lc), 1)
    tril = jnp.where(jj >= ss, 1.0, 0.0).astype(F32)
    bcum = jnp.dot(tril, logsig, precision=HI, preferred_element_type=F32)
    lane_g = lax.broadcasted_iota(I32, (lc, LANES), 1)
    jcol = lax.broadcasted_iota(I32, (lc, 1), 0)
    causal = jj >= ss
    ninf = -jnp.inf

    c_prev = c_scr[bi]
    c_prev_b = c_prev.astype(BF16)
    n_prev = n_scr[bi]
    m_prev = m_scr[bi]
    c_new = jnp.zeros_like(c_prev)
    carry_rows = jnp.zeros((hk, 1), F32)
    carry_lanes = jnp.zeros((1, hk), F32)
    ws_full = jnp.zeros((lc, hk), F32)
    m_out = m_prev
    srow_state = lax.broadcasted_iota(I32, (hk, 1), 0) // DK_D
    lane_state = lax.broadcasted_iota(I32, (1, hk), 1) // DK_D
    lane_m = lax.broadcasted_iota(I32, (1, LANES), 1)
    hs = []
    for h in range(H_D):
        mh = _head_mask(hk, h, DK_D)
        e_i = jnp.where(lane_g == h, 1.0, 0.0).astype(F32)
        e_f = jnp.where(lane_g == H_D + h, 1.0, 0.0).astype(F32)
        qm = q * mh
        vh = v_ref[bi, :, h * DV_D:(h + 1) * DV_D]
        hs.append(dict(
            h=h, mh=mh, qm=qm, qmb=qm.astype(BF16), vh=vh, vhb=vh.astype(BF16),
            logi_col=jnp.where(jcol < l_valid, gates[:, h:h + 1], ninf),
            b_col=bcum[:, H_D + h:H_D + h + 1],
            i_row=lax.dot_general(e_i, gates, NT, precision=HI, preferred_element_type=F32),
            b_row=lax.dot_general(e_f, bcum, NT, precision=HI, preferred_element_type=F32),
            m_h=m_prev[:, h:h + 1]))
    yield
    for d in hs:
        d['qk'] = lax.dot_general(d['qmb'], kb, NT, preferred_element_type=F32)
        d['qc'] = jnp.dot(d['qmb'], c_prev_b, preferred_element_type=F32)
    yield
    for d in hs:
        i_row = jnp.where(ss < l_valid, d['i_row'], ninf)
        d['inter'] = d['b_col'] + d['m_h']
        dmat = jnp.where(causal, d['b_col'] - d['b_row'] + i_row, ninf)
        d['m_j'] = jnp.maximum(d['inter'], jnp.max(dmat, axis=-1, keepdims=True))
        d['amat'] = jnp.exp(dmat - d['m_j']) * d['qk']
        d['sc'] = jnp.exp(d['inter'] - d['m_j'])
    yield
    for d in hs:
        num = jnp.dot(d['amat'].astype(BF16), d['vhb'], preferred_element_type=F32) + d['sc'] * d['qc']
        den = jnp.sum(d['amat'], axis=-1, keepdims=True) \
            + d['sc'] * jnp.sum(d['qm'] * n_prev, axis=-1, keepdims=True)
        hh = num / jnp.maximum(jnp.abs(den), jnp.exp(-d['m_j']))
        h = d['h']
        ogh = og_ref[bi, :, h * DV_D:(h + 1) * DV_D]
        d['out'] = _head_norm128(hh, gn_ref[h:h + 1, :]) * jax.nn.sigmoid(ogh)
    yield
    for d in hs:
        h = d['h']
        b_last = d['b_col'][l_valid - 1:l_valid, :]
        gs = b_last - d['b_col'] + d['logi_col']
        m_new = jnp.maximum(b_last + d['m_h'], jnp.max(gs, axis=0, keepdims=True))
        ws = jnp.exp(gs - m_new)
        carry = jnp.exp(b_last + d['m_h'] - m_new)
        c_new = c_new + lax.dot_general((k * d['mh']).astype(BF16), (d['vh'] * ws).astype(BF16), TN,
                                        preferred_element_type=F32)
        carry_rows = jnp.where(srow_state == h, carry, carry_rows)
        carry_lanes = jnp.where(lane_state == h, carry, carry_lanes)
        ws_full = ws_full + ws * d['mh']
        m_out = jnp.where(lane_m == h, m_new, m_out)
    c_fin = carry_rows * c_prev + c_new
    n_fin = carry_lanes * n_prev + jnp.sum(ws_full * k, axis=0, keepdims=True)
    yield
    for d in hs:
        y_ref[bi, :, d['h'] * DV_D:(d['h'] + 1) * DV_D] = d['out']
    c_scr[bi] = c_fin
    n_scr[bi] = n_fin
    m_scr[bi] = m_out

    @pl.when(c == pl.num_programs(1) - 1)
    def _():
        cout_ref[bi] = c_fin
        nout_ref[bi] = n_fin
        mout_ref[bi] = m_out


def mlstm_call(zb, zc, conv_buf, conv_w, conv_b, gate_bias, gn, c0, n0, m0, valid):
    b, t, _ = zb.shape
    lc = _row_tile(t, MLSTM_CHUNK)
    l_valid = lc if valid is None else valid
    assert valid is None or t == lc
    hk, hv = H_D * DK_D, H_D * DV_D
    nb = math.gcd(b, MLSTM_SEQS)
    blk = lambda j: pl.BlockSpec((nb, lc, 2 * hk), lambda i, c, j=j: (i, c, j))
    cst = lambda s: pl.BlockSpec(s, lambda i, c: (0,) * len(s))
    per_b = lambda s: pl.BlockSpec((nb,) + s, lambda i, c: (i,) + (0,) * len(s))
    return pl.pallas_call(
        functools.partial(_mlstm_body, l_valid=l_valid),
        grid=(b // nb, t // lc),
        in_specs=[blk(0), blk(1), blk(2),
                  pl.BlockSpec((nb, lc, LANES), lambda i, c: (i, c, 0)),
                  per_b((SUBLANES, 2 * hk)),
                  cst((CONV_W, 2 * hk)), cst((1, 2 * hk)), cst((1, LANES)), cst((H_D, DV_D)),
                  per_b((hk, DV_D)), per_b((1, hk)), per_b((1, LANES))],
        out_specs=[pl.BlockSpec((nb, lc, hv), lambda i, c: (i, c, 0)),
                   per_b((hk, DV_D)), per_b((1, hk)), per_b((1, LANES))],
        out_shape=[jax.ShapeDtypeStruct((b, t, hv), F32),
                   jax.ShapeDtypeStruct((b, hk, DV_D), F32),
                   jax.ShapeDtypeStruct((b, 1, hk), F32),
                   jax.ShapeDtypeStruct((b, 1, LANES), F32)],
        scratch_shapes=[pltpu.VMEM((nb, SUBLANES + lc, 2 * hk), F32),
                        pltpu.VMEM((nb, hk, DV_D), F32),
                        pltpu.VMEM((nb, 1, hk), F32),
                        pltpu.VMEM((nb, 1, LANES), F32)],
        compiler_params=_cparams(("parallel", "arbitrary")),
        name="mlstm",
    )(zb, zb, zb, zc, conv_buf, conv_w, conv_b, gate_bias, gn, c0, n0, m0)


def _pad_cols(w, n):
    return jnp.pad(w, ((0, 0), (0, n - w.shape[1])))


def _pad_time(a, tp):
    t = a.shape[1]
    if t == tp:
        return a
    return jnp.pad(a, ((0, 0), (0, tp - t)) + ((0, 0),) * (a.ndim - 2))


def _heads_major(a, h):
    b, t, _ = a.shape
    return a.reshape(b, t, h, -1).transpose(0, 2, 1, 3)


def _ab_mixer(xf, b, t, e, g_pre, P, prompt, shift_prev, s0, cache):
    m = b * t
    hd = H_A * DH_A
    w = P['ab_w_in'][e]
    o_qi = 3 * hd
    o_ki = o_qi + H_IDX * D_IDX
    o_wi = o_ki + D_IDX
    o_zb = o_wi + H_IDX
    q3, k3, v3 = (a.reshape(b, t, hd) for a in proj_split_call(xf, g_pre, w[:, :o_qi].astype(BF16), 3))
    qi = proj_in_call(xf, g_pre, w[:, o_qi:o_ki].astype(BF16), out_dtype=BF16)
    kw, ki2 = kiwi_call(xf, g_pre, w[:, o_ki:o_wi], w[:, o_wi:o_zb], P['kidx_g'][e])
    zb = proj_in_call(xf, g_pre, w[:, o_zb:].astype(BF16))
    ki3 = kw[:, :D_IDX].reshape(b, t, D_IDX)
    wi3 = kw[:, D_IDX:D_IDX + H_IDX].reshape(b, t, H_IDX)

    if prompt:
        ya = dsa_prompt_call(q3, k3, v3, qi.reshape(b, t, H_IDX * D_IDX),
                             ki2.reshape(b, t, LANES), kw.reshape(b, t, LANES)).reshape(m, hd)
    else:
        cache_k, cache_v, cache_kidx, page_table = cache
        n_pool = cache_k.shape[1]
        nq = SUBLANES
        qi_s = _pad_time(qi.reshape(b, t, H_IDX * D_IDX), nq).reshape(b, nq * H_IDX, D_IDX)
        wi_s = _pad_time(wi3, nq).reshape(b, 1, nq * H_IDX)
        keys_last = lambda a: jnp.moveaxis(a, 1, -1)
        new_page = lambda a: jnp.pad(keys_last(a), ((0, 0),) * (a.ndim - 1) + ((0, PAGE_SIZE - t),))
        isc, isc_new = dsa_sidx_call(page_table, qi_s, wi_s, new_page(ki3), keys_last(cache_kidx[e]))
        ya = dsa_satt_call(page_table, isc, isc_new, _pad_time(q3, nq),
                           new_page(k3.reshape(b, t, H_A, DH_A)), new_page(v3.reshape(b, t, H_A, DH_A)),
                           keys_last(cache_k[e]), keys_last(cache_v[e]), t)
        ya = ya[:, :t].reshape(m, hd)

    tp = -(-t // SUBLANES) * SUBLANES
    hb = H_B * DH_B
    zb3 = zb.reshape(b, t, D_B_IN)
    zpad = jnp.zeros((R_DECAY, hb), F32)
    w2p = jnp.concatenate([P['rwkv_w2'][e], zpad], axis=0).astype(BF16)
    a2p = jnp.concatenate([zpad, P['rwkv_a2'][e]], axis=0).astype(BF16)
    row = lambda a: a.reshape(1, -1)
    prep_args = (shift_prev.reshape(b, 1, D_B_IN), row(P['rwkv_mu'][e]), row(P['rwkv_w0'][e]),
                 row(P['rwkv_a0'][e]), w2p, a2p, P['rwkv_g2'][e].astype(BF16), row(P['rwkv_kk'][e]),
                 row(P['rwkv_ka'][e]))
    rk = jnp.broadcast_to(P['rwkv_rk'][e], (H_B, DH_B)).reshape(1, hb)
    post_args = (row(P['rwkv_lnx_g'][e]), row(P['rwkv_lnx_b'][e]), rk)
    if t % RWKV_CHUNK == 0:
        yb, s_new = rwkv_chunk_call(zb3, *prep_args, *post_args, s0)
        yb = yb.reshape(m, hb)
    else:
        r, dec, k2, v, an, bb, g = rwkv_prep_call(_pad_time(zb3, tp), *prep_args)
        s0p = s0.reshape(b, RWKV_PAIRS, 2, DH_B, DH_B).transpose(0, 1, 3, 2, 4).reshape(b, RWKV_PAIRS, DH_B, LANES)
        y, s_fin = rwkv_scan_call(r, dec, k2, v, an, bb, s0p, t)
        s_new = s_fin.reshape(b, RWKV_PAIRS, DH_B, 2, DH_B).transpose(0, 1, 3, 2, 4).reshape(b, H_B, DH_B, DH_B)
        fl = lambda a: a.reshape(b * tp, hb)
        yb = rwkv_post_call(fl(y), fl(r), fl(k2), fl(v), fl(g), *post_args)
        yb = yb.reshape(b, tp, hb)[:, :t].reshape(m, hb)
    st = (k3.reshape(b, t, H_A, DH_A), v3.reshape(b, t, H_A, DH_A), ki3, s_new, zb3[:, t - 1])
    return ya, yb, st


def _cd_mixer(xf, b, t, o, g_pre, pos, P, ret_s, m_c, m_n, m_m, conv_buf):
    w = P['cd_w_in'][o]
    hk, hv = H_C * DK_C, H_C * DV_C
    o_g = 2 * hk + 2 * hv
    o_vd = o_g + 2 * H_D * DK_D
    o_ig = o_vd + H_D * DV_D
    o_og = o_ig + 2 * H_D
    za = proj_in_call(xf, g_pre, w[:, :o_g].astype(BF16))
    zb = proj_in_call(xf, g_pre, jnp.concatenate([w[:, o_g:o_ig], w[:, o_og:]], axis=1).astype(BF16))
    zc = proj_in_call(xf, g_pre, _pad_cols(w[:, o_ig:o_og], LANES).astype(BF16))
    tp = -(-t // SUBLANES) * SUBLANES
    l_valid = t if tp != t else None
    za3 = _pad_time(za.reshape(b, t, -1), tp)
    zb3 = zb.reshape(b, t, -1)
    zc3 = _pad_time(zc.reshape(b, t, -1), tp)

    half = DK_C // 2
    inv = ROPE_BASE ** (-jnp.arange(half, dtype=F32) / half)
    ang = _pad_time(pos.astype(F32)[None], tp)[0][:, None] * inv[None, :]
    cos = jnp.tile(jnp.cos(ang), (1, 2 * H_C))
    sin = jnp.tile(jnp.sin(ang), (1, 2 * H_C))
    yc, ret_new = retention_call(za3, cos, sin, P['ret_gn'][o], ret_s.astype(F32).reshape(b, hk, DV_C), l_valid)

    hkd = H_D * DK_D
    cb = jnp.pad(conv_buf.astype(F32), ((0, 0), (SUBLANES - (CONV_W - 1), 0), (0, 0)))
    gate_bias = jnp.pad(P['mlstm_if_b'][o].astype(F32).reshape(1, 2 * H_D), ((0, 0), (0, LANES - 2 * H_D)))
    c0 = m_c.astype(F32).transpose(0, 1, 3, 2).reshape(b, hkd, DV_D)
    n0 = m_n.astype(F32).reshape(b, 1, hkd)
    m0 = jnp.pad(m_m.astype(F32), ((0, 0), (0, LANES - H_D))).reshape(b, 1, LANES)
    yd, c_new, n_new, m_new = mlstm_call(_pad_time(zb3, tp), zc3, cb, P['conv_w'][o], P['conv_b'][o].reshape(1, -1),
                                         gate_bias, P['mlstm_gn'][o], c0, n0, m0, l_valid)
    conv_new = jnp.concatenate([conv_buf.astype(F32), zb3[:, :, :2 * hkd]], axis=1)[:, -(CONV_W - 1):]
    st = (ret_new.reshape(b, H_C, DK_C, DV_C),
          c_new.reshape(b, H_D, DK_D, DV_D).transpose(0, 1, 3, 2),
          n_new.reshape(b, H_D, DK_D),
          m_new.reshape(b, LANES)[:, :H_D],
          conv_new)
    m = b * t
    return yc[:, :t].reshape(m, hv), yd[:, :t].reshape(m, H_D * DV_D), st


def _trunk(x, pos, prompt, ab_init, cd_init, cache, P):
    b, t, d = x.shape
    xf = x.reshape(b * t, d)
    depth = P['norm_g'].shape[0]
    ab_new, cd_new = [], []
    bf = lambda a: a.astype(BF16)
    ffn_w = (bf(P['ffn_wg']), bf(P['ffn_wu']), bf(P['ffn_wd']))
    for l in range(depth):
        g = P['norm_g'][l]
        xf = ffn_call(xf, g[0], g[1], *ffn_w, l, 0)
        if l % 2 == 0:
            e = l // 2
            shift_prev, s0 = ab_init(e)
            ya, yb, st = _ab_mixer(xf, b, t, e, g[2], P, prompt, shift_prev, s0, cache)
            ab_new.append(st)
            wo = bf(P['ab_w_out'][e])
            xf = proj_out_call(ya, yb, xf, g[3], wo[:H_A * DH_A], wo[H_A * DH_A:])
        else:
            o = l // 2
            yc, yd, st = _cd_mixer(xf, b, t, o, g[2], pos, P, *cd_init(o))
            cd_new.append(st)
            wo = bf(P['cd_w_out'][o])
            xf = proj_out_call(yc, yd, xf, g[3], wo[:H_C * DV_C], wo[H_C * DV_C:])
        xf = ffn_call(xf, g[4], g[5], *ffn_w, l, 1)
    ab = tuple(jnp.stack(s) for s in zip(*ab_new))
    cd = tuple(jnp.stack(s) for s in zip(*cd_new))
    return xf.reshape(b, t, d), ab, cd


def kernel(x_prompt, x_sample, cache_k, cache_v, cache_kidx, state_rwkv, state_shift, state_ret, state_mlstm_C, state_mlstm_n, state_mlstm_m, state_conv, page_table, norm_g, ffn_wg, ffn_wu, ffn_wd, ab_w_in, ab_w_out, kidx_g, rwkv_mu, rwkv_w0, rwkv_w2, rwkv_a0, rwkv_a2, rwkv_g2, rwkv_kk, rwkv_ka, rwkv_rk, rwkv_lnx_g, rwkv_lnx_b, cd_w_in, cd_w_out, ret_gn, conv_w, conv_b, mlstm_if_b, mlstm_gn):
    P = dict(norm_g=norm_g, ffn_wg=ffn_wg, ffn_wu=ffn_wu, ffn_wd=ffn_wd, ab_w_in=ab_w_in, ab_w_out=ab_w_out,
             kidx_g=kidx_g, rwkv_mu=rwkv_mu, rwkv_w0=rwkv_w0, rwkv_w2=rwkv_w2, rwkv_a0=rwkv_a0, rwkv_a2=rwkv_a2,
             rwkv_g2=rwkv_g2, rwkv_kk=rwkv_kk, rwkv_ka=rwkv_ka, rwkv_rk=rwkv_rk, rwkv_lnx_g=rwkv_lnx_g,
             rwkv_lnx_b=rwkv_lnx_b, cd_w_in=cd_w_in, cd_w_out=cd_w_out, ret_gn=ret_gn, conv_w=conv_w,
             conv_b=conv_b, mlstm_if_b=mlstm_if_b, mlstm_gn=mlstm_gn)
    B, T, _ = x_prompt.shape
    DB, DS, _ = x_sample.shape
    past = page_table.shape[1] * PAGE_SIZE

    def ab_zero(e):
        return (jnp.zeros((B, D_B_IN), F32), jnp.zeros((B, H_B, DH_B, DH_B), F32))

    def cd_zero(o):
        return (jnp.zeros((B, H_C, DK_C, DV_C), F32), jnp.zeros((B, H_D, DV_D, DK_D), F32),
                jnp.zeros((B, H_D, DK_D), F32), jnp.zeros((B, H_D), F32),
                jnp.zeros((B, CONV_W - 1, 2 * H_D * DK_D), F32))

    def ab_cached(e):
        return (state_shift[e], state_rwkv[e])

    def cd_cached(o):
        return (state_ret[o], state_mlstm_C[o], state_mlstm_n[o], state_mlstm_m[o], state_conv[o])

    y_p, (kp, vp, kip, rwp, shp), (rtp, cp, nvp, mp, cvp) = _trunk(
        x_prompt, jnp.arange(T), True, ab_zero, cd_zero, None, P)
    y_s, (ks_, vs_, kis, rws, shs), (rts, cs, nvs, ms, cvs) = _trunk(
        x_sample, past + jnp.arange(DS), False, ab_cached, cd_cached,
        (cache_k, cache_v, cache_kidx, page_table), P)
    return (y_p, y_s, kp, vp, kip, rwp, shp, rtp, cp, nvp, mp, cvp,
            ks_, vs_, kis, rws, shs, rts, cs, nvs, ms, cvs)
```
